```python
import math
import jax
import jax.numpy as jnp
from jax import lax
import numpy as np

D_MODEL = 2048
BATCH = 4
SEQ = 2048
DEPTH = 2
DEC_BATCH = 128
DEC_SEQ = 8
PAST_LEN = 2048
PAGE_SIZE = 128

HEAD_DIM = 128
EPS = 1e-6
NEG_INF = -1e30
Q_BLOCK = 128

CONV_WIDTH = 31
CONV_BUF = CONV_WIDTH - 1
D_CONV = D_MODEL // 2
POOL_WINDOWS = (2, 4, 8, 16)
N_POOL_GROUPS = len(POOL_WINDOWS)
D_POOL = D_MODEL // 2
POOL_GROUP = D_POOL // N_POOL_GROUPS
POOL_BUF = max(POOL_WINDOWS) - 1
AB_IN = 2 * D_CONV + D_POOL
AB_OUT = D_CONV + D_POOL

NSA_HEADS = (D_MODEL // 2) // HEAD_DIM
NSA_KV_HEADS = 2
NSA_GROUP = NSA_HEADS // NSA_KV_HEADS
NSA_BLOCK = 64
NSA_TOPN = 16
NSA_WINDOW = 512
NSA_BRANCHES = 3
NSA_KV_W = 2 * NSA_KV_HEADS * HEAD_DIM
DSA_HEADS = (D_MODEL // 2) // HEAD_DIM
DSA_KV_HEADS = 2
DSA_GROUP = DSA_HEADS // DSA_KV_HEADS
DSA_KV_W = 2 * DSA_KV_HEADS * HEAD_DIM
IDX_HEADS = 8
IDX_DIM = 64
DSA_TOPK_MAX = 256
CD_SPLITS = (NSA_HEADS * HEAD_DIM, NSA_KV_W, NSA_KV_W, NSA_KV_W, NSA_HEADS * NSA_BRANCHES,
             DSA_HEADS * HEAD_DIM, DSA_KV_W, IDX_HEADS * IDX_DIM, IDX_DIM, IDX_HEADS)
CD_IN = sum(CD_SPLITS)
CD_OUT = (NSA_HEADS + DSA_HEADS) * HEAD_DIM

N_ATTN_HEADS = NSA_HEADS + DSA_HEADS
N_BUCKETS = 32
MAX_DISTANCE = 128
D_FF = 4 * D_MODEL
N_EVEN = (DEPTH + 1) // 2
N_ODD = DEPTH // 2

kernel_name = "hybrid_conv_pool_nsa_dsa_decode_step"


def rmsnorm(x, g):
    xf = x.astype(jnp.float32)
    y = xf * lax.rsqrt(jnp.mean(xf * xf, -1, keepdims=True) + EPS)
    return (y * g.astype(jnp.float32)).astype(x.dtype)


def layernorm(x, g, b):
    xf = x.astype(jnp.float32)
    mu = jnp.mean(xf, -1, keepdims=True)
    xc = xf - mu
    y = xc * lax.rsqrt(jnp.mean(xc * xc, -1, keepdims=True) + EPS)
    return (y * g.astype(jnp.float32) + b.astype(jnp.float32)).astype(x.dtype)


def split_cols(a, sizes):
    return jnp.split(a, np.cumsum(sizes)[:-1].tolist(), axis=-1)


def rel_bucket(dist):
    n = jnp.maximum(dist, 0)
    exact = N_BUCKETS // 2
    nf = jnp.maximum(n, 1).astype(jnp.float32)
    big = exact + (jnp.log(nf / exact) / math.log(MAX_DISTANCE / exact) * (N_BUCKETS - exact)).astype(jnp.int32)
    return jnp.where(n < exact, n, jnp.minimum(big, N_BUCKETS - 1))


def masked_softmax(s, mask):
    s = jnp.where(mask, s, NEG_INF)
    m = jnp.max(s, -1, keepdims=True)
    p = jnp.exp(s - m) * mask.astype(jnp.float32)
    return p / jnp.maximum(jnp.sum(p, -1, keepdims=True), 1e-30)


def map_query_blocks(fn, qb, *arrays):
    B, T = arrays[0].shape[:2]
    nb = T // qb
    xs = tuple(jnp.swapaxes(a.reshape((B, nb, qb) + a.shape[2:]), 0, 1) for a in arrays)
    out = lax.map(lambda args: fn(*args), xs)
    return jax.tree_util.tree_map(lambda o: jnp.swapaxes(o, 0, 1).reshape((B, T) + o.shape[3:]), out)


def conv_module(a, gate, buf, w_dw, b_dw, ln_g, ln_b):
    u = a * jax.nn.sigmoid(gate)
    ext = jnp.concatenate([buf, u], 1)
    c = lax.conv_general_dilated(ext, w_dw[:, None, :], (1,), 'VALID',
                                 dimension_numbers=('NWC', 'WIO', 'NWC'),
                                 feature_group_count=D_CONV) + b_dw
    return jax.nn.silu(layernorm(c, ln_g, ln_b)), ext[:, -CONV_BUF:]


def pool_mixer(v, buf, pos, w_grp, scale):
    ext = jnp.concatenate([buf, v], 1)
    ext_f = ext.astype(jnp.float32)
    csum = jnp.concatenate([jnp.zeros_like(ext_f[:, :1]), jnp.cumsum(ext_f, 1)], 1)
    T = v.shape[1]
    end = csum[:, POOL_BUF + 1:]
    outs = []
    for g, w in enumerate(POOL_WINDOWS):
        sl = slice(g * POOL_GROUP, (g + 1) * POOL_GROUP)
        start = csum[:, POOL_BUF + 1 - w:POOL_BUF + 1 - w + T, sl]
        cnt = jnp.minimum(pos + 1, w).astype(jnp.float32)[None, :, None]
        outs.append((end[..., sl] - start) / cnt - ext_f[:, POOL_BUF:, sl])
    d = jnp.stack(outs, 2).astype(v.dtype)
    y = jnp.einsum('btgc,gcd->btgd', d, w_grp).reshape(v.shape) * scale
    return y, ext[:, -POOL_BUF:]


def ab_mixer(h, pos, conv_buf, pool_buf, w_in, conv_w, conv_b, ln_g, ln_b, pool_w, pool_scale, w_out):
    a, gate, v = split_cols(h @ w_in, (D_CONV, D_CONV, D_POOL))
    yc, new_conv = conv_module(a, gate, conv_buf, conv_w, conv_b, ln_g, ln_b)
    yp, new_pool = pool_mixer(v, pool_buf, pos, pool_w, pool_scale)
    return jnp.concatenate([yc, yp], -1) @ w_out, new_conv, new_pool


def cd_mixer(h, pos, pos0, past_cmp, past_sel, past_dsa, past_idx, win_buf, win_keep, qb,
             w_in, w_cmp, rel_bias, w_out):
    B, T, _ = h.shape
    dt = h.dtype
    G, R, DH = NSA_KV_HEADS, NSA_GROUP, HEAD_DIM
    scale = HEAD_DIM ** -0.5
    q_n, kv_c, kv_s, kv_w, gates, q_d, kv_d, q_i, k_i, w_i = split_cols(h @ w_in, CD_SPLITS)
    q_n = q_n.reshape(B, T, G, R, DH)
    kv_c = kv_c.reshape(B, T, 2, G, DH)
    kv_s = kv_s.reshape(B, T, 2, G, DH)
    kv_w = kv_w.reshape(B, T, 2, G, DH)
    gates = jax.nn.sigmoid(gates.astype(jnp.float32)).astype(dt).reshape(B, T, G, R, NSA_BRANCHES)
    q_d = q_d.reshape(B, T, DSA_KV_HEADS, DSA_GROUP, DH)
    kv_d = kv_d.reshape(B, T, 2, DSA_KV_HEADS, DH)
    q_i = q_i.reshape(B, T, IDX_HEADS, IDX_DIM)
    w_i = w_i * (IDX_HEADS ** -0.5)

    ctx_c = jnp.concatenate([past_cmp, kv_c], 1)
    ctx_s = jnp.concatenate([past_sel, kv_s], 1)
    ctx_d = jnp.concatenate([past_dsa, kv_d], 1)
    ctx_i = jnp.concatenate([past_idx, k_i], 1)
    L = ctx_c.shape[1]
    qpos = jnp.broadcast_to(pos[None], (B, T))
    bias_n = rel_bias[:, :NSA_HEADS].reshape(N_BUCKETS, G, R)
    bias_n_g = jnp.transpose(bias_n, (1, 0, 2))
    bias_d = rel_bias[:, NSA_HEADS:].reshape(N_BUCKETS, DSA_KV_HEADS, DSA_GROUP)

    n_cmp = L // NSA_BLOCK
    blocks = ctx_c[:, :n_cmp * NSA_BLOCK].reshape(B, n_cmp, NSA_BLOCK, 2, G, DH)
    comp = jnp.einsum('bnjcgd,cjg->bncgd', blocks, w_cmp)
    ck, cv = comp[:, :, 0], comp[:, :, 1]
    blk_end = (jnp.arange(n_cmp, dtype=jnp.int32) + 1) * NSA_BLOCK - 1
    dist_c = pos[:, None] - blk_end[None]
    bias_c = jnp.transpose(bias_n[rel_bucket(dist_c)], (0, 2, 3, 1))[None]
    s_c = jnp.einsum('btgrd,bngd->btgrn', q_n, ck).astype(jnp.float32) * scale + bias_c
    p_c = masked_softmax(s_c, (dist_c >= 0)[None, :, None, None, :])
    o_c = jnp.einsum('btgrn,bngd->btgrd', p_c.astype(dt), cv)

    n_sel = -(-L // NSA_BLOCK)
    imp = jnp.pad(jnp.sum(p_c, 3), ((0, 0), (0, 0), (0, 0), (0, n_sel - n_cmp)))
    blk = jnp.arange(n_sel, dtype=jnp.int32)[None]
    cur = (pos // NSA_BLOCK)[:, None]
    imp = jnp.where((blk == cur)[None, :, None, :], 2.0,
                    jnp.where((blk > cur)[None, :, None, :], -1.0, imp))
    n_top = min(NSA_TOPN, n_sel)
    _, sel_idx = lax.top_k(imp, n_top)
    sel_blocks = jnp.pad(ctx_s, ((0, 0), (0, n_sel * NSA_BLOCK - L), (0, 0), (0, 0), (0, 0)))
    sel_blocks = jnp.transpose(sel_blocks.reshape(B, n_sel, NSA_BLOCK, 2, G, DH), (0, 4, 1, 2, 3, 5))

    wb = win_buf.shape[1]
    win_ext = jnp.concatenate([jnp.zeros((B, NSA_WINDOW - wb, 2, G, DH), dt), win_buf, kv_w], 1)
    b_ix = jnp.arange(B)[:, None, None, None]
    g_ix = jnp.arange(G)[None, None, :, None]

    def nsa_block(q_b, idx_b, qp_b):
        nq = q_b.shape[1]
        gath = sel_blocks[b_ix, g_ix, idx_b]
        K = n_top * NSA_BLOCK
        gath = gath.reshape(B, nq, G, K, 2, DH)
        kpos = (idx_b[..., None] * NSA_BLOCK + jnp.arange(NSA_BLOCK, dtype=jnp.int32)).reshape(B, nq, G, K)
        dist = qp_b[:, :, None, None] - kpos
        bias = jnp.swapaxes(bias_n_g[g_ix, rel_bucket(dist)], -1, -2)
        s = jnp.einsum('btgrd,btgkd->btgrk', q_b, gath[..., 0, :]).astype(jnp.float32) * scale + bias
        p = masked_softmax(s, (dist >= 0)[:, :, :, None, :])
        o_s = jnp.einsum('btgrk,btgkd->btgrd', p.astype(dt), gath[..., 1, :])
        start = qp_b[0, 0] - pos0
        kw = lax.dynamic_slice_in_dim(win_ext, start, NSA_WINDOW + nq, axis=1)
        kpos_w = pos0 - NSA_WINDOW + start + jnp.arange(NSA_WINDOW + nq, dtype=jnp.int32)
        dist_w = qp_b[0][:, None] - kpos_w[None]
        bias_w = jnp.transpose(bias_n[rel_bucket(dist_w)], (0, 2, 3, 1))[None]
        s_w = jnp.einsum('btgrd,bkgd->btgrk', q_b, kw[:, :, 0]).astype(jnp.float32) * scale + bias_w
        mask_w = ((dist_w >= 0) & (dist_w < NSA_WINDOW) & (kpos_w[None] >= 0))[None, :, None, None, :]
        p_w = masked_softmax(s_w, mask_w)
        o_w = jnp.einsum('btgrk,bkgd->btgrd', p_w.astype(dt), kw[:, :, 1])
        return o_s, o_w

    o_s, o_w = map_query_blocks(nsa_block, qb, q_n, sel_idx, qpos)
    o_nsa = gates[..., 0, None] * o_c + gates[..., 1, None] * o_s + gates[..., 2, None] * o_w
    o_nsa = o_nsa.reshape(B, T, NSA_HEADS * DH)

    k_sel = min(DSA_TOPK_MAX, L // 4)
    kpos_all = jnp.arange(L, dtype=jnp.int32)
    bt_ix = jnp.arange(B)[:, None, None]

    def dsa_block(qd_b, qi_b, wi_b, qp_b):
        sc = jnp.einsum('bthi,bsi->bths', qi_b, ctx_i).astype(jnp.float32) * (IDX_DIM ** -0.5)
        score = jnp.einsum('bths,bth->bts', jax.nn.relu(sc), wi_b.astype(jnp.float32))
        score = jnp.where(kpos_all[None, None, :] <= qp_b[:, :, None], score, NEG_INF)
        _, idx = lax.top_k(score, k_sel)
        gath = ctx_d[bt_ix, idx]
        dist = qp_b[:, :, None] - idx
        bias = jnp.transpose(bias_d[rel_bucket(dist)], (0, 1, 3, 4, 2))
        s = jnp.einsum('btgrd,btkgd->btgrk', qd_b, gath[:, :, :, 0]).astype(jnp.float32) * scale + bias
        p = masked_softmax(s, (dist >= 0)[:, :, None, None, :])
        return jnp.einsum('btgrk,btkgd->btgrd', p.astype(dt), gath[:, :, :, 1])

    o_d = map_query_blocks(dsa_block, qb, q_d, q_i, w_i, qpos).reshape(B, T, DSA_HEADS * DH)
    y = jnp.concatenate([o_nsa, o_d], -1) @ w_out
    return y, kv_c, kv_s, win_ext[:, -win_keep:], kv_d, k_i


def run_trunk(x, pos0, conv_bufs, pool_bufs, past_cmp, past_sel, past_dsa, past_idx, win_bufs,
              win_keep, qb, weights):
    (norm_mix, norm_ffn, norm_final, ab_w_in, ab_conv_w, ab_conv_b, ab_ln_g, ab_ln_b, ab_pool_w,
     ab_pool_scale, ab_w_out, cd_w_in, cd_w_cmp, cd_w_out, rel_bias, ffn_w1, ffn_w2) = weights
    T = x.shape[1]
    pos = pos0 + jnp.arange(T, dtype=jnp.int32)
    n_conv, n_pool, n_cmp, n_sel, n_win, n_dsa, n_idx = [], [], [], [], [], [], []
    for i in range(DEPTH):
        j = i // 2
        h = rmsnorm(x, norm_mix[i])
        if i % 2 == 0:
            y, c, p = ab_mixer(h, pos, conv_bufs[j], pool_bufs[j], ab_w_in[j], ab_conv_w[j], ab_conv_b[j],
                               ab_ln_g[j], ab_ln_b[j], ab_pool_w[j], ab_pool_scale[j], ab_w_out[j])
            n_conv.append(c)
            n_pool.append(p)
        else:
            y, kc, ks, kw, kd, ki = cd_mixer(h, pos, pos0, past_cmp[j], past_sel[j], past_dsa[j], past_idx[j],
                                             win_bufs[j], win_keep, qb, cd_w_in[j], cd_w_cmp[j], rel_bias,
                                             cd_w_out[j])
            n_cmp.append(kc)
            n_sel.append(ks)
            n_win.append(kw)
            n_dsa.append(kd)
            n_idx.append(ki)
        x = x + y
        hf = rmsnorm(x, norm_ffn[i])
        x = x + jnp.square(jax.nn.relu(hf @ ffn_w1[i])) @ ffn_w2[i]
    return (rmsnorm(x, norm_final), jnp.stack(n_conv), jnp.stack(n_pool), jnp.stack(n_cmp), jnp.stack(n_sel),
            jnp.stack(n_win), jnp.stack(n_dsa), jnp.stack(n_idx))


def setup_inputs(seed: int = 0) -> dict:
    key = jax.random.key(seed)
    ks = iter(jax.random.split(key, 40))
    f32 = jnp.float32

    def nrm(shape, s=1.0):
        return jax.random.normal(next(ks), shape, f32) * s

    n_pages = PAST_LEN // PAGE_SIZE
    n_used = DEC_BATCH * n_pages
    n_pool = n_used + (n_used + 3) // 4
    win_len = min(NSA_WINDOW, PAST_LEN)
    page_table = jax.random.permutation(next(ks), n_pool)[:n_used].reshape(DEC_BATCH, n_pages).astype(jnp.int32)
    return {
        "x_prompt": nrm((BATCH, SEQ, D_MODEL)),
        "x_sample": nrm((DEC_BATCH, DEC_SEQ, D_MODEL)),
        "state_conv": nrm((N_EVEN, DEC_BATCH, CONV_BUF, D_CONV), 0.5),
        "state_pool": nrm((N_EVEN, DEC_BATCH, POOL_BUF, D_POOL)),
        "cache_nsa_cmp": nrm((N_ODD, n_pool, PAGE_SIZE, 2, NSA_KV_HEADS, HEAD_DIM)),
        "cache_nsa_sel": nrm((N_ODD, n_pool, PAGE_SIZE, 2, NSA_KV_HEADS, HEAD_DIM)),
        "cache_nsa_win": nrm((N_ODD, DEC_BATCH, win_len, 2, NSA_KV_HEADS, HEAD_DIM)),
        "cache_dsa_kv": nrm((N_ODD, n_pool, PAGE_SIZE, 2, DSA_KV_HEADS, HEAD_DIM)),
        "cache_dsa_idx": nrm((N_ODD, n_pool, PAGE_SIZE, IDX_DIM)),
        "page_table": page_table,
        "norm_mix": 1.0 + nrm((DEPTH, D_MODEL), 0.05),
        "norm_ffn": 1.0 + nrm((DEPTH, D_MODEL), 0.05),
        "norm_final": 1.0 + nrm((D_MODEL,), 0.05),
        "ab_w_in": nrm((N_EVEN, D_MODEL, AB_IN), D_MODEL ** -0.5),
        "ab_conv_w": nrm((N_EVEN, CONV_WIDTH, D_CONV), CONV_WIDTH ** -0.5),
        "ab_conv_b": nrm((N_EVEN, D_CONV), 0.02),
        "ab_ln_g": 1.0 + nrm((N_EVEN, D_CONV), 0.05),
        "ab_ln_b": nrm((N_EVEN, D_CONV), 0.02),
        "ab_pool_w": nrm((N_EVEN, N_POOL_GROUPS, POOL_GROUP, POOL_GROUP), POOL_GROUP ** -0.5),
        "ab_pool_scale": 1.0 + nrm((N_EVEN, D_POOL), 0.05),
        "ab_w_out": nrm((N_EVEN, AB_OUT, D_MODEL), AB_OUT ** -0.5),
        "cd_w_in": nrm((N_ODD, D_MODEL, CD_IN), D_MODEL ** -0.5),
        "cd_w_cmp": (1.0 + nrm((N_ODD, 2, NSA_BLOCK, NSA_KV_HEADS), 0.1)) / NSA_BLOCK,
        "cd_w_out": nrm((N_ODD, CD_OUT, D_MODEL), CD_OUT ** -0.5),
        "rel_bias": nrm((N_BUCKETS, N_ATTN_HEADS), 0.5),
        "ffn_w1": nrm((DEPTH, D_MODEL, D_FF), D_MODEL ** -0.5),
        "ffn_w2": nrm((DEPTH, D_FF, D_MODEL), D_FF ** -0.5),
    }


def reference(x_prompt, x_sample, state_conv, state_pool, cache_nsa_cmp, cache_nsa_sel, cache_nsa_win,
              cache_dsa_kv, cache_dsa_idx, page_table, norm_mix, norm_ffn, norm_final, ab_w_in, ab_conv_w,
              ab_conv_b, ab_ln_g, ab_ln_b, ab_pool_w, ab_pool_scale, ab_w_out, cd_w_in, cd_w_cmp, cd_w_out,
              rel_bias, ffn_w1, ffn_w2):
    weights = (norm_mix, norm_ffn, norm_final, ab_w_in, ab_conv_w, ab_conv_b, ab_ln_g, ab_ln_b, ab_pool_w,
               ab_pool_scale, ab_w_out, cd_w_in, cd_w_cmp, cd_w_out, rel_bias, ffn_w1, ffn_w2)
    bp, tp = x_prompt.shape[:2]
    dtp = x_prompt.dtype
    kv_empty = jnp.zeros((N_ODD, bp, 0, 2, NSA_KV_HEADS, HEAD_DIM), dtp)
    dkv_empty = jnp.zeros((N_ODD, bp, 0, 2, DSA_KV_HEADS, HEAD_DIM), dtp)
    idx_empty = jnp.zeros((N_ODD, bp, 0, IDX_DIM), dtp)
    (y_prompt, conv_p, pool_p, cmp_p, sel_p, win_p, dsa_p, idx_p) = run_trunk(
        x_prompt, 0,
        jnp.zeros((N_EVEN, bp, CONV_BUF, D_CONV), dtp), jnp.zeros((N_EVEN, bp, POOL_BUF, D_POOL), dtp),
        kv_empty, kv_empty, dkv_empty, idx_empty, kv_empty,
        min(NSA_WINDOW, tp), min(Q_BLOCK, tp), weights)
    db = x_sample.shape[0]
    past_len = page_table.shape[1] * cache_nsa_cmp.shape[2]

    def gather_pages(pool):
        g = pool[:, page_table]
        return g.reshape((g.shape[0], db, past_len) + g.shape[4:])

    (y_sample, conv_s, pool_s, cmp_s, sel_s, win_s, dsa_s, idx_s) = run_trunk(
        x_sample, past_len, state_conv, state_pool,
        gather_pages(cache_nsa_cmp), gather_pages(cache_nsa_sel), gather_pages(cache_dsa_kv),
        gather_pages(cache_dsa_idx), cache_nsa_win,
        cache_nsa_win.shape[2], 1, weights)
    return (y_prompt, y_sample, conv_p, conv_s, pool_p, pool_s, cmp_p, cmp_s, sel_p, sel_s,
            win_p, win_s, dsa_p, dsa_s, idx_p, idx_s)
```

```python
import functools
import math

import numpy as np
import jax
import jax.numpy as jnp
from jax import lax
from jax.experimental import pallas as pl
from jax.experimental.pallas import tpu as pltpu

F32 = jnp.float32
BF16 = jnp.bfloat16

EPS = 1e-6
NEG_INF = -1e30
HEAD_DIM = 128
LANES = 128
CONV_WIDTH = 31
CONV_BUF = CONV_WIDTH - 1
POOL_WINDOWS = (2, 4, 8, 16)
POOL_BUF = max(POOL_WINDOWS) - 1
HALO = 32
VMEM_LIMIT = 56 * 1024 * 1024


def _cparams(*sem):
    return pltpu.CompilerParams(dimension_semantics=sem, vmem_limit_bytes=VMEM_LIMIT)


def _norm_matmul_body(x_ref, g_ref, w_ref, o_ref, h_ref, *, relu2):
    @pl.when(pl.program_id(1) == 0)
    def _():
        x = x_ref[...]
        y = x * lax.rsqrt(jnp.mean(x * x, -1, keepdims=True) + EPS)
        h_ref[...] = (y * g_ref[...]).astype(BF16)

    y = jnp.dot(h_ref[...], w_ref[...], preferred_element_type=F32)
    if relu2:
        y = jnp.square(jnp.maximum(y, 0.0))
    o_ref[...] = y.astype(o_ref.dtype)


def norm_matmul(x, g, w, *, relu2=False, out_dtype=F32, tm=1024, tn=512):
    m, d = x.shape
    n = w.shape[1]
    assert m % tm == 0 and n % tn == 0
    return pl.pallas_call(
        functools.partial(_norm_matmul_body, relu2=relu2),
        grid=(m // tm, n // tn),
        in_specs=[pl.BlockSpec((tm, d), lambda i, j: (i, 0)),
                  pl.BlockSpec((1, d), lambda i, j: (0, 0)),
                  pl.BlockSpec((d, tn), lambda i, j: (0, j))],
        out_specs=pl.BlockSpec((tm, tn), lambda i, j: (i, j)),
        out_shape=jax.ShapeDtypeStruct((m, n), out_dtype),
        scratch_shapes=[pltpu.VMEM((tm, d), BF16)],
        compiler_params=_cparams("parallel", "arbitrary"),
        name="norm_matmul",
    )(x, g.reshape(1, d), w)


def _matmul_residual_body(a_ref, w_ref, r_ref, *rest, final_norm):
    if final_norm:
        g_ref, o_ref, n_ref, acc_ref = rest
    else:
        o_ref, acc_ref = rest
    k = pl.program_id(1)

    @pl.when(k == 0)
    def _():
        acc_ref[...] = jnp.zeros_like(acc_ref)

    acc_ref[...] += jnp.dot(a_ref[...], w_ref[...], preferred_element_type=F32)

    @pl.when(k == pl.num_programs(1) - 1)
    def _():
        o = r_ref[...] + acc_ref[...]
        o_ref[...] = o
        if final_norm:
            y = o * lax.rsqrt(jnp.mean(o * o, -1, keepdims=True) + EPS)
            n_ref[...] = y * g_ref[...]


def matmul_residual(a, w, r, g_final=None, *, tm=512, tk=1024):
    m, kdim = a.shape
    n = w.shape[1]
    assert m % tm == 0 and kdim % tk == 0
    final_norm = g_final is not None
    in_specs = [pl.BlockSpec((tm, tk), lambda i, k: (i, k)),
                pl.BlockSpec((tk, n), lambda i, k: (k, 0)),
                pl.BlockSpec((tm, n), lambda i, k: (i, 0))]
    out_spec = pl.BlockSpec((tm, n), lambda i, k: (i, 0))
    out_shape = jax.ShapeDtypeStruct((m, n), F32)
    args = [a, w, r]
    if final_norm:
        in_specs.append(pl.BlockSpec((1, n), lambda i, k: (0, 0)))
        args.append(g_final.reshape(1, n))
        out_spec, out_shape = [out_spec, out_spec], [out_shape, out_shape]
    return pl.pallas_call(
        functools.partial(_matmul_residual_body, final_norm=final_norm),
        grid=(m // tm, kdim // tk),
        in_specs=in_specs,
        out_specs=out_spec,
        out_shape=out_shape,
        scratch_shapes=[pltpu.VMEM((tm, n), F32)],
        compiler_params=_cparams("parallel", "arbitrary"),
        name="matmul_residual",
    )(*args)


def _layernorm_silu(c, g, b):
    mu = jnp.mean(c, -1, keepdims=True)
    xc = c - mu
    y = xc * lax.rsqrt(jnp.mean(xc * xc, -1, keepdims=True) + EPS)
    y = y * g + b
    return y * jax.nn.sigmoid(y)


def _ab_mid_body(z_ref, zp_ref, cw_ref, cb_ref, lg_ref, lb_ref, pw_ref, ps_ref, y_ref, u_ref,
                 ext_ref, vext_ref, conv_ref, *, tt, d_conv, d_pool):
    ti = pl.program_id(1)
    keep = (ti > 0).astype(F32)
    a_p = zp_ref[:, 0:d_conv]
    g_p = zp_ref[:, d_conv:2 * d_conv]
    ext_ref[0:HALO, :] = a_p * jax.nn.sigmoid(g_p) * keep
    vext_ref[0:HALO, :] = zp_ref[:, 2 * d_conv:] * keep
    u = z_ref[:, 0:d_conv] * jax.nn.sigmoid(z_ref[:, d_conv:2 * d_conv])
    ext_ref[HALO:, :] = u
    u_ref[...] = u
    vext_ref[HALO:, :] = z_ref[:, 2 * d_conv:]

    off = HALO - CONV_BUF
    for c in range(d_conv // LANES):
        cs = slice(c * LANES, (c + 1) * LANES)
        acc = jnp.zeros((tt, LANES), F32)
        for j in range(CONV_WIDTH):
            acc = acc + cw_ref[j:j + 1, cs] * ext_ref[off + j:off + j + tt, cs]
        conv_ref[:, cs] = acc + cb_ref[:, cs]
    y_ref[:, 0:d_conv] = _layernorm_silu(conv_ref[...], lg_ref[...], lb_ref[...]).astype(y_ref.dtype)

    pos = ti * tt + lax.broadcasted_iota(jnp.int32, (tt, 1), 0)
    pg = d_pool // len(POOL_WINDOWS)
    for gi, w in enumerate(POOL_WINDOWS):
        gs = slice(gi * pg, (gi + 1) * pg)
        tok = vext_ref[HALO:, gs]
        acc = tok
        for i in range(1, w):
            acc = acc + vext_ref[HALO - i:HALO - i + tt, gs]
        cnt = jnp.minimum(pos + 1, w).astype(F32)
        d = acc / cnt - tok
        yp = jnp.dot(d.astype(BF16), pw_ref[gi], preferred_element_type=F32) * ps_ref[:, gs]
        y_ref[:, d_conv + gi * pg:d_conv + (gi + 1) * pg] = yp.astype(y_ref.dtype)


def ab_mid_prompt(z, n_seq, t_len, conv_w, conv_b, ln_g, ln_b, pool_w, pool_scale, *, tt=256):
    d_conv = conv_w.shape[1]
    d_pool = pool_scale.shape[0]
    nt = t_len // tt
    hb = tt // HALO
    row = lambda b, t: (b * nt + t, 0)
    const = lambda b, t: (0, 0)
    return pl.pallas_call(
        functools.partial(_ab_mid_body, tt=tt, d_conv=d_conv, d_pool=d_pool),
        grid=(n_seq, nt),
        in_specs=[pl.BlockSpec((tt, z.shape[1]), row),
                  pl.BlockSpec((HALO, z.shape[1]), lambda b, t: (jnp.maximum((b * nt + t) * hb - 1, 0), 0)),
                  pl.BlockSpec(conv_w.shape, const),
                  pl.BlockSpec((1, d_conv), const),
                  pl.BlockSpec((1, d_conv), const),
                  pl.BlockSpec((1, d_conv), const),
                  pl.BlockSpec(pool_w.shape, lambda b, t: (0, 0, 0)),
                  pl.BlockSpec((1, d_pool), const)],
        out_specs=[pl.BlockSpec((tt, d_conv + d_pool), row),
                   pl.BlockSpec((tt, d_conv), row)],
        out_shape=[jax.ShapeDtypeStruct((n_seq * t_len, d_conv + d_pool), BF16),
                   jax.ShapeDtypeStruct((n_seq * t_len, d_conv), F32)],
        scratch_shapes=[pltpu.VMEM((HALO + tt, d_conv), F32),
                        pltpu.VMEM((HALO + tt, d_pool), F32),
                        pltpu.VMEM((tt, d_conv), F32)],
        compiler_params=_cparams("parallel", "parallel"),
        name="ab_mid_prompt",
    )(z, z, conv_w, conv_b.reshape(1, -1), ln_g.reshape(1, -1), ln_b.reshape(1, -1),
      pool_w.astype(BF16), pool_scale.reshape(1, -1))


def _ab_mid_step_body(z_ref, sc_ref, sp_ref, cw_ref, cb_ref, lg_ref, lb_ref, pw_ref, ps_ref,
                      y_ref, nc_ref, np_ref, ext_ref, vext_ref, *, nb, t, pos0, d_conv, d_pool):
    e0 = HALO - CONV_BUF
    p0 = 16 - POOL_BUF
    z = z_ref[...].reshape(nb, t, z_ref.shape[1])
    u = z[:, :, 0:d_conv] * jax.nn.sigmoid(z[:, :, d_conv:2 * d_conv])
    ext_ref[:, e0:HALO, :] = sc_ref[...]
    ext_ref[:, HALO:, :] = u
    vext_ref[:, p0:16, :] = sp_ref[...]
    vext_ref[:, 16:, :] = z[:, :, 2 * d_conv:]
    nc_ref[...] = ext_ref[:, HALO + t - CONV_BUF:, :]
    np_ref[...] = vext_ref[:, 16 + t - POOL_BUF:, :]

    acc = jnp.zeros((nb, t, d_conv), F32)
    for j in range(CONV_WIDTH):
        acc = acc + cw_ref[j:j + 1, :][None] * ext_ref[:, e0 + j:e0 + j + t, :]
    c = acc + cb_ref[...][None]
    yc = _layernorm_silu(c, lg_ref[...][None], lb_ref[...][None])
    y_ref[:, 0:d_conv] = yc.reshape(nb * t, d_conv).astype(y_ref.dtype)

    pg = d_pool // len(POOL_WINDOWS)
    for gi, w in enumerate(POOL_WINDOWS):
        gs = slice(gi * pg, (gi + 1) * pg)
        tok = vext_ref[:, 16:, gs]
        acc = tok
        for i in range(1, w):
            acc = acc + vext_ref[:, 16 - i:16 - i + t, gs]
        cnt = jnp.minimum(pos0 + 1 + lax.broadcasted_iota(jnp.int32, (1, t, 1), 1), w).astype(F32)
        d = (acc / cnt - tok).reshape(nb * t, pg)
        yp = jnp.dot(d.astype(BF16), pw_ref[gi], preferred_element_type=F32) * ps_ref[:, gs]
        y_ref[:, d_conv + gi * pg:d_conv + (gi + 1) * pg] = yp.astype(y_ref.dtype)


def ab_mid_step(z, row0, n_seq, t, pos0, state_conv, state_pool, conv_w, conv_b, ln_g, ln_b, pool_w,
                pool_scale, *, nb=16):
    d_conv = conv_w.shape[1]
    d_pool = pool_scale.shape[0]
    rb = nb * t
    assert row0 % rb == 0 and n_seq % nb == 0
    const = lambda i: (0, 0)
    seq3 = lambda i: (i, 0, 0)
    return pl.pallas_call(
        functools.partial(_ab_mid_step_body, nb=nb, t=t, pos0=pos0, d_conv=d_conv, d_pool=d_pool),
        grid=(n_seq // nb,),
        in_specs=[pl.BlockSpec((rb, z.shape[1]), lambda i: (row0 // rb + i, 0)),
                  pl.BlockSpec((nb, CONV_BUF, d_conv), seq3),
                  pl.BlockSpec((nb, POOL_BUF, d_pool), seq3),
                  pl.BlockSpec(conv_w.shape, const),
                  pl.BlockSpec((1, d_conv), const),
                  pl.BlockSpec((1, d_conv), const),
                  pl.BlockSpec((1, d_conv), const),
                  pl.BlockSpec(pool_w.shape, lambda i: (0, 0, 0)),
                  pl.BlockSpec((1, d_pool), const)],
        out_specs=[pl.BlockSpec((rb, d_conv + d_pool), lambda i: (i, 0)),
                   pl.BlockSpec((nb, CONV_BUF, d_conv), seq3),
                   pl.BlockSpec((nb, POOL_BUF, d_pool), seq3)],
        out_shape=[jax.ShapeDtypeStruct((n_seq * t, d_conv + d_pool), BF16),
                   jax.ShapeDtypeStruct((n_seq, CONV_BUF, d_conv), F32),
                   jax.ShapeDtypeStruct((n_seq, POOL_BUF, d_pool), F32)],
        scratch_shapes=[pltpu.VMEM((nb, HALO + t, d_conv), F32),
                        pltpu.VMEM((nb, 16 + t, d_pool), F32)],
        compiler_params=_cparams("parallel"),
        name="ab_mid_step",
    )(z, state_conv, state_pool, conv_w, conv_b.reshape(1, -1), ln_g.reshape(1, -1),
      ln_b.reshape(1, -1), pool_w.astype(BF16), pool_scale.reshape(1, -1))


N_BUCKETS = 32
MAX_DISTANCE = 128
NSA_BLOCK = 64
NSA_TOPN = 16
NSA_WINDOW = 512
DSA_TOPK = 256
IDX_HEADS = 8
IDX_DIM = 64
KV_GROUPS = 2
GROUP_HEADS = 4
PAGE = 128
BAND = 2 * PAGE
INT_MIN = -2 ** 31

COL_QN, COL_QD, COL_KVC, COL_KVS, COL_KVW, COL_KVD, COL_QI, COL_MISC = 0, 1024, 2048, 2560, 3072, 3584, 4096, 4608
MISC_KI, MISC_GATES, MISC_WI = 0, 64, 88
NZ = 5120


def _bucket_np(n):
    n = np.maximum(np.asarray(n, np.int32), 0)
    exact = N_BUCKETS // 2
    nf = np.maximum(n, 1).astype(np.float32)
    big = exact + (np.log(nf / np.float32(exact)) / np.float32(math.log(MAX_DISTANCE / exact))
                   * np.float32(N_BUCKETS - exact)).astype(np.int32)
    return np.where(n < exact, n, np.minimum(big, N_BUCKETS - 1))


_BUCKETS = _bucket_np(np.arange(BAND))
assert _BUCKETS[PAGE:].min() == N_BUCKETS - 1 and np.all(np.diff(_BUCKETS) >= 0)
_BUCKET_START = [int(np.argmax(_BUCKETS >= k)) for k in range(N_BUCKETS)]


def _softmax_rows(s, mask):
    s = jnp.where(mask, s, NEG_INF)
    m = jnp.max(s, -1, keepdims=True)
    p = jnp.where(mask, jnp.exp(s - m), 0.0)
    return p, jnp.sum(p, -1, keepdims=True)


def _dot_nt(a, b):
    return lax.dot_general(a, b, (((1,), (1,)), ((), ())), preferred_element_type=F32)


def _key_chunks(page_refs, new_ref, t_new):
    chunks = [r[0] for r in page_refs]
    for c in range(t_new // PAGE):
        chunks.append(new_ref[c * PAGE:(c + 1) * PAGE, :])
    rem = t_new % PAGE
    if rem:
        tail = new_ref[(t_new // PAGE) * PAGE:, :]
        chunks.append(jnp.concatenate([tail, jnp.zeros((PAGE - rem, tail.shape[1]), F32)], 0))
    return chunks


def _cmp_body(pt_ref, q_ref, kvn_ref, *rest, n_pages, t_new, tq, pos0):
    del pt_ref
    page_refs = rest[:n_pages]
    wexp_ref, rel_ref, o_ref, msel_ref, comp_ref, ck_ref, cv_ref = rest[n_pages:]
    qi = pl.program_id(1)
    n_keys = n_pages * PAGE + t_new
    n_cmp = n_keys // NSA_BLOCK
    n_sel = -(-n_keys // NSA_BLOCK)
    per = PAGE // NSA_BLOCK

    @pl.when(qi == 0)
    def _():
        comp_ref[...] = jnp.zeros_like(comp_ref)
        chunks = _key_chunks(page_refs, kvn_ref, t_new)[:n_cmp // per]
        for c, x in enumerate(chunks):
            x = x * wexp_ref[...]
            comp_ref[per * c:per * (c + 1), :] = x.reshape(per, NSA_BLOCK, x.shape[1]).sum(1)
        for g in range(KV_GROUPS):
            ck_ref[g] = comp_ref[:, g * HEAD_DIM:(g + 1) * HEAD_DIM].astype(BF16)
            cv_ref[g] = comp_ref[:, (KV_GROUPS + g) * HEAD_DIM:(KV_GROUPS + g + 1) * HEAD_DIM].astype(BF16)

    scale = HEAD_DIM ** -0.5
    qpos = pos0 + qi * tq + lax.broadcasted_iota(jnp.int32, (tq, 1), 0)
    blk = lax.broadcasted_iota(jnp.int32, (1, LANES), 1)
    dist = qpos - ((blk + 1) * NSA_BLOCK - 1)
    mask = (dist >= 0) & (blk < n_cmp)
    below = [dist < _BUCKET_START[k] for k in range(N_BUCKETS)]
    cur = qpos // NSA_BLOCK
    for g in range(KV_GROUPS):
        imp = jnp.zeros((tq, LANES), F32)
        for r in range(GROUP_HEADS):
            h = g * GROUP_HEADS + r
            bias = jnp.full((tq, LANES), rel_ref[(N_BUCKETS - 1) * 16 + h], F32)
            for k in range(N_BUCKETS - 2, -1, -1):
                bias = jnp.where(below[k + 1], rel_ref[k * 16 + h], bias)
            q = q_ref[:, h * HEAD_DIM:(h + 1) * HEAD_DIM].astype(BF16)
            s = _dot_nt(q, ck_ref[g]) * scale + bias
            p, l = _softmax_rows(s, mask)
            p = p / jnp.maximum(l, 1e-30)
            o_ref[:, h * HEAD_DIM:(h + 1) * HEAD_DIM] = jnp.dot(p.astype(BF16), cv_ref[g],
                                                               preferred_element_type=F32)
            imp = imp + p
        imp = jnp.where(blk == cur, 2.0, jnp.where(blk > cur, -1.0, imp))
        imp = jnp.where(blk < n_sel, imp, -2.0)
        rank = jnp.zeros((tq, LANES), F32)
        for i in range(n_sel):
            col = imp[:, i:i + 1]
            ahead = (col > imp) | ((col == imp) & (blk > i))
            rank = rank + jnp.where(ahead, 1.0, 0.0)
        chosen = (rank < float(min(NSA_TOPN, n_sel))) & (blk < n_sel)
        msel_ref[:, g * LANES:(g + 1) * LANES] = jnp.where(chosen, 1.0, 0.0)


def _attn_body(pt_ref, q_ref, kvn_ref, *rest, mode, n_pages, t_new, tq, pos0):
    del pt_ref
    page_refs = rest[:n_pages]
    rest = rest[n_pages:]
    if mode == "sel":
        msel_ref, rest = rest[0], rest[1:]
    elif mode == "dsa":
        qidx_ref, miscq_ref, misck_ref = rest[:3]
        ipage_refs = rest[3:3 + n_pages]
        rest = rest[3 + n_pages:]
    c31_ref, band_ref, o_ref, kc_ref, vc_ref, s_ref = rest[:6]
    qi = pl.program_id(1)
    n_chunks = n_pages + -(-t_new // PAGE)
    lk = n_chunks * PAGE
    n_keys = n_pages * PAGE + t_new
    kbase = pos0 - n_pages * PAGE

    @pl.when(qi == 0)
    def _():
        for c, x in enumerate(_key_chunks(page_refs, kvn_ref, t_new)):
            rows = slice(c * PAGE, (c + 1) * PAGE)
            for g in range(KV_GROUPS):
                kc_ref[g, rows, :] = x[:, g * HEAD_DIM:(g + 1) * HEAD_DIM].astype(BF16)
                vc_ref[g, rows, :] = x[:, (KV_GROUPS + g) * HEAD_DIM:(KV_GROUPS + g + 1) * HEAD_DIM].astype(BF16)
        if mode == "dsa":
            kidx_ref = rest[6]
            for c, x in enumerate(_key_chunks(ipage_refs, misck_ref, t_new)):
                kidx_ref[c * PAGE:(c + 1) * PAGE, :] = x[:, MISC_KI:MISC_KI + IDX_DIM].astype(BF16)

    scale = HEAD_DIM ** -0.5
    q0 = pos0 + qi * tq
    qpos = q0 + lax.broadcasted_iota(jnp.int32, (tq, 1), 0)
    col = lax.broadcasted_iota(jnp.int32, (1, lk), 1)
    dist = qpos - (kbase + col)
    visible = (dist >= 0) & (col < n_keys)
    if mode == "win":
        visible = visible & (dist < NSA_WINDOW)

    if mode == "dsa":
        kidx_ref = rest[6]
        score = jnp.zeros((tq, lk), F32)
        for hh in range(IDX_HEADS):
            qh = qidx_ref[:, hh * IDX_DIM:(hh + 1) * IDX_DIM].astype(BF16)
            sc = jnp.maximum(_dot_nt(qh, kidx_ref[...]) * (IDX_DIM ** -0.5), 0.0)
            wi = miscq_ref[:, MISC_WI + hh:MISC_WI + hh + 1] * (IDX_HEADS ** -0.5)
            score = score + sc * wi
        score = jnp.where(visible, score, NEG_INF) + 0.0
        bits = lax.bitcast_convert_type(score, jnp.int32)
        key = jnp.where(bits >= 0, bits, bits ^ 0x7FFFFFFF)
        key = jnp.where(col < n_keys, key, INT_MIN)
        s_ref[...] = lax.bitcast_convert_type(key, F32)
        kf = float(DSA_TOPK)

        def thr_step(i, tu):
            cand = tu | jnp.left_shift(jnp.int32(1), 31 - i)
            kk = lax.bitcast_convert_type(s_ref[...], jnp.int32)
            cnt = jnp.sum(jnp.where(kk >= (cand ^ INT_MIN), 1.0, 0.0), -1, keepdims=True)
            return jnp.where(cnt >= kf, cand, tu)

        thr = lax.fori_loop(0, 32, thr_step, jnp.zeros((tq, 1), jnp.int32)) ^ INT_MIN
        key = lax.bitcast_convert_type(s_ref[...], jnp.int32)
        above = key > thr
        tied = key == thr
        need = kf - jnp.sum(jnp.where(above, 1.0, 0.0), -1, keepdims=True)

        def tie_step(i, j0):
            cand = j0 | jnp.left_shift(jnp.int32(1), 11 - i)
            cnt = jnp.sum(jnp.where(tied & (col < cand), 1.0, 0.0), -1, keepdims=True)
            return jnp.where(cnt < need, cand, j0)

        assert lk <= 4096
        j0 = lax.fori_loop(0, 12, tie_step, jnp.zeros((tq, 1), jnp.int32))
        visible = visible & (above | (tied & (col <= j0)))

    for g in range(KV_GROUPS):
        mask = visible
        if mode == "sel":
            expand = (lax.broadcasted_iota(jnp.int32, (LANES, lk), 1) // NSA_BLOCK
                      == lax.broadcasted_iota(jnp.int32, (LANES, lk), 0))
            chosen = jnp.dot(msel_ref[:, g * LANES:(g + 1) * LANES].astype(BF16),
                             jnp.where(expand, 1.0, 0.0).astype(BF16), preferred_element_type=F32)
            mask = visible & (chosen > 0.5)
        for r in range(GROUP_HEADS):
            h = g * GROUP_HEADS + r
            q = q_ref[:, h * HEAD_DIM:(h + 1) * HEAD_DIM].astype(BF16)
            s_ref[...] = _dot_nt(q, kc_ref[g]) * scale + c31_ref[h]
            if t_new == tq:
                w0 = pos0 - PAGE - kbase
                s_ref[:, w0:w0 + BAND] += band_ref[h]
            else:
                w0 = q0 - PAGE - kbase
                @pl.when(qi == 0)
                def _():
                    s_ref[:, 0:PAGE] += band_ref[h, :, PAGE:]

                @pl.when(qi > 0)
                def _():
                    ws = pl.multiple_of(w0, PAGE)
                    s_ref[:, pl.ds(ws, BAND)] += band_ref[h]
            p, l = _softmax_rows(s_ref[...], mask)
            o = jnp.dot(p.astype(BF16), vc_ref[g], preferred_element_type=F32)
            o_ref[:, h * HEAD_DIM:(h + 1) * HEAD_DIM] = o / jnp.maximum(l, 1e-30)


class _Group:
    def __init__(self, row0, n_seq, t_new, tq, pos0, n_pages):
        assert t_new % tq == 0 and row0 % tq == 0 and row0 % t_new == 0
        assert pos0 == n_pages * PAGE or n_pages * PAGE < pos0
        assert t_new == tq or (tq == PAGE and pos0 == 0)
        assert t_new % PAGE == 0 or t_new % PAGE < NSA_BLOCK
        self.row0, self.n_seq, self.t_new, self.tq, self.pos0, self.n_pages = row0, n_seq, t_new, tq, pos0, n_pages
        self.nq = t_new // tq
        self.rows = n_seq * t_new
        self.lk = (n_pages + -(-t_new // PAGE)) * PAGE

    def q_spec(self, width, col):
        return pl.BlockSpec((self.tq, width), lambda b, qi, pt: (self.row0 // self.tq + b * self.nq + qi, col // width))

    def seq_spec(self, width, col):
        return pl.BlockSpec((self.t_new, width), lambda b, qi, pt: (self.row0 // self.t_new + b, col // width))

    def page_specs(self, width):
        return [pl.BlockSpec((1, PAGE, width), lambda b, qi, pt, p=p: (pt[b, p], 0, 0)) for p in range(self.n_pages)]

    def out_spec(self, width):
        return pl.BlockSpec((self.tq, width), lambda b, qi, pt: (b * self.nq + qi, 0))

    def statics(self):
        return dict(n_pages=self.n_pages, t_new=self.t_new, tq=self.tq, pos0=self.pos0)


def nsa_compress(grp, z, page_table, pool, wexp, rel_flat):
    kv_w = KV_GROUPS * 2 * HEAD_DIM
    qw = KV_GROUPS * GROUP_HEADS * HEAD_DIM
    const2 = lambda b, qi, pt: (0, 0)
    return pl.pallas_call(
        functools.partial(_cmp_body, **grp.statics()),
        grid_spec=pltpu.PrefetchScalarGridSpec(
            num_scalar_prefetch=1,
            grid=(grp.n_seq, grp.nq),
            in_specs=[grp.q_spec(qw, COL_QN), grp.seq_spec(kv_w, COL_KVC)] + grp.page_specs(kv_w)
            + [pl.BlockSpec((PAGE, kv_w), const2), pl.BlockSpec(memory_space=pltpu.SMEM)],
            out_specs=[grp.out_spec(qw), grp.out_spec(KV_GROUPS * LANES)],
            scratch_shapes=[pltpu.VMEM((LANES, kv_w), F32),
                            pltpu.VMEM((KV_GROUPS, LANES, HEAD_DIM), BF16),
                            pltpu.VMEM((KV_GROUPS, LANES, HEAD_DIM), BF16)]),
        out_shape=[jax.ShapeDtypeStruct((grp.rows, qw), F32),
                   jax.ShapeDtypeStruct((grp.rows, KV_GROUPS * LANES), F32)],
        compiler_params=_cparams("parallel", "arbitrary"),
        name="nsa_compress",
    )(page_table, z, z, *([pool] * grp.n_pages), wexp, rel_flat)


def sparse_attention(mode, grp, z, page_table, pool, c31, band, *, q_col, kv_col, msel=None, idx_pool=None):
    kv_w = KV_GROUPS * 2 * HEAD_DIM
    qw = KV_GROUPS * GROUP_HEADS * HEAD_DIM
    in_specs = [grp.q_spec(qw, q_col), grp.seq_spec(kv_w, kv_col)] + grp.page_specs(kv_w)
    args = [z, z] + [pool] * grp.n_pages
    scratch = [pltpu.VMEM((KV_GROUPS, grp.lk, HEAD_DIM), BF16),
               pltpu.VMEM((KV_GROUPS, grp.lk, HEAD_DIM), BF16),
               pltpu.VMEM((grp.tq, grp.lk), F32)]
    if mode == "sel":
        in_specs.append(grp.out_spec(KV_GROUPS * LANES))
        args.append(msel)
    elif mode == "dsa":
        in_specs += [grp.q_spec(IDX_HEADS * IDX_DIM, COL_QI), grp.q_spec(LANES, COL_MISC),
                     grp.seq_spec(LANES, COL_MISC)] + grp.page_specs(IDX_DIM)
        args += [z, z, z] + [idx_pool] * grp.n_pages
        scratch.append(pltpu.VMEM((grp.lk, IDX_DIM), BF16))
    in_specs += [pl.BlockSpec(memory_space=pltpu.SMEM),
                 pl.BlockSpec((KV_GROUPS * GROUP_HEADS, grp.tq, BAND), lambda b, qi, pt: (0, 0, 0))]
    args += [c31, band]
    return pl.pallas_call(
        functools.partial(_attn_body, mode=mode, **grp.statics()),
        grid_spec=pltpu.PrefetchScalarGridSpec(
            num_scalar_prefetch=1,
            grid=(grp.n_seq, grp.nq),
            in_specs=in_specs,
            out_specs=grp.out_spec(qw),
            scratch_shapes=scratch),
        out_shape=jax.ShapeDtypeStruct((grp.rows, qw), F32),
        compiler_params=_cparams("parallel", "arbitrary"),
        name="sparse_attention_" + mode,
    )(page_table, *args)


def _combine_body(oc_ref, os_ref, ow_ref, od_ref, misc_ref, y_ref):
    n_heads = KV_GROUPS * GROUP_HEADS
    gates = jax.nn.sigmoid(misc_ref[:, MISC_GATES:MISC_GATES + 3 * n_heads])
    for h in range(n_heads):
        hs = slice(h * HEAD_DIM, (h + 1) * HEAD_DIM)
        o = (gates[:, 3 * h:3 * h + 1] * oc_ref[:, hs] + gates[:, 3 * h + 1:3 * h + 2] * os_ref[:, hs]
             + gates[:, 3 * h + 2:3 * h + 3] * ow_ref[:, hs])
        y_ref[:, hs] = o.astype(y_ref.dtype)
    y_ref[:, n_heads * HEAD_DIM:] = od_ref[...].astype(y_ref.dtype)


def nsa_dsa_combine(o_c, o_s, o_w, o_d, z, row0, *, tm):
    m, w = o_c.shape
    assert m % tm == 0 and row0 % tm == 0
    blk = pl.BlockSpec((tm, w), lambda i: (i, 0))
    return pl.pallas_call(
        _combine_body,
        grid=(m // tm,),
        in_specs=[blk, blk, blk, blk, pl.BlockSpec((tm, LANES), lambda i: (row0 // tm + i, COL_MISC // LANES))],
        out_specs=pl.BlockSpec((tm, 2 * w), lambda i: (i, 0)),
        out_shape=jax.ShapeDtypeStruct((m, 2 * w), BF16),
        compiler_params=_cparams("parallel"),
        name="nsa_dsa_combine",
    )(o_c, o_s, o_w, o_d, z)


def _band_tiles(rel_bias, tq):
    d = np.arange(tq)[:, None] + PAGE - np.arange(BAND)[None, :]
    tab = rel_bias[_BUCKETS[np.clip(d, 0, BAND - 1)]]
    far = rel_bias[N_BUCKETS - 1]
    return jnp.transpose(jnp.where((d >= 0)[:, :, None], tab - far, 0.0), (2, 0, 1))


def _widen_cd_w_in(w):
    sizes = (1024, 512, 512, 512, 24, 1024, 512, 512, 64, 8)
    q_n, kv_c, kv_s, kv_w, gates, q_d, kv_d, q_i, k_i, w_i = jnp.split(w, np.cumsum(sizes)[:-1].tolist(), axis=1)
    cols = [q_n, q_d, kv_c, kv_s, kv_w, kv_d, q_i, k_i, gates, w_i]
    used = sum(c.shape[1] for c in cols)
    return jnp.concatenate(cols + [jnp.zeros((w.shape[0], NZ - used), w.dtype)], axis=1)


def kernel(x_prompt, x_sample, state_conv, state_pool, cache_nsa_cmp, cache_nsa_sel, cache_nsa_win, cache_dsa_kv, cache_dsa_idx, page_table, norm_mix, norm_ffn, norm_final, ab_w_in, ab_conv_w, ab_conv_b, ab_ln_g, ab_ln_b, ab_pool_w, ab_pool_scale, ab_w_out, cd_w_in, cd_w_cmp, cd_w_out, rel_bias, ffn_w1, ffn_w2):
    bp, tp, d_model = x_prompt.shape
    bs, ts, _ = x_sample.shape
    mp, ms = bp * tp, bs * ts
    depth = norm_mix.shape[0]
    n_pages = page_table.shape[1]
    n_pool = cache_nsa_cmp.shape[1]
    past_len = n_pages * PAGE
    assert cache_nsa_cmp.shape[2] == PAGE
    win_len = cache_nsa_win.shape[2]
    assert win_len % PAGE == 0 and win_len == NSA_WINDOW and tp >= NSA_WINDOW
    kv_w = KV_GROUPS * 2 * HEAD_DIM

    x = jnp.concatenate([x_prompt.reshape(mp, d_model), x_sample.reshape(ms, d_model)], 0)
    grp_p = _Group(0, bp, tp, PAGE, 0, 0)
    grp_s = _Group(mp, bs, ts, ts, past_len, n_pages)
    grp_sw = _Group(mp, bs, ts, ts, past_len, win_len // PAGE)
    no_pages = jnp.zeros((1, 1), jnp.int32)
    win_pages = jnp.arange(bs * (win_len // PAGE), dtype=jnp.int32).reshape(bs, win_len // PAGE)

    outs = {k: [] for k in ("conv_p", "conv_s", "pool_p", "pool_s", "cmp_p", "cmp_s", "sel_p", "sel_s",
                            "win_p", "win_s", "dsa_p", "dsa_s", "idx_p", "idx_s")}
    y_final = None
    for i in range(depth):
        j = i // 2
        if i % 2 == 0:
            d_conv = ab_conv_w.shape[2]
            z = norm_matmul(x, norm_mix[i], ab_w_in[j].astype(BF16))
            mid_p, u_p = ab_mid_prompt(z, bp, tp, ab_conv_w[j], ab_conv_b[j], ab_ln_g[j], ab_ln_b[j],
                                       ab_pool_w[j], ab_pool_scale[j])
            mid_s, conv_s, pool_s = ab_mid_step(z, mp, bs, ts, past_len, state_conv[j], state_pool[j], ab_conv_w[j],
                                                ab_conv_b[j], ab_ln_g[j], ab_ln_b[j], ab_pool_w[j], ab_pool_scale[j])
            x = matmul_residual(jnp.concatenate([mid_p, mid_s], 0), ab_w_out[j].astype(BF16), x)
            outs["conv_p"].append(u_p.reshape(bp, tp, d_conv)[:, tp - CONV_BUF:])
            outs["conv_s"].append(conv_s)
            outs["pool_p"].append(z[:mp, 2 * d_conv:].reshape(bp, tp, -1)[:, tp - POOL_BUF:])
            outs["pool_s"].append(pool_s)
        else:
            z = norm_matmul(x, norm_mix[i], _widen_cd_w_in(cd_w_in[j]).astype(BF16))
            rel_flat = rel_bias.reshape(-1)
            n_heads = KV_GROUPS * GROUP_HEADS
            far = rel_bias[N_BUCKETS - 1]
            band_p, band_s = _band_tiles(rel_bias, grp_p.tq), _band_tiles(rel_bias, grp_s.tq)
            wexp = jnp.tile(jnp.repeat(jnp.transpose(cd_w_cmp[j], (1, 0, 2)).reshape(NSA_BLOCK, 2 * KV_GROUPS),
                                       HEAD_DIM, axis=1), (PAGE // NSA_BLOCK, 1))
            pt = page_table + j * n_pool
            pools = [c.reshape(-1, PAGE, kv_w) for c in (cache_nsa_cmp, cache_nsa_sel, cache_dsa_kv)]
            idx_pool = cache_dsa_idx.reshape(-1, PAGE, IDX_DIM)
            win_pool = cache_nsa_win.reshape(-1, PAGE, kv_w)
            wpt = win_pages + j * bs * (win_len // PAGE)
            mids = []
            for grp, gw, ptab, wtab, band in ((grp_p, grp_p, no_pages, no_pages, band_p),
                                              (grp_s, grp_sw, pt, wpt, band_s)):
                o_c, msel = nsa_compress(grp, z, ptab, pools[0], wexp, rel_flat)
                o_s = sparse_attention("sel", grp, z, ptab, pools[1], far[:n_heads], band[:n_heads],
                                       q_col=COL_QN, kv_col=COL_KVS, msel=msel)
                o_w = sparse_attention("win", gw, z, wtab, win_pool, far[:n_heads], band[:n_heads],
                                       q_col=COL_QN, kv_col=COL_KVW)
                o_d = sparse_attention("dsa", grp, z, ptab, pools[2], far[n_heads:], band[n_heads:],
                                       q_col=COL_QD, kv_col=COL_KVD, idx_pool=idx_pool)
                mids.append(nsa_dsa_combine(o_c, o_s, o_w, o_d, z, grp.row0, tm=min(512, grp.rows)))
            x = matmul_residual(jnp.concatenate(mids, 0), cd_w_out[j].astype(BF16), x)

            def kv_out(col, width, tail):
                seg = z[:, col:col + width]
                return seg[:mp].reshape((bp, tp) + tail), seg[mp:].reshape((bs, ts) + tail)

            kv_tail = (2, KV_GROUPS, HEAD_DIM)
            for name, col in (("cmp", COL_KVC), ("sel", COL_KVS), ("dsa", COL_KVD)):
                p_new, s_new = kv_out(col, kv_w, kv_tail)
                outs[name + "_p"].append(p_new)
                outs[name + "_s"].append(s_new)
            w_p, w_s = kv_out(COL_KVW, kv_w, kv_tail)
            outs["win_p"].append(w_p[:, tp - NSA_WINDOW:])
            outs["win_s"].append(jnp.concatenate([cache_nsa_win[j], w_s], 1)[:, ts:])
            i_p, i_s = kv_out(COL_MISC + MISC_KI, IDX_DIM, (IDX_DIM,))
            outs["idx_p"].append(i_p)
            outs["idx_s"].append(i_s)
        a = norm_matmul(x, norm_ffn[i], ffn_w1[i].astype(BF16), relu2=True, out_dtype=BF16)
        if i == depth - 1:
            x, y_final = matmul_residual(a, ffn_w2[i].astype(BF16), x, norm_final)
        else:
            x = matmul_residual(a, ffn_w2[i].astype(BF16), x)

    st = {k: jnp.stack(v) for k, v in outs.items()}
    return (y_final[:mp].reshape(bp, tp, d_model), y_final[mp:].reshape(bs, ts, d_model),
            st["conv_p"], st["conv_s"], st["pool_p"], st["pool_s"], st["cmp_p"], st["cmp_s"],
            st["sel_p"], st["sel_s"], st["win_p"], st["win_s"], st["dsa_p"], st["dsa_s"],
            st["idx_p"], st["idx_s"])
```

```python
import functools
import math

import numpy as np
import jax
import jax.numpy as jnp
from jax import lax
from jax.experimental import pallas as pl
from jax.experimental.pallas import tpu as pltpu

F32 = jnp.float32
BF16 = jnp.bfloat16

EPS = 1e-6
NEG_INF = -1e30
HEAD_DIM = 128
LANES = 128
CONV_WIDTH = 31
CONV_BUF = CONV_WIDTH - 1
POOL_WINDOWS = (2, 4, 8, 16)
POOL_BUF = max(POOL_WINDOWS) - 1
HALO = 32
VMEM_LIMIT = 56 * 1024 * 1024


def _cparams(*sem):
    return pltpu.CompilerParams(dimension_semantics=sem, vmem_limit_bytes=VMEM_LIMIT)


def _norm_matmul_body(x_ref, g_ref, w_ref, o_ref, h_ref, *, relu2):
    @pl.when(pl.program_id(1) == 0)
    def _():
        x = x_ref[...]
        y = x * lax.rsqrt(jnp.mean(x * x, -1, keepdims=True) + EPS)
        h_ref[...] = (y * g_ref[...]).astype(BF16)

    y = jnp.dot(h_ref[...], w_ref[...], preferred_element_type=F32)
    if relu2:
        y = jnp.square(jnp.maximum(y, 0.0))
    o_ref[...] = y.astype(o_ref.dtype)


def norm_matmul(x, g, w, *, relu2=False, out_dtype=F32, tm=1024, tn=512):
    m, d = x.shape
    n = w.shape[1]
    assert m % tm == 0 and n % tn == 0
    return pl.pallas_call(
        functools.partial(_norm_matmul_body, relu2=relu2),
        grid=(m // tm, n // tn),
        in_specs=[pl.BlockSpec((tm, d), lambda i, j: (i, 0)),
                  pl.BlockSpec((1, d), lambda i, j: (0, 0)),
                  pl.BlockSpec((d, tn), lambda i, j: (0, j))],
        out_specs=pl.BlockSpec((tm, tn), lambda i, j: (i, j)),
        out_shape=jax.ShapeDtypeStruct((m, n), out_dtype),
        scratch_shapes=[pltpu.VMEM((tm, d), BF16)],
        compiler_params=_cparams("parallel", "arbitrary"),
        name="norm_matmul",
    )(x, g.reshape(1, d), w)


def _matmul_residual_body(a_ref, w_ref, r_ref, *rest, final_norm):
    if final_norm:
        g_ref, o_ref, n_ref, acc_ref = rest
    else:
        o_ref, acc_ref = rest
    k = pl.program_id(1)

    @pl.when(k == 0)
    def _():
        acc_ref[...] = jnp.zeros_like(acc_ref)

    acc_ref[...] += jnp.dot(a_ref[...], w_ref[...], preferred_element_type=F32)

    @pl.when(k == pl.num_programs(1) - 1)
    def _():
        o = r_ref[...] + acc_ref[...]
        o_ref[...] = o
        if final_norm:
            y = o * lax.rsqrt(jnp.mean(o * o, -1, keepdims=True) + EPS)
            n_ref[...] = y * g_ref[...]


def matmul_residual(a, w, r, g_final=None, *, tm=512, tk=1024):
    m, kdim = a.shape
    n = w.shape[1]
    assert m % tm == 0 and kdim % tk == 0
    final_norm = g_final is not None
    in_specs = [pl.BlockSpec((tm, tk), lambda i, k: (i, k)),
                pl.BlockSpec((tk, n), lambda i, k: (k, 0)),
                pl.BlockSpec((tm, n), lambda i, k: (i, 0))]
    out_spec = pl.BlockSpec((tm, n), lambda i, k: (i, 0))
    out_shape = jax.ShapeDtypeStruct((m, n), F32)
    args = [a, w, r]
    if final_norm:
        in_specs.append(pl.BlockSpec((1, n), lambda i, k: (0, 0)))
        args.append(g_final.reshape(1, n))
        out_spec, out_shape = [out_spec, out_spec], [out_shape, out_shape]
    return pl.pallas_call(
        functools.partial(_matmul_residual_body, final_norm=final_norm),
        grid=(m // tm, kdim // tk),
        in_specs=in_specs,
        out_specs=out_spec,
        out_shape=out_shape,
        scratch_shapes=[pltpu.VMEM((tm, n), F32)],
        compiler_params=_cparams("parallel", "arbitrary"),
        name="matmul_residual",
    )(*args)


def _layernorm_silu(c, g, b):
    mu = jnp.mean(c, -1, keepdims=True)
    xc = c - mu
    y = xc * lax.rsqrt(jnp.mean(xc * xc, -1, keepdims=True) + EPS)
    y = y * g + b
    return y * jax.nn.sigmoid(y)


def _ab_mid_body(z_ref, zp_ref, cw_ref, cb_ref, lg_ref, lb_ref, pw_ref, ps_ref, y_ref, u_ref,
                 ext_ref, vext_ref, conv_ref, *, tt, d_conv, d_pool):
    ti = pl.program_id(1)
    keep = (ti > 0).astype(F32)
    a_p = zp_ref[:, 0:d_conv]
    g_p = zp_ref[:, d_conv:2 * d_conv]
    ext_ref[0:HALO, :] = a_p * jax.nn.sigmoid(g_p) * keep
    vext_ref[0:HALO, :] = zp_ref[:, 2 * d_conv:] * keep
    u = z_ref[:, 0:d_conv] * jax.nn.sigmoid(z_ref[:, d_conv:2 * d_conv])
    ext_ref[HALO:, :] = u
    u_ref[...] = u
    vext_ref[HALO:, :] = z_ref[:, 2 * d_conv:]

    off = HALO - CONV_BUF
    for c in range(d_conv // LANES):
        cs = slice(c * LANES, (c + 1) * LANES)
        acc = jnp.zeros((tt, LANES), F32)
        for j in range(CONV_WIDTH):
            acc = acc + cw_ref[j:j + 1, cs] * ext_ref[off + j:off + j + tt, cs]
        conv_ref[:, cs] = acc + cb_ref[:, cs]
    y_ref[:, 0:d_conv] = _layernorm_silu(conv_ref[...], lg_ref[...], lb_ref[...]).astype(y_ref.dtype)

    pos = ti * tt + lax.broadcasted_iota(jnp.int32, (tt, 1), 0)
    pg = d_pool // len(POOL_WINDOWS)
    for gi, w in enumerate(POOL_WINDOWS):
        gs = slice(gi * pg, (gi + 1) * pg)
        tok = vext_ref[HALO:, gs]
        acc = tok
        for i in range(1, w):
            acc = acc + vext_ref[HALO - i:HALO - i + tt, gs]
        cnt = jnp.minimum(pos + 1, w).astype(F32)
        d = acc / cnt - tok
        yp = jnp.dot(d.astype(BF16), pw_ref[gi], preferred_element_type=F32) * ps_ref[:, gs]
        y_ref[:, d_conv + gi * pg:d_conv + (gi + 1) * pg] = yp.astype(y_ref.dtype)


def ab_mid_prompt(z, n_seq, t_len, conv_w, conv_b, ln_g, ln_b, pool_w, pool_scale, *, tt=256):
    d_conv = conv_w.shape[1]
    d_pool = pool_scale.shape[0]
    nt = t_len // tt
    hb = tt // HALO
    row = lambda b, t: (b * nt + t, 0)
    const = lambda b, t: (0, 0)
    return pl.pallas_call(
        functools.partial(_ab_mid_body, tt=tt, d_conv=d_conv, d_pool=d_pool),
        grid=(n_seq, nt),
        in_specs=[pl.BlockSpec((tt, z.shape[1]), row),
                  pl.BlockSpec((HALO, z.shape[1]), lambda b, t: (jnp.maximum((b * nt + t) * hb - 1, 0), 0)),
                  pl.BlockSpec(conv_w.shape, const),
                  pl.BlockSpec((1, d_conv), const),
                  pl.BlockSpec((1, d_conv), const),
                  pl.BlockSpec((1, d_conv), const),
                  pl.BlockSpec(pool_w.shape, lambda b, t: (0, 0, 0)),
                  pl.BlockSpec((1, d_pool), const)],
        out_specs=[pl.BlockSpec((tt, d_conv + d_pool), row),
                   pl.BlockSpec((tt, d_conv), row)],
        out_shape=[jax.ShapeDtypeStruct((n_seq * t_len, d_conv + d_pool), BF16),
                   jax.ShapeDtypeStruct((n_seq * t_len, d_conv), F32)],
        scratch_shapes=[pltpu.VMEM((HALO + tt, d_conv), F32),
                        pltpu.VMEM((HALO + tt, d_pool), F32),
                        pltpu.VMEM((tt, d_conv), F32)],
        compiler_params=_cparams("parallel", "parallel"),
        name="ab_mid_prompt",
    )(z, z, conv_w, conv_b.reshape(1, -1), ln_g.reshape(1, -1), ln_b.reshape(1, -1),
      pool_w.astype(BF16), pool_scale.reshape(1, -1))


def _ab_mid_step_body(z_ref, sc_ref, sp_ref, cw_ref, cb_ref, lg_ref, lb_ref, pw_ref, ps_ref,
                      y_ref, nc_ref, np_ref, ext_ref, vext_ref, *, nb, t, pos0, d_conv, d_pool):
    e0 = HALO - CONV_BUF
    p0 = 16 - POOL_BUF
    z = z_ref[...].reshape(nb, t, z_ref.shape[1])
    u = z[:, :, 0:d_conv] * jax.nn.sigmoid(z[:, :, d_conv:2 * d_conv])
    ext_ref[:, e0:HALO, :] = sc_ref[...]
    ext_ref[:, HALO:, :] = u
    vext_ref[:, p0:16, :] = sp_ref[...]
    vext_ref[:, 16:, :] = z[:, :, 2 * d_conv:]
    nc_ref[...] = ext_ref[:, HALO + t - CONV_BUF:, :]
    np_ref[...] = vext_ref[:, 16 + t - POOL_BUF:, :]

    acc = jnp.zeros((nb, t, d_conv), F32)
    for j in range(CONV_WIDTH):
        acc = acc + cw_ref[j:j + 1, :][None] * ext_ref[:, e0 + j:e0 + j + t, :]
    c = acc + cb_ref[...][None]
    yc = _layernorm_silu(c, lg_ref[...][None], lb_ref[...][None])
    y_ref[:, 0:d_conv] = yc.reshape(nb * t, d_conv).astype(y_ref.dtype)

    pg = d_pool // len(POOL_WINDOWS)
    for gi, w in enumerate(POOL_WINDOWS):
        gs = slice(gi * pg, (gi + 1) * pg)
        tok = vext_ref[:, 16:, gs]
        acc = tok
        for i in range(1, w):
            acc = acc + vext_ref[:, 16 - i:16 - i + t, gs]
        cnt = jnp.minimum(pos0 + 1 + lax.broadcasted_iota(jnp.int32, (1, t, 1), 1), w).astype(F32)
        d = (acc / cnt - tok).reshape(nb * t, pg)
        yp = jnp.dot(d.astype(BF16), pw_ref[gi], preferred_element_type=F32) * ps_ref[:, gs]
        y_ref[:, d_conv + gi * pg:d_conv + (gi + 1) * pg] = yp.astype(y_ref.dtype)


def ab_mid_step(z, row0, n_seq, t, pos0, state_conv, state_pool, conv_w, conv_b, ln_g, ln_b, pool_w,
                pool_scale, *, nb=16):
    d_conv = conv_w.shape[1]
    d_pool = pool_scale.shape[0]
    rb = nb * t
    assert row0 % rb == 0 and n_seq % nb == 0
    const = lambda i: (0, 0)
    seq3 = lambda i: (i, 0, 0)
    return pl.pallas_call(
        functools.partial(_ab_mid_step_body, nb=nb, t=t, pos0=pos0, d_conv=d_conv, d_pool=d_pool),
        grid=(n_seq // nb,),
        in_specs=[pl.BlockSpec((rb, z.shape[1]), lambda i: (row0 // rb + i, 0)),
                  pl.BlockSpec((nb, CONV_BUF, d_conv), seq3),
                  pl.BlockSpec((nb, POOL_BUF, d_pool), seq3),
                  pl.BlockSpec(conv_w.shape, const),
                  pl.BlockSpec((1, d_conv), const),
                  pl.BlockSpec((1, d_conv), const),
                  pl.BlockSpec((1, d_conv), const),
                  pl.BlockSpec(pool_w.shape, lambda i: (0, 0, 0)),
                  pl.BlockSpec((1, d_pool), const)],
        out_specs=[pl.BlockSpec((rb, d_conv + d_pool), lambda i: (i, 0)),
                   pl.BlockSpec((nb, CONV_BUF, d_conv), seq3),
                   pl.BlockSpec((nb, POOL_BUF, d_pool), seq3)],
        out_shape=[jax.ShapeDtypeStruct((n_seq * t, d_conv + d_pool), BF16),
                   jax.ShapeDtypeStruct((n_seq, CONV_BUF, d_conv), F32),
                   jax.ShapeDtypeStruct((n_seq, POOL_BUF, d_pool), F32)],
        scratch_shapes=[pltpu.VMEM((nb, HALO + t, d_conv), F32),
                        pltpu.VMEM((nb, 16 + t, d_pool), F32)],
        compiler_params=_cparams("parallel"),
        name="ab_mid_step",
    )(z, state_conv, state_pool, conv_w, conv_b.reshape(1, -1), ln_g.reshape(1, -1),
      ln_b.reshape(1, -1), pool_w.astype(BF16), pool_scale.reshape(1, -1))


N_BUCKETS = 32
MAX_DISTANCE = 128
NSA_BLOCK = 64
NSA_TOPN = 16
NSA_WINDOW = 512
DSA_TOPK = 256
IDX_HEADS = 8
IDX_DIM = 64
KV_GROUPS = 2
GROUP_HEADS = 4
PAGE = 128
BAND = 2 * PAGE
INT_MIN = -2 ** 31
KV_PAGE = (2 * KV_GROUPS * PAGE, HEAD_DIM)

COL_QN, COL_QD, COL_KVC, COL_KVS, COL_KVW, COL_KVD, COL_QI, COL_MISC = 0, 1024, 2048, 2560, 3072, 3584, 4096, 4608
MISC_KI, MISC_GATES, MISC_WI = 0, 64, 88
NZ = 5120


def _bucket_np(n):
    n = np.maximum(np.asarray(n, np.int32), 0)
    exact = N_BUCKETS // 2
    nf = np.maximum(n, 1).astype(np.float32)
    big = exact + (np.log(nf / np.float32(exact)) / np.float32(math.log(MAX_DISTANCE / exact))
                   * np.float32(N_BUCKETS - exact)).astype(np.int32)
    return np.where(n < exact, n, np.minimum(big, N_BUCKETS - 1))


_BUCKETS = _bucket_np(np.arange(BAND))
assert _BUCKETS[PAGE:].min() == N_BUCKETS - 1 and np.all(np.diff(_BUCKETS) >= 0)
_BUCKET_START = [int(np.argmax(_BUCKETS >= k)) for k in range(N_BUCKETS)]


def _softmax_rows(s, mask):
    s = jnp.where(mask, s, NEG_INF)
    m = jnp.max(s, -1, keepdims=True)
    p = jnp.where(mask, jnp.exp(s - m), 0.0)
    return p, jnp.sum(p, -1, keepdims=True)


def _dot_nt(a, b):
    return lax.dot_general(a, b, (((1,), (1,)), ((), ())), preferred_element_type=F32)


def _new_chunks(new_ref, t_new):
    chunks = [new_ref[c * PAGE:(c + 1) * PAGE, :] for c in range(t_new // PAGE)]
    rem = t_new % PAGE
    if rem:
        tail = new_ref[(t_new // PAGE) * PAGE:, :]
        chunks.append(jnp.concatenate([tail, jnp.zeros((PAGE - rem, tail.shape[1]), F32)], 0))
    return chunks


def _kv_chunks(page_refs, new_ref, t_new):
    n_parts = 2 * KV_GROUPS
    chunks = [[r[0, pl.ds(part, PAGE, stride=n_parts), :] for part in range(n_parts)] for r in page_refs]
    for x in _new_chunks(new_ref, t_new):
        chunks.append([x[:, part * HEAD_DIM:(part + 1) * HEAD_DIM] for part in range(n_parts)])
    return chunks


def _cmp_body(pt_ref, q_ref, kvn_ref, *rest, n_pages, t_new, tq, pos0):
    del pt_ref
    page_refs = rest[:n_pages]
    wexp_ref, rel_ref, o_ref, msel_ref, comp_ref, ck_ref, cv_ref = rest[n_pages:]
    qi = pl.program_id(1)
    n_keys = n_pages * PAGE + t_new
    n_cmp = n_keys // NSA_BLOCK
    n_sel = -(-n_keys // NSA_BLOCK)
    per = PAGE // NSA_BLOCK

    @pl.when(qi == 0)
    def _():
        comp_ref[...] = jnp.zeros_like(comp_ref)
        chunks = _kv_chunks(page_refs, kvn_ref, t_new)[:n_cmp // per]
        for c, parts in enumerate(chunks):
            for part, x in enumerate(parts):
                cols = slice(part * HEAD_DIM, (part + 1) * HEAD_DIM)
                x = x * wexp_ref[:, cols]
                comp_ref[per * c:per * (c + 1), cols] = x.reshape(per, NSA_BLOCK, HEAD_DIM).sum(1)
        for g in range(KV_GROUPS):
            ck_ref[g] = comp_ref[:, g * HEAD_DIM:(g + 1) * HEAD_DIM].astype(BF16)
            cv_ref[g] = comp_ref[:, (KV_GROUPS + g) * HEAD_DIM:(KV_GROUPS + g + 1) * HEAD_DIM].astype(BF16)

    scale = HEAD_DIM ** -0.5
    qpos = pos0 + qi * tq + lax.broadcasted_iota(jnp.int32, (tq, 1), 0)
    blk = lax.broadcasted_iota(jnp.int32, (1, LANES), 1)
    dist = qpos - ((blk + 1) * NSA_BLOCK - 1)
    mask = (dist >= 0) & (blk < n_cmp)
    below = [dist < _BUCKET_START[k] for k in range(N_BUCKETS)]
    cur = qpos // NSA_BLOCK
    for g in range(KV_GROUPS):
        imp = jnp.zeros((tq, LANES), F32)
        for r in range(GROUP_HEADS):
            h = g * GROUP_HEADS + r
            bias = jnp.full((tq, LANES), rel_ref[(N_BUCKETS - 1) * 16 + h], F32)
            for k in range(N_BUCKETS - 2, -1, -1):
                bias = jnp.where(below[k + 1], rel_ref[k * 16 + h], bias)
            q = q_ref[:, h * HEAD_DIM:(h + 1) * HEAD_DIM].astype(BF16)
            s = _dot_nt(q, ck_ref[g]) * scale + bias
            p, l = _softmax_rows(s, mask)
            p = p / jnp.maximum(l, 1e-30)
            o_ref[:, h * HEAD_DIM:(h + 1) * HEAD_DIM] = jnp.dot(p.astype(BF16), cv_ref[g],
                                                               preferred_element_type=F32)
            imp = imp + p
        imp = jnp.where(blk == cur, 2.0, jnp.where(blk > cur, -1.0, imp))
        imp = jnp.where(blk < n_sel, imp, -2.0)
        rank = jnp.zeros((tq, LANES), F32)
        for i in range(n_sel):
            col = imp[:, i:i + 1]
            ahead = (col > imp) | ((col == imp) & (blk > i))
            rank = rank + jnp.where(ahead, 1.0, 0.0)
        chosen = (rank < float(min(NSA_TOPN, n_sel))) & (blk < n_sel)
        msel_ref[:, g * LANES:(g + 1) * LANES] = jnp.where(chosen, 1.0, 0.0)


def _attn_body(pt_ref, q_ref, kvn_ref, *rest, mode, n_pages, t_new, tq, pos0):
    del pt_ref
    page_refs = rest[:n_pages]
    rest = rest[n_pages:]
    if mode == "sel":
        msel_ref, rest = rest[0], rest[1:]
    elif mode == "dsa":
        qidx_ref, miscq_ref, misck_ref = rest[:3]
        ipage_refs = rest[3:3 + n_pages]
        rest = rest[3 + n_pages:]
    c31_ref, band_ref, o_ref, kc_ref, vc_ref, s_ref = rest[:6]
    qi = pl.program_id(1)
    n_chunks = n_pages + -(-t_new // PAGE)
    lk = n_chunks * PAGE
    n_keys = n_pages * PAGE + t_new
    kbase = pos0 - n_pages * PAGE

    @pl.when(qi == 0)
    def _():
        for c, parts in enumerate(_kv_chunks(page_refs, kvn_ref, t_new)):
            rows = slice(c * PAGE, (c + 1) * PAGE)
            for g in range(KV_GROUPS):
                kc_ref[g, rows, :] = parts[g].astype(BF16)
                vc_ref[g, rows, :] = parts[KV_GROUPS + g].astype(BF16)
        if mode == "dsa":
            kidx_ref = rest[6]
            for c, r in enumerate(ipage_refs):
                kidx_ref[:, c * PAGE:(c + 1) * PAGE] = r[0].astype(BF16)
            for c, x in enumerate(_new_chunks(misck_ref, t_new)):
                cols = slice((n_pages + c) * PAGE, (n_pages + c + 1) * PAGE)
                kidx_ref[:, cols] = x.T[MISC_KI:MISC_KI + IDX_DIM, :].astype(BF16)

    scale = HEAD_DIM ** -0.5
    q0 = pos0 + qi * tq
    qpos = q0 + lax.broadcasted_iota(jnp.int32, (tq, 1), 0)
    col = lax.broadcasted_iota(jnp.int32, (1, lk), 1)
    dist = qpos - (kbase + col)
    visible = (dist >= 0) & (col < n_keys)
    if mode == "win":
        visible = visible & (dist < NSA_WINDOW)

    if mode == "dsa":
        kidx_ref = rest[6]
        score = jnp.zeros((tq, lk), F32)
        for hh in range(IDX_HEADS):
            qh = qidx_ref[:, hh * IDX_DIM:(hh + 1) * IDX_DIM].astype(BF16)
            sc = jnp.dot(qh, kidx_ref[...], preferred_element_type=F32)
            sc = jnp.maximum(sc * (IDX_DIM ** -0.5), 0.0)
            wi = miscq_ref[:, MISC_WI + hh:MISC_WI + hh + 1] * (IDX_HEADS ** -0.5)
            score = score + sc * wi
        score = jnp.where(visible, score, NEG_INF) + 0.0
        bits = lax.bitcast_convert_type(score, jnp.int32)
        key = jnp.where(bits >= 0, bits, bits ^ 0x7FFFFFFF)
        key = jnp.where(col < n_keys, key, INT_MIN)
        s_ref[...] = lax.bitcast_convert_type(key, F32)
        kf = float(DSA_TOPK)

        def thr_step(i, tu):
            cand = tu | jnp.left_shift(jnp.int32(1), 31 - i)
            kk = lax.bitcast_convert_type(s_ref[...], jnp.int32)
            cnt = jnp.sum(jnp.where(kk >= (cand ^ INT_MIN), 1.0, 0.0), -1, keepdims=True)
            return jnp.where(cnt >= kf, cand, tu)

        thr = lax.fori_loop(0, 32, thr_step, jnp.zeros((tq, 1), jnp.int32)) ^ INT_MIN
        key = lax.bitcast_convert_type(s_ref[...], jnp.int32)
        above = key > thr
        tied = key == thr
        need = kf - jnp.sum(jnp.where(above, 1.0, 0.0), -1, keepdims=True)

        def tie_step(i, j0):
            cand = j0 | jnp.left_shift(jnp.int32(1), 11 - i)
            cnt = jnp.sum(jnp.where(tied & (col < cand), 1.0, 0.0), -1, keepdims=True)
            return jnp.where(cnt < need, cand, j0)

        assert lk <= 4096
        j0 = lax.fori_loop(0, 12, tie_step, jnp.zeros((tq, 1), jnp.int32))
        visible = visible & (above | (tied & (col <= j0)))

    for g in range(KV_GROUPS):
        mask = visible
        if mode == "sel":
            expand = (lax.broadcasted_iota(jnp.int32, (LANES, lk), 1) // NSA_BLOCK
                      == lax.broadcasted_iota(jnp.int32, (LANES, lk), 0))
            chosen = jnp.dot(msel_ref[:, g * LANES:(g + 1) * LANES].astype(BF16),
                             jnp.where(expand, 1.0, 0.0).astype(BF16), preferred_element_type=F32)
            mask = visible & (chosen > 0.5)
        for r in range(GROUP_HEADS):
            h = g * GROUP_HEADS + r
            q = q_ref[:, h * HEAD_DIM:(h + 1) * HEAD_DIM].astype(BF16)
            s_ref[...] = _dot_nt(q, kc_ref[g]) * scale + c31_ref[h]
            if t_new == tq:
                w0 = pos0 - PAGE - kbase
                s_ref[:, w0:w0 + BAND] += band_ref[h]
            else:
                w0 = q0 - PAGE - kbase
                @pl.when(qi == 0)
                def _():
                    s_ref[:, 0:PAGE] += band_ref[h, :, PAGE:]

                @pl.when(qi > 0)
                def _():
                    ws = pl.multiple_of(w0, PAGE)
                    s_ref[:, pl.ds(ws, BAND)] += band_ref[h]
            p, l = _softmax_rows(s_ref[...], mask)
            o = jnp.dot(p.astype(BF16), vc_ref[g], preferred_element_type=F32)
            o_ref[:, h * HEAD_DIM:(h + 1) * HEAD_DIM] = o / jnp.maximum(l, 1e-30)


class _Group:
    def __init__(self, row0, n_seq, t_new, tq, pos0, n_pages):
        assert t_new % tq == 0 and row0 % tq == 0 and row0 % t_new == 0
        assert pos0 == n_pages * PAGE or n_pages * PAGE < pos0
        assert t_new == tq or (tq == PAGE and pos0 == 0)
        assert t_new % PAGE == 0 or t_new % PAGE < NSA_BLOCK
        self.row0, self.n_seq, self.t_new, self.tq, self.pos0, self.n_pages = row0, n_seq, t_new, tq, pos0, n_pages
        self.nq = t_new // tq
        self.rows = n_seq * t_new
        self.lk = (n_pages + -(-t_new // PAGE)) * PAGE

    def q_spec(self, width, col):
        return pl.BlockSpec((self.tq, width), lambda b, qi, pt: (self.row0 // self.tq + b * self.nq + qi, col // width))

    def seq_spec(self, width, col):
        return pl.BlockSpec((self.t_new, width), lambda b, qi, pt: (self.row0 // self.t_new + b, col // width))

    def page_specs(self, shape):
        return [pl.BlockSpec((1,) + shape, lambda b, qi, pt, p=p: (pt[b, p], 0, 0)) for p in range(self.n_pages)]

    def out_spec(self, width):
        return pl.BlockSpec((self.tq, width), lambda b, qi, pt: (b * self.nq + qi, 0))

    def statics(self):
        return dict(n_pages=self.n_pages, t_new=self.t_new, tq=self.tq, pos0=self.pos0)


def nsa_compress(grp, z, page_table, pool, wexp, rel_flat):
    kv_w = KV_GROUPS * 2 * HEAD_DIM
    qw = KV_GROUPS * GROUP_HEADS * HEAD_DIM
    const2 = lambda b, qi, pt: (0, 0)
    return pl.pallas_call(
        functools.partial(_cmp_body, **grp.statics()),
        grid_spec=pltpu.PrefetchScalarGridSpec(
            num_scalar_prefetch=1,
            grid=(grp.n_seq, grp.nq),
            in_specs=[grp.q_spec(qw, COL_QN), grp.seq_spec(kv_w, COL_KVC)] + grp.page_specs(KV_PAGE)
            + [pl.BlockSpec((PAGE, kv_w), const2), pl.BlockSpec(memory_space=pltpu.SMEM)],
            out_specs=[grp.out_spec(qw), grp.out_spec(KV_GROUPS * LANES)],
            scratch_shapes=[pltpu.VMEM((LANES, kv_w), F32),
                            pltpu.VMEM((KV_GROUPS, LANES, HEAD_DIM), BF16),
                            pltpu.VMEM((KV_GROUPS, LANES, HEAD_DIM), BF16)]),
        out_shape=[jax.ShapeDtypeStruct((grp.rows, qw), F32),
                   jax.ShapeDtypeStruct((grp.rows, KV_GROUPS * LANES), F32)],
        compiler_params=_cparams("parallel", "arbitrary"),
        name="nsa_compress",
    )(page_table, z, z, *([pool] * grp.n_pages), wexp, rel_flat)


def sparse_attention(mode, grp, z, page_table, pool, c31, band, *, q_col, kv_col, msel=None, idx_pool=None):
    kv_w = KV_GROUPS * 2 * HEAD_DIM
    qw = KV_GROUPS * GROUP_HEADS * HEAD_DIM
    in_specs = [grp.q_spec(qw, q_col), grp.seq_spec(kv_w, kv_col)] + grp.page_specs(KV_PAGE)
    args = [z, z] + [pool] * grp.n_pages
    scratch = [pltpu.VMEM((KV_GROUPS, grp.lk, HEAD_DIM), BF16),
               pltpu.VMEM((KV_GROUPS, grp.lk, HEAD_DIM), BF16),
               pltpu.VMEM((grp.tq, grp.lk), F32)]
    if mode == "sel":
        in_specs.append(grp.out_spec(KV_GROUPS * LANES))
        args.append(msel)
    elif mode == "dsa":
        in_specs += [grp.q_spec(IDX_HEADS * IDX_DIM, COL_QI), grp.q_spec(LANES, COL_MISC),
                     grp.seq_spec(LANES, COL_MISC)] + grp.page_specs((IDX_DIM, PAGE))
        args += [z, z, z] + [idx_pool] * grp.n_pages
        scratch.append(pltpu.VMEM((IDX_DIM, grp.lk), BF16))
    in_specs += [pl.BlockSpec(memory_space=pltpu.SMEM),
                 pl.BlockSpec((KV_GROUPS * GROUP_HEADS, grp.tq, BAND), lambda b, qi, pt: (0, 0, 0))]
    args += [c31, band]
    return pl.pallas_call(
        functools.partial(_attn_body, mode=mode, **grp.statics()),
        grid_spec=pltpu.PrefetchScalarGridSpec(
            num_scalar_prefetch=1,
            grid=(grp.n_seq, grp.nq),
            in_specs=in_specs,
            out_specs=grp.out_spec(qw),
            scratch_shapes=scratch),
        out_shape=jax.ShapeDtypeStruct((grp.rows, qw), F32),
        compiler_params=_cparams("parallel", "arbitrary"),
        name="sparse_attention_" + mode,
    )(page_table, *args)


def _combine_body(oc_ref, os_ref, ow_ref, od_ref, misc_ref, y_ref):
    n_heads = KV_GROUPS * GROUP_HEADS
    gates = jax.nn.sigmoid(misc_ref[:, MISC_GATES:MISC_GATES + 3 * n_heads])
    for h in range(n_heads):
        hs = slice(h * HEAD_DIM, (h + 1) * HEAD_DIM)
        o = (gates[:, 3 * h:3 * h + 1] * oc_ref[:, hs] + gates[:, 3 * h + 1:3 * h + 2] * os_ref[:, hs]
             + gates[:, 3 * h + 2:3 * h + 3] * ow_ref[:, hs])
        y_ref[:, hs] = o.astype(y_ref.dtype)
    y_ref[:, n_heads * HEAD_DIM:] = od_ref[...].astype(y_ref.dtype)


def nsa_dsa_combine(o_c, o_s, o_w, o_d, z, row0, *, tm):
    m, w = o_c.shape
    assert m % tm == 0 and row0 % tm == 0
    blk = pl.BlockSpec((tm, w), lambda i: (i, 0))
    return pl.pallas_call(
        _combine_body,
        grid=(m // tm,),
        in_specs=[blk, blk, blk, blk, pl.BlockSpec((tm, LANES), lambda i: (row0 // tm + i, COL_MISC // LANES))],
        out_specs=pl.BlockSpec((tm, 2 * w), lambda i: (i, 0)),
        out_shape=jax.ShapeDtypeStruct((m, 2 * w), BF16),
        compiler_params=_cparams("parallel"),
        name="nsa_dsa_combine",
    )(o_c, o_s, o_w, o_d, z)


def _band_tiles(rel_bias, tq):
    d = np.arange(tq)[:, None] + PAGE - np.arange(BAND)[None, :]
    tab = rel_bias[_BUCKETS[np.clip(d, 0, BAND - 1)]]
    far = rel_bias[N_BUCKETS - 1]
    return jnp.transpose(jnp.where((d >= 0)[:, :, None], tab - far, 0.0), (2, 0, 1))


def _widen_cd_w_in(w):
    sizes = (1024, 512, 512, 512, 24, 1024, 512, 512, 64, 8)
    q_n, kv_c, kv_s, kv_w, gates, q_d, kv_d, q_i, k_i, w_i = jnp.split(w, np.cumsum(sizes)[:-1].tolist(), axis=1)
    cols = [q_n, q_d, kv_c, kv_s, kv_w, kv_d, q_i, k_i, gates, w_i]
    used = sum(c.shape[1] for c in cols)
    return jnp.concatenate(cols + [jnp.zeros((w.shape[0], NZ - used), w.dtype)], axis=1)


def kernel(x_prompt, x_sample, state_conv, state_pool, cache_nsa_cmp, cache_nsa_sel, cache_nsa_win, cache_dsa_kv, cache_dsa_idx, page_table, norm_mix, norm_ffn, norm_final, ab_w_in, ab_conv_w, ab_conv_b, ab_ln_g, ab_ln_b, ab_pool_w, ab_pool_scale, ab_w_out, cd_w_in, cd_w_cmp, cd_w_out, rel_bias, ffn_w1, ffn_w2):
    bp, tp, d_model = x_prompt.shape
    bs, ts, _ = x_sample.shape
    mp, ms = bp * tp, bs * ts
    depth = norm_mix.shape[0]
    n_pages = page_table.shape[1]
    n_pool = cache_nsa_cmp.shape[1]
    past_len = n_pages * PAGE
    assert cache_nsa_cmp.shape[2] == PAGE
    win_len = cache_nsa_win.shape[2]
    assert win_len % PAGE == 0 and win_len == NSA_WINDOW and tp >= NSA_WINDOW
    kv_w = KV_GROUPS * 2 * HEAD_DIM

    x = jnp.concatenate([x_prompt.reshape(mp, d_model), x_sample.reshape(ms, d_model)], 0)
    grp_p = _Group(0, bp, tp, PAGE, 0, 0)
    grp_s = _Group(mp, bs, ts, ts, past_len, n_pages)
    grp_sw = _Group(mp, bs, ts, ts, past_len, win_len // PAGE)
    no_pages = jnp.zeros((1, 1), jnp.int32)
    win_pages = jnp.arange(bs * (win_len // PAGE), dtype=jnp.int32).reshape(bs, win_len // PAGE)

    outs = {k: [] for k in ("conv_p", "conv_s", "pool_p", "pool_s", "cmp_p", "cmp_s", "sel_p", "sel_s",
                            "win_p", "win_s", "dsa_p", "dsa_s", "idx_p", "idx_s")}
    y_final = None
    for i in range(depth):
        j = i // 2
        if i % 2 == 0:
            d_conv = ab_conv_w.shape[2]
            z = norm_matmul(x, norm_mix[i], ab_w_in[j].astype(BF16))
            mid_p, u_p = ab_mid_prompt(z, bp, tp, ab_conv_w[j], ab_conv_b[j], ab_ln_g[j], ab_ln_b[j],
                                       ab_pool_w[j], ab_pool_scale[j])
            mid_s, conv_s, pool_s = ab_mid_step(z, mp, bs, ts, past_len, state_conv[j], state_pool[j], ab_conv_w[j],
                                                ab_conv_b[j], ab_ln_g[j], ab_ln_b[j], ab_pool_w[j], ab_pool_scale[j])
            x = matmul_residual(jnp.concatenate([mid_p, mid_s], 0), ab_w_out[j].astype(BF16), x)
            outs["conv_p"].append(u_p.reshape(bp, tp, d_conv)[:, tp - CONV_BUF:])
            outs["conv_s"].append(conv_s)
            outs["pool_p"].append(z[:mp, 2 * d_conv:].reshape(bp, tp, -1)[:, tp - POOL_BUF:])
            outs["pool_s"].append(pool_s)
        else:
            z = norm_matmul(x, norm_mix[i], _widen_cd_w_in(cd_w_in[j]).astype(BF16))
            rel_flat = rel_bias.reshape(-1)
            n_heads = KV_GROUPS * GROUP_HEADS
            far = rel_bias[N_BUCKETS - 1]
            band_p, band_s = _band_tiles(rel_bias, grp_p.tq), _band_tiles(rel_bias, grp_s.tq)
            wexp = jnp.tile(jnp.repeat(jnp.transpose(cd_w_cmp[j], (1, 0, 2)).reshape(NSA_BLOCK, 2 * KV_GROUPS),
                                       HEAD_DIM, axis=1), (PAGE // NSA_BLOCK, 1))
            pt = page_table + j * n_pool
            pools = [c.reshape((-1,) + KV_PAGE) for c in (cache_nsa_cmp, cache_nsa_sel, cache_dsa_kv)]
            idx_pool = jnp.swapaxes(cache_dsa_idx, 2, 3).reshape(-1, IDX_DIM, PAGE)
            win_pool = cache_nsa_win.reshape((-1,) + KV_PAGE)
            wpt = win_pages + j * bs * (win_len // PAGE)
            mids = []
            for grp, gw, ptab, wtab, band in ((grp_p, grp_p, no_pages, no_pages, band_p),
                                              (grp_s, grp_sw, pt, wpt, band_s)):
                o_c, msel = nsa_compress(grp, z, ptab, pools[0], wexp, rel_flat)
                o_s = sparse_attention("sel", grp, z, ptab, pools[1], far[:n_heads], band[:n_heads],
                                       q_col=COL_QN, kv_col=COL_KVS, msel=msel)
                o_w = sparse_attention("win", gw, z, wtab, win_pool, far[:n_heads], band[:n_heads],
                                       q_col=COL_QN, kv_col=COL_KVW)
                o_d = sparse_attention("dsa", grp, z, ptab, pools[2], far[n_heads:], band[n_heads:],
                                       q_col=COL_QD, kv_col=COL_KVD, idx_pool=idx_pool)
                mids.append(nsa_dsa_combine(o_c, o_s, o_w, o_d, z, grp.row0, tm=min(512, grp.rows)))
            x = matmul_residual(jnp.concatenate(mids, 0), cd_w_out[j].astype(BF16), x)

            def kv_out(col, width, tail):
                seg = z[:, col:col + width]
                return seg[:mp].reshape((bp, tp) + tail), seg[mp:].reshape((bs, ts) + tail)

            kv_tail = (2, KV_GROUPS, HEAD_DIM)
            for name, col in (("cmp", COL_KVC), ("sel", COL_KVS), ("dsa", COL_KVD)):
                p_new, s_new = kv_out(col, kv_w, kv_tail)
                outs[name + "_p"].append(p_new)
                outs[name + "_s"].append(s_new)
            w_p, w_s = kv_out(COL_KVW, kv_w, kv_tail)
            outs["win_p"].append(w_p[:, tp - NSA_WINDOW:])
            outs["win_s"].append(jnp.concatenate([cache_nsa_win[j], w_s], 1)[:, ts:])
            i_p, i_s = kv_out(COL_MISC + MISC_KI, IDX_DIM, (IDX_DIM,))
            outs["idx_p"].append(i_p)
            outs["idx_s"].append(i_s)
        a = norm_matmul(x, norm_ffn[i], ffn_w1[i].astype(BF16), relu2=True, out_dtype=BF16)
        if i == depth - 1:
            x, y_final = matmul_residual(a, ffn_w2[i].astype(BF16), x, norm_final)
        else:
            x = matmul_residual(a, ffn_w2[i].astype(BF16), x)

    st = {k: jnp.stack(v) for k, v in outs.items()}
    return (y_final[:mp].reshape(bp, tp, d_model), y_final[mp:].reshape(bs, ts, d_model),
            st["conv_p"], st["conv_s"], st["pool_p"], st["pool_s"], st["cmp_p"], st["cmp_s"],
            st["sel_p"], st["sel_s"], st["win_p"], st["win_s"], st["dsa_p"], st["dsa_s"],
            st["idx_p"], st["idx_s"])
```

```python
import functools
import math

import numpy as np
import jax
import jax.numpy as jnp
from jax import lax
from jax.experimental import pallas as pl
from jax.experimental.pallas import tpu as pltpu

F32 = jnp.float32
BF16 = jnp.bfloat16

EPS = 1e-6
NEG_INF = -1e30
HEAD_DIM = 128
LANES = 128
CONV_WIDTH = 31
CONV_BUF = CONV_WIDTH - 1
POOL_WINDOWS = (2, 4, 8, 16)
POOL_BUF = max(POOL_WINDOWS) - 1
HALO = 32
VMEM_LIMIT = 56 * 1024 * 1024


def _cparams(*sem):
    return pltpu.CompilerParams(dimension_semantics=sem, vmem_limit_bytes=VMEM_LIMIT)


def _norm_matmul_body(x_ref, g_ref, w_ref, o_ref, h_ref, *, relu2):
    @pl.when(pl.program_id(1) == 0)
    def _():
        x = x_ref[...]
        y = x * lax.rsqrt(jnp.mean(x * x, -1, keepdims=True) + EPS)
        h_ref[...] = (y * g_ref[...]).astype(BF16)

    y = jnp.dot(h_ref[...], w_ref[...], preferred_element_type=F32)
    if relu2:
        y = jnp.square(jnp.maximum(y, 0.0))
    o_ref[...] = y.astype(o_ref.dtype)


def norm_matmul(x, g, w, *, relu2=False, out_dtype=F32, tm=1024, tn=512):
    m, d = x.shape
    n = w.shape[1]
    assert m % tm == 0 and n % tn == 0
    return pl.pallas_call(
        functools.partial(_norm_matmul_body, relu2=relu2),
        grid=(m // tm, n // tn),
        in_specs=[pl.BlockSpec((tm, d), lambda i, j: (i, 0)),
                  pl.BlockSpec((1, d), lambda i, j: (0, 0)),
                  pl.BlockSpec((d, tn), lambda i, j: (0, j))],
        out_specs=pl.BlockSpec((tm, tn), lambda i, j: (i, j)),
        out_shape=jax.ShapeDtypeStruct((m, n), out_dtype),
        scratch_shapes=[pltpu.VMEM((tm, d), BF16)],
        compiler_params=_cparams("parallel", "arbitrary"),
        name="norm_matmul",
    )(x, g.reshape(1, d), w)


def _matmul_residual_body(a_ref, w_ref, r_ref, *rest, final_norm):
    if final_norm:
        g_ref, o_ref, n_ref, acc_ref = rest
    else:
        o_ref, acc_ref = rest
    k = pl.program_id(1)

    @pl.when(k == 0)
    def _():
        acc_ref[...] = jnp.zeros_like(acc_ref)

    acc_ref[...] += jnp.dot(a_ref[...], w_ref[...], preferred_element_type=F32)

    @pl.when(k == pl.num_programs(1) - 1)
    def _():
        o = r_ref[...] + acc_ref[...]
        o_ref[...] = o
        if final_norm:
            y = o * lax.rsqrt(jnp.mean(o * o, -1, keepdims=True) + EPS)
            n_ref[...] = y * g_ref[...]


def matmul_residual(a, w, r, g_final=None, *, tm=512, tk=1024):
    m, kdim = a.shape
    n = w.shape[1]
    assert m % tm == 0 and kdim % tk == 0
    final_norm = g_final is not None
    in_specs = [pl.BlockSpec((tm, tk), lambda i, k: (i, k)),
                pl.BlockSpec((tk, n), lambda i, k: (k, 0)),
                pl.BlockSpec((tm, n), lambda i, k: (i, 0))]
    out_spec = pl.BlockSpec((tm, n), lambda i, k: (i, 0))
    out_shape = jax.ShapeDtypeStruct((m, n), F32)
    args = [a, w, r]
    if final_norm:
        in_specs.append(pl.BlockSpec((1, n), lambda i, k: (0, 0)))
        args.append(g_final.reshape(1, n))
        out_spec, out_shape = [out_spec, out_spec], [out_shape, out_shape]
    return pl.pallas_call(
        functools.partial(_matmul_residual_body, final_norm=final_norm),
        grid=(m // tm, kdim // tk),
        in_specs=in_specs,
        out_specs=out_spec,
        out_shape=out_shape,
        scratch_shapes=[pltpu.VMEM((tm, n), F32)],
        compiler_params=_cparams("parallel", "arbitrary"),
        name="matmul_residual",
    )(*args)


def _layernorm_silu(c, g, b):
    mu = jnp.mean(c, -1, keepdims=True)
    xc = c - mu
    y = xc * lax.rsqrt(jnp.mean(xc * xc, -1, keepdims=True) + EPS)
    y = y * g + b
    return y * jax.nn.sigmoid(y)


def _ab_mid_body(z_ref, zp_ref, cw_ref, cb_ref, lg_ref, lb_ref, pw_ref, ps_ref, y_ref, u_ref,
                 ext_ref, vext_ref, conv_ref, *, tt, d_conv, d_pool):
    ti = pl.program_id(1)
    keep = (ti > 0).astype(F32)
    a_p = zp_ref[:, 0:d_conv]
    g_p = zp_ref[:, d_conv:2 * d_conv]
    ext_ref[0:HALO, :] = a_p * jax.nn.sigmoid(g_p) * keep
    vext_ref[0:HALO, :] = zp_ref[:, 2 * d_conv:] * keep
    u = z_ref[:, 0:d_conv] * jax.nn.sigmoid(z_ref[:, d_conv:2 * d_conv])
    ext_ref[HALO:, :] = u
    u_ref[...] = u
    vext_ref[HALO:, :] = z_ref[:, 2 * d_conv:]

    off = HALO - CONV_BUF
    for c in range(d_conv // LANES):
        cs = slice(c * LANES, (c + 1) * LANES)
        acc = jnp.zeros((tt, LANES), F32)
        for j in range(CONV_WIDTH):
            acc = acc + cw_ref[j:j + 1, cs] * ext_ref[off + j:off + j + tt, cs]
        conv_ref[:, cs] = acc + cb_ref[:, cs]
    y_ref[:, 0:d_conv] = _layernorm_silu(conv_ref[...], lg_ref[...], lb_ref[...]).astype(y_ref.dtype)

    pos = ti * tt + lax.broadcasted_iota(jnp.int32, (tt, 1), 0)
    pg = d_pool // len(POOL_WINDOWS)
    for gi, w in enumerate(POOL_WINDOWS):
        gs = slice(gi * pg, (gi + 1) * pg)
        tok = vext_ref[HALO:, gs]
        acc = tok
        for i in range(1, w):
            acc = acc + vext_ref[HALO - i:HALO - i + tt, gs]
        cnt = jnp.minimum(pos + 1, w).astype(F32)
        d = acc / cnt - tok
        yp = jnp.dot(d.astype(BF16), pw_ref[gi], preferred_element_type=F32) * ps_ref[:, gs]
        y_ref[:, d_conv + gi * pg:d_conv + (gi + 1) * pg] = yp.astype(y_ref.dtype)


def ab_mid_prompt(z, n_seq, t_len, conv_w, conv_b, ln_g, ln_b, pool_w, pool_scale, *, tt=256):
    d_conv = conv_w.shape[1]
    d_pool = pool_scale.shape[0]
    nt = t_len // tt
    hb = tt // HALO
    row = lambda b, t: (b * nt + t, 0)
    const = lambda b, t: (0, 0)
    return pl.pallas_call(
        functools.partial(_ab_mid_body, tt=tt, d_conv=d_conv, d_pool=d_pool),
        grid=(n_seq, nt),
        in_specs=[pl.BlockSpec((tt, z.shape[1]), row),
                  pl.BlockSpec((HALO, z.shape[1]), lambda b, t: (jnp.maximum((b * nt + t) * hb - 1, 0), 0)),
                  pl.BlockSpec(conv_w.shape, const),
                  pl.BlockSpec((1, d_conv), const),
                  pl.BlockSpec((1, d_conv), const),
                  pl.BlockSpec((1, d_conv), const),
                  pl.BlockSpec(pool_w.shape, lambda b, t: (0, 0, 0)),
                  pl.BlockSpec((1, d_pool), const)],
        out_specs=[pl.BlockSpec((tt, d_conv + d_pool), row),
                   pl.BlockSpec((tt, d_conv), row)],
        out_shape=[jax.ShapeDtypeStruct((n_seq * t_len, d_conv + d_pool), BF16),
                   jax.ShapeDtypeStruct((n_seq * t_len, d_conv), F32)],
        scratch_shapes=[pltpu.VMEM((HALO + tt, d_conv), F32),
                        pltpu.VMEM((HALO + tt, d_pool), F32),
                        pltpu.VMEM((tt, d_conv), F32)],
        compiler_params=_cparams("parallel", "parallel"),
        name="ab_mid_prompt",
    )(z, z, conv_w, conv_b.reshape(1, -1), ln_g.reshape(1, -1), ln_b.reshape(1, -1),
      pool_w.astype(BF16), pool_scale.reshape(1, -1))


def _ab_mid_step_body(z_ref, sc_ref, sp_ref, cw_ref, cb_ref, lg_ref, lb_ref, pw_ref, ps_ref,
                      y_ref, nc_ref, np_ref, ext_ref, vext_ref, *, nb, t, pos0, d_conv, d_pool):
    e0 = HALO - CONV_BUF
    p0 = 16 - POOL_BUF
    z = z_ref[...].reshape(nb, t, z_ref.shape[1])
    u = z[:, :, 0:d_conv] * jax.nn.sigmoid(z[:, :, d_conv:2 * d_conv])
    ext_ref[:, e0:HALO, :] = sc_ref[...]
    ext_ref[:, HALO:, :] = u
    vext_ref[:, p0:16, :] = sp_ref[...]
    vext_ref[:, 16:, :] = z[:, :, 2 * d_conv:]
    nc_ref[...] = ext_ref[:, HALO + t - CONV_BUF:, :]
    np_ref[...] = vext_ref[:, 16 + t - POOL_BUF:, :]

    acc = jnp.zeros((nb, t, d_conv), F32)
    for j in range(CONV_WIDTH):
        acc = acc + cw_ref[j:j + 1, :][None] * ext_ref[:, e0 + j:e0 + j + t, :]
    c = acc + cb_ref[...][None]
    yc = _layernorm_silu(c, lg_ref[...][None], lb_ref[...][None])
    y_ref[:, 0:d_conv] = yc.reshape(nb * t, d_conv).astype(y_ref.dtype)

    pg = d_pool // len(POOL_WINDOWS)
    for gi, w in enumerate(POOL_WINDOWS):
        gs = slice(gi * pg, (gi + 1) * pg)
        tok = vext_ref[:, 16:, gs]
        acc = tok
        for i in range(1, w):
            acc = acc + vext_ref[:, 16 - i:16 - i + t, gs]
        cnt = jnp.minimum(pos0 + 1 + lax.broadcasted_iota(jnp.int32, (1, t, 1), 1), w).astype(F32)
        d = (acc / cnt - tok).reshape(nb * t, pg)
        yp = jnp.dot(d.astype(BF16), pw_ref[gi], preferred_element_type=F32) * ps_ref[:, gs]
        y_ref[:, d_conv + gi * pg:d_conv + (gi + 1) * pg] = yp.astype(y_ref.dtype)


def ab_mid_step(z, row0, n_seq, t, pos0, state_conv, state_pool, conv_w, conv_b, ln_g, ln_b, pool_w,
                pool_scale, *, nb=16):
    d_conv = conv_w.shape[1]
    d_pool = pool_scale.shape[0]
    rb = nb * t
    assert row0 % rb == 0 and n_seq % nb == 0
    const = lambda i: (0, 0)
    seq3 = lambda i: (i, 0, 0)
    return pl.pallas_call(
        functools.partial(_ab_mid_step_body, nb=nb, t=t, pos0=pos0, d_conv=d_conv, d_pool=d_pool),
        grid=(n_seq // nb,),
        in_specs=[pl.BlockSpec((rb, z.shape[1]), lambda i: (row0 // rb + i, 0)),
                  pl.BlockSpec((nb, CONV_BUF, d_conv), seq3),
                  pl.BlockSpec((nb, POOL_BUF, d_pool), seq3),
                  pl.BlockSpec(conv_w.shape, const),
                  pl.BlockSpec((1, d_conv), const),
                  pl.BlockSpec((1, d_conv), const),
                  pl.BlockSpec((1, d_conv), const),
                  pl.BlockSpec(pool_w.shape, lambda i: (0, 0, 0)),
                  pl.BlockSpec((1, d_pool), const)],
        out_specs=[pl.BlockSpec((rb, d_conv + d_pool), lambda i: (i, 0)),
                   pl.BlockSpec((nb, CONV_BUF, d_conv), seq3),
                   pl.BlockSpec((nb, POOL_BUF, d_pool), seq3)],
        out_shape=[jax.ShapeDtypeStruct((n_seq * t, d_conv + d_pool), BF16),
                   jax.ShapeDtypeStruct((n_seq, CONV_BUF, d_conv), F32),
                   jax.ShapeDtypeStruct((n_seq, POOL_BUF, d_pool), F32)],
        scratch_shapes=[pltpu.VMEM((nb, HALO + t, d_conv), F32),
                        pltpu.VMEM((nb, 16 + t, d_pool), F32)],
        compiler_params=_cparams("parallel"),
        name="ab_mid_step",
    )(z, state_conv, state_pool, conv_w, conv_b.reshape(1, -1), ln_g.reshape(1, -1),
      ln_b.reshape(1, -1), pool_w.astype(BF16), pool_scale.reshape(1, -1))


N_BUCKETS = 32
MAX_DISTANCE = 128
NSA_BLOCK = 64
NSA_TOPN = 16
NSA_WINDOW = 512
DSA_TOPK = 256
IDX_HEADS = 8
IDX_DIM = 64
KV_GROUPS = 2
GROUP_HEADS = 4
PAGE = 128
BAND = 2 * PAGE
INT_MIN = -2 ** 31
KV_PAGE = (2 * KV_GROUPS * PAGE, HEAD_DIM)

COL_QN, COL_QD, COL_KVC, COL_KVS, COL_KVW, COL_KVD, COL_QI, COL_MISC = 0, 1024, 2048, 2560, 3072, 3584, 4096, 4608
MISC_KI, MISC_GATES, MISC_WI = 0, 64, 88
NZ = 5120


def _bucket_np(n):
    n = np.maximum(np.asarray(n, np.int32), 0)
    exact = N_BUCKETS // 2
    nf = np.maximum(n, 1).astype(np.float32)
    big = exact + (np.log(nf / np.float32(exact)) / np.float32(math.log(MAX_DISTANCE / exact))
                   * np.float32(N_BUCKETS - exact)).astype(np.int32)
    return np.where(n < exact, n, np.minimum(big, N_BUCKETS - 1))


_BUCKETS = _bucket_np(np.arange(BAND))
assert _BUCKETS[PAGE:].min() == N_BUCKETS - 1 and np.all(np.diff(_BUCKETS) >= 0)
_BUCKET_START = [int(np.argmax(_BUCKETS >= k)) for k in range(N_BUCKETS)]


def _softmax_rows(s, mask):
    s = jnp.where(mask, s, NEG_INF)
    m = jnp.max(s, -1, keepdims=True)
    p = jnp.where(mask, jnp.exp(s - m), 0.0)
    return p, jnp.sum(p, -1, keepdims=True)


def _dot_nt(a, b):
    return lax.dot_general(a, b, (((1,), (1,)), ((), ())), preferred_element_type=F32)


def _new_chunks(new_ref, t_new):
    chunks = [new_ref[c * PAGE:(c + 1) * PAGE, :] for c in range(t_new // PAGE)]
    rem = t_new % PAGE
    if rem:
        tail = new_ref[(t_new // PAGE) * PAGE:, :]
        chunks.append(jnp.concatenate([tail, jnp.zeros((PAGE - rem, tail.shape[1]), F32)], 0))
    return chunks


def _kv_chunks(page_refs, new_ref, t_new):
    n_parts = 2 * KV_GROUPS
    chunks = [[r[0, pl.ds(part, PAGE, stride=n_parts), :] for part in range(n_parts)] for r in page_refs]
    for x in _new_chunks(new_ref, t_new):
        chunks.append([x[:, part * HEAD_DIM:(part + 1) * HEAD_DIM] for part in range(n_parts)])
    return chunks


def _cmp_body(pt_ref, q_ref, kvn_ref, *rest, n_pages, t_new, tq, pos0):
    del pt_ref
    page_refs = rest[:n_pages]
    wexp_ref, rel_ref, o_ref, msel_ref, comp_ref, ck_ref, cv_ref = rest[n_pages:]
    qi = pl.program_id(1)
    n_keys = n_pages * PAGE + t_new
    n_cmp = n_keys // NSA_BLOCK
    n_sel = -(-n_keys // NSA_BLOCK)
    per = PAGE // NSA_BLOCK

    @pl.when(qi == 0)
    def _():
        comp_ref[...] = jnp.zeros_like(comp_ref)
        chunks = _kv_chunks(page_refs, kvn_ref, t_new)[:n_cmp // per]
        for c, parts in enumerate(chunks):
            for part, x in enumerate(parts):
                cols = slice(part * HEAD_DIM, (part + 1) * HEAD_DIM)
                x = x * wexp_ref[:, cols]
                comp_ref[per * c:per * (c + 1), cols] = x.reshape(per, NSA_BLOCK, HEAD_DIM).sum(1)
        for g in range(KV_GROUPS):
            ck_ref[g] = comp_ref[:, g * HEAD_DIM:(g + 1) * HEAD_DIM].astype(BF16)
            cv_ref[g] = comp_ref[:, (KV_GROUPS + g) * HEAD_DIM:(KV_GROUPS + g + 1) * HEAD_DIM].astype(BF16)

    scale = HEAD_DIM ** -0.5
    qpos = pos0 + qi * tq + lax.broadcasted_iota(jnp.int32, (tq, 1), 0)
    blk = lax.broadcasted_iota(jnp.int32, (1, LANES), 1)
    dist = qpos - ((blk + 1) * NSA_BLOCK - 1)
    mask = (dist >= 0) & (blk < n_cmp)
    below = [dist < _BUCKET_START[k] for k in range(N_BUCKETS)]
    cur = qpos // NSA_BLOCK
    for g in range(KV_GROUPS):
        imp = jnp.zeros((tq, LANES), F32)
        for r in range(GROUP_HEADS):
            h = g * GROUP_HEADS + r
            bias = jnp.full((tq, LANES), rel_ref[(N_BUCKETS - 1) * 16 + h], F32)
            for k in range(N_BUCKETS - 2, -1, -1):
                bias = jnp.where(below[k + 1], rel_ref[k * 16 + h], bias)
            q = q_ref[:, h * HEAD_DIM:(h + 1) * HEAD_DIM].astype(BF16)
            s = _dot_nt(q, ck_ref[g]) * scale + bias
            p, l = _softmax_rows(s, mask)
            p = p / jnp.maximum(l, 1e-30)
            o_ref[:, h * HEAD_DIM:(h + 1) * HEAD_DIM] = jnp.dot(p.astype(BF16), cv_ref[g],
                                                               preferred_element_type=F32)
            imp = imp + p
        imp = jnp.where(blk == cur, 2.0, jnp.where(blk > cur, -1.0, imp))
        imp = jnp.where(blk < n_sel, imp, -2.0)
        rank = jnp.zeros((tq, LANES), F32)
        for i in range(n_sel):
            col = imp[:, i:i + 1]
            ahead = (col > imp) | ((col == imp) & (blk > i))
            rank = rank + jnp.where(ahead, 1.0, 0.0)
        chosen = (rank < float(min(NSA_TOPN, n_sel))) & (blk < n_sel)
        msel_ref[:, g * LANES:(g + 1) * LANES] = jnp.where(chosen, 1.0, 0.0)


def _on_causal_width(qi, tq, widths, tile):
    if len(widths) == 1:
        tile(widths[0], True)
        return
    need = (qi * tq + tq - 1) // widths[0]
    for nw, w in enumerate(widths):
        pl.when(need == nw)(functools.partial(tile, w, nw == 0))


def _attn_body(pt_ref, q_ref, kvn_ref, *rest, mode, n_pages, t_new, tq, pos0, widths):
    del pt_ref
    page_refs = rest[:n_pages]
    rest = rest[n_pages:]
    m_ref = None
    if mode in ("sel", "mask"):
        m_ref, rest = rest[0], rest[1:]
    band_ref, o_ref, kc_ref, vc_ref, s_ref = rest
    qi = pl.program_id(1)
    single = t_new == tq
    n_keys = n_pages * PAGE + t_new
    kbase = pos0 - n_pages * PAGE
    scale = HEAD_DIM ** -0.5
    q0 = pos0 if single else pos0 + qi * tq

    @pl.when(qi == 0)
    def _():
        for c, parts in enumerate(_kv_chunks(page_refs, kvn_ref, t_new)):
            rows = slice(c * PAGE, (c + 1) * PAGE)
            for g in range(KV_GROUPS):
                kc_ref[g, rows, :] = parts[g].astype(BF16)
                vc_ref[g, rows, :] = parts[KV_GROUPS + g].astype(BF16)

    def tile(c0, w, band_at, maybe_first):
        qpos = q0 + lax.broadcasted_iota(jnp.int32, (tq, 1), 0)
        col = c0 + lax.broadcasted_iota(jnp.int32, (1, w), 1)
        dist = qpos - (kbase + col)
        visible = (dist >= 0) & (col < n_keys)
        if mode == "win":
            visible = visible & (dist < NSA_WINDOW)
        if mode == "mask":
            visible = visible & (m_ref[:, 0:w] > 0.5)
        keys = pl.ds(c0, w)
        for g in range(KV_GROUPS):
            mask = visible
            if mode == "sel":
                expand = (lax.broadcasted_iota(jnp.int32, (LANES, w), 1) // NSA_BLOCK
                          == lax.broadcasted_iota(jnp.int32, (LANES, w), 0))
                chosen = jnp.dot(m_ref[:, g * LANES:(g + 1) * LANES].astype(BF16),
                                 jnp.where(expand, 1.0, 0.0).astype(BF16), preferred_element_type=F32)
                mask = visible & (chosen > 0.5)
            heads = [g * GROUP_HEADS + r for r in range(GROUP_HEADS)]
            q = jnp.concatenate([q_ref[:, h * HEAD_DIM:(h + 1) * HEAD_DIM] for h in heads], 0).astype(BF16)
            s_ref[:, 0:w] = _dot_nt(q, kc_ref[g, keys, :]) * scale
            if band_at is not None:
                s_ref[:, band_at:band_at + BAND] += band_ref[g]
            else:
                if maybe_first:
                    @pl.when(qi == 0)
                    def _():
                        s_ref[:, 0:PAGE] += band_ref[g, :, PAGE:]

                @pl.when(qi > 0)
                def _():
                    s_ref[:, pl.ds(pl.multiple_of(q0 - PAGE - kbase, PAGE), BAND)] += band_ref[g]
            sums = []
            for r in range(GROUP_HEADS):
                rows = slice(r * tq, (r + 1) * tq)
                p, l = _softmax_rows(s_ref[rows, 0:w], mask)
                s_ref[rows, 0:w] = p
                sums.append(l)
            o = jnp.dot(s_ref[:, 0:w].astype(BF16), vc_ref[g, keys, :], preferred_element_type=F32)
            for r, h in enumerate(heads):
                o_ref[:, h * HEAD_DIM:(h + 1) * HEAD_DIM] = o[r * tq:(r + 1) * tq] / jnp.maximum(sums[r], 1e-30)

    if single:
        tile(0, widths[0], pos0 - PAGE - kbase, False)
    elif mode == "win":
        wch = NSA_WINDOW // PAGE
        pl.when(qi < wch)(functools.partial(tile, 0, NSA_WINDOW, None, True))
        pl.when(qi >= wch)(lambda: tile(pl.multiple_of((qi - wch) * PAGE, PAGE), NSA_WINDOW + PAGE,
                                        NSA_WINDOW - PAGE, False))
    else:
        _on_causal_width(qi, tq, widths, lambda w, first: tile(0, w, None, first))


def _index_body(pt_ref, qidx_ref, miscq_ref, misck_ref, *rest, n_pages, t_new, tq, pos0, widths):
    del pt_ref
    ipage_refs = rest[:n_pages]
    o_ref, kidx_ref = rest[n_pages:]
    qi = pl.program_id(1)
    lk = o_ref.shape[1]
    n_keys = n_pages * PAGE + t_new
    kbase = pos0 - n_pages * PAGE
    q0 = pos0 if t_new == tq else pos0 + qi * tq

    @pl.when(qi == 0)
    def _():
        for c, r in enumerate(ipage_refs):
            kidx_ref[:, c * PAGE:(c + 1) * PAGE] = r[0].astype(BF16)
        for c, x in enumerate(_new_chunks(misck_ref, t_new)):
            cols = slice((n_pages + c) * PAGE, (n_pages + c + 1) * PAGE)
            kidx_ref[:, cols] = x.T[MISC_KI:MISC_KI + IDX_DIM, :].astype(BF16)

    def tile(w, maybe_first):
        del maybe_first
        qpos = q0 + lax.broadcasted_iota(jnp.int32, (tq, 1), 0)
        col = lax.broadcasted_iota(jnp.int32, (1, w), 1)
        visible = (qpos - (kbase + col) >= 0) & (col < n_keys)
        q = jnp.concatenate([qidx_ref[:, hh * IDX_DIM:(hh + 1) * IDX_DIM] for hh in range(IDX_HEADS)], 0)
        sc = jnp.dot(q.astype(BF16), kidx_ref[:, 0:w], preferred_element_type=F32)
        score = jnp.zeros((tq, w), F32)
        for hh in range(IDX_HEADS):
            wi = miscq_ref[:, MISC_WI + hh:MISC_WI + hh + 1] * (IDX_HEADS ** -0.5)
            score = score + jnp.maximum(sc[hh * tq:(hh + 1) * tq] * (IDX_DIM ** -0.5), 0.0) * wi
        o_ref[:, 0:w] = jnp.where(visible, score, NEG_INF)
        if w < lk:
            o_ref[:, w:] = jnp.full((tq, lk - w), NEG_INF, F32)

    _on_causal_width(qi, tq, widths, tile)


def _topk_body(s_ref, m_ref, key_ref, *, k, nq, tr, widths):
    lk = s_ref.shape[1]
    assert lk <= 4096
    neg_key = int(np.array(NEG_INF, np.float32).view(np.int32)) ^ 0x7FFFFFFF
    kf = float(k)

    def tile(w, maybe_first):
        del maybe_first
        bits = lax.bitcast_convert_type(s_ref[:, 0:w] + 0.0, jnp.int32)
        key_ref[:, 0:w] = jnp.where(bits >= 0, bits, bits ^ 0x7FFFFFFF)
        col = lax.broadcasted_iota(jnp.int32, (1, w), 1)
        unseen = float(lk - w)

        def thr_step(i, tu):
            cand = (tu | jnp.left_shift(jnp.int32(1), 31 - i))
            cs = cand ^ INT_MIN
            cnt = jnp.sum(jnp.where(key_ref[:, 0:w] >= cs, 1.0, 0.0), -1, keepdims=True)
            cnt = cnt + jnp.where(cs <= neg_key, unseen, 0.0)
            return jnp.where(cnt >= kf, cand, tu)

        thr = lax.fori_loop(0, 32, thr_step, jnp.zeros((tr, 1), jnp.int32)) ^ INT_MIN
        key = key_ref[:, 0:w]
        above = key > thr
        tied = key == thr
        need = kf - jnp.sum(jnp.where(above, 1.0, 0.0), -1, keepdims=True)

        def tie_step(i, j0):
            cand = j0 | jnp.left_shift(jnp.int32(1), 11 - i)
            cnt = jnp.sum(jnp.where((key_ref[:, 0:w] == thr) & (col < cand), 1.0, 0.0), -1, keepdims=True)
            return jnp.where(cnt < need, cand, j0)

        j0 = lax.fori_loop(0, 12, tie_step, jnp.zeros((tr, 1), jnp.int32))
        m_ref[:, 0:w] = jnp.where(above | (tied & (col <= j0)), 1.0, 0.0)
        if w < lk:
            m_ref[:, w:] = jnp.zeros((tr, lk - w), F32)

    _on_causal_width(pl.program_id(0) % nq, tr, widths, tile)


class _Group:
    def __init__(self, row0, n_seq, t_new, tq, pos0, n_pages):
        assert t_new % tq == 0 and row0 % tq == 0 and row0 % t_new == 0
        assert pos0 == n_pages * PAGE or n_pages * PAGE < pos0
        assert t_new == tq or (tq == PAGE and pos0 == 0)
        assert t_new % PAGE == 0 or t_new % PAGE < NSA_BLOCK
        self.row0, self.n_seq, self.t_new, self.tq, self.pos0, self.n_pages = row0, n_seq, t_new, tq, pos0, n_pages
        self.nq = t_new // tq
        self.rows = n_seq * t_new
        self.lk = (n_pages + -(-t_new // PAGE)) * PAGE
        self.n_keys = n_pages * PAGE + t_new

    def q_spec(self, width, col):
        return pl.BlockSpec((self.tq, width), lambda b, qi, pt: (self.row0 // self.tq + b * self.nq + qi, col // width))

    def seq_spec(self, width, col):
        return pl.BlockSpec((self.t_new, width), lambda b, qi, pt: (self.row0 // self.t_new + b, col // width))

    def page_specs(self, shape):
        return [pl.BlockSpec((1,) + shape, lambda b, qi, pt, p=p: (pt[b, p], 0, 0)) for p in range(self.n_pages)]

    def out_spec(self, width):
        return pl.BlockSpec((self.tq, width), lambda b, qi, pt: (b * self.nq + qi, 0))

    def statics(self):
        return dict(n_pages=self.n_pages, t_new=self.t_new, tq=self.tq, pos0=self.pos0)

    def widths(self):
        if self.nq == 1:
            return (self.lk,)
        step = 4 * PAGE
        assert self.lk % step == 0
        return tuple(range(step, self.lk + 1, step))


def nsa_compress(grp, z, page_table, pool, wexp, rel_flat):
    kv_w = KV_GROUPS * 2 * HEAD_DIM
    qw = KV_GROUPS * GROUP_HEADS * HEAD_DIM
    const2 = lambda b, qi, pt: (0, 0)
    return pl.pallas_call(
        functools.partial(_cmp_body, **grp.statics()),
        grid_spec=pltpu.PrefetchScalarGridSpec(
            num_scalar_prefetch=1,
            grid=(grp.n_seq, grp.nq),
            in_specs=[grp.q_spec(qw, COL_QN), grp.seq_spec(kv_w, COL_KVC)] + grp.page_specs(KV_PAGE)
            + [pl.BlockSpec((PAGE, kv_w), const2), pl.BlockSpec(memory_space=pltpu.SMEM)],
            out_specs=[grp.out_spec(qw), grp.out_spec(KV_GROUPS * LANES)],
            scratch_shapes=[pltpu.VMEM((LANES, kv_w), F32),
                            pltpu.VMEM((KV_GROUPS, LANES, HEAD_DIM), BF16),
                            pltpu.VMEM((KV_GROUPS, LANES, HEAD_DIM), BF16)]),
        out_shape=[jax.ShapeDtypeStruct((grp.rows, qw), F32),
                   jax.ShapeDtypeStruct((grp.rows, KV_GROUPS * LANES), F32)],
        compiler_params=_cparams("parallel", "arbitrary"),
        name="nsa_compress",
    )(page_table, z, z, *([pool] * grp.n_pages), wexp, rel_flat)


def sparse_attention(mode, grp, z, page_table, pool, band, *, q_col, kv_col, mask=None):
    kv_w = KV_GROUPS * 2 * HEAD_DIM
    qw = KV_GROUPS * GROUP_HEADS * HEAD_DIM
    in_specs = [grp.q_spec(qw, q_col), grp.seq_spec(kv_w, kv_col)] + grp.page_specs(KV_PAGE)
    args = [z, z] + [pool] * grp.n_pages
    if mode in ("sel", "mask"):
        in_specs.append(grp.out_spec(mask.shape[1]))
        args.append(mask)
    in_specs.append(pl.BlockSpec((KV_GROUPS, GROUP_HEADS * grp.tq, BAND), lambda b, qi, pt: (0, 0, 0)))
    args.append(band)
    widths = grp.widths()
    s_cols = max(widths) if (mode != "win" or grp.nq == 1) else NSA_WINDOW + PAGE
    return pl.pallas_call(
        functools.partial(_attn_body, mode=mode, widths=widths, **grp.statics()),
        grid_spec=pltpu.PrefetchScalarGridSpec(
            num_scalar_prefetch=1,
            grid=(grp.n_seq, grp.nq),
            in_specs=in_specs,
            out_specs=grp.out_spec(qw),
            scratch_shapes=[pltpu.VMEM((KV_GROUPS, grp.lk, HEAD_DIM), BF16),
                            pltpu.VMEM((KV_GROUPS, grp.lk, HEAD_DIM), BF16),
                            pltpu.VMEM((GROUP_HEADS * grp.tq, s_cols), F32)]),
        out_shape=jax.ShapeDtypeStruct((grp.rows, qw), F32),
        compiler_params=_cparams("parallel", "arbitrary"),
        name="sparse_attention_" + mode,
    )(page_table, *args)


def dsa_index_scores(grp, z, page_table, idx_pool):
    return pl.pallas_call(
        functools.partial(_index_body, widths=grp.widths(), **grp.statics()),
        grid_spec=pltpu.PrefetchScalarGridSpec(
            num_scalar_prefetch=1,
            grid=(grp.n_seq, grp.nq),
            in_specs=[grp.q_spec(IDX_HEADS * IDX_DIM, COL_QI), grp.q_spec(LANES, COL_MISC),
                      grp.seq_spec(LANES, COL_MISC)] + grp.page_specs((IDX_DIM, PAGE)),
            out_specs=grp.out_spec(grp.lk),
            scratch_shapes=[pltpu.VMEM((IDX_DIM, grp.lk), BF16)]),
        out_shape=jax.ShapeDtypeStruct((grp.rows, grp.lk), F32),
        compiler_params=_cparams("parallel", "arbitrary"),
        name="dsa_index_scores",
    )(page_table, z, z, z, *([idx_pool] * grp.n_pages))


def topk_mask(grp, scores, k):
    rows, lk = scores.shape
    tr = PAGE
    assert rows % tr == 0 and (grp.nq == 1 or grp.tq == tr)
    blk = pl.BlockSpec((tr, lk), lambda i: (i, 0))
    return pl.pallas_call(
        functools.partial(_topk_body, k=k, nq=grp.nq, tr=tr, widths=grp.widths()),
        grid=(rows // tr,),
        in_specs=[blk],
        out_specs=blk,
        out_shape=jax.ShapeDtypeStruct((rows, lk), F32),
        scratch_shapes=[pltpu.VMEM((tr, lk), jnp.int32)],
        compiler_params=_cparams("parallel"),
        name="topk_mask",
    )(scores)


def _combine_body(oc_ref, os_ref, ow_ref, od_ref, misc_ref, y_ref):
    n_heads = KV_GROUPS * GROUP_HEADS
    gates = jax.nn.sigmoid(misc_ref[:, MISC_GATES:MISC_GATES + 3 * n_heads])
    for h in range(n_heads):
        hs = slice(h * HEAD_DIM, (h + 1) * HEAD_DIM)
        o = (gates[:, 3 * h:3 * h + 1] * oc_ref[:, hs] + gates[:, 3 * h + 1:3 * h + 2] * os_ref[:, hs]
             + gates[:, 3 * h + 2:3 * h + 3] * ow_ref[:, hs])
        y_ref[:, hs] = o.astype(y_ref.dtype)
    y_ref[:, n_heads * HEAD_DIM:] = od_ref[...].astype(y_ref.dtype)


def nsa_dsa_combine(o_c, o_s, o_w, o_d, z, row0, *, tm):
    m, w = o_c.shape
    assert m % tm == 0 and row0 % tm == 0
    blk = pl.BlockSpec((tm, w), lambda i: (i, 0))
    return pl.pallas_call(
        _combine_body,
        grid=(m // tm,),
        in_specs=[blk, blk, blk, blk, pl.BlockSpec((tm, LANES), lambda i: (row0 // tm + i, COL_MISC // LANES))],
        out_specs=pl.BlockSpec((tm, 2 * w), lambda i: (i, 0)),
        out_shape=jax.ShapeDtypeStruct((m, 2 * w), BF16),
        compiler_params=_cparams("parallel"),
        name="nsa_dsa_combine",
    )(o_c, o_s, o_w, o_d, z)


def _band_tiles(rel_bias, tq):
    delta = (rel_bias[_BUCKETS] - rel_bias[N_BUCKETS - 1]).T
    rev = jnp.concatenate([delta[:, ::-1], jnp.zeros((delta.shape[0], PAGE), delta.dtype)], 1)
    tiles = jnp.stack([rev[:, PAGE - 1 - i:PAGE - 1 - i + BAND] for i in range(tq)], 1)
    return tiles.reshape(-1, GROUP_HEADS * tq, BAND)


def _widen_cd_w_in(w):
    sizes = (1024, 512, 512, 512, 24, 1024, 512, 512, 64, 8)
    q_n, kv_c, kv_s, kv_w, gates, q_d, kv_d, q_i, k_i, w_i = jnp.split(w, np.cumsum(sizes)[:-1].tolist(), axis=1)
    cols = [q_n, q_d, kv_c, kv_s, kv_w, kv_d, q_i, k_i, gates, w_i]
    used = sum(c.shape[1] for c in cols)
    return jnp.concatenate(cols + [jnp.zeros((w.shape[0], NZ - used), w.dtype)], axis=1)


def kernel(x_prompt, x_sample, state_conv, state_pool, cache_nsa_cmp, cache_nsa_sel, cache_nsa_win, cache_dsa_kv, cache_dsa_idx, page_table, norm_mix, norm_ffn, norm_final, ab_w_in, ab_conv_w, ab_conv_b, ab_ln_g, ab_ln_b, ab_pool_w, ab_pool_scale, ab_w_out, cd_w_in, cd_w_cmp, cd_w_out, rel_bias, ffn_w1, ffn_w2):
    bp, tp, d_model = x_prompt.shape
    bs, ts, _ = x_sample.shape
    mp, ms = bp * tp, bs * ts
    depth = norm_mix.shape[0]
    n_pages = page_table.shape[1]
    n_pool = cache_nsa_cmp.shape[1]
    past_len = n_pages * PAGE
    assert cache_nsa_cmp.shape[2] == PAGE
    win_len = cache_nsa_win.shape[2]
    assert win_len % PAGE == 0 and win_len == NSA_WINDOW and tp >= NSA_WINDOW
    kv_w = KV_GROUPS * 2 * HEAD_DIM

    x = jnp.concatenate([x_prompt.reshape(mp, d_model), x_sample.reshape(ms, d_model)], 0)
    grp_p = _Group(0, bp, tp, PAGE, 0, 0)
    grp_s = _Group(mp, bs, ts, ts, past_len, n_pages)
    grp_sw = _Group(mp, bs, ts, ts, past_len, win_len // PAGE)
    no_pages = jnp.zeros((1, 1), jnp.int32)
    win_pages = jnp.arange(bs * (win_len // PAGE), dtype=jnp.int32).reshape(bs, win_len // PAGE)

    outs = {k: [] for k in ("conv_p", "conv_s", "pool_p", "pool_s", "cmp_p", "cmp_s", "sel_p", "sel_s",
                            "win_p", "win_s", "dsa_p", "dsa_s", "idx_p", "idx_s")}
    y_final = None
    for i in range(depth):
        j = i // 2
        if i % 2 == 0:
            d_conv = ab_conv_w.shape[2]
            z = norm_matmul(x, norm_mix[i], ab_w_in[j].astype(BF16))
            mid_p, u_p = ab_mid_prompt(z, bp, tp, ab_conv_w[j], ab_conv_b[j], ab_ln_g[j], ab_ln_b[j],
                                       ab_pool_w[j], ab_pool_scale[j])
            mid_s, conv_s, pool_s = ab_mid_step(z, mp, bs, ts, past_len, state_conv[j], state_pool[j], ab_conv_w[j],
                                                ab_conv_b[j], ab_ln_g[j], ab_ln_b[j], ab_pool_w[j], ab_pool_scale[j])
            x = matmul_residual(jnp.concatenate([mid_p, mid_s], 0), ab_w_out[j].astype(BF16), x)
            outs["conv_p"].append(u_p.reshape(bp, tp, d_conv)[:, tp - CONV_BUF:])
            outs["conv_s"].append(conv_s)
            outs["pool_p"].append(z[:mp, 2 * d_conv:].reshape(bp, tp, -1)[:, tp - POOL_BUF:])
            outs["pool_s"].append(pool_s)
        else:
            z = norm_matmul(x, norm_mix[i], _widen_cd_w_in(cd_w_in[j]).astype(BF16))
            rel_flat = rel_bias.reshape(-1)
            band_p, band_s = _band_tiles(rel_bias, grp_p.tq), _band_tiles(rel_bias, grp_s.tq)
            wexp = jnp.tile(jnp.repeat(jnp.transpose(cd_w_cmp[j], (1, 0, 2)).reshape(NSA_BLOCK, 2 * KV_GROUPS),
                                       HEAD_DIM, axis=1), (PAGE // NSA_BLOCK, 1))
            pt = page_table + j * n_pool
            pools = [c.reshape((-1,) + KV_PAGE) for c in (cache_nsa_cmp, cache_nsa_sel, cache_dsa_kv)]
            idx_pool = jnp.swapaxes(cache_dsa_idx, 2, 3).reshape(-1, IDX_DIM, PAGE)
            win_pool = cache_nsa_win.reshape((-1,) + KV_PAGE)
            wpt = win_pages + j * bs * (win_len // PAGE)
            mids = []
            for grp, gw, ptab, wtab, band in ((grp_p, grp_p, no_pages, no_pages, band_p),
                                              (grp_s, grp_sw, pt, wpt, band_s)):
                o_c, msel = nsa_compress(grp, z, ptab, pools[0], wexp, rel_flat)
                o_s = sparse_attention("sel", grp, z, ptab, pools[1], band[:KV_GROUPS],
                                       q_col=COL_QN, kv_col=COL_KVS, mask=msel)
                o_w = sparse_attention("win", gw, z, wtab, win_pool, band[:KV_GROUPS],
                                       q_col=COL_QN, kv_col=COL_KVW)
                top = topk_mask(grp, dsa_index_scores(grp, z, ptab, idx_pool), min(DSA_TOPK, grp.n_keys // 4))
                o_d = sparse_attention("mask", grp, z, ptab, pools[2], band[KV_GROUPS:],
                                       q_col=COL_QD, kv_col=COL_KVD, mask=top)
                mids.append(nsa_dsa_combine(o_c, o_s, o_w, o_d, z, grp.row0, tm=min(512, grp.rows)))
            x = matmul_residual(jnp.concatenate(mids, 0), cd_w_out[j].astype(BF16), x)

            def kv_out(col, width, tail):
                seg = z[:, col:col + width]
                return seg[:mp].reshape((bp, tp) + tail), seg[mp:].reshape((bs, ts) + tail)

            kv_tail = (2, KV_GROUPS, HEAD_DIM)
            for name, col in (("cmp", COL_KVC), ("sel", COL_KVS), ("dsa", COL_KVD)):
                p_new, s_new = kv_out(col, kv_w, kv_tail)
                outs[name + "_p"].append(p_new)
                outs[name + "_s"].append(s_new)
            w_p, w_s = kv_out(COL_KVW, kv_w, kv_tail)
            outs["win_p"].append(w_p[:, tp - NSA_WINDOW:])
            outs["win_s"].append(jnp.concatenate([cache_nsa_win[j], w_s], 1)[:, ts:])
            i_p, i_s = kv_out(COL_MISC + MISC_KI, IDX_DIM, (IDX_DIM,))
            outs["idx_p"].append(i_p)
            outs["idx_s"].append(i_s)
        a = norm_matmul(x, norm_ffn[i], ffn_w1[i].astype(BF16), relu2=True, out_dtype=BF16)
        if i == depth - 1:
            x, y_final = matmul_residual(a, ffn_w2[i].astype(BF16), x, norm_final)
        else:
            x = matmul_residual(a, ffn_w2[i].astype(BF16), x)

    st = {k: jnp.stack(v) for k, v in outs.items()}
    return (y_final[:mp].reshape(bp, tp, d_model), y_final[mp:].reshape(bs, ts, d_model),
            st["conv_p"], st["conv_s"], st["pool_p"], st["pool_s"], st["cmp_p"], st["cmp_s"],
            st["sel_p"], st["sel_s"], st["win_p"], st["win_s"], st["dsa_p"], st["dsa_s"],
            st["idx_p"], st["idx_s"])
```

```python
import functools
import math

import numpy as np
import jax
import jax.numpy as jnp
from jax import lax
from jax.experimental import pallas as pl
from jax.experimental.pallas import tpu as pltpu

F32 = jnp.float32
BF16 = jnp.bfloat16

EPS = 1e-6
NEG_INF = -1e30
HEAD_DIM = 128
LANES = 128
CONV_WIDTH = 31
CONV_BUF = CONV_WIDTH - 1
POOL_WINDOWS = (2, 4, 8, 16)
POOL_BUF = max(POOL_WINDOWS) - 1
HALO = 32
VMEM_LIMIT = 56 * 1024 * 1024


def _cparams(*sem):
    return pltpu.CompilerParams(dimension_semantics=sem, vmem_limit_bytes=VMEM_LIMIT)


def _norm_matmul_body(x_ref, g_ref, w_ref, o_ref, h_ref, *, relu2):
    @pl.when(pl.program_id(1) == 0)
    def _():
        x = x_ref[...]
        y = x * lax.rsqrt(jnp.mean(x * x, -1, keepdims=True) + EPS)
        h_ref[...] = (y * g_ref[...]).astype(BF16)

    y = jnp.dot(h_ref[...], w_ref[...], preferred_element_type=F32)
    if relu2:
        y = jnp.square(jnp.maximum(y, 0.0))
    o_ref[...] = y.astype(o_ref.dtype)


def norm_matmul(x, g, w, *, relu2=False, out_dtype=F32, tm=1024, tn=1024):
    m, d = x.shape
    n = w.shape[1]
    assert m % tm == 0 and n % tn == 0
    return pl.pallas_call(
        functools.partial(_norm_matmul_body, relu2=relu2),
        grid=(m // tm, n // tn),
        in_specs=[pl.BlockSpec((tm, d), lambda i, j: (i, 0)),
                  pl.BlockSpec((1, d), lambda i, j: (0, 0)),
                  pl.BlockSpec((d, tn), lambda i, j: (0, j))],
        out_specs=pl.BlockSpec((tm, tn), lambda i, j: (i, j)),
        out_shape=jax.ShapeDtypeStruct((m, n), out_dtype),
        scratch_shapes=[pltpu.VMEM((tm, d), BF16)],
        compiler_params=_cparams("parallel", "arbitrary"),
        name="norm_matmul",
    )(x, g.reshape(1, d), w)


def _matmul_residual_body(a_ref, w_ref, r_ref, *rest, final_norm):
    if final_norm:
        g_ref, o_ref, n_ref, acc_ref = rest
    else:
        o_ref, acc_ref = rest
    k = pl.program_id(1)

    @pl.when(k == 0)
    def _():
        acc_ref[...] = jnp.zeros_like(acc_ref)

    acc_ref[...] += jnp.dot(a_ref[...], w_ref[...], preferred_element_type=F32)

    @pl.when(k == pl.num_programs(1) - 1)
    def _():
        o = r_ref[...] + acc_ref[...]
        o_ref[...] = o
        if final_norm:
            y = o * lax.rsqrt(jnp.mean(o * o, -1, keepdims=True) + EPS)
            n_ref[...] = y * g_ref[...]


def matmul_residual(a, w, r, g_final=None, *, tm=512, tk=2048):
    m, kdim = a.shape
    n = w.shape[1]
    assert m % tm == 0 and kdim % tk == 0
    final_norm = g_final is not None
    in_specs = [pl.BlockSpec((tm, tk), lambda i, k: (i, k)),
                pl.BlockSpec((tk, n), lambda i, k: (k, 0)),
                pl.BlockSpec((tm, n), lambda i, k: (i, 0))]
    out_spec = pl.BlockSpec((tm, n), lambda i, k: (i, 0))
    out_shape = jax.ShapeDtypeStruct((m, n), F32)
    args = [a, w, r]
    if final_norm:
        in_specs.append(pl.BlockSpec((1, n), lambda i, k: (0, 0)))
        args.append(g_final.reshape(1, n))
        out_spec, out_shape = [out_spec, out_spec], [out_shape, out_shape]
    return pl.pallas_call(
        functools.partial(_matmul_residual_body, final_norm=final_norm),
        grid=(m // tm, kdim // tk),
        in_specs=in_specs,
        out_specs=out_spec,
        out_shape=out_shape,
        scratch_shapes=[pltpu.VMEM((tm, n), F32)],
        compiler_params=_cparams("parallel", "arbitrary"),
        name="matmul_residual",
    )(*args)


def _layernorm_silu(c, g, b):
    mu = jnp.mean(c, -1, keepdims=True)
    xc = c - mu
    y = xc * lax.rsqrt(jnp.mean(xc * xc, -1, keepdims=True) + EPS)
    y = y * g + b
    return y * jax.nn.sigmoid(y)


def _ab_mid_body(z_ref, zp_ref, cw_ref, cb_ref, lg_ref, lb_ref, pw_ref, ps_ref, y_ref, u_ref,
                 ext_ref, vext_ref, conv_ref, *, tt, d_conv, d_pool):
    ti = pl.program_id(1)
    keep = (ti > 0).astype(F32)
    a_p = zp_ref[:, 0:d_conv]
    g_p = zp_ref[:, d_conv:2 * d_conv]
    ext_ref[0:HALO, :] = a_p * jax.nn.sigmoid(g_p) * keep
    vext_ref[0:HALO, :] = zp_ref[:, 2 * d_conv:] * keep
    u = z_ref[:, 0:d_conv] * jax.nn.sigmoid(z_ref[:, d_conv:2 * d_conv])
    ext_ref[HALO:, :] = u
    u_ref[...] = u
    vext_ref[HALO:, :] = z_ref[:, 2 * d_conv:]

    off = HALO - CONV_BUF
    for c in range(d_conv // LANES):
        cs = slice(c * LANES, (c + 1) * LANES)
        acc = jnp.zeros((tt, LANES), F32)
        for j in range(CONV_WIDTH):
            acc = acc + cw_ref[j:j + 1, cs] * ext_ref[off + j:off + j + tt, cs]
        conv_ref[:, cs] = acc + cb_ref[:, cs]
    y_ref[:, 0:d_conv] = _layernorm_silu(conv_ref[...], lg_ref[...], lb_ref[...]).astype(y_ref.dtype)

    pos = ti * tt + lax.broadcasted_iota(jnp.int32, (tt, 1), 0)
    pg = d_pool // len(POOL_WINDOWS)
    for gi, w in enumerate(POOL_WINDOWS):
        gs = slice(gi * pg, (gi + 1) * pg)
        tok = vext_ref[HALO:, gs]
        acc = tok
        for i in range(1, w):
            acc = acc + vext_ref[HALO - i:HALO - i + tt, gs]
        cnt = jnp.minimum(pos + 1, w).astype(F32)
        d = acc / cnt - tok
        yp = jnp.dot(d.astype(BF16), pw_ref[gi], preferred_element_type=F32) * ps_ref[:, gs]
        y_ref[:, d_conv + gi * pg:d_conv + (gi + 1) * pg] = yp.astype(y_ref.dtype)


def ab_mid_prompt(z, n_seq, t_len, conv_w, conv_b, ln_g, ln_b, pool_w, pool_scale, *, tt=256):
    d_conv = conv_w.shape[1]
    d_pool = pool_scale.shape[0]
    nt = t_len // tt
    hb = tt // HALO
    row = lambda b, t: (b * nt + t, 0)
    const = lambda b, t: (0, 0)
    return pl.pallas_call(
        functools.partial(_ab_mid_body, tt=tt, d_conv=d_conv, d_pool=d_pool),
        grid=(n_seq, nt),
        in_specs=[pl.BlockSpec((tt, z.shape[1]), row),
                  pl.BlockSpec((HALO, z.shape[1]), lambda b, t: (jnp.maximum((b * nt + t) * hb - 1, 0), 0)),
                  pl.BlockSpec(conv_w.shape, const),
                  pl.BlockSpec((1, d_conv), const),
                  pl.BlockSpec((1, d_conv), const),
                  pl.BlockSpec((1, d_conv), const),
                  pl.BlockSpec(pool_w.shape, lambda b, t: (0, 0, 0)),
                  pl.BlockSpec((1, d_pool), const)],
        out_specs=[pl.BlockSpec((tt, d_conv + d_pool), row),
                   pl.BlockSpec((tt, d_conv), row)],
        out_shape=[jax.ShapeDtypeStruct((n_seq * t_len, d_conv + d_pool), BF16),
                   jax.ShapeDtypeStruct((n_seq * t_len, d_conv), F32)],
        scratch_shapes=[pltpu.VMEM((HALO + tt, d_conv), F32),
                        pltpu.VMEM((HALO + tt, d_pool), F32),
                        pltpu.VMEM((tt, d_conv), F32)],
        compiler_params=_cparams("parallel", "parallel"),
        name="ab_mid_prompt",
    )(z, z, conv_w, conv_b.reshape(1, -1), ln_g.reshape(1, -1), ln_b.reshape(1, -1),
      pool_w.astype(BF16), pool_scale.reshape(1, -1))


def _ab_mid_step_body(z_ref, sc_ref, sp_ref, cw_ref, cb_ref, lg_ref, lb_ref, pw_ref, ps_ref,
                      y_ref, nc_ref, np_ref, ext_ref, vext_ref, *, nb, t, pos0, d_conv, d_pool):
    e0 = HALO - CONV_BUF
    p0 = 16 - POOL_BUF
    z = z_ref[...].reshape(nb, t, z_ref.shape[1])
    u = z[:, :, 0:d_conv] * jax.nn.sigmoid(z[:, :, d_conv:2 * d_conv])
    ext_ref[:, e0:HALO, :] = sc_ref[...]
    ext_ref[:, HALO:, :] = u
    vext_ref[:, p0:16, :] = sp_ref[...]
    vext_ref[:, 16:, :] = z[:, :, 2 * d_conv:]
    nc_ref[...] = ext_ref[:, HALO + t - CONV_BUF:, :]
    np_ref[...] = vext_ref[:, 16 + t - POOL_BUF:, :]

    acc = jnp.zeros((nb, t, d_conv), F32)
    for j in range(CONV_WIDTH):
        acc = acc + cw_ref[j:j + 1, :][None] * ext_ref[:, e0 + j:e0 + j + t, :]
    c = acc + cb_ref[...][None]
    yc = _layernorm_silu(c, lg_ref[...][None], lb_ref[...][None])
    y_ref[:, 0:d_conv] = yc.reshape(nb * t, d_conv).astype(y_ref.dtype)

    pg = d_pool // len(POOL_WINDOWS)
    for gi, w in enumerate(POOL_WINDOWS):
        gs = slice(gi * pg, (gi + 1) * pg)
        tok = vext_ref[:, 16:, gs]
        acc = tok
        for i in range(1, w):
            acc = acc + vext_ref[:, 16 - i:16 - i + t, gs]
        cnt = jnp.minimum(pos0 + 1 + lax.broadcasted_iota(jnp.int32, (1, t, 1), 1), w).astype(F32)
        d = (acc / cnt - tok).reshape(nb * t, pg)
        yp = jnp.dot(d.astype(BF16), pw_ref[gi], preferred_element_type=F32) * ps_ref[:, gs]
        y_ref[:, d_conv + gi * pg:d_conv + (gi + 1) * pg] = yp.astype(y_ref.dtype)


def ab_mid_step(z, row0, n_seq, t, pos0, state_conv, state_pool, conv_w, conv_b, ln_g, ln_b, pool_w,
                pool_scale, *, nb=16):
    d_conv = conv_w.shape[1]
    d_pool = pool_scale.shape[0]
    rb = nb * t
    assert row0 % rb == 0 and n_seq % nb == 0
    const = lambda i: (0, 0)
    seq3 = lambda i: (i, 0, 0)
    return pl.pallas_call(
        functools.partial(_ab_mid_step_body, nb=nb, t=t, pos0=pos0, d_conv=d_conv, d_pool=d_pool),
        grid=(n_seq // nb,),
        in_specs=[pl.BlockSpec((rb, z.shape[1]), lambda i: (row0 // rb + i, 0)),
                  pl.BlockSpec((nb, CONV_BUF, d_conv), seq3),
                  pl.BlockSpec((nb, POOL_BUF, d_pool), seq3),
                  pl.BlockSpec(conv_w.shape, const),
                  pl.BlockSpec((1, d_conv), const),
                  pl.BlockSpec((1, d_conv), const),
                  pl.BlockSpec((1, d_conv), const),
                  pl.BlockSpec(pool_w.shape, lambda i: (0, 0, 0)),
                  pl.BlockSpec((1, d_pool), const)],
        out_specs=[pl.BlockSpec((rb, d_conv + d_pool), lambda i: (i, 0)),
                   pl.BlockSpec((nb, CONV_BUF, d_conv), seq3),
                   pl.BlockSpec((nb, POOL_BUF, d_pool), seq3)],
        out_shape=[jax.ShapeDtypeStruct((n_seq * t, d_conv + d_pool), BF16),
                   jax.ShapeDtypeStruct((n_seq, CONV_BUF, d_conv), F32),
                   jax.ShapeDtypeStruct((n_seq, POOL_BUF, d_pool), F32)],
        scratch_shapes=[pltpu.VMEM((nb, HALO + t, d_conv), F32),
                        pltpu.VMEM((nb, 16 + t, d_pool), F32)],
        compiler_params=_cparams("parallel"),
        name="ab_mid_step",
    )(z, state_conv, state_pool, conv_w, conv_b.reshape(1, -1), ln_g.reshape(1, -1),
      ln_b.reshape(1, -1), pool_w.astype(BF16), pool_scale.reshape(1, -1))


N_BUCKETS = 32
MAX_DISTANCE = 128
NSA_BLOCK = 64
NSA_TOPN = 16
NSA_WINDOW = 512
DSA_TOPK = 256
IDX_HEADS = 8
IDX_DIM = 64
KV_GROUPS = 2
GROUP_HEADS = 4
PAGE = 128
BAND = 2 * PAGE
INT_MIN = -2 ** 31
KV_PAGE = (2 * KV_GROUPS * PAGE, HEAD_DIM)
TOPK_ROW_GROUPS = 4
CMP_BIAS_LANE0 = 64

COL_QN, COL_QD, COL_KVC, COL_KVS, COL_KVW, COL_KVD, COL_QI, COL_MISC = 0, 1024, 2048, 2560, 3072, 3584, 4096, 4608
MISC_KI, MISC_GATES, MISC_WI = 0, 64, 88
NZ = 5120


def _bucket_np(n):
    n = np.maximum(np.asarray(n, np.int32), 0)
    exact = N_BUCKETS // 2
    nf = np.maximum(n, 1).astype(np.float32)
    big = exact + (np.log(nf / np.float32(exact)) / np.float32(math.log(MAX_DISTANCE / exact))
                   * np.float32(N_BUCKETS - exact)).astype(np.int32)
    return np.where(n < exact, n, np.minimum(big, N_BUCKETS - 1))


_BUCKETS = _bucket_np(np.arange(BAND))
assert _BUCKETS[PAGE:].min() == N_BUCKETS - 1


def _softmax_rows(s, mask):
    s = jnp.where(mask, s, NEG_INF)
    m = jnp.max(s, -1, keepdims=True)
    p = jnp.where(mask, jnp.exp(s - m), 0.0)
    return p, jnp.sum(p, -1, keepdims=True)


def _dot_nt(a, b):
    return lax.dot_general(a, b, (((1,), (1,)), ((), ())), preferred_element_type=F32)


def _new_chunks(new_ref, t_new):
    chunks = [new_ref[c * PAGE:(c + 1) * PAGE, :] for c in range(t_new // PAGE)]
    rem = t_new % PAGE
    if rem:
        tail = new_ref[(t_new // PAGE) * PAGE:, :]
        chunks.append(jnp.concatenate([tail, jnp.zeros((PAGE - rem, tail.shape[1]), F32)], 0))
    return chunks


def _kv_chunks(page_refs, new_ref, t_new):
    n_parts = 2 * KV_GROUPS
    chunks = [[r[0, pl.ds(part, PAGE, stride=n_parts), :] for part in range(n_parts)] for r in page_refs]
    for x in _new_chunks(new_ref, t_new):
        chunks.append([x[:, part * HEAD_DIM:(part + 1) * HEAD_DIM] for part in range(n_parts)])
    return chunks


def _cmp_body(pt_ref, q_ref, kvn_ref, *rest, n_pages, t_new, tq, pos0):
    del pt_ref
    page_refs = rest[:n_pages]
    wexp_ref, bias_ref, o_ref, msel_ref, comp_ref, ck_ref, cv_ref = rest[n_pages:]
    qi = pl.program_id(1)
    n_keys = n_pages * PAGE + t_new
    n_cmp = n_keys // NSA_BLOCK
    n_sel = -(-n_keys // NSA_BLOCK)
    per = PAGE // NSA_BLOCK

    @pl.when(qi == 0)
    def _():
        comp_ref[...] = jnp.zeros_like(comp_ref)
        chunks = _kv_chunks(page_refs, kvn_ref, t_new)[:n_cmp // per]
        for c, parts in enumerate(chunks):
            for part, x in enumerate(parts):
                cols = slice(part * HEAD_DIM, (part + 1) * HEAD_DIM)
                x = x * wexp_ref[:, cols]
                comp_ref[per * c:per * (c + 1), cols] = x.reshape(per, NSA_BLOCK, HEAD_DIM).sum(1)
        for g in range(KV_GROUPS):
            ck_ref[g] = comp_ref[:, g * HEAD_DIM:(g + 1) * HEAD_DIM].astype(BF16)
            cv_ref[g] = comp_ref[:, (KV_GROUPS + g) * HEAD_DIM:(KV_GROUPS + g + 1) * HEAD_DIM].astype(BF16)

    scale = HEAD_DIM ** -0.5
    qpos = pos0 + qi * tq + lax.broadcasted_iota(jnp.int32, (tq, 1), 0)
    blk = lax.broadcasted_iota(jnp.int32, (1, LANES), 1)
    dist = qpos - ((blk + 1) * NSA_BLOCK - 1)
    mask = (dist >= 0) & (blk < n_cmp)
    cur = qpos // NSA_BLOCK
    for g in range(KV_GROUPS):
        imp = jnp.zeros((tq, LANES), F32)
        for r in range(GROUP_HEADS):
            h = g * GROUP_HEADS + r
            bias = bias_ref[h]
            if t_new != tq:
                bias = pltpu.roll(bias, qi * (tq // NSA_BLOCK) + (LANES - CMP_BIAS_LANE0), 1)
            q = q_ref[:, h * HEAD_DIM:(h + 1) * HEAD_DIM].astype(BF16)
            s = _dot_nt(q, ck_ref[g]) * scale + bias
            p, l = _softmax_rows(s, mask)
            p = p / jnp.maximum(l, 1e-30)
            o_ref[:, h * HEAD_DIM:(h + 1) * HEAD_DIM] = jnp.dot(p.astype(BF16), cv_ref[g],
                                                               preferred_element_type=F32)
            imp = imp + p
        imp = jnp.where(blk == cur, 2.0, jnp.where(blk > cur, -1.0, imp))
        imp = jnp.where(blk < n_sel, imp, -2.0)
        n_top = min(NSA_TOPN, n_sel)
        cols = slice(g * LANES, (g + 1) * LANES)

        def by_rank(imp=imp, cols=cols):
            rank = jnp.zeros((tq, LANES), F32)
            for i in range(n_sel):
                col = imp[:, i:i + 1]
                ahead = (col > imp) | ((col == imp) & (blk > i))
                rank = rank + jnp.where(ahead, 1.0, 0.0)
            msel_ref[:, cols] = jnp.where((rank < float(n_top)) & (blk < n_sel), 1.0, 0.0)

        def first_blocks(cols=cols):
            msel_ref[:, cols] = jnp.where(blk < n_top, 1.0, 0.0) + jnp.zeros((tq, LANES), F32)

        if t_new == tq:
            if pos0 + tq <= n_top * NSA_BLOCK:
                first_blocks()
            else:
                by_rank()
        else:
            early = pos0 + (qi + 1) * tq <= n_top * NSA_BLOCK
            pl.when(early)(first_blocks)
            pl.when(jnp.logical_not(early))(by_rank)


def _on_causal_width(qi, tq, widths, tile):
    if len(widths) == 1:
        tile(widths[0], True)
        return
    need = (qi * tq + tq - 1) // widths[0]
    for nw, w in enumerate(widths):
        pl.when(need == nw)(functools.partial(tile, w, nw == 0))


def _attn_body(pt_ref, q_ref, kvn_ref, *rest, mode, n_pages, t_new, tq, pos0, widths):
    del pt_ref
    page_refs = rest[:n_pages]
    rest = rest[n_pages:]
    m_ref = None
    if mode in ("sel", "mask"):
        m_ref, rest = rest[0], rest[1:]
    band_ref, o_ref, kc_ref, vc_ref, s_ref = rest
    qi = pl.program_id(1)
    single = t_new == tq
    n_keys = n_pages * PAGE + t_new
    kbase = pos0 - n_pages * PAGE
    scale = HEAD_DIM ** -0.5
    q0 = pos0 if single else pos0 + qi * tq

    @pl.when(qi == 0)
    def _():
        for c, parts in enumerate(_kv_chunks(page_refs, kvn_ref, t_new)):
            rows = slice(c * PAGE, (c + 1) * PAGE)
            for g in range(KV_GROUPS):
                kc_ref[g, rows, :] = parts[g].astype(BF16)
                vc_ref[g, rows, :] = parts[KV_GROUPS + g].astype(BF16)

    def tile(c0, w, band_at, maybe_first):
        qpos = q0 + lax.broadcasted_iota(jnp.int32, (tq, 1), 0)
        col = c0 + lax.broadcasted_iota(jnp.int32, (1, w), 1)
        dist = qpos - (kbase + col)
        visible = (dist >= 0) & (col < n_keys)
        if mode == "win":
            visible = visible & (dist < NSA_WINDOW)
        if mode == "mask":
            visible = visible & (m_ref[:, 0:w] > 0.5)
        keys = pl.ds(c0, w)
        for g in range(KV_GROUPS):
            mask = visible
            if mode == "sel":
                expand = (lax.broadcasted_iota(jnp.int32, (LANES, w), 1) // NSA_BLOCK
                          == lax.broadcasted_iota(jnp.int32, (LANES, w), 0))
                chosen = jnp.dot(m_ref[:, g * LANES:(g + 1) * LANES].astype(BF16),
                                 jnp.where(expand, 1.0, 0.0).astype(BF16), preferred_element_type=F32)
                mask = visible & (chosen > 0.5)
            heads = [g * GROUP_HEADS + r for r in range(GROUP_HEADS)]
            q = jnp.concatenate([q_ref[:, h * HEAD_DIM:(h + 1) * HEAD_DIM] for h in heads], 0).astype(BF16)
            s_ref[:, 0:w] = _dot_nt(q, kc_ref[g, keys, :]) * scale
            if band_at is not None:
                s_ref[:, band_at:band_at + BAND] += band_ref[g]
            else:
                if maybe_first:
                    @pl.when(qi == 0)
                    def _():
                        s_ref[:, 0:PAGE] += band_ref[g, :, PAGE:]

                @pl.when(qi > 0)
                def _():
                    s_ref[:, pl.ds(pl.multiple_of(q0 - PAGE - kbase, PAGE), BAND)] += band_ref[g]
            sums = []
            for r in range(GROUP_HEADS):
                rows = slice(r * tq, (r + 1) * tq)
                p, l = _softmax_rows(s_ref[rows, 0:w], mask)
                s_ref[rows, 0:w] = p
                sums.append(l)
            o = jnp.dot(s_ref[:, 0:w].astype(BF16), vc_ref[g, keys, :], preferred_element_type=F32)
            for r, h in enumerate(heads):
                o_ref[:, h * HEAD_DIM:(h + 1) * HEAD_DIM] = o[r * tq:(r + 1) * tq] / jnp.maximum(sums[r], 1e-30)

    if single:
        tile(0, widths[0], pos0 - PAGE - kbase, False)
    elif mode == "win":
        wch = NSA_WINDOW // PAGE
        pl.when(qi < wch)(functools.partial(tile, 0, NSA_WINDOW, None, True))
        pl.when(qi >= wch)(lambda: tile(pl.multiple_of((qi - wch) * PAGE, PAGE), NSA_WINDOW + PAGE,
                                        NSA_WINDOW - PAGE, False))
    else:
        _on_causal_width(qi, tq, widths, lambda w, first: tile(0, w, None, first))


def _index_body(pt_ref, qidx_ref, miscq_ref, misck_ref, *rest, n_pages, t_new, tq, pos0, widths):
    del pt_ref
    ipage_refs = rest[:n_pages]
    o_ref, kidx_ref = rest[n_pages:]
    qi = pl.program_id(1)
    lk = o_ref.shape[1]
    n_keys = n_pages * PAGE + t_new
    kbase = pos0 - n_pages * PAGE
    q0 = pos0 if t_new == tq else pos0 + qi * tq

    @pl.when(qi == 0)
    def _():
        for c, r in enumerate(ipage_refs):
            kidx_ref[:, c * PAGE:(c + 1) * PAGE] = r[0].astype(BF16)
        for c, x in enumerate(_new_chunks(misck_ref, t_new)):
            cols = slice((n_pages + c) * PAGE, (n_pages + c + 1) * PAGE)
            kidx_ref[:, cols] = x.T[MISC_KI:MISC_KI + IDX_DIM, :].astype(BF16)

    def tile(w, maybe_first):
        del maybe_first
        qpos = q0 + lax.broadcasted_iota(jnp.int32, (tq, 1), 0)
        col = lax.broadcasted_iota(jnp.int32, (1, w), 1)
        visible = (qpos - (kbase + col) >= 0) & (col < n_keys)
        q = jnp.concatenate([qidx_ref[:, hh * IDX_DIM:(hh + 1) * IDX_DIM] for hh in range(IDX_HEADS)], 0)
        sc = jnp.dot(q.astype(BF16), kidx_ref[:, 0:w], preferred_element_type=F32)
        score = jnp.zeros((tq, w), F32)
        for hh in range(IDX_HEADS):
            wi = miscq_ref[:, MISC_WI + hh:MISC_WI + hh + 1] * (IDX_HEADS ** -0.5)
            score = score + jnp.maximum(sc[hh * tq:(hh + 1) * tq] * (IDX_DIM ** -0.5), 0.0) * wi
        o_ref[:, 0:w] = jnp.where(visible, score, NEG_INF)
        if w < lk:
            o_ref[:, w:] = jnp.full((tq, lk - w), NEG_INF, F32)

    _on_causal_width(qi, tq, widths, tile)


def _topk_body(s_ref, m_ref, key_ref, *, k, nq, tr, widths):
    lk = s_ref.shape[1]
    assert lk <= 4096
    neg_key = int(np.array(NEG_INF, np.float32).view(np.int32)) ^ 0x7FFFFFFF
    kf = float(k)

    def tile(w, maybe_first):
        del maybe_first
        bits = lax.bitcast_convert_type(s_ref[:, 0:w] + 0.0, jnp.int32)
        key_ref[:, 0:w] = jnp.where(bits >= 0, bits, bits ^ 0x7FFFFFFF)
        col = lax.broadcasted_iota(jnp.int32, (1, w), 1)
        unseen = float(lk - w)

        groups = [slice(a * (tr // TOPK_ROW_GROUPS), (a + 1) * (tr // TOPK_ROW_GROUPS)) for a in range(TOPK_ROW_GROUPS)]
        zeros = tuple(jnp.zeros((tr // TOPK_ROW_GROUPS, 1), jnp.int32) for _ in groups)

        def thr_step(i, tus):
            out = []
            for rows, tu in zip(groups, tus):
                cand = tu | jnp.left_shift(jnp.int32(1), 31 - i)
                cs = cand ^ INT_MIN
                cnt = jnp.sum(jnp.where(key_ref[rows, 0:w] >= cs, 1.0, 0.0), -1, keepdims=True)
                cnt = cnt + jnp.where(cs <= neg_key, unseen, 0.0)
                out.append(jnp.where(cnt >= kf, cand, tu))
            return tuple(out)

        thr = jnp.concatenate(lax.fori_loop(0, 32, thr_step, zeros), 0) ^ INT_MIN
        key = key_ref[:, 0:w]
        above = key > thr
        tied = key == thr
        need = kf - jnp.sum(jnp.where(above, 1.0, 0.0), -1, keepdims=True)

        def tie_step(i, j0s):
            out = []
            for rows, j0 in zip(groups, j0s):
                cand = j0 | jnp.left_shift(jnp.int32(1), 11 - i)
                hit = (key_ref[rows, 0:w] == thr[rows]) & (col < cand)
                cnt = jnp.sum(jnp.where(hit, 1.0, 0.0), -1, keepdims=True)
                out.append(jnp.where(cnt < need[rows], cand, j0))
            return tuple(out)

        j0 = jnp.concatenate(lax.fori_loop(0, 12, tie_step, zeros), 0)
        m_ref[:, 0:w] = jnp.where(above | (tied & (col <= j0)), 1.0, 0.0)
        if w < lk:
            m_ref[:, w:] = jnp.zeros((tr, lk - w), F32)

    _on_causal_width(pl.program_id(0) % nq, tr, widths, tile)


class _Group:
    def __init__(self, row0, n_seq, t_new, tq, pos0, n_pages):
        assert t_new % tq == 0 and row0 % tq == 0 and row0 % t_new == 0
        assert pos0 == n_pages * PAGE or n_pages * PAGE < pos0
        assert t_new == tq or (tq == PAGE and pos0 == 0)
        assert t_new % PAGE == 0 or t_new % PAGE < NSA_BLOCK
        self.row0, self.n_seq, self.t_new, self.tq, self.pos0, self.n_pages = row0, n_seq, t_new, tq, pos0, n_pages
        self.nq = t_new // tq
        self.rows = n_seq * t_new
        self.lk = (n_pages + -(-t_new // PAGE)) * PAGE
        self.n_keys = n_pages * PAGE + t_new

    def q_spec(self, width, col):
        return pl.BlockSpec((self.tq, width), lambda b, qi, pt: (self.row0 // self.tq + b * self.nq + qi, col // width))

    def seq_spec(self, width, col):
        return pl.BlockSpec((self.t_new, width), lambda b, qi, pt: (self.row0 // self.t_new + b, col // width))

    def page_specs(self, shape):
        return [pl.BlockSpec((1,) + shape, lambda b, qi, pt, p=p: (pt[b, p], 0, 0)) for p in range(self.n_pages)]

    def out_spec(self, width):
        return pl.BlockSpec((self.tq, width), lambda b, qi, pt: (b * self.nq + qi, 0))

    def statics(self):
        return dict(n_pages=self.n_pages, t_new=self.t_new, tq=self.tq, pos0=self.pos0)

    def widths(self):
        if self.nq == 1:
            return (self.lk,)
        step = 4 * PAGE
        assert self.lk % step == 0
        return tuple(range(step, self.lk + 1, step))


def _cmp_bias_table(rel_bias, grp):
    lane0 = 0 if grp.nq == 1 else CMP_BIAS_LANE0
    tab = rel_bias[_BUCKETS]
    far = jnp.broadcast_to(rel_bias[N_BUCKETS - 1], (grp.tq, rel_bias.shape[1]))
    cols = []
    for lane in range(LANES):
        d0 = grp.pos0 - (NSA_BLOCK * (lane - lane0 + 1) - 1)
        if d0 >= PAGE or d0 + grp.tq - 1 < 0:
            cols.append(far)
        else:
            cols.append(tab[np.clip(d0 + np.arange(grp.tq), 0, BAND - 1)])
    return jnp.transpose(jnp.stack(cols, 1), (2, 0, 1))


def nsa_compress(grp, z, page_table, pool, wexp, bias):
    kv_w = KV_GROUPS * 2 * HEAD_DIM
    qw = KV_GROUPS * GROUP_HEADS * HEAD_DIM
    const2 = lambda b, qi, pt: (0, 0)
    return pl.pallas_call(
        functools.partial(_cmp_body, **grp.statics()),
        grid_spec=pltpu.PrefetchScalarGridSpec(
            num_scalar_prefetch=1,
            grid=(grp.n_seq, grp.nq),
            in_specs=[grp.q_spec(qw, COL_QN), grp.seq_spec(kv_w, COL_KVC)] + grp.page_specs(KV_PAGE)
            + [pl.BlockSpec((PAGE, kv_w), const2), pl.BlockSpec(bias.shape, lambda b, qi, pt: (0, 0, 0))],
            out_specs=[grp.out_spec(qw), grp.out_spec(KV_GROUPS * LANES)],
            scratch_shapes=[pltpu.VMEM((LANES, kv_w), F32),
                            pltpu.VMEM((KV_GROUPS, LANES, HEAD_DIM), BF16),
                            pltpu.VMEM((KV_GROUPS, LANES, HEAD_DIM), BF16)]),
        out_shape=[jax.ShapeDtypeStruct((grp.rows, qw), F32),
                   jax.ShapeDtypeStruct((grp.rows, KV_GROUPS * LANES), F32)],
        compiler_params=_cparams("parallel", "arbitrary"),
        name="nsa_compress",
    )(page_table, z, z, *([pool] * grp.n_pages), wexp, bias)


def sparse_attention(mode, grp, z, page_table, pool, band, *, q_col, kv_col, mask=None):
    kv_w = KV_GROUPS * 2 * HEAD_DIM
    qw = KV_GROUPS * GROUP_HEADS * HEAD_DIM
    in_specs = [grp.q_spec(qw, q_col), grp.seq_spec(kv_w, kv_col)] + grp.page_specs(KV_PAGE)
    args = [z, z] + [pool] * grp.n_pages
    if mode in ("sel", "mask"):
        in_specs.append(grp.out_spec(mask.shape[1]))
        args.append(mask)
    in_specs.append(pl.BlockSpec((KV_GROUPS, GROUP_HEADS * grp.tq, BAND), lambda b, qi, pt: (0, 0, 0)))
    args.append(band)
    widths = grp.widths()
    s_cols = max(widths) if (mode != "win" or grp.nq == 1) else NSA_WINDOW + PAGE
    return pl.pallas_call(
        functools.partial(_attn_body, mode=mode, widths=widths, **grp.statics()),
        grid_spec=pltpu.PrefetchScalarGridSpec(
            num_scalar_prefetch=1,
            grid=(grp.n_seq, grp.nq),
            in_specs=in_specs,
            out_specs=grp.out_spec(qw),
            scratch_shapes=[pltpu.VMEM((KV_GROUPS, grp.lk, HEAD_DIM), BF16),
                            pltpu.VMEM((KV_GROUPS, grp.lk, HEAD_DIM), BF16),
                            pltpu.VMEM((GROUP_HEADS * grp.tq, s_cols), F32)]),
        out_shape=jax.ShapeDtypeStruct((grp.rows, qw), F32),
        compiler_params=_cparams("parallel", "arbitrary"),
        name="sparse_attention_" + mode,
    )(page_table, *args)


def dsa_index_scores(grp, z, page_table, idx_pool):
    return pl.pallas_call(
        functools.partial(_index_body, widths=grp.widths(), **grp.statics()),
        grid_spec=pltpu.PrefetchScalarGridSpec(
            num_scalar_prefetch=1,
            grid=(grp.n_seq, grp.nq),
            in_specs=[grp.q_spec(IDX_HEADS * IDX_DIM, COL_QI), grp.q_spec(LANES, COL_MISC),
                      grp.seq_spec(LANES, COL_MISC)] + grp.page_specs((IDX_DIM, PAGE)),
            out_specs=grp.out_spec(grp.lk),
            scratch_shapes=[pltpu.VMEM((IDX_DIM, grp.lk), BF16)]),
        out_shape=jax.ShapeDtypeStruct((grp.rows, grp.lk), F32),
        compiler_params=_cparams("parallel", "arbitrary"),
        name="dsa_index_scores",
    )(page_table, z, z, z, *([idx_pool] * grp.n_pages))


def topk_mask(grp, scores, k):
    rows, lk = scores.shape
    tr = PAGE
    assert rows % tr == 0 and (grp.nq == 1 or grp.tq == tr)
    blk = pl.BlockSpec((tr, lk), lambda i: (i, 0))
    return pl.pallas_call(
        functools.partial(_topk_body, k=k, nq=grp.nq, tr=tr, widths=grp.widths()),
        grid=(rows // tr,),
        in_specs=[blk],
        out_specs=blk,
        out_shape=jax.ShapeDtypeStruct((rows, lk), F32),
        scratch_shapes=[pltpu.VMEM((tr, lk), jnp.int32)],
        compiler_params=_cparams("parallel"),
        name="topk_mask",
    )(scores)


def _combine_body(oc_ref, os_ref, ow_ref, od_ref, misc_ref, y_ref):
    n_heads = KV_GROUPS * GROUP_HEADS
    gates = jax.nn.sigmoid(misc_ref[:, MISC_GATES:MISC_GATES + 3 * n_heads])
    for h in range(n_heads):
        hs = slice(h * HEAD_DIM, (h + 1) * HEAD_DIM)
        o = (gates[:, 3 * h:3 * h + 1] * oc_ref[:, hs] + gates[:, 3 * h + 1:3 * h + 2] * os_ref[:, hs]
             + gates[:, 3 * h + 2:3 * h + 3] * ow_ref[:, hs])
        y_ref[:, hs] = o.astype(y_ref.dtype)
    y_ref[:, n_heads * HEAD_DIM:] = od_ref[...].astype(y_ref.dtype)


def nsa_dsa_combine(o_c, o_s, o_w, o_d, z, row0, *, tm):
    m, w = o_c.shape
    assert m % tm == 0 and row0 % tm == 0
    blk = pl.BlockSpec((tm, w), lambda i: (i, 0))
    return pl.pallas_call(
        _combine_body,
        grid=(m // tm,),
        in_specs=[blk, blk, blk, blk, pl.BlockSpec((tm, LANES), lambda i: (row0 // tm + i, COL_MISC // LANES))],
        out_specs=pl.BlockSpec((tm, 2 * w), lambda i: (i, 0)),
        out_shape=jax.ShapeDtypeStruct((m, 2 * w), BF16),
        compiler_params=_cparams("parallel"),
        name="nsa_dsa_combine",
    )(o_c, o_s, o_w, o_d, z)


def _band_tiles(rel_bias, tq):
    delta = (rel_bias[_BUCKETS] - rel_bias[N_BUCKETS - 1]).T
    rev = jnp.concatenate([delta[:, ::-1], jnp.zeros((delta.shape[0], PAGE), delta.dtype)], 1)
    tiles = jnp.stack([rev[:, PAGE - 1 - i:PAGE - 1 - i + BAND] for i in range(tq)], 1)
    return tiles.reshape(-1, GROUP_HEADS * tq, BAND)


def _widen_cd_w_in(w):
    sizes = (1024, 512, 512, 512, 24, 1024, 512, 512, 64, 8)
    q_n, kv_c, kv_s, kv_w, gates, q_d, kv_d, q_i, k_i, w_i = jnp.split(w, np.cumsum(sizes)[:-1].tolist(), axis=1)
    cols = [q_n, q_d, kv_c, kv_s, kv_w, kv_d, q_i, k_i, gates, w_i]
    used = sum(c.shape[1] for c in cols)
    return jnp.concatenate(cols + [jnp.zeros((w.shape[0], NZ - used), w.dtype)], axis=1)


def kernel(x_prompt, x_sample, state_conv, state_pool, cache_nsa_cmp, cache_nsa_sel, cache_nsa_win, cache_dsa_kv, cache_dsa_idx, page_table, norm_mix, norm_ffn, norm_final, ab_w_in, ab_conv_w, ab_conv_b, ab_ln_g, ab_ln_b, ab_pool_w, ab_pool_scale, ab_w_out, cd_w_in, cd_w_cmp, cd_w_out, rel_bias, ffn_w1, ffn_w2):
    bp, tp, d_model = x_prompt.shape
    bs, ts, _ = x_sample.shape
    mp, ms = bp * tp, bs * ts
    depth = norm_mix.shape[0]
    n_pages = page_table.shape[1]
    n_pool = cache_nsa_cmp.shape[1]
    past_len = n_pages * PAGE
    assert cache_nsa_cmp.shape[2] == PAGE
    win_len = cache_nsa_win.shape[2]
    assert win_len % PAGE == 0 and win_len == NSA_WINDOW and tp >= NSA_WINDOW
    kv_w = KV_GROUPS * 2 * HEAD_DIM

    x = jnp.concatenate([x_prompt.reshape(mp, d_model), x_sample.reshape(ms, d_model)], 0)
    grp_p = _Group(0, bp, tp, PAGE, 0, 0)
    grp_s = _Group(mp, bs, ts, ts, past_len, n_pages)
    grp_sw = _Group(mp, bs, ts, ts, past_len, win_len // PAGE)
    no_pages = jnp.zeros((1, 1), jnp.int32)
    win_pages = jnp.arange(bs * (win_len // PAGE), dtype=jnp.int32).reshape(bs, win_len // PAGE)

    outs = {k: [] for k in ("conv_p", "conv_s", "pool_p", "pool_s", "cmp_p", "cmp_s", "sel_p", "sel_s",
                            "win_p", "win_s", "dsa_p", "dsa_s", "idx_p", "idx_s")}
    y_final = None
    for i in range(depth):
        j = i // 2
        if i % 2 == 0:
            d_conv = ab_conv_w.shape[2]
            z = norm_matmul(x, norm_mix[i], ab_w_in[j].astype(BF16))
            mid_p, u_p = ab_mid_prompt(z, bp, tp, ab_conv_w[j], ab_conv_b[j], ab_ln_g[j], ab_ln_b[j],
                                       ab_pool_w[j], ab_pool_scale[j])
            mid_s, conv_s, pool_s = ab_mid_step(z, mp, bs, ts, past_len, state_conv[j], state_pool[j], ab_conv_w[j],
                                                ab_conv_b[j], ab_ln_g[j], ab_ln_b[j], ab_pool_w[j], ab_pool_scale[j])
            x = matmul_residual(jnp.concatenate([mid_p, mid_s], 0), ab_w_out[j].astype(BF16), x)
            outs["conv_p"].append(u_p.reshape(bp, tp, d_conv)[:, tp - CONV_BUF:])
            outs["conv_s"].append(conv_s)
            outs["pool_p"].append(z[:mp, 2 * d_conv:].reshape(bp, tp, -1)[:, tp - POOL_BUF:])
            outs["pool_s"].append(pool_s)
        else:
            z = norm_matmul(x, norm_mix[i], _widen_cd_w_in(cd_w_in[j]).astype(BF16))
            nsa_bias = rel_bias[:, :KV_GROUPS * GROUP_HEADS]
            band_p, band_s = _band_tiles(rel_bias, grp_p.tq), _band_tiles(rel_bias, grp_s.tq)
            wexp = jnp.tile(jnp.repeat(jnp.transpose(cd_w_cmp[j], (1, 0, 2)).reshape(NSA_BLOCK, 2 * KV_GROUPS),
                                       HEAD_DIM, axis=1), (PAGE // NSA_BLOCK, 1))
            pt = page_table + j * n_pool
            pools = [c.reshape((-1,) + KV_PAGE) for c in (cache_nsa_cmp, cache_nsa_sel, cache_dsa_kv)]
            idx_pool = jnp.swapaxes(cache_dsa_idx, 2, 3).reshape(-1, IDX_DIM, PAGE)
            win_pool = cache_nsa_win.reshape((-1,) + KV_PAGE)
            wpt = win_pages + j * bs * (win_len // PAGE)
            mids = []
            for grp, gw, ptab, wtab, band in ((grp_p, grp_p, no_pages, no_pages, band_p),
                                              (grp_s, grp_sw, pt, wpt, band_s)):
                o_c, msel = nsa_compress(grp, z, ptab, pools[0], wexp, _cmp_bias_table(nsa_bias, grp))
                o_s = sparse_attention("sel", grp, z, ptab, pools[1], band[:KV_GROUPS],
                                       q_col=COL_QN, kv_col=COL_KVS, mask=msel)
                o_w = sparse_attention("win", gw, z, wtab, win_pool, band[:KV_GROUPS],
                                       q_col=COL_QN, kv_col=COL_KVW)
                top = topk_mask(grp, dsa_index_scores(grp, z, ptab, idx_pool), min(DSA_TOPK, grp.n_keys // 4))
                o_d = sparse_attention("mask", grp, z, ptab, pools[2], band[KV_GROUPS:],
                                       q_col=COL_QD, kv_col=COL_KVD, mask=top)
                mids.append(nsa_dsa_combine(o_c, o_s, o_w, o_d, z, grp.row0, tm=min(512, grp.rows)))
            x = matmul_residual(jnp.concatenate(mids, 0), cd_w_out[j].astype(BF16), x)

            def kv_out(col, width, tail):
                seg = z[:, col:col + width]
                return seg[:mp].reshape((bp, tp) + tail), seg[mp:].reshape((bs, ts) + tail)

            kv_tail = (2, KV_GROUPS, HEAD_DIM)
            for name, col in (("cmp", COL_KVC), ("sel", COL_KVS), ("dsa", COL_KVD)):
                p_new, s_new = kv_out(col, kv_w, kv_tail)
                outs[name + "_p"].append(p_new)
                outs[name + "_s"].append(s_new)
            w_p, w_s = kv_out(COL_KVW, kv_w, kv_tail)
            outs["win_p"].append(w_p[:, tp - NSA_WINDOW:])
            outs["win_s"].append(jnp.concatenate([cache_nsa_win[j], w_s], 1)[:, ts:])
            i_p, i_s = kv_out(COL_MISC + MISC_KI, IDX_DIM, (IDX_DIM,))
            outs["idx_p"].append(i_p)
            outs["idx_s"].append(i_s)
        a = norm_matmul(x, norm_ffn[i], ffn_w1[i].astype(BF16), relu2=True, out_dtype=BF16)
        if i == depth - 1:
            x, y_final = matmul_residual(a, ffn_w2[i].astype(BF16), x, norm_final)
        else:
            x = matmul_residual(a, ffn_w2[i].astype(BF16), x)

    st = {k: jnp.stack(v) for k, v in outs.items()}
    return (y_final[:mp].reshape(bp, tp, d_model), y_final[mp:].reshape(bs, ts, d_model),
            st["conv_p"], st["conv_s"], st["pool_p"], st["pool_s"], st["cmp_p"], st["cmp_s"],
            st["sel_p"], st["sel_s"], st["win_p"], st["win_s"], st["dsa_p"], st["dsa_s"],
            st["idx_p"], st["idx_s"])
```

```python
import functools
import math

import numpy as np
import jax
import jax.numpy as jnp
from jax import lax
from jax.experimental import pallas as pl
from jax.experimental.pallas import tpu as pltpu

F32 = jnp.float32
BF16 = jnp.bfloat16

EPS = 1e-6
NEG_INF = -1e30
HEAD_DIM = 128
LANES = 128
CONV_WIDTH = 31
CONV_BUF = CONV_WIDTH - 1
POOL_WINDOWS = (2, 4, 8, 16)
POOL_BUF = max(POOL_WINDOWS) - 1
HALO = 32
VMEM_LIMIT = 56 * 1024 * 1024


def _cparams(*sem):
    return pltpu.CompilerParams(dimension_semantics=sem, vmem_limit_bytes=VMEM_LIMIT)


def _norm_matmul_body(x_ref, g_ref, w_ref, o_ref, h_ref, *, relu2):
    @pl.when(pl.program_id(1) == 0)
    def _():
        x = x_ref[...]
        y = x * lax.rsqrt(jnp.mean(x * x, -1, keepdims=True) + EPS)
        h_ref[...] = (y * g_ref[...]).astype(BF16)

    y = jnp.dot(h_ref[...], w_ref[...], preferred_element_type=F32)
    if relu2:
        y = jnp.square(jnp.maximum(y, 0.0))
    o_ref[...] = y.astype(o_ref.dtype)


def norm_matmul(x, g, w, *, relu2=False, out_dtype=F32, tm=1024, tn=1024):
    m, d = x.shape
    n = w.shape[1]
    assert m % tm == 0 and n % tn == 0
    return pl.pallas_call(
        functools.partial(_norm_matmul_body, relu2=relu2),
        grid=(m // tm, n // tn),
        in_specs=[pl.BlockSpec((tm, d), lambda i, j: (i, 0)),
                  pl.BlockSpec((1, d), lambda i, j: (0, 0)),
                  pl.BlockSpec((d, tn), lambda i, j: (0, j))],
        out_specs=pl.BlockSpec((tm, tn), lambda i, j: (i, j)),
        out_shape=jax.ShapeDtypeStruct((m, n), out_dtype),
        scratch_shapes=[pltpu.VMEM((tm, d), BF16)],
        compiler_params=_cparams("parallel", "arbitrary"),
        name="norm_matmul",
    )(x, g.reshape(1, d), w)


def _matmul_residual_body(a_ref, w_ref, r_ref, *rest, final_norm):
    if final_norm:
        g_ref, o_ref, n_ref, acc_ref = rest
    else:
        o_ref, acc_ref = rest
    k = pl.program_id(1)

    @pl.when(k == 0)
    def _():
        acc_ref[...] = jnp.zeros_like(acc_ref)

    acc_ref[...] += jnp.dot(a_ref[...], w_ref[...], preferred_element_type=F32)

    @pl.when(k == pl.num_programs(1) - 1)
    def _():
        o = r_ref[...] + acc_ref[...]
        o_ref[...] = o
        if final_norm:
            y = o * lax.rsqrt(jnp.mean(o * o, -1, keepdims=True) + EPS)
            n_ref[...] = y * g_ref[...]


def matmul_residual(a, w, r, g_final=None, *, tm=512, tk=2048):
    m, kdim = a.shape
    n = w.shape[1]
    assert m % tm == 0 and kdim % tk == 0
    final_norm = g_final is not None
    in_specs = [pl.BlockSpec((tm, tk), lambda i, k: (i, k)),
                pl.BlockSpec((tk, n), lambda i, k: (k, 0)),
                pl.BlockSpec((tm, n), lambda i, k: (i, 0))]
    out_spec = pl.BlockSpec((tm, n), lambda i, k: (i, 0))
    out_shape = jax.ShapeDtypeStruct((m, n), F32)
    args = [a, w, r]
    if final_norm:
        in_specs.append(pl.BlockSpec((1, n), lambda i, k: (0, 0)))
        args.append(g_final.reshape(1, n))
        out_spec, out_shape = [out_spec, out_spec], [out_shape, out_shape]
    return pl.pallas_call(
        functools.partial(_matmul_residual_body, final_norm=final_norm),
        grid=(m // tm, kdim // tk),
        in_specs=in_specs,
        out_specs=out_spec,
        out_shape=out_shape,
        scratch_shapes=[pltpu.VMEM((tm, n), F32)],
        compiler_params=_cparams("parallel", "arbitrary"),
        name="matmul_residual",
    )(*args)


def _layernorm_silu(c, g, b):
    mu = jnp.mean(c, -1, keepdims=True)
    xc = c - mu
    y = xc * lax.rsqrt(jnp.mean(xc * xc, -1, keepdims=True) + EPS)
    y = y * g + b
    return y * jax.nn.sigmoid(y)


def _ab_mid_body(z_ref, zp_ref, cw_ref, cb_ref, lg_ref, lb_ref, pw_ref, ps_ref, y_ref, u_ref,
                 ext_ref, vext_ref, conv_ref, *, tt, d_conv, d_pool):
    ti = pl.program_id(1)
    keep = (ti > 0).astype(F32)
    a_p = zp_ref[:, 0:d_conv]
    g_p = zp_ref[:, d_conv:2 * d_conv]
    ext_ref[0:HALO, :] = a_p * jax.nn.sigmoid(g_p) * keep
    vext_ref[0:HALO, :] = zp_ref[:, 2 * d_conv:] * keep
    u = z_ref[:, 0:d_conv] * jax.nn.sigmoid(z_ref[:, d_conv:2 * d_conv])
    ext_ref[HALO:, :] = u
    u_ref[...] = u
    vext_ref[HALO:, :] = z_ref[:, 2 * d_conv:]

    off = HALO - CONV_BUF
    for c in range(d_conv // LANES):
        cs = slice(c * LANES, (c + 1) * LANES)
        acc = jnp.zeros((tt, LANES), F32)
        for j in range(CONV_WIDTH):
            acc = acc + cw_ref[j:j + 1, cs] * ext_ref[off + j:off + j + tt, cs]
        conv_ref[:, cs] = acc + cb_ref[:, cs]
    y_ref[:, 0:d_conv] = _layernorm_silu(conv_ref[...], lg_ref[...], lb_ref[...]).astype(y_ref.dtype)

    pos = ti * tt + lax.broadcasted_iota(jnp.int32, (tt, 1), 0)
    pg = d_pool // len(POOL_WINDOWS)
    for gi, w in enumerate(POOL_WINDOWS):
        gs = slice(gi * pg, (gi + 1) * pg)
        tok = vext_ref[HALO:, gs]
        acc = tok
        for i in range(1, w):
            acc = acc + vext_ref[HALO - i:HALO - i + tt, gs]
        cnt = jnp.minimum(pos + 1, w).astype(F32)
        d = acc / cnt - tok
        yp = jnp.dot(d.astype(BF16), pw_ref[gi], preferred_element_type=F32) * ps_ref[:, gs]
        y_ref[:, d_conv + gi * pg:d_conv + (gi + 1) * pg] = yp.astype(y_ref.dtype)


def ab_mid_prompt(z, n_seq, t_len, conv_w, conv_b, ln_g, ln_b, pool_w, pool_scale, *, tt=256):
    d_conv = conv_w.shape[1]
    d_pool = pool_scale.shape[0]
    nt = t_len // tt
    hb = tt // HALO
    row = lambda b, t: (b * nt + t, 0)
    const = lambda b, t: (0, 0)
    return pl.pallas_call(
        functools.partial(_ab_mid_body, tt=tt, d_conv=d_conv, d_pool=d_pool),
        grid=(n_seq, nt),
        in_specs=[pl.BlockSpec((tt, z.shape[1]), row),
                  pl.BlockSpec((HALO, z.shape[1]), lambda b, t: (jnp.maximum((b * nt + t) * hb - 1, 0), 0)),
                  pl.BlockSpec(conv_w.shape, const),
                  pl.BlockSpec((1, d_conv), const),
                  pl.BlockSpec((1, d_conv), const),
                  pl.BlockSpec((1, d_conv), const),
                  pl.BlockSpec(pool_w.shape, lambda b, t: (0, 0, 0)),
                  pl.BlockSpec((1, d_pool), const)],
        out_specs=[pl.BlockSpec((tt, d_conv + d_pool), row),
                   pl.BlockSpec((tt, d_conv), row)],
        out_shape=[jax.ShapeDtypeStruct((n_seq * t_len, d_conv + d_pool), BF16),
                   jax.ShapeDtypeStruct((n_seq * t_len, d_conv), F32)],
        scratch_shapes=[pltpu.VMEM((HALO + tt, d_conv), F32),
                        pltpu.VMEM((HALO + tt, d_pool), F32),
                        pltpu.VMEM((tt, d_conv), F32)],
        compiler_params=_cparams("parallel", "parallel"),
        name="ab_mid_prompt",
    )(z, z, conv_w, conv_b.reshape(1, -1), ln_g.reshape(1, -1), ln_b.reshape(1, -1),
      pool_w.astype(BF16), pool_scale.reshape(1, -1))


def _ab_mid_step_body(z_ref, sc_ref, sp_ref, cw_ref, cb_ref, lg_ref, lb_ref, pw_ref, ps_ref,
                      y_ref, nc_ref, np_ref, ext_ref, vext_ref, *, nb, t, pos0, d_conv, d_pool):
    e0 = HALO - CONV_BUF
    p0 = 16 - POOL_BUF
    z = z_ref[...].reshape(nb, t, z_ref.shape[1])
    u = z[:, :, 0:d_conv] * jax.nn.sigmoid(z[:, :, d_conv:2 * d_conv])
    ext_ref[:, e0:HALO, :] = sc_ref[...]
    ext_ref[:, HALO:, :] = u
    vext_ref[:, p0:16, :] = sp_ref[...]
    vext_ref[:, 16:, :] = z[:, :, 2 * d_conv:]
    nc_ref[...] = ext_ref[:, HALO + t - CONV_BUF:, :]
    np_ref[...] = vext_ref[:, 16 + t - POOL_BUF:, :]

    acc = jnp.zeros((nb, t, d_conv), F32)
    for j in range(CONV_WIDTH):
        acc = acc + cw_ref[j:j + 1, :][None] * ext_ref[:, e0 + j:e0 + j + t, :]
    c = acc + cb_ref[...][None]
    yc = _layernorm_silu(c, lg_ref[...][None], lb_ref[...][None])
    y_ref[:, 0:d_conv] = yc.reshape(nb * t, d_conv).astype(y_ref.dtype)

    pg = d_pool // len(POOL_WINDOWS)
    for gi, w in enumerate(POOL_WINDOWS):
        gs = slice(gi * pg, (gi + 1) * pg)
        tok = vext_ref[:, 16:, gs]
        acc = tok
        for i in range(1, w):
            acc = acc + vext_ref[:, 16 - i:16 - i + t, gs]
        cnt = jnp.minimum(pos0 + 1 + lax.broadcasted_iota(jnp.int32, (1, t, 1), 1), w).astype(F32)
        d = (acc / cnt - tok).reshape(nb * t, pg)
        yp = jnp.dot(d.astype(BF16), pw_ref[gi], preferred_element_type=F32) * ps_ref[:, gs]
        y_ref[:, d_conv + gi * pg:d_conv + (gi + 1) * pg] = yp.astype(y_ref.dtype)


def ab_mid_step(z, row0, n_seq, t, pos0, state_conv, state_pool, conv_w, conv_b, ln_g, ln_b, pool_w,
                pool_scale, *, nb=16):
    d_conv = conv_w.shape[1]
    d_pool = pool_scale.shape[0]
    rb = nb * t
    assert row0 % rb == 0 and n_seq % nb == 0
    const = lambda i: (0, 0)
    seq3 = lambda i: (i, 0, 0)
    return pl.pallas_call(
        functools.partial(_ab_mid_step_body, nb=nb, t=t, pos0=pos0, d_conv=d_conv, d_pool=d_pool),
        grid=(n_seq // nb,),
        in_specs=[pl.BlockSpec((rb, z.shape[1]), lambda i: (row0 // rb + i, 0)),
                  pl.BlockSpec((nb, CONV_BUF, d_conv), seq3),
                  pl.BlockSpec((nb, POOL_BUF, d_pool), seq3),
                  pl.BlockSpec(conv_w.shape, const),
                  pl.BlockSpec((1, d_conv), const),
                  pl.BlockSpec((1, d_conv), const),
                  pl.BlockSpec((1, d_conv), const),
                  pl.BlockSpec(pool_w.shape, lambda i: (0, 0, 0)),
                  pl.BlockSpec((1, d_pool), const)],
        out_specs=[pl.BlockSpec((rb, d_conv + d_pool), lambda i: (i, 0)),
                   pl.BlockSpec((nb, CONV_BUF, d_conv), seq3),
                   pl.BlockSpec((nb, POOL_BUF, d_pool), seq3)],
        out_shape=[jax.ShapeDtypeStruct((n_seq * t, d_conv + d_pool), BF16),
                   jax.ShapeDtypeStruct((n_seq, CONV_BUF, d_conv), F32),
                   jax.ShapeDtypeStruct((n_seq, POOL_BUF, d_pool), F32)],
        scratch_shapes=[pltpu.VMEM((nb, HALO + t, d_conv), F32),
                        pltpu.VMEM((nb, 16 + t, d_pool), F32)],
        compiler_params=_cparams("parallel"),
        name="ab_mid_step",
    )(z, state_conv, state_pool, conv_w, conv_b.reshape(1, -1), ln_g.reshape(1, -1),
      ln_b.reshape(1, -1), pool_w.astype(BF16), pool_scale.reshape(1, -1))


N_BUCKETS = 32
MAX_DISTANCE = 128
NSA_BLOCK = 64
NSA_TOPN = 16
NSA_WINDOW = 512
DSA_TOPK = 256
IDX_HEADS = 8
IDX_DIM = 64
KV_GROUPS = 2
GROUP_HEADS = 4
PAGE = 128
BAND = 2 * PAGE
INT_MIN = -2 ** 31
KV_PAGE = (2 * KV_GROUPS * PAGE, HEAD_DIM)
TOPK_ROW_GROUPS = 4
CMP_BIAS_LANE0 = 64

COL_QN, COL_QD, COL_KVC, COL_KVS, COL_KVW, COL_KVD, COL_QI, COL_MISC = 0, 1024, 2048, 2560, 3072, 3584, 4096, 4608
MISC_KI, MISC_GATES, MISC_WI = 0, 64, 88
NZ = 5120


def _bucket_np(n):
    n = np.maximum(np.asarray(n, np.int32), 0)
    exact = N_BUCKETS // 2
    nf = np.maximum(n, 1).astype(np.float32)
    big = exact + (np.log(nf / np.float32(exact)) / np.float32(math.log(MAX_DISTANCE / exact))
                   * np.float32(N_BUCKETS - exact)).astype(np.int32)
    return np.where(n < exact, n, np.minimum(big, N_BUCKETS - 1))


_BUCKETS = _bucket_np(np.arange(BAND))
assert _BUCKETS[PAGE:].min() == N_BUCKETS - 1


def _softmax_rows(s, mask):
    s = jnp.where(mask, s, NEG_INF)
    m = jnp.max(s, -1, keepdims=True)
    p = jnp.where(mask, jnp.exp(s - m), 0.0)
    return p, jnp.sum(p, -1, keepdims=True)


def _dot_nt(a, b):
    return lax.dot_general(a, b, (((1,), (1,)), ((), ())), preferred_element_type=F32)


def _new_chunks(new_ref, t_new):
    chunks = [new_ref[c * PAGE:(c + 1) * PAGE, :] for c in range(t_new // PAGE)]
    rem = t_new % PAGE
    if rem:
        tail = new_ref[(t_new // PAGE) * PAGE:, :]
        chunks.append(jnp.concatenate([tail, jnp.zeros((PAGE - rem, tail.shape[1]), F32)], 0))
    return chunks


def _kv_chunks(page_refs, new_ref, t_new):
    n_parts = 2 * KV_GROUPS
    chunks = [[r[0, pl.ds(part, PAGE, stride=n_parts), :] for part in range(n_parts)] for r in page_refs]
    for x in _new_chunks(new_ref, t_new):
        chunks.append([x[:, part * HEAD_DIM:(part + 1) * HEAD_DIM] for part in range(n_parts)])
    return chunks


def _cmp_body(pt_ref, q_ref, kvn_ref, *rest, n_pages, t_new, tq, pos0):
    del pt_ref
    page_refs = rest[:n_pages]
    wexp_ref, bias_ref, o_ref, msel_ref, comp_ref, ck_ref, cv_ref = rest[n_pages:]
    qi = pl.program_id(1)
    n_keys = n_pages * PAGE + t_new
    n_cmp = n_keys // NSA_BLOCK
    n_sel = -(-n_keys // NSA_BLOCK)
    per = PAGE // NSA_BLOCK

    @pl.when(qi == 0)
    def _():
        comp_ref[...] = jnp.zeros_like(comp_ref)
        chunks = _kv_chunks(page_refs, kvn_ref, t_new)[:n_cmp // per]
        for part in range(2 * KV_GROUPS):
            cols = slice(part * HEAD_DIM, (part + 1) * HEAD_DIM)
            xw = jnp.concatenate([parts[part] * wexp_ref[:, cols] for parts in chunks], 0)
            comp_ref[0:per * len(chunks), cols] = xw.reshape(per * len(chunks), NSA_BLOCK, HEAD_DIM).sum(1)
        for g in range(KV_GROUPS):
            ck_ref[g] = comp_ref[:, g * HEAD_DIM:(g + 1) * HEAD_DIM].astype(BF16)
            cv_ref[g] = comp_ref[:, (KV_GROUPS + g) * HEAD_DIM:(KV_GROUPS + g + 1) * HEAD_DIM].astype(BF16)

    scale = HEAD_DIM ** -0.5
    rows = GROUP_HEADS * tq
    blk = lax.broadcasted_iota(jnp.int32, (1, LANES), 1)
    q0 = pos0 + qi * tq
    assert tq & (tq - 1) == 0
    qpos_st = q0 + (lax.broadcasted_iota(jnp.int32, (rows, 1), 0) & (tq - 1))
    mask = (qpos_st - ((blk + 1) * NSA_BLOCK - 1) >= 0) & (blk < n_cmp)
    cur = (q0 + lax.broadcasted_iota(jnp.int32, (tq, 1), 0)) // NSA_BLOCK
    scores = []
    for g in range(KV_GROUPS):
        heads = [g * GROUP_HEADS + r for r in range(GROUP_HEADS)]
        bias = bias_ref[g]
        if t_new != tq:
            bias = pltpu.roll(bias, qi * (tq // NSA_BLOCK) + (LANES - CMP_BIAS_LANE0), 1)
        q = jnp.concatenate([q_ref[:, h * HEAD_DIM:(h + 1) * HEAD_DIM] for h in heads], 0).astype(BF16)
        scores.append(_dot_nt(q, ck_ref[g]) * scale + bias)
    probs = []
    for g in range(KV_GROUPS):
        p, l = _softmax_rows(scores[g], mask)
        probs.append(p / jnp.maximum(l, 1e-30))
    for g in range(KV_GROUPS):
        o = jnp.dot(probs[g].astype(BF16), cv_ref[g], preferred_element_type=F32)
        for r in range(GROUP_HEADS):
            h = g * GROUP_HEADS + r
            o_ref[:, h * HEAD_DIM:(h + 1) * HEAD_DIM] = o[r * tq:(r + 1) * tq]
    for g in range(KV_GROUPS):
        imp = probs[g][0:tq]
        for r in range(1, GROUP_HEADS):
            imp = imp + probs[g][r * tq:(r + 1) * tq]
        imp = jnp.where(blk == cur, 2.0, jnp.where(blk > cur, -1.0, imp))
        imp = jnp.where(blk < n_sel, imp, -2.0)
        n_top = min(NSA_TOPN, n_sel)
        cols = slice(g * LANES, (g + 1) * LANES)

        def by_rank(imp=imp, cols=cols):
            rank = jnp.zeros((tq, LANES), F32)
            for i in range(n_sel):
                col = imp[:, i:i + 1]
                ahead = (col > imp) | ((col == imp) & (blk > i))
                rank = rank + jnp.where(ahead, 1.0, 0.0)
            msel_ref[:, cols] = jnp.where((rank < float(n_top)) & (blk < n_sel), 1.0, 0.0)

        def first_blocks(cols=cols):
            msel_ref[:, cols] = jnp.where(blk < n_top, 1.0, 0.0) + jnp.zeros((tq, LANES), F32)

        if t_new == tq:
            if pos0 + tq <= n_top * NSA_BLOCK:
                first_blocks()
            else:
                by_rank()
        else:
            early = pos0 + (qi + 1) * tq <= n_top * NSA_BLOCK
            pl.when(early)(first_blocks)
            pl.when(jnp.logical_not(early))(by_rank)


def _on_causal_width(qi, tq, widths, tile):
    if len(widths) == 1:
        tile(widths[0], True)
        return
    need = (qi * tq + tq - 1) // widths[0]
    for nw, w in enumerate(widths):
        pl.when(need == nw)(functools.partial(tile, w, nw == 0))


def _attn_body(pt_ref, q_ref, kvn_ref, *rest, mode, n_pages, t_new, tq, pos0, widths):
    del pt_ref
    page_refs = rest[:n_pages]
    rest = rest[n_pages:]
    m_ref = None
    if mode in ("sel", "mask"):
        m_ref, rest = rest[0], rest[1:]
    band_ref, o_ref, kc_ref, vc_ref, s_ref = rest
    qi = pl.program_id(1)
    single = t_new == tq
    n_keys = n_pages * PAGE + t_new
    kbase = pos0 - n_pages * PAGE
    scale = HEAD_DIM ** -0.5
    q0 = pos0 if single else pos0 + qi * tq

    @pl.when(qi == 0)
    def _():
        for c, parts in enumerate(_kv_chunks(page_refs, kvn_ref, t_new)):
            rows = slice(c * PAGE, (c + 1) * PAGE)
            for g in range(KV_GROUPS):
                kc_ref[g, rows, :] = parts[g].astype(BF16)
                vc_ref[g, rows, :] = parts[KV_GROUPS + g].astype(BF16)

    def tile(c0, w, band_at, maybe_first):
        qpos = q0 + lax.broadcasted_iota(jnp.int32, (tq, 1), 0)
        col = c0 + lax.broadcasted_iota(jnp.int32, (1, w), 1)
        dist = qpos - (kbase + col)
        visible = (dist >= 0) & (col < n_keys)
        if mode == "win":
            visible = visible & (dist < NSA_WINDOW)
        if mode == "mask":
            visible = visible & (m_ref[:, 0:w] > 0.5)
        keys = pl.ds(c0, w)
        groups = range(KV_GROUPS)
        masks = []
        for g in groups:
            if mode == "sel":
                expand = (lax.broadcasted_iota(jnp.int32, (LANES, w), 1) // NSA_BLOCK
                          == lax.broadcasted_iota(jnp.int32, (LANES, w), 0))
                chosen = jnp.dot(m_ref[:, g * LANES:(g + 1) * LANES].astype(BF16),
                                 jnp.where(expand, 1.0, 0.0).astype(BF16), preferred_element_type=F32)
                masks.append(visible & (chosen > 0.5))
            else:
                masks.append(visible)
            heads = [g * GROUP_HEADS + r for r in range(GROUP_HEADS)]
            q = jnp.concatenate([q_ref[:, h * HEAD_DIM:(h + 1) * HEAD_DIM] for h in heads], 0).astype(BF16)
            s_ref[g, :, 0:w] = _dot_nt(q, kc_ref[g, keys, :]) * scale
        for g in groups:
            if band_at is not None:
                s_ref[g, :, band_at:band_at + BAND] += band_ref[g]
            else:
                if maybe_first:
                    @pl.when(qi == 0)
                    def _():
                        s_ref[g, :, 0:PAGE] += band_ref[g, :, PAGE:]

                @pl.when(qi > 0)
                def _():
                    s_ref[g, :, pl.ds(pl.multiple_of(q0 - PAGE - kbase, PAGE), BAND)] += band_ref[g]
        sums = []
        for g in groups:
            for r in range(GROUP_HEADS):
                rows = slice(r * tq, (r + 1) * tq)
                p, l = _softmax_rows(s_ref[g, rows, 0:w], masks[g])
                s_ref[g, rows, 0:w] = p
                sums.append(l)
        for g in groups:
            o = jnp.dot(s_ref[g, :, 0:w].astype(BF16), vc_ref[g, keys, :], preferred_element_type=F32)
            for r in range(GROUP_HEADS):
                h = g * GROUP_HEADS + r
                o_ref[:, h * HEAD_DIM:(h + 1) * HEAD_DIM] = o[r * tq:(r + 1) * tq] / jnp.maximum(sums[h], 1e-30)

    if single:
        tile(0, widths[0], pos0 - PAGE - kbase, False)
    elif mode == "win":
        wch = NSA_WINDOW // PAGE
        pl.when(qi < wch)(functools.partial(tile, 0, NSA_WINDOW, None, True))
        pl.when(qi >= wch)(lambda: tile(pl.multiple_of((qi - wch) * PAGE, PAGE), NSA_WINDOW + PAGE,
                                        NSA_WINDOW - PAGE, False))
    else:
        _on_causal_width(qi, tq, widths, lambda w, first: tile(0, w, None, first))


def _index_body(pt_ref, qidx_ref, miscq_ref, misck_ref, *rest, n_pages, t_new, tq, pos0, widths):
    del pt_ref
    ipage_refs = rest[:n_pages]
    o_ref, kidx_ref = rest[n_pages:]
    qi = pl.program_id(1)
    lk = o_ref.shape[1]
    n_keys = n_pages * PAGE + t_new
    kbase = pos0 - n_pages * PAGE
    q0 = pos0 if t_new == tq else pos0 + qi * tq

    @pl.when(qi == 0)
    def _():
        for c, r in enumerate(ipage_refs):
            kidx_ref[:, c * PAGE:(c + 1) * PAGE] = r[0].astype(BF16)
        for c, x in enumerate(_new_chunks(misck_ref, t_new)):
            cols = slice((n_pages + c) * PAGE, (n_pages + c + 1) * PAGE)
            kidx_ref[:, cols] = x.T[MISC_KI:MISC_KI + IDX_DIM, :].astype(BF16)

    def tile(w, maybe_first):
        del maybe_first
        qpos = q0 + lax.broadcasted_iota(jnp.int32, (tq, 1), 0)
        col = lax.broadcasted_iota(jnp.int32, (1, w), 1)
        visible = (qpos - (kbase + col) >= 0) & (col < n_keys)
        q = jnp.concatenate([qidx_ref[:, hh * IDX_DIM:(hh + 1) * IDX_DIM] for hh in range(IDX_HEADS)], 0)
        sc = jnp.dot(q.astype(BF16), kidx_ref[:, 0:w], preferred_element_type=F32)
        score = jnp.zeros((tq, w), F32)
        for hh in range(IDX_HEADS):
            wi = miscq_ref[:, MISC_WI + hh:MISC_WI + hh + 1] * (IDX_HEADS ** -0.5)
            score = score + jnp.maximum(sc[hh * tq:(hh + 1) * tq] * (IDX_DIM ** -0.5), 0.0) * wi
        o_ref[:, 0:w] = jnp.where(visible, score, NEG_INF)
        if w < lk:
            o_ref[:, w:] = jnp.full((tq, lk - w), NEG_INF, F32)

    _on_causal_width(qi, tq, widths, tile)


def _topk_body(s_ref, m_ref, key_ref, *, k, nq, tr, widths):
    lk = s_ref.shape[1]
    assert lk <= 4096
    neg_key = int(np.array(NEG_INF, np.float32).view(np.int32)) ^ 0x7FFFFFFF
    kf = float(k)

    def tile(w, maybe_first):
        del maybe_first
        bits = lax.bitcast_convert_type(s_ref[:, 0:w] + 0.0, jnp.int32)
        key_ref[:, 0:w] = jnp.where(bits >= 0, bits, bits ^ 0x7FFFFFFF)
        col = lax.broadcasted_iota(jnp.int32, (1, w), 1)
        unseen = float(lk - w)

        groups = [slice(a * (tr // TOPK_ROW_GROUPS), (a + 1) * (tr // TOPK_ROW_GROUPS)) for a in range(TOPK_ROW_GROUPS)]
        zeros = tuple(jnp.zeros((tr // TOPK_ROW_GROUPS, 1), jnp.int32) for _ in groups)

        def thr_step(i, tus):
            out = []
            for rows, tu in zip(groups, tus):
                cand = tu | jnp.left_shift(jnp.int32(1), 31 - i)
                cs = cand ^ INT_MIN
                cnt = jnp.sum(jnp.where(key_ref[rows, 0:w] >= cs, 1.0, 0.0), -1, keepdims=True)
                cnt = cnt + jnp.where(cs <= neg_key, unseen, 0.0)
                out.append(jnp.where(cnt >= kf, cand, tu))
            return tuple(out)

        thr = jnp.concatenate(lax.fori_loop(0, 32, thr_step, zeros, unroll=4), 0) ^ INT_MIN
        key = key_ref[:, 0:w]
        above = key > thr
        tied = key == thr
        need = kf - jnp.sum(jnp.where(above, 1.0, 0.0), -1, keepdims=True)

        def tie_step(i, j0s):
            out = []
            for rows, j0 in zip(groups, j0s):
                cand = j0 | jnp.left_shift(jnp.int32(1), 11 - i)
                hit = (key_ref[rows, 0:w] == thr[rows]) & (col < cand)
                cnt = jnp.sum(jnp.where(hit, 1.0, 0.0), -1, keepdims=True)
                out.append(jnp.where(cnt < need[rows], cand, j0))
            return tuple(out)

        j0 = jnp.concatenate(lax.fori_loop(0, 12, tie_step, zeros, unroll=4), 0)
        m_ref[:, 0:w] = jnp.where(above | (tied & (col <= j0)), 1.0, 0.0)
        if w < lk:
            m_ref[:, w:] = jnp.zeros((tr, lk - w), F32)

    _on_causal_width(pl.program_id(0) % nq, tr, widths, tile)


class _Group:
    def __init__(self, row0, n_seq, t_new, tq, pos0, n_pages):
        assert t_new % tq == 0 and row0 % tq == 0 and row0 % t_new == 0
        assert pos0 == n_pages * PAGE or n_pages * PAGE < pos0
        assert t_new == tq or (tq == PAGE and pos0 == 0)
        assert t_new % PAGE == 0 or t_new % PAGE < NSA_BLOCK
        self.row0, self.n_seq, self.t_new, self.tq, self.pos0, self.n_pages = row0, n_seq, t_new, tq, pos0, n_pages
        self.nq = t_new // tq
        self.rows = n_seq * t_new
        self.lk = (n_pages + -(-t_new // PAGE)) * PAGE
        self.n_keys = n_pages * PAGE + t_new

    def q_spec(self, width, col):
        return pl.BlockSpec((self.tq, width), lambda b, qi, pt: (self.row0 // self.tq + b * self.nq + qi, col // width))

    def seq_spec(self, width, col):
        return pl.BlockSpec((self.t_new, width), lambda b, qi, pt: (self.row0 // self.t_new + b, col // width))

    def page_specs(self, shape):
        return [pl.BlockSpec((1,) + shape, lambda b, qi, pt, p=p: (pt[b, p], 0, 0)) for p in range(self.n_pages)]

    def out_spec(self, width):
        return pl.BlockSpec((self.tq, width), lambda b, qi, pt: (b * self.nq + qi, 0))

    def statics(self):
        return dict(n_pages=self.n_pages, t_new=self.t_new, tq=self.tq, pos0=self.pos0)

    def widths(self):
        if self.nq == 1:
            return (self.lk,)
        step = 4 * PAGE
        assert self.lk % step == 0
        return tuple(range(step, self.lk + 1, step))


def _cmp_bias_table(rel_bias, grp):
    lane0 = 0 if grp.nq == 1 else CMP_BIAS_LANE0
    tab = rel_bias[_BUCKETS]
    pieces, n_far = [], 0
    for lane in range(LANES + 1):
        d0 = grp.pos0 - (NSA_BLOCK * (lane - lane0 + 1) - 1)
        plain = lane < LANES and (d0 >= PAGE or d0 + grp.tq - 1 < 0)
        if plain:
            n_far += 1
            continue
        if n_far:
            pieces.append(jnp.broadcast_to(rel_bias[N_BUCKETS - 1], (grp.tq, n_far, rel_bias.shape[1])))
            n_far = 0
        if lane < LANES:
            pieces.append(tab[np.clip(d0 + np.arange(grp.tq), 0, BAND - 1)][:, None, :])
    table = jnp.transpose(jnp.concatenate(pieces, 1), (2, 0, 1))
    return table.reshape(-1, GROUP_HEADS * grp.tq, LANES)


def nsa_compress(grp, z, page_table, pool, wexp, bias):
    kv_w = KV_GROUPS * 2 * HEAD_DIM
    qw = KV_GROUPS * GROUP_HEADS * HEAD_DIM
    const2 = lambda b, qi, pt: (0, 0)
    return pl.pallas_call(
        functools.partial(_cmp_body, **grp.statics()),
        grid_spec=pltpu.PrefetchScalarGridSpec(
            num_scalar_prefetch=1,
            grid=(grp.n_seq, grp.nq),
            in_specs=[grp.q_spec(qw, COL_QN), grp.seq_spec(kv_w, COL_KVC)] + grp.page_specs(KV_PAGE)
            + [pl.BlockSpec((PAGE, kv_w), const2), pl.BlockSpec(bias.shape, lambda b, qi, pt: (0, 0, 0))],
            out_specs=[grp.out_spec(qw), grp.out_spec(KV_GROUPS * LANES)],
            scratch_shapes=[pltpu.VMEM((LANES, kv_w), F32),
                            pltpu.VMEM((KV_GROUPS, LANES, HEAD_DIM), BF16),
                            pltpu.VMEM((KV_GROUPS, LANES, HEAD_DIM), BF16)]),
        out_shape=[jax.ShapeDtypeStruct((grp.rows, qw), F32),
                   jax.ShapeDtypeStruct((grp.rows, KV_GROUPS * LANES), F32)],
        compiler_params=_cparams("parallel", "arbitrary"),
        name="nsa_compress",
    )(page_table, z, z, *([pool] * grp.n_pages), wexp, bias)


def sparse_attention(mode, grp, z, page_table, pool, band, *, q_col, kv_col, mask=None):
    kv_w = KV_GROUPS * 2 * HEAD_DIM
    qw = KV_GROUPS * GROUP_HEADS * HEAD_DIM
    in_specs = [grp.q_spec(qw, q_col), grp.seq_spec(kv_w, kv_col)] + grp.page_specs(KV_PAGE)
    args = [z, z] + [pool] * grp.n_pages
    if mode in ("sel", "mask"):
        in_specs.append(grp.out_spec(mask.shape[1]))
        args.append(mask)
    in_specs.append(pl.BlockSpec((KV_GROUPS, GROUP_HEADS * grp.tq, BAND), lambda b, qi, pt: (0, 0, 0)))
    args.append(band)
    widths = grp.widths()
    s_cols = max(widths) if (mode != "win" or grp.nq == 1) else NSA_WINDOW + PAGE
    return pl.pallas_call(
        functools.partial(_attn_body, mode=mode, widths=widths, **grp.statics()),
        grid_spec=pltpu.PrefetchScalarGridSpec(
            num_scalar_prefetch=1,
            grid=(grp.n_seq, grp.nq),
            in_specs=in_specs,
            out_specs=grp.out_spec(qw),
            scratch_shapes=[pltpu.VMEM((KV_GROUPS, grp.lk, HEAD_DIM), BF16),
                            pltpu.VMEM((KV_GROUPS, grp.lk, HEAD_DIM), BF16),
                            pltpu.VMEM((KV_GROUPS, GROUP_HEADS * grp.tq, s_cols), F32)]),
        out_shape=jax.ShapeDtypeStruct((grp.rows, qw), F32),
        compiler_params=_cparams("parallel", "arbitrary"),
        name="sparse_attention_" + mode,
    )(page_table, *args)


def dsa_index_scores(grp, z, page_table, idx_pool):
    return pl.pallas_call(
        functools.partial(_index_body, widths=grp.widths(), **grp.statics()),
        grid_spec=pltpu.PrefetchScalarGridSpec(
            num_scalar_prefetch=1,
            grid=(grp.n_seq, grp.nq),
            in_specs=[grp.q_spec(IDX_HEADS * IDX_DIM, COL_QI), grp.q_spec(LANES, COL_MISC),
                      grp.seq_spec(LANES, COL_MISC)] + grp.page_specs((IDX_DIM, PAGE)),
            out_specs=grp.out_spec(grp.lk),
            scratch_shapes=[pltpu.VMEM((IDX_DIM, grp.lk), BF16)]),
        out_shape=jax.ShapeDtypeStruct((grp.rows, grp.lk), F32),
        compiler_params=_cparams("parallel", "arbitrary"),
        name="dsa_index_scores",
    )(page_table, z, z, z, *([idx_pool] * grp.n_pages))


def topk_mask(grp, scores, k):
    rows, lk = scores.shape
    tr = PAGE
    assert rows % tr == 0 and (grp.nq == 1 or grp.tq == tr)
    blk = pl.BlockSpec((tr, lk), lambda i: (i, 0))
    return pl.pallas_call(
        functools.partial(_topk_body, k=k, nq=grp.nq, tr=tr, widths=grp.widths()),
        grid=(rows // tr,),
        in_specs=[blk],
        out_specs=blk,
        out_shape=jax.ShapeDtypeStruct((rows, lk), F32),
        scratch_shapes=[pltpu.VMEM((tr, lk), jnp.int32)],
        compiler_params=_cparams("parallel"),
        name="topk_mask",
    )(scores)


def _combine_body(oc_ref, os_ref, ow_ref, od_ref, misc_ref, y_ref):
    n_heads = KV_GROUPS * GROUP_HEADS
    gates = jax.nn.sigmoid(misc_ref[:, MISC_GATES:MISC_GATES + 3 * n_heads])
    for h in range(n_heads):
        hs = slice(h * HEAD_DIM, (h + 1) * HEAD_DIM)
        o = (gates[:, 3 * h:3 * h + 1] * oc_ref[:, hs] + gates[:, 3 * h + 1:3 * h + 2] * os_ref[:, hs]
             + gates[:, 3 * h + 2:3 * h + 3] * ow_ref[:, hs])
        y_ref[:, hs] = o.astype(y_ref.dtype)
    y_ref[:, n_heads * HEAD_DIM:] = od_ref[...].astype(y_ref.dtype)


def nsa_dsa_combine(o_c, o_s, o_w, o_d, z, row0, *, tm):
    m, w = o_c.shape
    assert m % tm == 0 and row0 % tm == 0
    blk = pl.BlockSpec((tm, w), lambda i: (i, 0))
    return pl.pallas_call(
        _combine_body,
        grid=(m // tm,),
        in_specs=[blk, blk, blk, blk, pl.BlockSpec((tm, LANES), lambda i: (row0 // tm + i, COL_MISC // LANES))],
        out_specs=pl.BlockSpec((tm, 2 * w), lambda i: (i, 0)),
        out_shape=jax.ShapeDtypeStruct((m, 2 * w), BF16),
        compiler_params=_cparams("parallel"),
        name="nsa_dsa_combine",
    )(o_c, o_s, o_w, o_d, z)


def _band_tiles(rel_bias, tq):
    delta = (rel_bias[_BUCKETS] - rel_bias[N_BUCKETS - 1]).T
    rev = jnp.concatenate([delta[:, ::-1], jnp.zeros((delta.shape[0], PAGE), delta.dtype)], 1)
    tiles = jnp.stack([rev[:, PAGE - 1 - i:PAGE - 1 - i + BAND] for i in range(tq)], 1)
    return tiles.reshape(-1, GROUP_HEADS * tq, BAND)


def _widen_cd_w_in(w):
    sizes = (1024, 512, 512, 512, 24, 1024, 512, 512, 64, 8)
    q_n, kv_c, kv_s, kv_w, gates, q_d, kv_d, q_i, k_i, w_i = jnp.split(w, np.cumsum(sizes)[:-1].tolist(), axis=1)
    cols = [q_n, q_d, kv_c, kv_s, kv_w, kv_d, q_i, k_i, gates, w_i]
    used = sum(c.shape[1] for c in cols)
    return jnp.concatenate(cols + [jnp.zeros((w.shape[0], NZ - used), w.dtype)], axis=1)


def kernel(x_prompt, x_sample, state_conv, state_pool, cache_nsa_cmp, cache_nsa_sel, cache_nsa_win, cache_dsa_kv, cache_dsa_idx, page_table, norm_mix, norm_ffn, norm_final, ab_w_in, ab_conv_w, ab_conv_b, ab_ln_g, ab_ln_b, ab_pool_w, ab_pool_scale, ab_w_out, cd_w_in, cd_w_cmp, cd_w_out, rel_bias, ffn_w1, ffn_w2):
    bp, tp, d_model = x_prompt.shape
    bs, ts, _ = x_sample.shape
    mp, ms = bp * tp, bs * ts
    depth = norm_mix.shape[0]
    n_pages = page_table.shape[1]
    n_pool = cache_nsa_cmp.shape[1]
    past_len = n_pages * PAGE
    assert cache_nsa_cmp.shape[2] == PAGE
    win_len = cache_nsa_win.shape[2]
    assert win_len % PAGE == 0 and win_len == NSA_WINDOW and tp >= NSA_WINDOW
    kv_w = KV_GROUPS * 2 * HEAD_DIM

    x = jnp.concatenate([x_prompt.reshape(mp, d_model), x_sample.reshape(ms, d_model)], 0)
    grp_p = _Group(0, bp, tp, PAGE, 0, 0)
    grp_s = _Group(mp, bs, ts, ts, past_len, n_pages)
    grp_sw = _Group(mp, bs, ts, ts, past_len, win_len // PAGE)
    no_pages = jnp.zeros((1, 1), jnp.int32)
    win_pages = jnp.arange(bs * (win_len // PAGE), dtype=jnp.int32).reshape(bs, win_len // PAGE)

    outs = {k: [] for k in ("conv_p", "conv_s", "pool_p", "pool_s", "cmp_p", "cmp_s", "sel_p", "sel_s",
                            "win_p", "win_s", "dsa_p", "dsa_s", "idx_p", "idx_s")}
    y_final = None
    for i in range(depth):
        j = i // 2
        if i % 2 == 0:
            d_conv = ab_conv_w.shape[2]
            z = norm_matmul(x, norm_mix[i], ab_w_in[j].astype(BF16))
            mid_p, u_p = ab_mid_prompt(z, bp, tp, ab_conv_w[j], ab_conv_b[j], ab_ln_g[j], ab_ln_b[j],
                                       ab_pool_w[j], ab_pool_scale[j])
            mid_s, conv_s, pool_s = ab_mid_step(z, mp, bs, ts, past_len, state_conv[j], state_pool[j], ab_conv_w[j],
                                                ab_conv_b[j], ab_ln_g[j], ab_ln_b[j], ab_pool_w[j], ab_pool_scale[j])
            x = matmul_residual(jnp.concatenate([mid_p, mid_s], 0), ab_w_out[j].astype(BF16), x)
            outs["conv_p"].append(u_p.reshape(bp, tp, d_conv)[:, tp - CONV_BUF:])
            outs["conv_s"].append(conv_s)
            outs["pool_p"].append(z[:mp, 2 * d_conv:].reshape(bp, tp, -1)[:, tp - POOL_BUF:])
            outs["pool_s"].append(pool_s)
        else:
            z = norm_matmul(x, norm_mix[i], _widen_cd_w_in(cd_w_in[j]).astype(BF16))
            nsa_bias = rel_bias[:, :KV_GROUPS * GROUP_HEADS]
            band_p, band_s = _band_tiles(rel_bias, grp_p.tq), _band_tiles(rel_bias, grp_s.tq)
            wexp = jnp.tile(jnp.repeat(jnp.transpose(cd_w_cmp[j], (1, 0, 2)).reshape(NSA_BLOCK, 2 * KV_GROUPS),
                                       HEAD_DIM, axis=1), (PAGE // NSA_BLOCK, 1))
            pt = page_table + j * n_pool
            pools = [c.reshape((-1,) + KV_PAGE) for c in (cache_nsa_cmp, cache_nsa_sel, cache_dsa_kv)]
            idx_pool = jnp.swapaxes(cache_dsa_idx, 2, 3).reshape(-1, IDX_DIM, PAGE)
            win_pool = cache_nsa_win.reshape((-1,) + KV_PAGE)
            wpt = win_pages + j * bs * (win_len // PAGE)
            mids = []
            for grp, gw, ptab, wtab, band in ((grp_p, grp_p, no_pages, no_pages, band_p),
                                              (grp_s, grp_sw, pt, wpt, band_s)):
                o_c, msel = nsa_compress(grp, z, ptab, pools[0], wexp, _cmp_bias_table(nsa_bias, grp))
                o_s = sparse_attention("sel", grp, z, ptab, pools[1], band[:KV_GROUPS],
                                       q_col=COL_QN, kv_col=COL_KVS, mask=msel)
                o_w = sparse_attention("win", gw, z, wtab, win_pool, band[:KV_GROUPS],
                                       q_col=COL_QN, kv_col=COL_KVW)
                top = topk_mask(grp, dsa_index_scores(grp, z, ptab, idx_pool), min(DSA_TOPK, grp.n_keys // 4))
                o_d = sparse_attention("mask", grp, z, ptab, pools[2], band[KV_GROUPS:],
                                       q_col=COL_QD, kv_col=COL_KVD, mask=top)
                mids.append(nsa_dsa_combine(o_c, o_s, o_w, o_d, z, grp.row0, tm=min(512, grp.rows)))
            x = matmul_residual(jnp.concatenate(mids, 0), cd_w_out[j].astype(BF16), x)

            def kv_out(col, width, tail):
                seg = z[:, col:col + width]
                return seg[:mp].reshape((bp, tp) + tail), seg[mp:].reshape((bs, ts) + tail)

            kv_tail = (2, KV_GROUPS, HEAD_DIM)
            for name, col in (("cmp", COL_KVC), ("sel", COL_KVS), ("dsa", COL_KVD)):
                p_new, s_new = kv_out(col, kv_w, kv_tail)
                outs[name + "_p"].append(p_new)
                outs[name + "_s"].append(s_new)
            w_p, w_s = kv_out(COL_KVW, kv_w, kv_tail)
            outs["win_p"].append(w_p[:, tp - NSA_WINDOW:])
            outs["win_s"].append(jnp.concatenate([cache_nsa_win[j], w_s], 1)[:, ts:])
            i_p, i_s = kv_out(COL_MISC + MISC_KI, IDX_DIM, (IDX_DIM,))
            outs["idx_p"].append(i_p)
            outs["idx_s"].append(i_s)
        a = norm_matmul(x, norm_ffn[i], ffn_w1[i].astype(BF16), relu2=True, out_dtype=BF16)
        if i == depth - 1:
            x, y_final = matmul_residual(a, ffn_w2[i].astype(BF16), x, norm_final)
        else:
            x = matmul_residual(a, ffn_w2[i].astype(BF16), x)

    st = {k: jnp.stack(v) for k, v in outs.items()}
    return (y_final[:mp].reshape(bp, tp, d_model), y_final[mp:].reshape(bs, ts, d_model),
            st["conv_p"], st["conv_s"], st["pool_p"], st["pool_s"], st["cmp_p"], st["cmp_s"],
            st["sel_p"], st["sel_s"], st["win_p"], st["win_s"], st["dsa_p"], st["dsa_s"],
            st["idx_p"], st["idx_s"])
```

```python
import functools
import math

import numpy as np
import jax
import jax.numpy as jnp
from jax import lax
from jax.experimental import pallas as pl
from jax.experimental.pallas import tpu as pltpu

F32 = jnp.float32
BF16 = jnp.bfloat16

EPS = 1e-6
NEG_INF = -1e30
LOG2E = math.log2(math.e)
HEAD_DIM = 128
LANES = 128
CONV_WIDTH = 31
CONV_BUF = CONV_WIDTH - 1
POOL_WINDOWS = (2, 4, 8, 16)
POOL_BUF = max(POOL_WINDOWS) - 1
HALO = 32
VMEM_LIMIT = 56 * 1024 * 1024


def _cparams(*sem):
    return pltpu.CompilerParams(dimension_semantics=sem, vmem_limit_bytes=VMEM_LIMIT)


class _Rows:
    def __init__(self, arrays, tm):
        self.arrays = list(arrays)
        self.tm = tm
        assert all(a.shape[0] % tm == 0 for a in self.arrays)
        self.tiles = [a.shape[0] // tm for a in self.arrays]
        self.n_tiles = sum(self.tiles)
        self.n = len(self.arrays)

    def specs(self, width, col):
        out, t0 = [], 0
        for nt in self.tiles:
            out.append(pl.BlockSpec((self.tm, width), lambda i, j, t0=t0, nt=nt: (jnp.clip(i - t0, 0, nt - 1), col(j))))
            t0 += nt
        return out

    def select(self, i, refs, fn):
        if self.n == 1:
            fn(refs[0])
            return
        t0 = 0
        for nt, ref in zip(self.tiles, refs):
            pl.when((i >= t0) & (i < t0 + nt))(functools.partial(fn, ref))
            t0 += nt


def _rmsnorm_rows(x, g):
    return (x * lax.rsqrt(jnp.mean(x * x, -1, keepdims=True) + EPS)) * g


def _norm_matmul_body(*refs, rows, relu2):
    x_refs = refs[:rows.n]
    g_ref, w_ref, o_ref, h_ref = refs[rows.n:]

    @pl.when(pl.program_id(1) == 0)
    def _():
        def norm(x_ref):
            h_ref[...] = _rmsnorm_rows(x_ref[...], g_ref[...]).astype(BF16)

        rows.select(pl.program_id(0), x_refs, norm)

    y = jnp.dot(h_ref[...], w_ref[...], preferred_element_type=F32)
    if relu2:
        y = jnp.square(jnp.maximum(y, 0.0))
    o_ref[...] = y.astype(o_ref.dtype)


def norm_matmul(xs, g, w, *, relu2=False, out_dtype=F32, tm=1024, tn=1024):
    rows = _Rows(xs, tm)
    d, n = w.shape
    assert n % tn == 0
    return pl.pallas_call(
        functools.partial(_norm_matmul_body, rows=rows, relu2=relu2),
        grid=(rows.n_tiles, n // tn),
        in_specs=rows.specs(d, lambda j: 0) + [pl.BlockSpec((1, d), lambda i, j: (0, 0)),
                                               pl.BlockSpec((d, tn), lambda i, j: (0, j))],
        out_specs=pl.BlockSpec((tm, tn), lambda i, j: (i, j)),
        out_shape=jax.ShapeDtypeStruct((rows.n_tiles * tm, n), out_dtype),
        scratch_shapes=[pltpu.VMEM((tm, d), BF16)],
        compiler_params=_cparams("parallel", "arbitrary"),
        name="norm_matmul",
    )(*rows.arrays, g.reshape(1, d), w)


def _matmul_residual_body(*refs, a_rows, r_rows, o_rows, final_norm):
    a_refs, refs = refs[:a_rows.n], refs[a_rows.n:]
    w_ref, refs = refs[0], refs[1:]
    r_refs, refs = refs[:r_rows.n], refs[r_rows.n:]
    if final_norm:
        g_ref, refs = refs[0], refs[1:]
    o_refs, acc_ref = refs[:o_rows.n], refs[o_rows.n]
    i, k = pl.program_id(0), pl.program_id(1)

    @pl.when(k == 0)
    def _():
        acc_ref[...] = jnp.zeros_like(acc_ref)

    def accumulate(a_ref):
        acc_ref[...] += jnp.dot(a_ref[...], w_ref[...], preferred_element_type=F32)

    a_rows.select(i, a_refs, accumulate)

    @pl.when(k == pl.num_programs(1) - 1)
    def _():
        def add_residual(r_ref):
            acc_ref[...] += r_ref[...]

        def write(o_ref):
            o = acc_ref[...]
            o_ref[...] = _rmsnorm_rows(o, g_ref[...]) if final_norm else o

        r_rows.select(i, r_refs, add_residual)
        o_rows.select(i, o_refs, write)


def matmul_residual(a_list, w, r_list, g_final=None, *, split_out=None, tm=512, tk=2048):
    a_rows, r_rows = _Rows(a_list, tm), _Rows(r_list, tm)
    kdim, n = w.shape
    m = a_rows.n_tiles * tm
    assert kdim % tk == 0 and r_rows.n_tiles == a_rows.n_tiles
    final_norm = g_final is not None
    o_rows = _Rows([jax.ShapeDtypeStruct((r, n), F32) for r in (split_out or (m,))], tm)
    assert o_rows.n_tiles == a_rows.n_tiles
    in_specs = (a_rows.specs(tk, lambda k: k) + [pl.BlockSpec((tk, n), lambda i, k: (k, 0))]
                + r_rows.specs(n, lambda k: 0))
    args = a_rows.arrays + [w] + r_rows.arrays
    if final_norm:
        in_specs.append(pl.BlockSpec((1, n), lambda i, k: (0, 0)))
        args.append(g_final.reshape(1, n))
    out = pl.pallas_call(
        functools.partial(_matmul_residual_body, a_rows=a_rows, r_rows=r_rows, o_rows=o_rows, final_norm=final_norm),
        grid=(a_rows.n_tiles, kdim // tk),
        in_specs=in_specs,
        out_specs=o_rows.specs(n, lambda k: 0),
        out_shape=o_rows.arrays,
        scratch_shapes=[pltpu.VMEM((tm, n), F32)],
        compiler_params=_cparams("parallel", "arbitrary"),
        name="matmul_residual",
    )(*args)
    return out if split_out else out[0]


def _layernorm_silu(c, g, b):
    mu = jnp.mean(c, -1, keepdims=True)
    xc = c - mu
    y = xc * lax.rsqrt(jnp.mean(xc * xc, -1, keepdims=True) + EPS)
    y = y * g + b
    return y * jax.nn.sigmoid(y)


def _ab_mid_body(z_ref, zp_ref, cw_ref, cb_ref, lg_ref, lb_ref, pw_ref, ps_ref, y_ref, u_ref,
                 ext_ref, vext_ref, conv_ref, *, tt, d_conv, d_pool):
    ti = pl.program_id(1)
    keep = (ti > 0).astype(F32)
    a_p = zp_ref[:, 0:d_conv]
    g_p = zp_ref[:, d_conv:2 * d_conv]
    ext_ref[0:HALO, :] = a_p * jax.nn.sigmoid(g_p) * keep
    vext_ref[0:HALO, :] = zp_ref[:, 2 * d_conv:] * keep
    u = z_ref[:, 0:d_conv] * jax.nn.sigmoid(z_ref[:, d_conv:2 * d_conv])
    ext_ref[HALO:, :] = u
    u_ref[...] = u
    vext_ref[HALO:, :] = z_ref[:, 2 * d_conv:]

    off = HALO - CONV_BUF
    for c in range(d_conv // LANES):
        cs = slice(c * LANES, (c + 1) * LANES)
        acc = jnp.zeros((tt, LANES), F32)
        for j in range(CONV_WIDTH):
            acc = acc + cw_ref[j:j + 1, cs] * ext_ref[off + j:off + j + tt, cs]
        conv_ref[:, cs] = acc + cb_ref[:, cs]
    y_ref[:, 0:d_conv] = _layernorm_silu(conv_ref[...], lg_ref[...], lb_ref[...]).astype(y_ref.dtype)

    pos = ti * tt + lax.broadcasted_iota(jnp.int32, (tt, 1), 0)
    pg = d_pool // len(POOL_WINDOWS)
    for gi, w in enumerate(POOL_WINDOWS):
        gs = slice(gi * pg, (gi + 1) * pg)
        tok = vext_ref[HALO:, gs]
        acc = tok
        for i in range(1, w):
            acc = acc + vext_ref[HALO - i:HALO - i + tt, gs]
        cnt = jnp.minimum(pos + 1, w).astype(F32)
        d = acc / cnt - tok
        yp = jnp.dot(d.astype(BF16), pw_ref[gi], preferred_element_type=F32) * ps_ref[:, gs]
        y_ref[:, d_conv + gi * pg:d_conv + (gi + 1) * pg] = yp.astype(y_ref.dtype)


def ab_mid_prompt(z, n_seq, t_len, conv_w, conv_b, ln_g, ln_b, pool_w, pool_scale, *, tt=256):
    d_conv = conv_w.shape[1]
    d_pool = pool_scale.shape[0]
    nt = t_len // tt
    hb = tt // HALO
    row = lambda b, t: (b * nt + t, 0)
    const = lambda b, t: (0, 0)
    return pl.pallas_call(
        functools.partial(_ab_mid_body, tt=tt, d_conv=d_conv, d_pool=d_pool),
        grid=(n_seq, nt),
        in_specs=[pl.BlockSpec((tt, z.shape[1]), row),
                  pl.BlockSpec((HALO, z.shape[1]), lambda b, t: (jnp.maximum((b * nt + t) * hb - 1, 0), 0)),
                  pl.BlockSpec(conv_w.shape, const),
                  pl.BlockSpec((1, d_conv), const),
                  pl.BlockSpec((1, d_conv), const),
                  pl.BlockSpec((1, d_conv), const),
                  pl.BlockSpec(pool_w.shape, lambda b, t: (0, 0, 0)),
                  pl.BlockSpec((1, d_pool), const)],
        out_specs=[pl.BlockSpec((tt, d_conv + d_pool), row),
                   pl.BlockSpec((tt, d_conv), row)],
        out_shape=[jax.ShapeDtypeStruct((n_seq * t_len, d_conv + d_pool), BF16),
                   jax.ShapeDtypeStruct((n_seq * t_len, d_conv), F32)],
        scratch_shapes=[pltpu.VMEM((HALO + tt, d_conv), F32),
                        pltpu.VMEM((HALO + tt, d_pool), F32),
                        pltpu.VMEM((tt, d_conv), F32)],
        compiler_params=_cparams("parallel", "parallel"),
        name="ab_mid_prompt",
    )(z, z, conv_w, conv_b.reshape(1, -1), ln_g.reshape(1, -1), ln_b.reshape(1, -1),
      pool_w.astype(BF16), pool_scale.reshape(1, -1))


def _ab_mid_step_body(z_ref, sc_ref, sp_ref, cw_ref, cb_ref, lg_ref, lb_ref, pw_ref, ps_ref,
                      y_ref, nc_ref, np_ref, ext_ref, vext_ref, *, nb, t, pos0, d_conv, d_pool):
    e0 = HALO - CONV_BUF
    p0 = 16 - POOL_BUF
    z = z_ref[...].reshape(nb, t, z_ref.shape[1])
    u = z[:, :, 0:d_conv] * jax.nn.sigmoid(z[:, :, d_conv:2 * d_conv])
    ext_ref[:, e0:HALO, :] = sc_ref[...]
    ext_ref[:, HALO:, :] = u
    vext_ref[:, p0:16, :] = sp_ref[...]
    vext_ref[:, 16:, :] = z[:, :, 2 * d_conv:]
    nc_ref[...] = ext_ref[:, HALO + t - CONV_BUF:, :]
    np_ref[...] = vext_ref[:, 16 + t - POOL_BUF:, :]

    acc = jnp.zeros((nb, t, d_conv), F32)
    for j in range(CONV_WIDTH):
        acc = acc + cw_ref[j:j + 1, :][None] * ext_ref[:, e0 + j:e0 + j + t, :]
    c = acc + cb_ref[...][None]
    yc = _layernorm_silu(c, lg_ref[...][None], lb_ref[...][None])
    y_ref[:, 0:d_conv] = yc.reshape(nb * t, d_conv).astype(y_ref.dtype)

    pg = d_pool // len(POOL_WINDOWS)
    for gi, w in enumerate(POOL_WINDOWS):
        gs = slice(gi * pg, (gi + 1) * pg)
        tok = vext_ref[:, 16:, gs]
        acc = tok
        for i in range(1, w):
            acc = acc + vext_ref[:, 16 - i:16 - i + t, gs]
        cnt = jnp.minimum(pos0 + 1 + lax.broadcasted_iota(jnp.int32, (1, t, 1), 1), w).astype(F32)
        d = (acc / cnt - tok).reshape(nb * t, pg)
        yp = jnp.dot(d.astype(BF16), pw_ref[gi], preferred_element_type=F32) * ps_ref[:, gs]
        y_ref[:, d_conv + gi * pg:d_conv + (gi + 1) * pg] = yp.astype(y_ref.dtype)


def ab_mid_step(z, row0, n_seq, t, pos0, state_conv, state_pool, conv_w, conv_b, ln_g, ln_b, pool_w,
                pool_scale, *, nb=16):
    d_conv = conv_w.shape[1]
    d_pool = pool_scale.shape[0]
    rb = nb * t
    assert row0 % rb == 0 and n_seq % nb == 0
    const = lambda i: (0, 0)
    seq3 = lambda i: (i, 0, 0)
    return pl.pallas_call(
        functools.partial(_ab_mid_step_body, nb=nb, t=t, pos0=pos0, d_conv=d_conv, d_pool=d_pool),
        grid=(n_seq // nb,),
        in_specs=[pl.BlockSpec((rb, z.shape[1]), lambda i: (row0 // rb + i, 0)),
                  pl.BlockSpec((nb, CONV_BUF, d_conv), seq3),
                  pl.BlockSpec((nb, POOL_BUF, d_pool), seq3),
                  pl.BlockSpec(conv_w.shape, const),
                  pl.BlockSpec((1, d_conv), const),
                  pl.BlockSpec((1, d_conv), const),
                  pl.BlockSpec((1, d_conv), const),
                  pl.BlockSpec(pool_w.shape, lambda i: (0, 0, 0)),
                  pl.BlockSpec((1, d_pool), const)],
        out_specs=[pl.BlockSpec((rb, d_conv + d_pool), lambda i: (i, 0)),
                   pl.BlockSpec((nb, CONV_BUF, d_conv), seq3),
                   pl.BlockSpec((nb, POOL_BUF, d_pool), seq3)],
        out_shape=[jax.ShapeDtypeStruct((n_seq * t, d_conv + d_pool), BF16),
                   jax.ShapeDtypeStruct((n_seq, CONV_BUF, d_conv), F32),
                   jax.ShapeDtypeStruct((n_seq, POOL_BUF, d_pool), F32)],
        scratch_shapes=[pltpu.VMEM((nb, HALO + t, d_conv), F32),
                        pltpu.VMEM((nb, 16 + t, d_pool), F32)],
        compiler_params=_cparams("parallel"),
        name="ab_mid_step",
    )(z, state_conv, state_pool, conv_w, conv_b.reshape(1, -1), ln_g.reshape(1, -1),
      ln_b.reshape(1, -1), pool_w.astype(BF16), pool_scale.reshape(1, -1))


N_BUCKETS = 32
MAX_DISTANCE = 128
NSA_BLOCK = 64
NSA_TOPN = 16
NSA_WINDOW = 512
DSA_TOPK = 256
IDX_HEADS = 8
IDX_DIM = 64
KV_GROUPS = 2
GROUP_HEADS = 4
PAGE = 128
BAND = 2 * PAGE
INT_MIN = -2 ** 31
KV_PAGE = (2 * KV_GROUPS * PAGE, HEAD_DIM)
TOPK_ROW_GROUPS = 4
CMP_BIAS_LANE0 = 64

COL_QN, COL_QD, COL_KVC, COL_KVS, COL_KVW, COL_KVD, COL_QI, COL_MISC = 0, 1024, 2048, 2560, 3072, 3584, 4096, 4608
MISC_KI, MISC_GATES, MISC_WI = 0, 64, 88
NZ = 5120


def _bucket_np(n):
    n = np.maximum(np.asarray(n, np.int32), 0)
    exact = N_BUCKETS // 2
    nf = np.maximum(n, 1).astype(np.float32)
    big = exact + (np.log(nf / np.float32(exact)) / np.float32(math.log(MAX_DISTANCE / exact))
                   * np.float32(N_BUCKETS - exact)).astype(np.int32)
    return np.where(n < exact, n, np.minimum(big, N_BUCKETS - 1))


_BUCKETS = _bucket_np(np.arange(BAND))
assert _BUCKETS[PAGE:].min() == N_BUCKETS - 1


def _softmax_rows(s, mask):
    s = jnp.where(mask, s, NEG_INF)
    m = jnp.max(s, -1, keepdims=True)
    p = jnp.where(mask, jnp.exp(s - m), 0.0)
    return p, jnp.sum(p, -1, keepdims=True)


def _dot_nt(a, b):
    return lax.dot_general(a, b, (((1,), (1,)), ((), ())), preferred_element_type=F32)


def _new_chunks(new_ref, t_new):
    chunks = [new_ref[c * PAGE:(c + 1) * PAGE, :] for c in range(t_new // PAGE)]
    rem = t_new % PAGE
    if rem:
        tail = new_ref[(t_new // PAGE) * PAGE:, :]
        chunks.append(jnp.concatenate([tail, jnp.zeros((PAGE - rem, tail.shape[1]), F32)], 0))
    return chunks


def _kv_chunks(page_refs, new_ref, t_new):
    n_parts = 2 * KV_GROUPS
    chunks = [[r[0, pl.ds(part, PAGE, stride=n_parts), :] for part in range(n_parts)] for r in page_refs]
    for x in _new_chunks(new_ref, t_new):
        chunks.append([x[:, part * HEAD_DIM:(part + 1) * HEAD_DIM] for part in range(n_parts)])
    return chunks


def _cmp_body(pt_ref, q_ref, kvn_ref, *rest, n_pages, t_new, tq, pos0):
    del pt_ref
    page_refs = rest[:n_pages]
    wexp_ref, bias_ref, o_ref, msel_ref, comp_ref, ck_ref, cv_ref = rest[n_pages:]
    qi = pl.program_id(1)
    n_keys = n_pages * PAGE + t_new
    n_cmp = n_keys // NSA_BLOCK
    n_sel = -(-n_keys // NSA_BLOCK)
    per = PAGE // NSA_BLOCK

    @pl.when(qi == 0)
    def _():
        comp_ref[...] = jnp.zeros_like(comp_ref)
        chunks = _kv_chunks(page_refs, kvn_ref, t_new)[:n_cmp // per]
        for part in range(2 * KV_GROUPS):
            cols = slice(part * HEAD_DIM, (part + 1) * HEAD_DIM)
            xw = jnp.concatenate([parts[part] * wexp_ref[:, cols] for parts in chunks], 0)
            comp_ref[0:per * len(chunks), cols] = xw.reshape(per * len(chunks), NSA_BLOCK, HEAD_DIM).sum(1)
        for g in range(KV_GROUPS):
            ck_ref[g] = comp_ref[:, g * HEAD_DIM:(g + 1) * HEAD_DIM].astype(BF16)
            cv_ref[g] = comp_ref[:, (KV_GROUPS + g) * HEAD_DIM:(KV_GROUPS + g + 1) * HEAD_DIM].astype(BF16)

    scale = HEAD_DIM ** -0.5
    rows = GROUP_HEADS * tq
    blk = lax.broadcasted_iota(jnp.int32, (1, LANES), 1)
    q0 = pos0 + qi * tq
    assert tq & (tq - 1) == 0
    qpos_st = q0 + (lax.broadcasted_iota(jnp.int32, (rows, 1), 0) & (tq - 1))
    mask = (qpos_st - ((blk + 1) * NSA_BLOCK - 1) >= 0) & (blk < n_cmp)
    cur = (q0 + lax.broadcasted_iota(jnp.int32, (tq, 1), 0)) // NSA_BLOCK
    scores = []
    for g in range(KV_GROUPS):
        heads = [g * GROUP_HEADS + r for r in range(GROUP_HEADS)]
        bias = bias_ref[g]
        if t_new != tq:
            bias = pltpu.roll(bias, qi * (tq // NSA_BLOCK) + (LANES - CMP_BIAS_LANE0), 1)
        q = jnp.concatenate([q_ref[:, h * HEAD_DIM:(h + 1) * HEAD_DIM] for h in heads], 0).astype(BF16)
        scores.append(_dot_nt(q, ck_ref[g]) * scale + bias)
    probs = []
    for g in range(KV_GROUPS):
        p, l = _softmax_rows(scores[g], mask)
        probs.append(p / jnp.maximum(l, 1e-30))
    for g in range(KV_GROUPS):
        o = jnp.dot(probs[g].astype(BF16), cv_ref[g], preferred_element_type=F32)
        for r in range(GROUP_HEADS):
            h = g * GROUP_HEADS + r
            o_ref[:, h * HEAD_DIM:(h + 1) * HEAD_DIM] = o[r * tq:(r + 1) * tq]
    for g in range(KV_GROUPS):
        imp = probs[g][0:tq]
        for r in range(1, GROUP_HEADS):
            imp = imp + probs[g][r * tq:(r + 1) * tq]
        imp = jnp.where(blk == cur, 2.0, jnp.where(blk > cur, -1.0, imp))
        imp = jnp.where(blk < n_sel, imp, -2.0)
        n_top = min(NSA_TOPN, n_sel)
        cols = slice(g * LANES, (g + 1) * LANES)

        def by_rank(imp=imp, cols=cols):
            rank = jnp.zeros((tq, LANES), F32)
            for i in range(n_sel):
                col = imp[:, i:i + 1]
                ahead = (col > imp) | ((col == imp) & (blk > i))
                rank = rank + jnp.where(ahead, 1.0, 0.0)
            msel_ref[:, cols] = jnp.where((rank < float(n_top)) & (blk < n_sel), 1.0, 0.0)

        def first_blocks(cols=cols):
            msel_ref[:, cols] = jnp.where(blk < n_top, 1.0, 0.0) + jnp.zeros((tq, LANES), F32)

        if t_new == tq:
            if pos0 + tq <= n_top * NSA_BLOCK:
                first_blocks()
            else:
                by_rank()
        else:
            early = pos0 + (qi + 1) * tq <= n_top * NSA_BLOCK
            pl.when(early)(first_blocks)
            pl.when(jnp.logical_not(early))(by_rank)


def _on_causal_width(qi, tq, widths, tile):
    if len(widths) == 1:
        tile(widths[0], True)
        return
    need = (qi * tq + tq - 1) // widths[0]
    for nw, w in enumerate(widths):
        pl.when(need == nw)(functools.partial(tile, w, nw == 0))


def _attn_body(pt_ref, q_ref, kvn_ref, *rest, mode, n_pages, t_new, tq, pos0, widths):
    del pt_ref
    page_refs = rest[:n_pages]
    rest = rest[n_pages:]
    m_ref = None
    if mode in ("sel", "mask"):
        m_ref, rest = rest[0], rest[1:]
    band_ref, o_ref, kc_ref, vc_ref, s_ref = rest
    qi = pl.program_id(1)
    single = t_new == tq
    n_keys = n_pages * PAGE + t_new
    kbase = pos0 - n_pages * PAGE
    scale = HEAD_DIM ** -0.5
    q0 = pos0 if single else pos0 + qi * tq

    @pl.when(qi == 0)
    def _():
        for c, parts in enumerate(_kv_chunks(page_refs, kvn_ref, t_new)):
            rows = slice(c * PAGE, (c + 1) * PAGE)
            for g in range(KV_GROUPS):
                kc_ref[g, rows, :] = parts[g].astype(BF16)
                vc_ref[g, rows, :] = parts[KV_GROUPS + g].astype(BF16)

    def tile(c0, w, band_at, maybe_first):
        qpos = q0 + lax.broadcasted_iota(jnp.int32, (tq, 1), 0)
        col = c0 + lax.broadcasted_iota(jnp.int32, (1, w), 1)
        dist = qpos - (kbase + col)
        visible = (dist >= 0) & (col < n_keys)
        if mode == "win":
            visible = visible & (dist < NSA_WINDOW)
        if mode == "mask":
            visible = visible & (m_ref[:, 0:w] > 0.5)
        keys = pl.ds(c0, w)
        groups = range(KV_GROUPS)
        masks = []
        for g in groups:
            if mode == "sel":
                expand = (lax.broadcasted_iota(jnp.int32, (LANES, w), 1) // NSA_BLOCK
                          == lax.broadcasted_iota(jnp.int32, (LANES, w), 0))
                chosen = jnp.dot(m_ref[:, g * LANES:(g + 1) * LANES].astype(BF16),
                                 jnp.where(expand, 1.0, 0.0).astype(BF16), preferred_element_type=F32)
                masks.append(visible & (chosen > 0.5))
            else:
                masks.append(visible)
            heads = [g * GROUP_HEADS + r for r in range(GROUP_HEADS)]
            q = jnp.concatenate([q_ref[:, h * HEAD_DIM:(h + 1) * HEAD_DIM] for h in heads], 0).astype(BF16)
            s_ref[g, :, 0:w] = _dot_nt(q, kc_ref[g, keys, :]) * (scale * LOG2E)
        for g in groups:
            if band_at is not None:
                s_ref[g, :, band_at:band_at + BAND] += band_ref[g]
            else:
                if maybe_first:
                    @pl.when(qi == 0)
                    def _():
                        s_ref[g, :, 0:PAGE] += band_ref[g, :, PAGE:]

                @pl.when(qi > 0)
                def _():
                    s_ref[g, :, pl.ds(pl.multiple_of(q0 - PAGE - kbase, PAGE), BAND)] += band_ref[g]
        sums, alive = [], []
        for g in groups:
            for r in range(GROUP_HEADS):
                rows = slice(r * tq, (r + 1) * tq)
                s = jnp.where(masks[g], s_ref[g, rows, 0:w], NEG_INF)
                m = jnp.max(s, -1, keepdims=True)
                p = jnp.exp2(s - m)
                s_ref[g, rows, 0:w] = p
                sums.append(jnp.sum(p, -1, keepdims=True))
                alive.append(m > NEG_INF)
        for g in groups:
            o = jnp.dot(s_ref[g, :, 0:w].astype(BF16), vc_ref[g, keys, :], preferred_element_type=F32)
            for r in range(GROUP_HEADS):
                h = g * GROUP_HEADS + r
                o_h = o[r * tq:(r + 1) * tq] / jnp.maximum(sums[h], 1e-30)
                o_ref[:, h * HEAD_DIM:(h + 1) * HEAD_DIM] = jnp.where(alive[h], o_h, 0.0)

    if single:
        tile(0, widths[0], pos0 - PAGE - kbase, False)
    elif mode == "win":
        wch = NSA_WINDOW // PAGE
        pl.when(qi < wch)(functools.partial(tile, 0, NSA_WINDOW, None, True))
        pl.when(qi >= wch)(lambda: tile(pl.multiple_of((qi - wch) * PAGE, PAGE), NSA_WINDOW + PAGE,
                                        NSA_WINDOW - PAGE, False))
    else:
        _on_causal_width(qi, tq, widths, lambda w, first: tile(0, w, None, first))


def _index_body(pt_ref, qidx_ref, miscq_ref, misck_ref, *rest, n_pages, t_new, tq, pos0, widths):
    del pt_ref
    ipage_refs = rest[:n_pages]
    o_ref, kidx_ref = rest[n_pages:]
    qi = pl.program_id(1)
    lk = o_ref.shape[1]
    n_keys = n_pages * PAGE + t_new
    kbase = pos0 - n_pages * PAGE
    q0 = pos0 if t_new == tq else pos0 + qi * tq

    @pl.when(qi == 0)
    def _():
        for c, r in enumerate(ipage_refs):
            kidx_ref[:, c * PAGE:(c + 1) * PAGE] = r[0].astype(BF16)
        for c, x in enumerate(_new_chunks(misck_ref, t_new)):
            cols = slice((n_pages + c) * PAGE, (n_pages + c + 1) * PAGE)
            kidx_ref[:, cols] = x.T[MISC_KI:MISC_KI + IDX_DIM, :].astype(BF16)

    def tile(w, maybe_first):
        del maybe_first
        qpos = q0 + lax.broadcasted_iota(jnp.int32, (tq, 1), 0)
        col = lax.broadcasted_iota(jnp.int32, (1, w), 1)
        visible = (qpos - (kbase + col) >= 0) & (col < n_keys)
        q = jnp.concatenate([qidx_ref[:, hh * IDX_DIM:(hh + 1) * IDX_DIM] for hh in range(IDX_HEADS)], 0)
        sc = jnp.dot(q.astype(BF16), kidx_ref[:, 0:w], preferred_element_type=F32)
        score = jnp.zeros((tq, w), F32)
        for hh in range(IDX_HEADS):
            wi = miscq_ref[:, MISC_WI + hh:MISC_WI + hh + 1] * (IDX_HEADS ** -0.5)
            score = score + jnp.maximum(sc[hh * tq:(hh + 1) * tq] * (IDX_DIM ** -0.5), 0.0) * wi
        o_ref[:, 0:w] = jnp.where(visible, score, NEG_INF)
        if w < lk:
            o_ref[:, w:] = jnp.full((tq, lk - w), NEG_INF, F32)

    _on_causal_width(qi, tq, widths, tile)


def _topk_body(s_ref, m_ref, key_ref, *, k, nq, tr, widths):
    lk = s_ref.shape[1]
    assert lk <= 4096
    neg_key = int(np.array(NEG_INF, np.float32).view(np.int32)) ^ 0x7FFFFFFF
    kf = float(k)

    def tile(w, maybe_first):
        del maybe_first
        bits = lax.bitcast_convert_type(s_ref[:, 0:w] + 0.0, jnp.int32)
        key_ref[:, 0:w] = jnp.where(bits >= 0, bits, bits ^ 0x7FFFFFFF)
        col = lax.broadcasted_iota(jnp.int32, (1, w), 1)
        unseen = float(lk - w)

        groups = [slice(a * (tr // TOPK_ROW_GROUPS), (a + 1) * (tr // TOPK_ROW_GROUPS)) for a in range(TOPK_ROW_GROUPS)]
        zeros = tuple(jnp.zeros((tr // TOPK_ROW_GROUPS, 1), jnp.int32) for _ in groups)

        def thr_step(i, tus):
            out = []
            for rows, tu in zip(groups, tus):
                cand = tu | jnp.left_shift(jnp.int32(1), 31 - i)
                cs = cand ^ INT_MIN
                cnt = jnp.sum(jnp.where(key_ref[rows, 0:w] >= cs, 1.0, 0.0), -1, keepdims=True)
                cnt = cnt + jnp.where(cs <= neg_key, unseen, 0.0)
                out.append(jnp.where(cnt >= kf, cand, tu))
            return tuple(out)

        thr = jnp.concatenate(lax.fori_loop(0, 32, thr_step, zeros, unroll=4), 0) ^ INT_MIN
        key = key_ref[:, 0:w]
        above = key > thr
        tied = key == thr
        need = kf - jnp.sum(jnp.where(above, 1.0, 0.0), -1, keepdims=True)

        def tie_step(i, j0s):
            out = []
            for rows, j0 in zip(groups, j0s):
                cand = j0 | jnp.left_shift(jnp.int32(1), 11 - i)
                hit = (key_ref[rows, 0:w] == thr[rows]) & (col < cand)
                cnt = jnp.sum(jnp.where(hit, 1.0, 0.0), -1, keepdims=True)
                out.append(jnp.where(cnt < need[rows], cand, j0))
            return tuple(out)

        j0 = jnp.concatenate(lax.fori_loop(0, 12, tie_step, zeros, unroll=4), 0)
        m_ref[:, 0:w] = jnp.where(above | (tied & (col <= j0)), 1.0, 0.0)
        if w < lk:
            m_ref[:, w:] = jnp.zeros((tr, lk - w), F32)

    _on_causal_width(pl.program_id(0) % nq, tr, widths, tile)


class _Group:
    def __init__(self, row0, n_seq, t_new, tq, pos0, n_pages):
        assert t_new % tq == 0 and row0 % tq == 0 and row0 % t_new == 0
        assert pos0 == n_pages * PAGE or n_pages * PAGE < pos0
        assert t_new == tq or (tq == PAGE and pos0 == 0)
        assert t_new % PAGE == 0 or t_new % PAGE < NSA_BLOCK
        self.row0, self.n_seq, self.t_new, self.tq, self.pos0, self.n_pages = row0, n_seq, t_new, tq, pos0, n_pages
        self.nq = t_new // tq
        self.rows = n_seq * t_new
        self.lk = (n_pages + -(-t_new // PAGE)) * PAGE
        self.n_keys = n_pages * PAGE + t_new

    def q_spec(self, width, col):
        return pl.BlockSpec((self.tq, width), lambda b, qi, pt: (self.row0 // self.tq + b * self.nq + qi, col // width))

    def seq_spec(self, width, col):
        return pl.BlockSpec((self.t_new, width), lambda b, qi, pt: (self.row0 // self.t_new + b, col // width))

    def page_specs(self, shape):
        return [pl.BlockSpec((1,) + shape, lambda b, qi, pt, p=p: (pt[b, p], 0, 0)) for p in range(self.n_pages)]

    def out_spec(self, width):
        return pl.BlockSpec((self.tq, width), lambda b, qi, pt: (b * self.nq + qi, 0))

    def statics(self):
        return dict(n_pages=self.n_pages, t_new=self.t_new, tq=self.tq, pos0=self.pos0)

    def widths(self):
        if self.nq == 1:
            return (self.lk,)
        step = 4 * PAGE
        assert self.lk % step == 0
        return tuple(range(step, self.lk + 1, step))


def _cmp_bias_table(rel_bias, grp):
    lane0 = 0 if grp.nq == 1 else CMP_BIAS_LANE0
    tab = rel_bias[_BUCKETS]
    pieces, n_far = [], 0
    for lane in range(LANES + 1):
        d0 = grp.pos0 - (NSA_BLOCK * (lane - lane0 + 1) - 1)
        plain = lane < LANES and (d0 >= PAGE or d0 + grp.tq - 1 < 0)
        if plain:
            n_far += 1
            continue
        if n_far:
            pieces.append(jnp.broadcast_to(rel_bias[N_BUCKETS - 1], (grp.tq, n_far, rel_bias.shape[1])))
            n_far = 0
        if lane < LANES:
            pieces.append(tab[np.clip(d0 + np.arange(grp.tq), 0, BAND - 1)][:, None, :])
    table = jnp.transpose(jnp.concatenate(pieces, 1), (2, 0, 1))
    return table.reshape(-1, GROUP_HEADS * grp.tq, LANES)


def nsa_compress(grp, z, page_table, pool, wexp, bias):
    kv_w = KV_GROUPS * 2 * HEAD_DIM
    qw = KV_GROUPS * GROUP_HEADS * HEAD_DIM
    const2 = lambda b, qi, pt: (0, 0)
    return pl.pallas_call(
        functools.partial(_cmp_body, **grp.statics()),
        grid_spec=pltpu.PrefetchScalarGridSpec(
            num_scalar_prefetch=1,
            grid=(grp.n_seq, grp.nq),
            in_specs=[grp.q_spec(qw, COL_QN), grp.seq_spec(kv_w, COL_KVC)] + grp.page_specs(KV_PAGE)
            + [pl.BlockSpec((PAGE, kv_w), const2), pl.BlockSpec(bias.shape, lambda b, qi, pt: (0, 0, 0))],
            out_specs=[grp.out_spec(qw), grp.out_spec(KV_GROUPS * LANES)],
            scratch_shapes=[pltpu.VMEM((LANES, kv_w), F32),
                            pltpu.VMEM((KV_GROUPS, LANES, HEAD_DIM), BF16),
                            pltpu.VMEM((KV_GROUPS, LANES, HEAD_DIM), BF16)]),
        out_shape=[jax.ShapeDtypeStruct((grp.rows, qw), F32),
                   jax.ShapeDtypeStruct((grp.rows, KV_GROUPS * LANES), F32)],
        compiler_params=_cparams("parallel", "arbitrary"),
        name="nsa_compress",
    )(page_table, z, z, *([pool] * grp.n_pages), wexp, bias)


def sparse_attention(mode, grp, z, page_table, pool, band, *, q_col, kv_col, mask=None):
    kv_w = KV_GROUPS * 2 * HEAD_DIM
    qw = KV_GROUPS * GROUP_HEADS * HEAD_DIM
    in_specs = [grp.q_spec(qw, q_col), grp.seq_spec(kv_w, kv_col)] + grp.page_specs(KV_PAGE)
    args = [z, z] + [pool] * grp.n_pages
    if mode in ("sel", "mask"):
        in_specs.append(grp.out_spec(mask.shape[1]))
        args.append(mask)
    in_specs.append(pl.BlockSpec((KV_GROUPS, GROUP_HEADS * grp.tq, BAND), lambda b, qi, pt: (0, 0, 0)))
    args.append(band)
    widths = grp.widths()
    s_cols = max(widths) if (mode != "win" or grp.nq == 1) else NSA_WINDOW + PAGE
    return pl.pallas_call(
        functools.partial(_attn_body, mode=mode, widths=widths, **grp.statics()),
        grid_spec=pltpu.PrefetchScalarGridSpec(
            num_scalar_prefetch=1,
            grid=(grp.n_seq, grp.nq),
            in_specs=in_specs,
            out_specs=grp.out_spec(qw),
            scratch_shapes=[pltpu.VMEM((KV_GROUPS, grp.lk, HEAD_DIM), BF16),
                            pltpu.VMEM((KV_GROUPS, grp.lk, HEAD_DIM), BF16),
                            pltpu.VMEM((KV_GROUPS, GROUP_HEADS * grp.tq, s_cols), F32)]),
        out_shape=jax.ShapeDtypeStruct((grp.rows, qw), F32),
        compiler_params=_cparams("parallel", "arbitrary"),
        name="sparse_attention_" + mode,
    )(page_table, *args)


def dsa_index_scores(grp, z, page_table, idx_pool):
    return pl.pallas_call(
        functools.partial(_index_body, widths=grp.widths(), **grp.statics()),
        grid_spec=pltpu.PrefetchScalarGridSpec(
            num_scalar_prefetch=1,
            grid=(grp.n_seq, grp.nq),
            in_specs=[grp.q_spec(IDX_HEADS * IDX_DIM, COL_QI), grp.q_spec(LANES, COL_MISC),
                      grp.seq_spec(LANES, COL_MISC)] + grp.page_specs((IDX_DIM, PAGE)),
            out_specs=grp.out_spec(grp.lk),
            scratch_shapes=[pltpu.VMEM((IDX_DIM, grp.lk), BF16)]),
        out_shape=jax.ShapeDtypeStruct((grp.rows, grp.lk), F32),
        compiler_params=_cparams("parallel", "arbitrary"),
        name="dsa_index_scores",
    )(page_table, z, z, z, *([idx_pool] * grp.n_pages))


def topk_mask(grp, scores, k):
    rows, lk = scores.shape
    tr = PAGE
    assert rows % tr == 0 and (grp.nq == 1 or grp.tq == tr)
    blk = pl.BlockSpec((tr, lk), lambda i: (i, 0))
    return pl.pallas_call(
        functools.partial(_topk_body, k=k, nq=grp.nq, tr=tr, widths=grp.widths()),
        grid=(rows // tr,),
        in_specs=[blk],
        out_specs=blk,
        out_shape=jax.ShapeDtypeStruct((rows, lk), F32),
        scratch_shapes=[pltpu.VMEM((tr, lk), jnp.int32)],
        compiler_params=_cparams("parallel"),
        name="topk_mask",
    )(scores)


def _combine_body(oc_ref, os_ref, ow_ref, od_ref, misc_ref, y_ref):
    n_heads = KV_GROUPS * GROUP_HEADS
    gates = jax.nn.sigmoid(misc_ref[:, MISC_GATES:MISC_GATES + 3 * n_heads])
    for h in range(n_heads):
        hs = slice(h * HEAD_DIM, (h + 1) * HEAD_DIM)
        o = (gates[:, 3 * h:3 * h + 1] * oc_ref[:, hs] + gates[:, 3 * h + 1:3 * h + 2] * os_ref[:, hs]
             + gates[:, 3 * h + 2:3 * h + 3] * ow_ref[:, hs])
        y_ref[:, hs] = o.astype(y_ref.dtype)
    y_ref[:, n_heads * HEAD_DIM:] = od_ref[...].astype(y_ref.dtype)


def nsa_dsa_combine(o_c, o_s, o_w, o_d, z, row0, *, tm):
    m, w = o_c.shape
    assert m % tm == 0 and row0 % tm == 0
    blk = pl.BlockSpec((tm, w), lambda i: (i, 0))
    return pl.pallas_call(
        _combine_body,
        grid=(m // tm,),
        in_specs=[blk, blk, blk, blk, pl.BlockSpec((tm, LANES), lambda i: (row0 // tm + i, COL_MISC // LANES))],
        out_specs=pl.BlockSpec((tm, 2 * w), lambda i: (i, 0)),
        out_shape=jax.ShapeDtypeStruct((m, 2 * w), BF16),
        compiler_params=_cparams("parallel"),
        name="nsa_dsa_combine",
    )(o_c, o_s, o_w, o_d, z)


def _band_tiles(rel_bias, tq):
    delta = (rel_bias[_BUCKETS] - rel_bias[N_BUCKETS - 1]).T
    rev = jnp.concatenate([delta[:, ::-1], jnp.zeros((delta.shape[0], PAGE), delta.dtype)], 1)
    tiles = jnp.stack([rev[:, PAGE - 1 - i:PAGE - 1 - i + BAND] for i in range(tq)], 1)
    return tiles.reshape(-1, GROUP_HEADS * tq, BAND) * LOG2E


def _widen_cd_w_in(w):
    sizes = (1024, 512, 512, 512, 24, 1024, 512, 512, 64, 8)
    q_n, kv_c, kv_s, kv_w, gates, q_d, kv_d, q_i, k_i, w_i = jnp.split(w, np.cumsum(sizes)[:-1].tolist(), axis=1)
    cols = [q_n, q_d, kv_c, kv_s, kv_w, kv_d, q_i, k_i, gates, w_i]
    used = sum(c.shape[1] for c in cols)
    return jnp.concatenate(cols + [jnp.zeros((w.shape[0], NZ - used), w.dtype)], axis=1)


def kernel(x_prompt, x_sample, state_conv, state_pool, cache_nsa_cmp, cache_nsa_sel, cache_nsa_win, cache_dsa_kv, cache_dsa_idx, page_table, norm_mix, norm_ffn, norm_final, ab_w_in, ab_conv_w, ab_conv_b, ab_ln_g, ab_ln_b, ab_pool_w, ab_pool_scale, ab_w_out, cd_w_in, cd_w_cmp, cd_w_out, rel_bias, ffn_w1, ffn_w2):
    bp, tp, d_model = x_prompt.shape
    bs, ts, _ = x_sample.shape
    mp, ms = bp * tp, bs * ts
    depth = norm_mix.shape[0]
    n_pages = page_table.shape[1]
    n_pool = cache_nsa_cmp.shape[1]
    past_len = n_pages * PAGE
    assert cache_nsa_cmp.shape[2] == PAGE
    win_len = cache_nsa_win.shape[2]
    assert win_len % PAGE == 0 and win_len == NSA_WINDOW and tp >= NSA_WINDOW
    kv_w = KV_GROUPS * 2 * HEAD_DIM

    xs = [x_prompt.reshape(mp, d_model), x_sample.reshape(ms, d_model)]
    grp_p = _Group(0, bp, tp, PAGE, 0, 0)
    grp_s = _Group(mp, bs, ts, ts, past_len, n_pages)
    grp_sw = _Group(mp, bs, ts, ts, past_len, win_len // PAGE)
    no_pages = jnp.zeros((1, 1), jnp.int32)
    win_pages = jnp.arange(bs * (win_len // PAGE), dtype=jnp.int32).reshape(bs, win_len // PAGE)

    outs = {k: [] for k in ("conv_p", "conv_s", "pool_p", "pool_s", "cmp_p", "cmp_s", "sel_p", "sel_s",
                            "win_p", "win_s", "dsa_p", "dsa_s", "idx_p", "idx_s")}
    y_p = y_s = None
    for i in range(depth):
        j = i // 2
        if i % 2 == 0:
            d_conv = ab_conv_w.shape[2]
            z = norm_matmul(xs, norm_mix[i], ab_w_in[j].astype(BF16))
            mid_p, u_p = ab_mid_prompt(z, bp, tp, ab_conv_w[j], ab_conv_b[j], ab_ln_g[j], ab_ln_b[j],
                                       ab_pool_w[j], ab_pool_scale[j])
            mid_s, conv_s, pool_s = ab_mid_step(z, mp, bs, ts, past_len, state_conv[j], state_pool[j], ab_conv_w[j],
                                                ab_conv_b[j], ab_ln_g[j], ab_ln_b[j], ab_pool_w[j], ab_pool_scale[j])
            xs = [matmul_residual([mid_p, mid_s], ab_w_out[j].astype(BF16), xs)]
            outs["conv_p"].append(u_p.reshape(bp, tp, d_conv)[:, tp - CONV_BUF:])
            outs["conv_s"].append(conv_s)
            outs["pool_p"].append(z[:mp, 2 * d_conv:].reshape(bp, tp, -1)[:, tp - POOL_BUF:])
            outs["pool_s"].append(pool_s)
        else:
            z = norm_matmul(xs, norm_mix[i], _widen_cd_w_in(cd_w_in[j]).astype(BF16))
            nsa_bias = rel_bias[:, :KV_GROUPS * GROUP_HEADS]
            band_p, band_s = _band_tiles(rel_bias, grp_p.tq), _band_tiles(rel_bias, grp_s.tq)
            wexp = jnp.tile(jnp.repeat(jnp.transpose(cd_w_cmp[j], (1, 0, 2)).reshape(NSA_BLOCK, 2 * KV_GROUPS),
                                       HEAD_DIM, axis=1), (PAGE // NSA_BLOCK, 1))
            pt = page_table + j * n_pool
            pools = [c.reshape((-1,) + KV_PAGE) for c in (cache_nsa_cmp, cache_nsa_sel, cache_dsa_kv)]
            idx_pool = jnp.swapaxes(cache_dsa_idx, 2, 3).reshape(-1, IDX_DIM, PAGE)
            win_pool = cache_nsa_win.reshape((-1,) + KV_PAGE)
            wpt = win_pages + j * bs * (win_len // PAGE)
            mids = []
            for grp, gw, ptab, wtab, band in ((grp_p, grp_p, no_pages, no_pages, band_p),
                                              (grp_s, grp_sw, pt, wpt, band_s)):
                o_c, msel = nsa_compress(grp, z, ptab, pools[0], wexp, _cmp_bias_table(nsa_bias, grp))
                o_s = sparse_attention("sel", grp, z, ptab, pools[1], band[:KV_GROUPS],
                                       q_col=COL_QN, kv_col=COL_KVS, mask=msel)
                o_w = sparse_attention("win", gw, z, wtab, win_pool, band[:KV_GROUPS],
                                       q_col=COL_QN, kv_col=COL_KVW)
                top = topk_mask(grp, dsa_index_scores(grp, z, ptab, idx_pool), min(DSA_TOPK, grp.n_keys // 4))
                o_d = sparse_attention("mask", grp, z, ptab, pools[2], band[KV_GROUPS:],
                                       q_col=COL_QD, kv_col=COL_KVD, mask=top)
                mids.append(nsa_dsa_combine(o_c, o_s, o_w, o_d, z, grp.row0, tm=min(512, grp.rows)))
            xs = [matmul_residual(mids, cd_w_out[j].astype(BF16), xs)]

            def kv_out(col, width, tail):
                seg = z[:, col:col + width]
                return seg[:mp].reshape((bp, tp) + tail), seg[mp:].reshape((bs, ts) + tail)

            kv_tail = (2, KV_GROUPS, HEAD_DIM)
            for name, col in (("cmp", COL_KVC), ("sel", COL_KVS), ("dsa", COL_KVD)):
                p_new, s_new = kv_out(col, kv_w, kv_tail)
                outs[name + "_p"].append(p_new)
                outs[name + "_s"].append(s_new)
            w_p, w_s = kv_out(COL_KVW, kv_w, kv_tail)
            outs["win_p"].append(w_p[:, tp - NSA_WINDOW:])
            outs["win_s"].append(jnp.concatenate([cache_nsa_win[j], w_s], 1)[:, ts:])
            i_p, i_s = kv_out(COL_MISC + MISC_KI, IDX_DIM, (IDX_DIM,))
            outs["idx_p"].append(i_p)
            outs["idx_s"].append(i_s)
        a = norm_matmul(xs, norm_ffn[i], ffn_w1[i].astype(BF16), relu2=True, out_dtype=BF16)
        if i == depth - 1:
            y_p, y_s = matmul_residual([a], ffn_w2[i].astype(BF16), xs, norm_final, split_out=(mp, ms))
        else:
            xs = [matmul_residual([a], ffn_w2[i].astype(BF16), xs)]

    st = {k: jnp.stack(v) for k, v in outs.items()}
    return (y_p.reshape(bp, tp, d_model), y_s.reshape(bs, ts, d_model),
            st["conv_p"], st["conv_s"], st["pool_p"], st["pool_s"], st["cmp_p"], st["cmp_s"],
            st["sel_p"], st["sel_s"], st["win_p"], st["win_s"], st["dsa_p"], st["dsa_s"],
            st["idx_p"], st["idx_s"])
```

```python
import functools
import math

import numpy as np
import jax
import jax.numpy as jnp
from jax import lax
from jax.experimental import pallas as pl
from jax.experimental.pallas import tpu as pltpu

F32 = jnp.float32
BF16 = jnp.bfloat16

EPS = 1e-6
NEG_INF = -1e30
LOG2E = math.log2(math.e)
HEAD_DIM = 128
LANES = 128
BF16_SUBLANES = 16
CONV_WIDTH = 31
CONV_BUF = CONV_WIDTH - 1
POOL_WINDOWS = (2, 4, 8, 16)
POOL_BUF = max(POOL_WINDOWS) - 1
HALO = 32
VMEM_LIMIT = 56 * 1024 * 1024


def _cparams(*sem):
    return pltpu.CompilerParams(dimension_semantics=sem, vmem_limit_bytes=VMEM_LIMIT)


class _Rows:
    def __init__(self, arrays, tm):
        self.arrays = list(arrays)
        self.tm = tm
        assert all(a.shape[0] % tm == 0 for a in self.arrays)
        self.tiles = [a.shape[0] // tm for a in self.arrays]
        self.n_tiles = sum(self.tiles)
        self.n = len(self.arrays)

    def specs(self, width, col):
        out, t0 = [], 0
        for nt in self.tiles:
            out.append(pl.BlockSpec((self.tm, width), lambda i, j, t0=t0, nt=nt: (jnp.clip(i - t0, 0, nt - 1), col(j))))
            t0 += nt
        return out

    def select(self, i, refs, fn):
        if self.n == 1:
            fn(refs[0])
            return
        t0 = 0
        for nt, ref in zip(self.tiles, refs):
            pl.when((i >= t0) & (i < t0 + nt))(functools.partial(fn, ref))
            t0 += nt


def _rmsnorm_rows(x, g):
    return (x * lax.rsqrt(jnp.mean(x * x, -1, keepdims=True) + EPS)) * g


def _norm_matmul_body(*refs, rows, relu2):
    x_refs = refs[:rows.n]
    g_ref, w_ref, o_ref, h_ref = refs[rows.n:]

    @pl.when(pl.program_id(1) == 0)
    def _():
        def norm(x_ref):
            h_ref[...] = _rmsnorm_rows(x_ref[...], g_ref[...]).astype(BF16)

        rows.select(pl.program_id(0), x_refs, norm)

    y = jnp.dot(h_ref[...], w_ref[...], preferred_element_type=F32)
    if relu2:
        y = jnp.square(jnp.maximum(y, 0.0))
    o_ref[...] = y.astype(o_ref.dtype)


def norm_matmul(xs, g, w, layer, *, relu2=False, out_dtype=F32, tm=1024, tn=1024):
    rows = _Rows(xs, tm)
    _, d, n = w.shape
    assert n % tn == 0
    return pl.pallas_call(
        functools.partial(_norm_matmul_body, rows=rows, relu2=relu2),
        grid=(rows.n_tiles, n // tn),
        in_specs=rows.specs(d, lambda j: 0) + [pl.BlockSpec((1, d), lambda i, j: (0, 0)),
                                               pl.BlockSpec((None, d, tn), lambda i, j: (layer, 0, j))],
        out_specs=pl.BlockSpec((tm, tn), lambda i, j: (i, j)),
        out_shape=jax.ShapeDtypeStruct((rows.n_tiles * tm, n), out_dtype),
        scratch_shapes=[pltpu.VMEM((tm, d), BF16)],
        compiler_params=_cparams("parallel", "arbitrary"),
        name="norm_matmul",
    )(*rows.arrays, g.reshape(1, d), w)


def _matmul_residual_body(*refs, a_rows, r_rows, o_rows, final_norm):
    a_refs, refs = refs[:a_rows.n], refs[a_rows.n:]
    w_ref, refs = refs[0], refs[1:]
    r_refs, refs = refs[:r_rows.n], refs[r_rows.n:]
    if final_norm:
        g_ref, refs = refs[0], refs[1:]
    o_refs, acc_ref = refs[:o_rows.n], refs[o_rows.n]
    i, k = pl.program_id(0), pl.program_id(1)

    @pl.when(k == 0)
    def _():
        acc_ref[...] = jnp.zeros_like(acc_ref)

    def accumulate(a_ref):
        acc_ref[...] += jnp.dot(a_ref[...], w_ref[...], preferred_element_type=F32)

    a_rows.select(i, a_refs, accumulate)

    @pl.when(k == pl.num_programs(1) - 1)
    def _():
        def add_residual(r_ref):
            acc_ref[...] += r_ref[...]

        def write(o_ref):
            o = acc_ref[...]
            o_ref[...] = _rmsnorm_rows(o, g_ref[...]) if final_norm else o

        r_rows.select(i, r_refs, add_residual)
        o_rows.select(i, o_refs, write)


def matmul_residual(a_list, w, layer, r_list, g_final=None, *, split_out=None, tm=512, tk=2048):
    a_rows, r_rows = _Rows(a_list, tm), _Rows(r_list, tm)
    _, kdim, n = w.shape
    m = a_rows.n_tiles * tm
    assert kdim % tk == 0 and r_rows.n_tiles == a_rows.n_tiles
    final_norm = g_final is not None
    o_rows = _Rows([jax.ShapeDtypeStruct((r, n), F32) for r in (split_out or (m,))], tm)
    assert o_rows.n_tiles == a_rows.n_tiles
    in_specs = (a_rows.specs(tk, lambda k: k) + [pl.BlockSpec((None, tk, n), lambda i, k: (layer, k, 0))]
                + r_rows.specs(n, lambda k: 0))
    args = a_rows.arrays + [w] + r_rows.arrays
    if final_norm:
        in_specs.append(pl.BlockSpec((1, n), lambda i, k: (0, 0)))
        args.append(g_final.reshape(1, n))
    out = pl.pallas_call(
        functools.partial(_matmul_residual_body, a_rows=a_rows, r_rows=r_rows, o_rows=o_rows, final_norm=final_norm),
        grid=(a_rows.n_tiles, kdim // tk),
        in_specs=in_specs,
        out_specs=o_rows.specs(n, lambda k: 0),
        out_shape=o_rows.arrays,
        scratch_shapes=[pltpu.VMEM((tm, n), F32)],
        compiler_params=_cparams("parallel", "arbitrary"),
        name="matmul_residual",
    )(*args)
    return out if split_out else out[0]


def _layernorm_silu(c, g, b):
    mu = jnp.mean(c, -1, keepdims=True)
    xc = c - mu
    y = xc * lax.rsqrt(jnp.mean(xc * xc, -1, keepdims=True) + EPS)
    y = y * g + b
    return y * jax.nn.sigmoid(y)


def _ab_mid_body(z_ref, zp_ref, cw_ref, cb_ref, lg_ref, lb_ref, pw_ref, ps_ref, y_ref, u_ref,
                 ext_ref, vext_ref, conv_ref, *, tt, d_conv, d_pool):
    ti = pl.program_id(1)
    keep = (ti > 0).astype(F32)
    a_p = zp_ref[:, 0:d_conv]
    g_p = zp_ref[:, d_conv:2 * d_conv]
    ext_ref[0:HALO, :] = a_p * jax.nn.sigmoid(g_p) * keep
    vext_ref[0:HALO, :] = zp_ref[:, 2 * d_conv:] * keep
    u = z_ref[:, 0:d_conv] * jax.nn.sigmoid(z_ref[:, d_conv:2 * d_conv])
    ext_ref[HALO:, :] = u
    u_ref[...] = u
    vext_ref[HALO:, :] = z_ref[:, 2 * d_conv:]

    off = HALO - CONV_BUF
    for c in range(d_conv // LANES):
        cs = slice(c * LANES, (c + 1) * LANES)
        acc = jnp.zeros((tt, LANES), F32)
        for j in range(CONV_WIDTH):
            acc = acc + cw_ref[j:j + 1, cs] * ext_ref[off + j:off + j + tt, cs]
        conv_ref[:, cs] = acc + cb_ref[:, cs]
    y_ref[:, 0:d_conv] = _layernorm_silu(conv_ref[...], lg_ref[...], lb_ref[...]).astype(y_ref.dtype)

    pos = ti * tt + lax.broadcasted_iota(jnp.int32, (tt, 1), 0)
    pg = d_pool // len(POOL_WINDOWS)
    for gi, w in enumerate(POOL_WINDOWS):
        gs = slice(gi * pg, (gi + 1) * pg)
        tok = vext_ref[HALO:, gs]
        acc = tok
        for i in range(1, w):
            acc = acc + vext_ref[HALO - i:HALO - i + tt, gs]
        cnt = jnp.minimum(pos + 1, w).astype(F32)
        d = acc / cnt - tok
        yp = jnp.dot(d.astype(BF16), pw_ref[gi], preferred_element_type=F32) * ps_ref[:, gs]
        y_ref[:, d_conv + gi * pg:d_conv + (gi + 1) * pg] = yp.astype(y_ref.dtype)


def ab_mid_prompt(z, n_seq, t_len, conv_w, conv_b, ln_g, ln_b, pool_w, pool_scale, *, tt=256):
    d_conv = conv_w.shape[1]
    d_pool = pool_scale.shape[0]
    nt = t_len // tt
    hb = tt // HALO
    row = lambda b, t: (b * nt + t, 0)
    const = lambda b, t: (0, 0)
    return pl.pallas_call(
        functools.partial(_ab_mid_body, tt=tt, d_conv=d_conv, d_pool=d_pool),
        grid=(n_seq, nt),
        in_specs=[pl.BlockSpec((tt, z.shape[1]), row),
                  pl.BlockSpec((HALO, z.shape[1]), lambda b, t: (jnp.maximum((b * nt + t) * hb - 1, 0), 0)),
                  pl.BlockSpec(conv_w.shape, const),
                  pl.BlockSpec((1, d_conv), const),
                  pl.BlockSpec((1, d_conv), const),
                  pl.BlockSpec((1, d_conv), const),
                  pl.BlockSpec(pool_w.shape, lambda b, t: (0, 0, 0)),
                  pl.BlockSpec((1, d_pool), const)],
        out_specs=[pl.BlockSpec((tt, d_conv + d_pool), row),
                   pl.BlockSpec((tt, d_conv), row)],
        out_shape=[jax.ShapeDtypeStruct((n_seq * t_len, d_conv + d_pool), BF16),
                   jax.ShapeDtypeStruct((n_seq * t_len, d_conv), F32)],
        scratch_shapes=[pltpu.VMEM((HALO + tt, d_conv), F32),
                        pltpu.VMEM((HALO + tt, d_pool), F32),
                        pltpu.VMEM((tt, d_conv), F32)],
        compiler_params=_cparams("parallel", "parallel"),
        name="ab_mid_prompt",
    )(z, z, conv_w, conv_b.reshape(1, -1), ln_g.reshape(1, -1), ln_b.reshape(1, -1),
      pool_w.astype(BF16), pool_scale.reshape(1, -1))


def _ab_mid_step_body(z_ref, sc_ref, sp_ref, cw_ref, cb_ref, lg_ref, lb_ref, pw_ref, ps_ref,
                      y_ref, nc_ref, np_ref, ext_ref, vext_ref, *, nb, t, pos0, d_conv, d_pool):
    e0 = HALO - CONV_BUF
    p0 = 16 - POOL_BUF
    z = z_ref[...].reshape(nb, t, z_ref.shape[1])
    u = z[:, :, 0:d_conv] * jax.nn.sigmoid(z[:, :, d_conv:2 * d_conv])
    ext_ref[:, e0:HALO, :] = sc_ref[...]
    ext_ref[:, HALO:, :] = u
    vext_ref[:, p0:16, :] = sp_ref[...]
    vext_ref[:, 16:, :] = z[:, :, 2 * d_conv:]
    nc_ref[...] = ext_ref[:, HALO + t - CONV_BUF:, :]
    np_ref[...] = vext_ref[:, 16 + t - POOL_BUF:, :]

    acc = jnp.zeros((nb, t, d_conv), F32)
    for j in range(CONV_WIDTH):
        acc = acc + cw_ref[j:j + 1, :][None] * ext_ref[:, e0 + j:e0 + j + t, :]
    c = acc + cb_ref[...][None]
    yc = _layernorm_silu(c, lg_ref[...][None], lb_ref[...][None])
    y_ref[:, 0:d_conv] = yc.reshape(nb * t, d_conv).astype(y_ref.dtype)

    pg = d_pool // len(POOL_WINDOWS)
    for gi, w in enumerate(POOL_WINDOWS):
        gs = slice(gi * pg, (gi + 1) * pg)
        tok = vext_ref[:, 16:, gs]
        acc = tok
        for i in range(1, w):
            acc = acc + vext_ref[:, 16 - i:16 - i + t, gs]
        cnt = jnp.minimum(pos0 + 1 + lax.broadcasted_iota(jnp.int32, (1, t, 1), 1), w).astype(F32)
        d = (acc / cnt - tok).reshape(nb * t, pg)
        yp = jnp.dot(d.astype(BF16), pw_ref[gi], preferred_element_type=F32) * ps_ref[:, gs]
        y_ref[:, d_conv + gi * pg:d_conv + (gi + 1) * pg] = yp.astype(y_ref.dtype)


def ab_mid_step(z, row0, n_seq, t, pos0, state_conv, state_pool, conv_w, conv_b, ln_g, ln_b, pool_w,
                pool_scale, *, nb=16):
    d_conv = conv_w.shape[1]
    d_pool = pool_scale.shape[0]
    rb = nb * t
    assert row0 % rb == 0 and n_seq % nb == 0
    const = lambda i: (0, 0)
    seq3 = lambda i: (i, 0, 0)
    return pl.pallas_call(
        functools.partial(_ab_mid_step_body, nb=nb, t=t, pos0=pos0, d_conv=d_conv, d_pool=d_pool),
        grid=(n_seq // nb,),
        in_specs=[pl.BlockSpec((rb, z.shape[1]), lambda i: (row0 // rb + i, 0)),
                  pl.BlockSpec((nb, CONV_BUF, d_conv), seq3),
                  pl.BlockSpec((nb, POOL_BUF, d_pool), seq3),
                  pl.BlockSpec(conv_w.shape, const),
                  pl.BlockSpec((1, d_conv), const),
                  pl.BlockSpec((1, d_conv), const),
                  pl.BlockSpec((1, d_conv), const),
                  pl.BlockSpec(pool_w.shape, lambda i: (0, 0, 0)),
                  pl.BlockSpec((1, d_pool), const)],
        out_specs=[pl.BlockSpec((rb, d_conv + d_pool), lambda i: (i, 0)),
                   pl.BlockSpec((nb, CONV_BUF, d_conv), seq3),
                   pl.BlockSpec((nb, POOL_BUF, d_pool), seq3)],
        out_shape=[jax.ShapeDtypeStruct((n_seq * t, d_conv + d_pool), BF16),
                   jax.ShapeDtypeStruct((n_seq, CONV_BUF, d_conv), F32),
                   jax.ShapeDtypeStruct((n_seq, POOL_BUF, d_pool), F32)],
        scratch_shapes=[pltpu.VMEM((nb, HALO + t, d_conv), F32),
                        pltpu.VMEM((nb, 16 + t, d_pool), F32)],
        compiler_params=_cparams("parallel"),
        name="ab_mid_step",
    )(z, state_conv, state_pool, conv_w, conv_b.reshape(1, -1), ln_g.reshape(1, -1),
      ln_b.reshape(1, -1), pool_w.astype(BF16), pool_scale.reshape(1, -1))


N_BUCKETS = 32
MAX_DISTANCE = 128
NSA_BLOCK = 64
NSA_TOPN = 16
NSA_WINDOW = 512
DSA_TOPK = 256
IDX_HEADS = 8
IDX_DIM = 64
KV_GROUPS = 2
GROUP_HEADS = 4
PAGE = 128
BAND = 2 * PAGE
INT_MIN = -2 ** 31
KV_PAGE = (2 * KV_GROUPS * PAGE, HEAD_DIM)
TOPK_ROW_GROUPS = 4
CMP_BIAS_LANE0 = 64

COL_QN, COL_QD, COL_KVC, COL_KVS, COL_KVW, COL_KVD, COL_QI, COL_MISC = 0, 1024, 2048, 2560, 3072, 3584, 4096, 4608
MISC_KI, MISC_GATES, MISC_WI = 0, 64, 88
NZ = 5120


def _bucket_np(n):
    n = np.maximum(np.asarray(n, np.int32), 0)
    exact = N_BUCKETS // 2
    nf = np.maximum(n, 1).astype(np.float32)
    big = exact + (np.log(nf / np.float32(exact)) / np.float32(math.log(MAX_DISTANCE / exact))
                   * np.float32(N_BUCKETS - exact)).astype(np.int32)
    return np.where(n < exact, n, np.minimum(big, N_BUCKETS - 1))


_BUCKETS = _bucket_np(np.arange(BAND))
assert _BUCKETS[PAGE:].min() == N_BUCKETS - 1


def _softmax_rows(s, mask):
    s = jnp.where(mask, s, NEG_INF)
    m = jnp.max(s, -1, keepdims=True)
    p = jnp.where(mask, jnp.exp(s - m), 0.0)
    return p, jnp.sum(p, -1, keepdims=True)


def _dot_nt(a, b):
    return lax.dot_general(a, b, (((1,), (1,)), ((), ())), preferred_element_type=F32)


def _new_chunks(new_ref, t_new):
    chunks = [new_ref[c * PAGE:(c + 1) * PAGE, :] for c in range(t_new // PAGE)]
    rem = t_new % PAGE
    if rem:
        tail = new_ref[(t_new // PAGE) * PAGE:, :]
        chunks.append(jnp.concatenate([tail, jnp.zeros((PAGE - rem, tail.shape[1]), F32)], 0))
    return chunks


def _kv_chunks(page_refs, new_ref, t_new):
    n_parts = 2 * KV_GROUPS
    chunks = [[r[0, pl.ds(part, PAGE, stride=n_parts), :] for part in range(n_parts)] for r in page_refs]
    for x in _new_chunks(new_ref, t_new):
        chunks.append([x[:, part * HEAD_DIM:(part + 1) * HEAD_DIM] for part in range(n_parts)])
    return chunks


def _cmp_body(pt_ref, q_ref, kvn_ref, *rest, n_pages, t_new, tq, pos0):
    del pt_ref
    page_refs = rest[:n_pages]
    wexp_ref, bias_ref, o_ref, msel_ref, comp_ref, ck_ref, cv_ref = rest[n_pages:]
    qi = pl.program_id(1)
    n_keys = n_pages * PAGE + t_new
    n_cmp = n_keys // NSA_BLOCK
    n_sel = -(-n_keys // NSA_BLOCK)
    per = PAGE // NSA_BLOCK

    @pl.when(qi == 0)
    def _():
        comp_ref[...] = jnp.zeros_like(comp_ref)
        chunks = _kv_chunks(page_refs, kvn_ref, t_new)[:n_cmp // per]
        for part in range(2 * KV_GROUPS):
            cols = slice(part * HEAD_DIM, (part + 1) * HEAD_DIM)
            xw = jnp.concatenate([parts[part] * wexp_ref[:, cols] for parts in chunks], 0)
            comp_ref[0:per * len(chunks), cols] = xw.reshape(per * len(chunks), NSA_BLOCK, HEAD_DIM).sum(1)
        for g in range(KV_GROUPS):
            ck_ref[g] = comp_ref[:, g * HEAD_DIM:(g + 1) * HEAD_DIM].astype(BF16)
            cv_ref[g] = comp_ref[:, (KV_GROUPS + g) * HEAD_DIM:(KV_GROUPS + g + 1) * HEAD_DIM].astype(BF16)

    scale = HEAD_DIM ** -0.5
    rows = GROUP_HEADS * tq
    blk = lax.broadcasted_iota(jnp.int32, (1, LANES), 1)
    q0 = pos0 + qi * tq
    assert tq & (tq - 1) == 0
    qpos_st = q0 + (lax.broadcasted_iota(jnp.int32, (rows, 1), 0) & (tq - 1))
    mask = (qpos_st - ((blk + 1) * NSA_BLOCK - 1) >= 0) & (blk < n_cmp)
    cur = (q0 + lax.broadcasted_iota(jnp.int32, (tq, 1), 0)) // NSA_BLOCK
    scores = []
    for g in range(KV_GROUPS):
        heads = [g * GROUP_HEADS + r for r in range(GROUP_HEADS)]
        bias = bias_ref[g]
        if t_new != tq:
            bias = pltpu.roll(bias, qi * (tq // NSA_BLOCK) + (LANES - CMP_BIAS_LANE0), 1)
        q = jnp.concatenate([q_ref[:, h * HEAD_DIM:(h + 1) * HEAD_DIM] for h in heads], 0).astype(BF16)
        scores.append(_dot_nt(q, ck_ref[g]) * scale + bias)
    probs = []
    for g in range(KV_GROUPS):
        p, l = _softmax_rows(scores[g], mask)
        probs.append(p / jnp.maximum(l, 1e-30))
    for g in range(KV_GROUPS):
        o = jnp.dot(probs[g].astype(BF16), cv_ref[g], preferred_element_type=F32)
        for r in range(GROUP_HEADS):
            h = g * GROUP_HEADS + r
            o_ref[:, h * HEAD_DIM:(h + 1) * HEAD_DIM] = o[r * tq:(r + 1) * tq]
    for g in range(KV_GROUPS):
        imp = probs[g][0:tq]
        for r in range(1, GROUP_HEADS):
            imp = imp + probs[g][r * tq:(r + 1) * tq]
        imp = jnp.where(blk == cur, 2.0, jnp.where(blk > cur, -1.0, imp))
        imp = jnp.where(blk < n_sel, imp, -2.0)
        n_top = min(NSA_TOPN, n_sel)
        cols = slice(g * LANES, (g + 1) * LANES)

        def by_rank(imp=imp, cols=cols):
            rank = jnp.zeros((tq, LANES), F32)
            for i in range(n_sel):
                col = imp[:, i:i + 1]
                ahead = (col > imp) | ((col == imp) & (blk > i))
                rank = rank + jnp.where(ahead, 1.0, 0.0)
            msel_ref[:, cols] = jnp.where((rank < float(n_top)) & (blk < n_sel), 1.0, 0.0)

        def first_blocks(cols=cols):
            msel_ref[:, cols] = jnp.where(blk < n_top, 1.0, 0.0) + jnp.zeros((tq, LANES), F32)

        if t_new == tq:
            if pos0 + tq <= n_top * NSA_BLOCK:
                first_blocks()
            else:
                by_rank()
        else:
            early = pos0 + (qi + 1) * tq <= n_top * NSA_BLOCK
            pl.when(early)(first_blocks)
            pl.when(jnp.logical_not(early))(by_rank)


def _on_causal_width(qi, tq, widths, tile):
    if len(widths) == 1:
        tile(widths[0], True)
        return
    need = (qi * tq + tq - 1) // widths[0]
    for nw, w in enumerate(widths):
        pl.when(need == nw)(functools.partial(tile, w, nw == 0))


def _attn_body(pt_ref, q_ref, kvn_ref, *rest, mode, n_pages, t_new, tq, pos0, widths):
    del pt_ref
    page_refs = rest[:n_pages]
    rest = rest[n_pages:]
    m_ref = None
    if mode in ("sel", "mask"):
        m_ref, rest = rest[0], rest[1:]
    band_ref, o_ref, kc_ref, vc_ref, s_ref, cap_ref = rest[:6]
    p_ref = rest[6] if len(rest) > 6 else s_ref
    qi = pl.program_id(1)
    single = t_new == tq
    n_keys = n_pages * PAGE + t_new
    kbase = pos0 - n_pages * PAGE
    scale = HEAD_DIM ** -0.5
    q0 = pos0 if single else pos0 + qi * tq

    @pl.when(qi == 0)
    def _():
        for c, parts in enumerate(_kv_chunks(page_refs, kvn_ref, t_new)):
            rows = slice(c * PAGE, (c + 1) * PAGE)
            for g in range(KV_GROUPS):
                kc_ref[g, rows, :] = parts[g].astype(BF16)
                vc_ref[g, rows, :] = parts[KV_GROUPS + g].astype(BF16)

    def tile(c0, w, band_at, maybe_first):
        qpos = q0 + lax.broadcasted_iota(jnp.int32, (tq, 1), 0)
        col = c0 + lax.broadcasted_iota(jnp.int32, (1, w), 1)
        dist = qpos - (kbase + col)
        visible = (dist >= 0) & (col < n_keys)
        if mode == "win":
            visible = visible & (dist < NSA_WINDOW)
        if mode == "mask":
            visible = visible & (m_ref[:, 0:w] > 0.5)
        keys = pl.ds(c0, w)
        groups = range(KV_GROUPS)
        for g in groups:
            mask = visible
            if mode == "sel":
                expand = (lax.broadcasted_iota(jnp.int32, (LANES, w), 1) // NSA_BLOCK
                          == lax.broadcasted_iota(jnp.int32, (LANES, w), 0))
                chosen = jnp.dot(m_ref[:, g * LANES:(g + 1) * LANES].astype(BF16),
                                 jnp.where(expand, 1.0, 0.0).astype(BF16), preferred_element_type=F32)
                mask = visible & (chosen > 0.5)
            cap_ref[g, :, 0:w] = jnp.where(mask, jnp.inf, NEG_INF)
            heads = [g * GROUP_HEADS + r for r in range(GROUP_HEADS)]
            q = jnp.concatenate([q_ref[:, h * HEAD_DIM:(h + 1) * HEAD_DIM] for h in heads], 0).astype(BF16)
            s_ref[g, :, 0:w] = _dot_nt(q, kc_ref[g, keys, :]) * (scale * LOG2E)
        for g in groups:
            if band_at is not None:
                s_ref[g, :, band_at:band_at + BAND] += band_ref[g]
            else:
                if maybe_first:
                    @pl.when(qi == 0)
                    def _():
                        s_ref[g, :, 0:PAGE] += band_ref[g, :, PAGE:]

                @pl.when(qi > 0)
                def _():
                    s_ref[g, :, pl.ds(pl.multiple_of(q0 - PAGE - kbase, PAGE), BAND)] += band_ref[g]
        sums, alive = [], []
        for g in groups:
            for r in range(GROUP_HEADS):
                rows = slice(r * tq, (r + 1) * tq)
                m = jnp.max(jnp.minimum(s_ref[g, rows, 0:w], cap_ref[g, :, 0:w]), -1, keepdims=True)
                p = jnp.exp2(jnp.minimum(s_ref[g, rows, 0:w], cap_ref[g, :, 0:w]) - m)
                p_ref[g, rows, 0:w] = p.astype(p_ref.dtype)
                sums.append(jnp.sum(p, -1, keepdims=True))
                alive.append(m > NEG_INF)
        for g in groups:
            o = jnp.dot(p_ref[g, :, 0:w].astype(BF16), vc_ref[g, keys, :], preferred_element_type=F32)
            for r in range(GROUP_HEADS):
                h = g * GROUP_HEADS + r
                o_h = o[r * tq:(r + 1) * tq] / jnp.maximum(sums[h], 1e-30)
                o_ref[:, h * HEAD_DIM:(h + 1) * HEAD_DIM] = jnp.where(alive[h], o_h, 0.0)

    if single:
        tile(0, widths[0], pos0 - PAGE - kbase, False)
    elif mode == "win":
        wch = NSA_WINDOW // PAGE
        pl.when(qi < wch)(functools.partial(tile, 0, NSA_WINDOW, None, True))
        pl.when(qi >= wch)(lambda: tile(pl.multiple_of((qi - wch) * PAGE, PAGE), NSA_WINDOW + PAGE,
                                        NSA_WINDOW - PAGE, False))
    else:
        _on_causal_width(qi, tq, widths, lambda w, first: tile(0, w, None, first))


def _index_body(pt_ref, qidx_ref, miscq_ref, misck_ref, *rest, n_pages, t_new, tq, pos0, widths):
    del pt_ref
    ipage_refs = rest[:n_pages]
    o_ref, kidx_ref = rest[n_pages:]
    qi = pl.program_id(1)
    lk = o_ref.shape[1]
    n_keys = n_pages * PAGE + t_new
    kbase = pos0 - n_pages * PAGE
    q0 = pos0 if t_new == tq else pos0 + qi * tq

    @pl.when(qi == 0)
    def _():
        for c, r in enumerate(ipage_refs):
            kidx_ref[:, c * PAGE:(c + 1) * PAGE] = r[0].astype(BF16)
        for c, x in enumerate(_new_chunks(misck_ref, t_new)):
            cols = slice((n_pages + c) * PAGE, (n_pages + c + 1) * PAGE)
            kidx_ref[:, cols] = x.T[MISC_KI:MISC_KI + IDX_DIM, :].astype(BF16)

    def tile(w, maybe_first):
        del maybe_first
        qpos = q0 + lax.broadcasted_iota(jnp.int32, (tq, 1), 0)
        col = lax.broadcasted_iota(jnp.int32, (1, w), 1)
        visible = (qpos - (kbase + col) >= 0) & (col < n_keys)
        q = jnp.concatenate([qidx_ref[:, hh * IDX_DIM:(hh + 1) * IDX_DIM] for hh in range(IDX_HEADS)], 0)
        sc = jnp.dot(q.astype(BF16), kidx_ref[:, 0:w], preferred_element_type=F32)
        score = jnp.zeros((tq, w), F32)
        for hh in range(IDX_HEADS):
            wi = miscq_ref[:, MISC_WI + hh:MISC_WI + hh + 1] * (IDX_HEADS ** -0.5)
            score = score + jnp.maximum(sc[hh * tq:(hh + 1) * tq] * (IDX_DIM ** -0.5), 0.0) * wi
        o_ref[:, 0:w] = jnp.where(visible, score, NEG_INF)
        if w < lk:
            o_ref[:, w:] = jnp.full((tq, lk - w), NEG_INF, F32)

    _on_causal_width(qi, tq, widths, tile)


def _topk_body(s_ref, m_ref, key_ref, *, k, nq, tr, widths):
    lk = s_ref.shape[1]
    assert lk <= 4096
    neg_key = int(np.array(NEG_INF, np.float32).view(np.int32)) ^ 0x7FFFFFFF
    kf = float(k)

    def tile(w, maybe_first):
        del maybe_first
        bits = lax.bitcast_convert_type(s_ref[:, 0:w] + 0.0, jnp.int32)
        key_ref[:, 0:w] = jnp.where(bits >= 0, bits, bits ^ 0x7FFFFFFF)
        col = lax.broadcasted_iota(jnp.int32, (1, w), 1)
        unseen = float(lk - w)

        groups = [slice(a * (tr // TOPK_ROW_GROUPS), (a + 1) * (tr // TOPK_ROW_GROUPS)) for a in range(TOPK_ROW_GROUPS)]
        zeros = tuple(jnp.zeros((tr // TOPK_ROW_GROUPS, 1), jnp.int32) for _ in groups)

        def thr_step(i, tus):
            out = []
            for rows, tu in zip(groups, tus):
                cand = tu | jnp.left_shift(jnp.int32(1), 31 - i)
                cs = cand ^ INT_MIN
                cnt = jnp.sum(jnp.where(key_ref[rows, 0:w] >= cs, 1.0, 0.0), -1, keepdims=True)
                cnt = cnt + jnp.where(cs <= neg_key, unseen, 0.0)
                out.append(jnp.where(cnt >= kf, cand, tu))
            return tuple(out)

        thr = jnp.concatenate(lax.fori_loop(0, 32, thr_step, zeros, unroll=4), 0) ^ INT_MIN
        key = key_ref[:, 0:w]
        above = key > thr
        tied = key == thr
        need = kf - jnp.sum(jnp.where(above, 1.0, 0.0), -1, keepdims=True)

        def tie_step(i, j0s):
            out = []
            for rows, j0 in zip(groups, j0s):
                cand = j0 | jnp.left_shift(jnp.int32(1), 11 - i)
                hit = (key_ref[rows, 0:w] == thr[rows]) & (col < cand)
                cnt = jnp.sum(jnp.where(hit, 1.0, 0.0), -1, keepdims=True)
                out.append(jnp.where(cnt < need[rows], cand, j0))
            return tuple(out)

        m_ref[:, 0:w] = jnp.where(above | tied, 1.0, 0.0)
        n_tied = jnp.sum(jnp.where(tied, 1.0, 0.0), -1, keepdims=True)
        excess = jnp.max(jnp.where(thr > neg_key, n_tied - need, 0.0))

        @pl.when(excess > 0.0)
        def _():
            j0 = jnp.concatenate(lax.fori_loop(0, 12, tie_step, zeros, unroll=4), 0)
            k2 = key_ref[:, 0:w]
            m_ref[:, 0:w] = jnp.where((k2 > thr) | ((k2 == thr) & (col <= j0)), 1.0, 0.0)
        if w < lk:
            m_ref[:, w:] = jnp.zeros((tr, lk - w), F32)

    _on_causal_width(pl.program_id(0) % nq, tr, widths, tile)


class _Group:
    def __init__(self, row0, n_seq, t_new, tq, pos0, n_pages):
        assert t_new % tq == 0 and row0 % tq == 0 and row0 % t_new == 0
        assert pos0 == n_pages * PAGE or n_pages * PAGE < pos0
        assert t_new == tq or (tq == PAGE and pos0 == 0)
        assert t_new % PAGE == 0 or t_new % PAGE < NSA_BLOCK
        self.row0, self.n_seq, self.t_new, self.tq, self.pos0, self.n_pages = row0, n_seq, t_new, tq, pos0, n_pages
        self.nq = t_new // tq
        self.rows = n_seq * t_new
        self.lk = (n_pages + -(-t_new // PAGE)) * PAGE
        self.n_keys = n_pages * PAGE + t_new

    def q_spec(self, width, col):
        return pl.BlockSpec((self.tq, width), lambda b, qi, pt: (self.row0 // self.tq + b * self.nq + qi, col // width))

    def seq_spec(self, width, col):
        return pl.BlockSpec((self.t_new, width), lambda b, qi, pt: (self.row0 // self.t_new + b, col // width))

    def page_specs(self, shape):
        return [pl.BlockSpec((1,) + shape, lambda b, qi, pt, p=p: (pt[b, p], 0, 0)) for p in range(self.n_pages)]

    def out_spec(self, width):
        return pl.BlockSpec((self.tq, width), lambda b, qi, pt: (b * self.nq + qi, 0))

    def statics(self):
        return dict(n_pages=self.n_pages, t_new=self.t_new, tq=self.tq, pos0=self.pos0)

    def widths(self):
        if self.nq == 1:
            return (self.lk,)
        step = 4 * PAGE
        assert self.lk % step == 0
        return tuple(range(step, self.lk + 1, step))


def _cmp_bias_table(rel_bias, grp):
    lane0 = 0 if grp.nq == 1 else CMP_BIAS_LANE0
    tab = rel_bias[_BUCKETS]
    pieces, n_far = [], 0
    for lane in range(LANES + 1):
        d0 = grp.pos0 - (NSA_BLOCK * (lane - lane0 + 1) - 1)
        plain = lane < LANES and (d0 >= PAGE or d0 + grp.tq - 1 < 0)
        if plain:
            n_far += 1
            continue
        if n_far:
            pieces.append(jnp.broadcast_to(rel_bias[N_BUCKETS - 1], (grp.tq, n_far, rel_bias.shape[1])))
            n_far = 0
        if lane < LANES:
            pieces.append(tab[np.clip(d0 + np.arange(grp.tq), 0, BAND - 1)][:, None, :])
    table = jnp.transpose(jnp.concatenate(pieces, 1), (2, 0, 1))
    return table.reshape(-1, GROUP_HEADS * grp.tq, LANES)


def nsa_compress(grp, z, page_table, pool, wexp, bias):
    kv_w = KV_GROUPS * 2 * HEAD_DIM
    qw = KV_GROUPS * GROUP_HEADS * HEAD_DIM
    const2 = lambda b, qi, pt: (0, 0)
    return pl.pallas_call(
        functools.partial(_cmp_body, **grp.statics()),
        grid_spec=pltpu.PrefetchScalarGridSpec(
            num_scalar_prefetch=1,
            grid=(grp.n_seq, grp.nq),
            in_specs=[grp.q_spec(qw, COL_QN), grp.seq_spec(kv_w, COL_KVC)] + grp.page_specs(KV_PAGE)
            + [pl.BlockSpec((PAGE, kv_w), const2), pl.BlockSpec(bias.shape, lambda b, qi, pt: (0, 0, 0))],
            out_specs=[grp.out_spec(qw), grp.out_spec(KV_GROUPS * LANES)],
            scratch_shapes=[pltpu.VMEM((LANES, kv_w), F32),
                            pltpu.VMEM((KV_GROUPS, LANES, HEAD_DIM), BF16),
                            pltpu.VMEM((KV_GROUPS, LANES, HEAD_DIM), BF16)]),
        out_shape=[jax.ShapeDtypeStruct((grp.rows, qw), F32),
                   jax.ShapeDtypeStruct((grp.rows, KV_GROUPS * LANES), F32)],
        compiler_params=_cparams("parallel", "arbitrary"),
        name="nsa_compress",
    )(page_table, z, z, *([pool] * grp.n_pages), wexp, bias)


def sparse_attention(mode, grp, z, page_table, pool, band, *, q_col, kv_col, mask=None):
    kv_w = KV_GROUPS * 2 * HEAD_DIM
    qw = KV_GROUPS * GROUP_HEADS * HEAD_DIM
    in_specs = [grp.q_spec(qw, q_col), grp.seq_spec(kv_w, kv_col)] + grp.page_specs(KV_PAGE)
    args = [z, z] + [pool] * grp.n_pages
    if mode in ("sel", "mask"):
        in_specs.append(grp.out_spec(mask.shape[1]))
        args.append(mask)
    in_specs.append(pl.BlockSpec((KV_GROUPS, GROUP_HEADS * grp.tq, BAND), lambda b, qi, pt: (0, 0, 0)))
    args.append(band)
    widths = grp.widths()
    s_cols = max(widths) if (mode != "win" or grp.nq == 1) else NSA_WINDOW + PAGE
    rows = GROUP_HEADS * grp.tq
    scratch = [pltpu.VMEM((KV_GROUPS, grp.lk, HEAD_DIM), BF16),
               pltpu.VMEM((KV_GROUPS, grp.lk, HEAD_DIM), BF16),
               pltpu.VMEM((KV_GROUPS, rows, s_cols), F32),
               pltpu.VMEM((KV_GROUPS, grp.tq, s_cols), F32)]
    if grp.tq % BF16_SUBLANES == 0:
        scratch.append(pltpu.VMEM((KV_GROUPS, rows, s_cols), BF16))
    return pl.pallas_call(
        functools.partial(_attn_body, mode=mode, widths=widths, **grp.statics()),
        grid_spec=pltpu.PrefetchScalarGridSpec(
            num_scalar_prefetch=1,
            grid=(grp.n_seq, grp.nq),
            in_specs=in_specs,
            out_specs=grp.out_spec(qw),
            scratch_shapes=scratch),
        out_shape=jax.ShapeDtypeStruct((grp.rows, qw), F32),
        compiler_params=_cparams("parallel", "arbitrary"),
        name="sparse_attention_" + mode,
    )(page_table, *args)


def dsa_index_scores(grp, z, page_table, idx_pool):
    return pl.pallas_call(
        functools.partial(_index_body, widths=grp.widths(), **grp.statics()),
        grid_spec=pltpu.PrefetchScalarGridSpec(
            num_scalar_prefetch=1,
            grid=(grp.n_seq, grp.nq),
            in_specs=[grp.q_spec(IDX_HEADS * IDX_DIM, COL_QI), grp.q_spec(LANES, COL_MISC),
                      grp.seq_spec(LANES, COL_MISC)] + grp.page_specs((IDX_DIM, PAGE)),
            out_specs=grp.out_spec(grp.lk),
            scratch_shapes=[pltpu.VMEM((IDX_DIM, grp.lk), BF16)]),
        out_shape=jax.ShapeDtypeStruct((grp.rows, grp.lk), F32),
        compiler_params=_cparams("parallel", "arbitrary"),
        name="dsa_index_scores",
    )(page_table, z, z, z, *([idx_pool] * grp.n_pages))


def topk_mask(grp, scores, k):
    rows, lk = scores.shape
    tr = PAGE
    assert rows % tr == 0 and (grp.nq == 1 or grp.tq == tr)
    blk = pl.BlockSpec((tr, lk), lambda i: (i, 0))
    return pl.pallas_call(
        functools.partial(_topk_body, k=k, nq=grp.nq, tr=tr, widths=grp.widths()),
        grid=(rows // tr,),
        in_specs=[blk],
        out_specs=blk,
        out_shape=jax.ShapeDtypeStruct((rows, lk), F32),
        scratch_shapes=[pltpu.VMEM((tr, lk), jnp.int32)],
        compiler_params=_cparams("parallel"),
        name="topk_mask",
    )(scores)


def _combine_body(oc_ref, os_ref, ow_ref, od_ref, misc_ref, y_ref):
    n_heads = KV_GROUPS * GROUP_HEADS
    gates = jax.nn.sigmoid(misc_ref[:, MISC_GATES:MISC_GATES + 3 * n_heads])
    for h in range(n_heads):
        hs = slice(h * HEAD_DIM, (h + 1) * HEAD_DIM)
        o = (gates[:, 3 * h:3 * h + 1] * oc_ref[:, hs] + gates[:, 3 * h + 1:3 * h + 2] * os_ref[:, hs]
             + gates[:, 3 * h + 2:3 * h + 3] * ow_ref[:, hs])
        y_ref[:, hs] = o.astype(y_ref.dtype)
    y_ref[:, n_heads * HEAD_DIM:] = od_ref[...].astype(y_ref.dtype)


def nsa_dsa_combine(o_c, o_s, o_w, o_d, z, row0, *, tm):
    m, w = o_c.shape
    assert m % tm == 0 and row0 % tm == 0
    blk = pl.BlockSpec((tm, w), lambda i: (i, 0))
    return pl.pallas_call(
        _combine_body,
        grid=(m // tm,),
        in_specs=[blk, blk, blk, blk, pl.BlockSpec((tm, LANES), lambda i: (row0 // tm + i, COL_MISC // LANES))],
        out_specs=pl.BlockSpec((tm, 2 * w), lambda i: (i, 0)),
        out_shape=jax.ShapeDtypeStruct((m, 2 * w), BF16),
        compiler_params=_cparams("parallel"),
        name="nsa_dsa_combine",
    )(o_c, o_s, o_w, o_d, z)


def _band_tiles(rel_bias, tq):
    delta = (rel_bias[_BUCKETS] - rel_bias[N_BUCKETS - 1]).T
    rev = jnp.concatenate([delta[:, ::-1], jnp.zeros((delta.shape[0], PAGE), delta.dtype)], 1)
    tiles = jnp.stack([rev[:, PAGE - 1 - i:PAGE - 1 - i + BAND] for i in range(tq)], 1)
    return tiles.reshape(-1, GROUP_HEADS * tq, BAND) * LOG2E


def _widen_cd_w_in(w):
    sizes = (1024, 512, 512, 512, 24, 1024, 512, 512, 64, 8)
    q_n, kv_c, kv_s, kv_w, gates, q_d, kv_d, q_i, k_i, w_i = jnp.split(w, np.cumsum(sizes)[:-1].tolist(), axis=-1)
    cols = [q_n, q_d, kv_c, kv_s, kv_w, kv_d, q_i, k_i, gates, w_i]
    used = sum(c.shape[-1] for c in cols)
    return jnp.concatenate(cols + [jnp.zeros(w.shape[:-1] + (NZ - used,), w.dtype)], axis=-1)


def kernel(x_prompt, x_sample, state_conv, state_pool, cache_nsa_cmp, cache_nsa_sel, cache_nsa_win, cache_dsa_kv, cache_dsa_idx, page_table, norm_mix, norm_ffn, norm_final, ab_w_in, ab_conv_w, ab_conv_b, ab_ln_g, ab_ln_b, ab_pool_w, ab_pool_scale, ab_w_out, cd_w_in, cd_w_cmp, cd_w_out, rel_bias, ffn_w1, ffn_w2):
    bp, tp, d_model = x_prompt.shape
    bs, ts, _ = x_sample.shape
    mp, ms = bp * tp, bs * ts
    depth = norm_mix.shape[0]
    n_pages = page_table.shape[1]
    n_pool = cache_nsa_cmp.shape[1]
    past_len = n_pages * PAGE
    assert cache_nsa_cmp.shape[2] == PAGE
    win_len = cache_nsa_win.shape[2]
    assert win_len % PAGE == 0 and win_len == NSA_WINDOW and tp >= NSA_WINDOW
    kv_w = KV_GROUPS * 2 * HEAD_DIM

    xs = [x_prompt.reshape(mp, d_model), x_sample.reshape(ms, d_model)]
    grp_p = _Group(0, bp, tp, PAGE, 0, 0)
    grp_s = _Group(mp, bs, ts, ts, past_len, n_pages)
    grp_sw = _Group(mp, bs, ts, ts, past_len, win_len // PAGE)
    no_pages = jnp.zeros((1, 1), jnp.int32)
    win_pages = jnp.arange(bs * (win_len // PAGE), dtype=jnp.int32).reshape(bs, win_len // PAGE)

    outs = {k: [] for k in ("conv_p", "conv_s", "pool_p", "pool_s", "cmp_p", "cmp_s", "sel_p", "sel_s",
                            "win_p", "win_s", "dsa_p", "dsa_s", "idx_p", "idx_s")}
    w_bf16 = {"ab_w_in": ab_w_in, "ab_w_out": ab_w_out, "cd_w_in": _widen_cd_w_in(cd_w_in), "cd_w_out": cd_w_out,
              "ffn_w1": ffn_w1, "ffn_w2": ffn_w2}
    w_bf16 = {k: v.astype(BF16) for k, v in w_bf16.items()}
    y_p = y_s = None
    for i in range(depth):
        j = i // 2
        if i % 2 == 0:
            d_conv = ab_conv_w.shape[2]
            z = norm_matmul(xs, norm_mix[i], w_bf16["ab_w_in"], j)
            mid_p, u_p = ab_mid_prompt(z, bp, tp, ab_conv_w[j], ab_conv_b[j], ab_ln_g[j], ab_ln_b[j],
                                       ab_pool_w[j], ab_pool_scale[j])
            mid_s, conv_s, pool_s = ab_mid_step(z, mp, bs, ts, past_len, state_conv[j], state_pool[j], ab_conv_w[j],
                                                ab_conv_b[j], ab_ln_g[j], ab_ln_b[j], ab_pool_w[j], ab_pool_scale[j])
            xs = [matmul_residual([mid_p, mid_s], w_bf16["ab_w_out"], j, xs)]
            outs["conv_p"].append(u_p.reshape(bp, tp, d_conv)[:, tp - CONV_BUF:])
            outs["conv_s"].append(conv_s)
            outs["pool_p"].append(jnp.stack([z[(b + 1) * tp - POOL_BUF:(b + 1) * tp, 2 * d_conv:] for b in range(bp)]))
            outs["pool_s"].append(pool_s)
        else:
            z = norm_matmul(xs, norm_mix[i], w_bf16["cd_w_in"], j)
            nsa_bias = rel_bias[:, :KV_GROUPS * GROUP_HEADS]
            band_p, band_s = _band_tiles(rel_bias, grp_p.tq), _band_tiles(rel_bias, grp_s.tq)
            wexp = jnp.tile(jnp.repeat(jnp.transpose(cd_w_cmp[j], (1, 0, 2)).reshape(NSA_BLOCK, 2 * KV_GROUPS),
                                       HEAD_DIM, axis=1), (PAGE // NSA_BLOCK, 1))
            pt = page_table + j * n_pool
            pools = [c.reshape((-1,) + KV_PAGE) for c in (cache_nsa_cmp, cache_nsa_sel, cache_dsa_kv)]
            idx_pool = jnp.swapaxes(cache_dsa_idx, 2, 3).reshape(-1, IDX_DIM, PAGE)
            win_pool = cache_nsa_win.reshape((-1,) + KV_PAGE)
            wpt = win_pages + j * bs * (win_len // PAGE)
            mids = []
            for grp, gw, ptab, wtab, band in ((grp_p, grp_p, no_pages, no_pages, band_p),
                                              (grp_s, grp_sw, pt, wpt, band_s)):
                o_c, msel = nsa_compress(grp, z, ptab, pools[0], wexp, _cmp_bias_table(nsa_bias, grp))
                o_s = sparse_attention("sel", grp, z, ptab, pools[1], band[:KV_GROUPS],
                                       q_col=COL_QN, kv_col=COL_KVS, mask=msel)
                o_w = sparse_attention("win", gw, z, wtab, win_pool, band[:KV_GROUPS],
                                       q_col=COL_QN, kv_col=COL_KVW)
                top = topk_mask(grp, dsa_index_scores(grp, z, ptab, idx_pool), min(DSA_TOPK, grp.n_keys // 4))
                o_d = sparse_attention("mask", grp, z, ptab, pools[2], band[KV_GROUPS:],
                                       q_col=COL_QD, kv_col=COL_KVD, mask=top)
                mids.append(nsa_dsa_combine(o_c, o_s, o_w, o_d, z, grp.row0, tm=min(512, grp.rows)))
            xs = [matmul_residual(mids, w_bf16["cd_w_out"], j, xs)]

            def kv_out(col, width, tail):
                seg = z[:, col:col + width]
                return seg[:mp].reshape((bp, tp) + tail), seg[mp:].reshape((bs, ts) + tail)

            kv_tail = (2, KV_GROUPS, HEAD_DIM)
            for name, col in (("cmp", COL_KVC), ("sel", COL_KVS), ("dsa", COL_KVD)):
                p_new, s_new = kv_out(col, kv_w, kv_tail)
                outs[name + "_p"].append(p_new)
                outs[name + "_s"].append(s_new)
            w_p, w_s = kv_out(COL_KVW, kv_w, kv_tail)
            outs["win_p"].append(w_p[:, tp - NSA_WINDOW:])
            outs["win_s"].append(jnp.concatenate([cache_nsa_win[j], w_s], 1)[:, ts:])
            i_p, i_s = kv_out(COL_MISC + MISC_KI, IDX_DIM, (IDX_DIM,))
            outs["idx_p"].append(i_p)
            outs["idx_s"].append(i_s)
        a = norm_matmul(xs, norm_ffn[i], w_bf16["ffn_w1"], i, relu2=True, out_dtype=BF16)
        if i == depth - 1:
            y_p, y_s = matmul_residual([a], w_bf16["ffn_w2"], i, xs, norm_final, split_out=(mp, ms))
        else:
            xs = [matmul_residual([a], w_bf16["ffn_w2"], i, xs)]

    st = {k: jnp.stack(v) for k, v in outs.items()}
    return (y_p.reshape(bp, tp, d_model), y_s.reshape(bs, ts, d_model),
            st["conv_p"], st["conv_s"], st["pool_p"], st["pool_s"], st["cmp_p"], st["cmp_s"],
            st["sel_p"], st["sel_s"], st["win_p"], st["win_s"], st["dsa_p"], st["dsa_s"],
            st["idx_p"], st["idx_s"])
```

```python
import functools
import math

import numpy as np
import jax
import jax.numpy as jnp
from jax import lax
from jax.experimental import pallas as pl
from jax.experimental.pallas import tpu as pltpu

F32 = jnp.float32
BF16 = jnp.bfloat16

EPS = 1e-6
NEG_INF = -1e30
LOG2E = math.log2(math.e)
HEAD_DIM = 128
LANES = 128
BF16_SUBLANES = 16
CONV_WIDTH = 31
CONV_BUF = CONV_WIDTH - 1
POOL_WINDOWS = (2, 4, 8, 16)
POOL_BUF = max(POOL_WINDOWS) - 1
HALO = 32
VMEM_LIMIT = 56 * 1024 * 1024


def _cparams(*sem):
    return pltpu.CompilerParams(dimension_semantics=sem, vmem_limit_bytes=VMEM_LIMIT)


class _Rows:
    def __init__(self, arrays, tm):
        self.arrays = list(arrays)
        self.tm = tm
        assert all(a.shape[0] % tm == 0 for a in self.arrays)
        self.tiles = [a.shape[0] // tm for a in self.arrays]
        self.n_tiles = sum(self.tiles)
        self.n = len(self.arrays)

    def specs(self, width, col):
        out, t0 = [], 0
        for nt in self.tiles:
            out.append(pl.BlockSpec((self.tm, width), lambda i, j, t0=t0, nt=nt: (jnp.clip(i - t0, 0, nt - 1), col(j))))
            t0 += nt
        return out

    def select(self, i, refs, fn):
        if self.n == 1:
            fn(refs[0])
            return
        t0 = 0
        for nt, ref in zip(self.tiles, refs):
            pl.when((i >= t0) & (i < t0 + nt))(functools.partial(fn, ref))
            t0 += nt


def _rmsnorm_rows(x, g):
    return (x * lax.rsqrt(jnp.mean(x * x, -1, keepdims=True) + EPS)) * g


def _norm_matmul_body(*refs, rows, relu2):
    x_refs = refs[:rows.n]
    g_ref, w_ref, o_ref, h_ref = refs[rows.n:]

    @pl.when(pl.program_id(1) == 0)
    def _():
        def norm(x_ref):
            h_ref[...] = _rmsnorm_rows(x_ref[...], g_ref[...]).astype(BF16)

        rows.select(pl.program_id(0), x_refs, norm)

    y = jnp.dot(h_ref[...], w_ref[...], preferred_element_type=F32)
    if relu2:
        y = jnp.square(jnp.maximum(y, 0.0))
    o_ref[...] = y.astype(o_ref.dtype)


def norm_matmul(xs, g, w, layer, *, relu2=False, out_dtype=F32, tm=1024, tn=1024):
    rows = _Rows(xs, tm)
    _, d, n = w.shape
    assert n % tn == 0
    return pl.pallas_call(
        functools.partial(_norm_matmul_body, rows=rows, relu2=relu2),
        grid=(rows.n_tiles, n // tn),
        in_specs=rows.specs(d, lambda j: 0) + [pl.BlockSpec((1, d), lambda i, j: (0, 0)),
                                               pl.BlockSpec((None, d, tn), lambda i, j: (layer, 0, j))],
        out_specs=pl.BlockSpec((tm, tn), lambda i, j: (i, j)),
        out_shape=jax.ShapeDtypeStruct((rows.n_tiles * tm, n), out_dtype),
        scratch_shapes=[pltpu.VMEM((tm, d), BF16)],
        compiler_params=_cparams("parallel", "arbitrary"),
        name="norm_matmul",
    )(*rows.arrays, g.reshape(1, d), w)


def _matmul_residual_body(*refs, a_rows, r_rows, o_rows, final_norm):
    a_refs, refs = refs[:a_rows.n], refs[a_rows.n:]
    w_ref, refs = refs[0], refs[1:]
    r_refs, refs = refs[:r_rows.n], refs[r_rows.n:]
    if final_norm:
        g_ref, refs = refs[0], refs[1:]
    o_refs, acc_ref = refs[:o_rows.n], refs[o_rows.n]
    i, k = pl.program_id(0), pl.program_id(1)

    @pl.when(k == 0)
    def _():
        acc_ref[...] = jnp.zeros_like(acc_ref)

    def accumulate(a_ref):
        acc_ref[...] += jnp.dot(a_ref[...], w_ref[...], preferred_element_type=F32)

    a_rows.select(i, a_refs, accumulate)

    @pl.when(k == pl.num_programs(1) - 1)
    def _():
        def add_residual(r_ref):
            acc_ref[...] += r_ref[...]

        def write(o_ref):
            o = acc_ref[...]
            o_ref[...] = _rmsnorm_rows(o, g_ref[...]) if final_norm else o

        r_rows.select(i, r_refs, add_residual)
        o_rows.select(i, o_refs, write)


def matmul_residual(a_list, w, layer, r_list, g_final=None, *, split_out=None, tm=512, tk=2048):
    a_rows, r_rows = _Rows(a_list, tm), _Rows(r_list, tm)
    _, kdim, n = w.shape
    m = a_rows.n_tiles * tm
    assert kdim % tk == 0 and r_rows.n_tiles == a_rows.n_tiles
    final_norm = g_final is not None
    o_rows = _Rows([jax.ShapeDtypeStruct((r, n), F32) for r in (split_out or (m,))], tm)
    assert o_rows.n_tiles == a_rows.n_tiles
    in_specs = (a_rows.specs(tk, lambda k: k) + [pl.BlockSpec((None, tk, n), lambda i, k: (layer, k, 0))]
                + r_rows.specs(n, lambda k: 0))
    args = a_rows.arrays + [w] + r_rows.arrays
    if final_norm:
        in_specs.append(pl.BlockSpec((1, n), lambda i, k: (0, 0)))
        args.append(g_final.reshape(1, n))
    out = pl.pallas_call(
        functools.partial(_matmul_residual_body, a_rows=a_rows, r_rows=r_rows, o_rows=o_rows, final_norm=final_norm),
        grid=(a_rows.n_tiles, kdim // tk),
        in_specs=in_specs,
        out_specs=o_rows.specs(n, lambda k: 0),
        out_shape=o_rows.arrays,
        scratch_shapes=[pltpu.VMEM((tm, n), F32)],
        compiler_params=_cparams("parallel", "arbitrary"),
        name="matmul_residual",
    )(*args)
    return out if split_out else out[0]


def _layernorm_silu(c, g, b):
    mu = jnp.mean(c, -1, keepdims=True)
    xc = c - mu
    y = xc * lax.rsqrt(jnp.mean(xc * xc, -1, keepdims=True) + EPS)
    y = y * g + b
    return y * jax.nn.sigmoid(y)


def _ab_mid_body(z_ref, zp_ref, cw_ref, cb_ref, lg_ref, lb_ref, pw_ref, ps_ref, y_ref, u_ref,
                 ext_ref, vext_ref, conv_ref, *, tt, d_conv, d_pool):
    ti = pl.program_id(1)
    keep = (ti > 0).astype(F32)
    a_p = zp_ref[:, 0:d_conv]
    g_p = zp_ref[:, d_conv:2 * d_conv]
    ext_ref[0:HALO, :] = a_p * jax.nn.sigmoid(g_p) * keep
    vext_ref[0:HALO, :] = zp_ref[:, 2 * d_conv:] * keep
    u = z_ref[:, 0:d_conv] * jax.nn.sigmoid(z_ref[:, d_conv:2 * d_conv])
    ext_ref[HALO:, :] = u
    u_ref[...] = u
    vext_ref[HALO:, :] = z_ref[:, 2 * d_conv:]

    off = HALO - CONV_BUF
    sub = 8
    for c in range(d_conv // LANES):
        cs = slice(c * LANES, (c + 1) * LANES)
        acc = jnp.zeros((tt, LANES), F32)
        for s in range(sub):
            n = tt if s == 0 else tt + sub
            part = jnp.zeros((n, LANES), F32)
            for j in range(CONV_WIDTH):
                if (off + j) % sub == s:
                    start = off + j - s
                    part = part + cw_ref[j:j + 1, cs] * ext_ref[start:start + n, cs]
            acc = acc + part[s:s + tt]
        conv_ref[:, cs] = acc + cb_ref[:, cs]
    y_ref[:, 0:d_conv] = _layernorm_silu(conv_ref[...], lg_ref[...], lb_ref[...]).astype(y_ref.dtype)

    pos = ti * tt + lax.broadcasted_iota(jnp.int32, (tt, 1), 0)
    pg = d_pool // len(POOL_WINDOWS)
    for gi, w in enumerate(POOL_WINDOWS):
        gs = slice(gi * pg, (gi + 1) * pg)
        tok = vext_ref[HALO:, gs]
        acc = tok
        for i in range(1, w):
            acc = acc + vext_ref[HALO - i:HALO - i + tt, gs]
        cnt = jnp.minimum(pos + 1, w).astype(F32)
        d = acc / cnt - tok
        yp = jnp.dot(d.astype(BF16), pw_ref[gi], preferred_element_type=F32) * ps_ref[:, gs]
        y_ref[:, d_conv + gi * pg:d_conv + (gi + 1) * pg] = yp.astype(y_ref.dtype)


def ab_mid_prompt(z, n_seq, t_len, conv_w, conv_b, ln_g, ln_b, pool_w, pool_scale, *, tt=256):
    d_conv = conv_w.shape[1]
    d_pool = pool_scale.shape[0]
    nt = t_len // tt
    hb = tt // HALO
    row = lambda b, t: (b * nt + t, 0)
    const = lambda b, t: (0, 0)
    return pl.pallas_call(
        functools.partial(_ab_mid_body, tt=tt, d_conv=d_conv, d_pool=d_pool),
        grid=(n_seq, nt),
        in_specs=[pl.BlockSpec((tt, z.shape[1]), row),
                  pl.BlockSpec((HALO, z.shape[1]), lambda b, t: (jnp.maximum((b * nt + t) * hb - 1, 0), 0)),
                  pl.BlockSpec(conv_w.shape, const),
                  pl.BlockSpec((1, d_conv), const),
                  pl.BlockSpec((1, d_conv), const),
                  pl.BlockSpec((1, d_conv), const),
                  pl.BlockSpec(pool_w.shape, lambda b, t: (0, 0, 0)),
                  pl.BlockSpec((1, d_pool), const)],
        out_specs=[pl.BlockSpec((tt, d_conv + d_pool), row),
                   pl.BlockSpec((tt, d_conv), row)],
        out_shape=[jax.ShapeDtypeStruct((n_seq * t_len, d_conv + d_pool), BF16),
                   jax.ShapeDtypeStruct((n_seq * t_len, d_conv), F32)],
        scratch_shapes=[pltpu.VMEM((HALO + tt, d_conv), F32),
                        pltpu.VMEM((HALO + tt, d_pool), F32),
                        pltpu.VMEM((tt, d_conv), F32)],
        compiler_params=_cparams("parallel", "parallel"),
        name="ab_mid_prompt",
    )(z, z, conv_w, conv_b.reshape(1, -1), ln_g.reshape(1, -1), ln_b.reshape(1, -1),
      pool_w.astype(BF16), pool_scale.reshape(1, -1))


def _ab_mid_step_body(z_ref, sc_ref, sp_ref, cw_ref, cb_ref, lg_ref, lb_ref, pw_ref, ps_ref,
                      y_ref, nc_ref, np_ref, ext_ref, vext_ref, *, nb, t, pos0, d_conv, d_pool):
    e0 = HALO - CONV_BUF
    p0 = 16 - POOL_BUF
    z = z_ref[...].reshape(nb, t, z_ref.shape[1])
    u = z[:, :, 0:d_conv] * jax.nn.sigmoid(z[:, :, d_conv:2 * d_conv])
    ext_ref[:, e0:HALO, :] = sc_ref[...]
    ext_ref[:, HALO:, :] = u
    vext_ref[:, p0:16, :] = sp_ref[...]
    vext_ref[:, 16:, :] = z[:, :, 2 * d_conv:]
    nc_ref[...] = ext_ref[:, HALO + t - CONV_BUF:, :]
    np_ref[...] = vext_ref[:, 16 + t - POOL_BUF:, :]

    acc = jnp.zeros((nb, t, d_conv), F32)
    for j in range(CONV_WIDTH):
        acc = acc + cw_ref[j:j + 1, :][None] * ext_ref[:, e0 + j:e0 + j + t, :]
    c = acc + cb_ref[...][None]
    yc = _layernorm_silu(c, lg_ref[...][None], lb_ref[...][None])
    y_ref[:, 0:d_conv] = yc.reshape(nb * t, d_conv).astype(y_ref.dtype)

    pg = d_pool // len(POOL_WINDOWS)
    for gi, w in enumerate(POOL_WINDOWS):
        gs = slice(gi * pg, (gi + 1) * pg)
        tok = vext_ref[:, 16:, gs]
        acc = tok
        for i in range(1, w):
            acc = acc + vext_ref[:, 16 - i:16 - i + t, gs]
        cnt = jnp.minimum(pos0 + 1 + lax.broadcasted_iota(jnp.int32, (1, t, 1), 1), w).astype(F32)
        d = (acc / cnt - tok).reshape(nb * t, pg)
        yp = jnp.dot(d.astype(BF16), pw_ref[gi], preferred_element_type=F32) * ps_ref[:, gs]
        y_ref[:, d_conv + gi * pg:d_conv + (gi + 1) * pg] = yp.astype(y_ref.dtype)


def ab_mid_step(z, row0, n_seq, t, pos0, state_conv, state_pool, conv_w, conv_b, ln_g, ln_b, pool_w,
                pool_scale, *, nb=16):
    d_conv = conv_w.shape[1]
    d_pool = pool_scale.shape[0]
    rb = nb * t
    assert row0 % rb == 0 and n_seq % nb == 0
    const = lambda i: (0, 0)
    seq3 = lambda i: (i, 0, 0)
    return pl.pallas_call(
        functools.partial(_ab_mid_step_body, nb=nb, t=t, pos0=pos0, d_conv=d_conv, d_pool=d_pool),
        grid=(n_seq // nb,),
        in_specs=[pl.BlockSpec((rb, z.shape[1]), lambda i: (row0 // rb + i, 0)),
                  pl.BlockSpec((nb, CONV_BUF, d_conv), seq3),
                  pl.BlockSpec((nb, POOL_BUF, d_pool), seq3),
                  pl.BlockSpec(conv_w.shape, const),
                  pl.BlockSpec((1, d_conv), const),
                  pl.BlockSpec((1, d_conv), const),
                  pl.BlockSpec((1, d_conv), const),
                  pl.BlockSpec(pool_w.shape, lambda i: (0, 0, 0)),
                  pl.BlockSpec((1, d_pool), const)],
        out_specs=[pl.BlockSpec((rb, d_conv + d_pool), lambda i: (i, 0)),
                   pl.BlockSpec((nb, CONV_BUF, d_conv), seq3),
                   pl.BlockSpec((nb, POOL_BUF, d_pool), seq3)],
        out_shape=[jax.ShapeDtypeStruct((n_seq * t, d_conv + d_pool), BF16),
                   jax.ShapeDtypeStruct((n_seq, CONV_BUF, d_conv), F32),
                   jax.ShapeDtypeStruct((n_seq, POOL_BUF, d_pool), F32)],
        scratch_shapes=[pltpu.VMEM((nb, HALO + t, d_conv), F32),
                        pltpu.VMEM((nb, 16 + t, d_pool), F32)],
        compiler_params=_cparams("parallel"),
        name="ab_mid_step",
    )(z, state_conv, state_pool, conv_w, conv_b.reshape(1, -1), ln_g.reshape(1, -1),
      ln_b.reshape(1, -1), pool_w.astype(BF16), pool_scale.reshape(1, -1))


N_BUCKETS = 32
MAX_DISTANCE = 128
NSA_BLOCK = 64
NSA_TOPN = 16
NSA_WINDOW = 512
DSA_TOPK = 256
IDX_HEADS = 8
IDX_DIM = 64
KV_GROUPS = 2
GROUP_HEADS = 4
PAGE = 128
BAND = 2 * PAGE
INT_MIN = -2 ** 31
KV_PAGE = (2 * KV_GROUPS * PAGE, HEAD_DIM)
TOPK_ROW_GROUPS = 4
CMP_BIAS_LANE0 = 64

COL_QN, COL_QD, COL_KVC, COL_KVS, COL_KVW, COL_KVD, COL_QI, COL_MISC = 0, 1024, 2048, 2560, 3072, 3584, 4096, 4608
MISC_KI, MISC_GATES, MISC_WI = 0, 64, 88
NZ = 5120


def _bucket_np(n):
    n = np.maximum(np.asarray(n, np.int32), 0)
    exact = N_BUCKETS // 2
    nf = np.maximum(n, 1).astype(np.float32)
    big = exact + (np.log(nf / np.float32(exact)) / np.float32(math.log(MAX_DISTANCE / exact))
                   * np.float32(N_BUCKETS - exact)).astype(np.int32)
    return np.where(n < exact, n, np.minimum(big, N_BUCKETS - 1))


_BUCKETS = _bucket_np(np.arange(BAND))
assert _BUCKETS[PAGE:].min() == N_BUCKETS - 1


def _softmax_rows(s, mask):
    s = jnp.where(mask, s, NEG_INF)
    m = jnp.max(s, -1, keepdims=True)
    p = jnp.where(mask, jnp.exp(s - m), 0.0)
    return p, jnp.sum(p, -1, keepdims=True)


def _dot_nt(a, b):
    return lax.dot_general(a, b, (((1,), (1,)), ((), ())), preferred_element_type=F32)


def _new_chunks(new_ref, t_new):
    chunks = [new_ref[c * PAGE:(c + 1) * PAGE, :] for c in range(t_new // PAGE)]
    rem = t_new % PAGE
    if rem:
        tail = new_ref[(t_new // PAGE) * PAGE:, :]
        chunks.append(jnp.concatenate([tail, jnp.zeros((PAGE - rem, tail.shape[1]), F32)], 0))
    return chunks


def _kv_chunks(page_refs, new_ref, t_new):
    n_parts = 2 * KV_GROUPS
    chunks = [[r[0, pl.ds(part, PAGE, stride=n_parts), :] for part in range(n_parts)] for r in page_refs]
    for x in _new_chunks(new_ref, t_new):
        chunks.append([x[:, part * HEAD_DIM:(part + 1) * HEAD_DIM] for part in range(n_parts)])
    return chunks


def _cmp_body(pt_ref, q_ref, kvn_ref, *rest, n_pages, t_new, tq, pos0):
    del pt_ref
    page_refs = rest[:n_pages]
    wexp_ref, bias_ref, o_ref, msel_ref, comp_ref, ck_ref, cv_ref = rest[n_pages:]
    qi = pl.program_id(1)
    n_keys = n_pages * PAGE + t_new
    n_cmp = n_keys // NSA_BLOCK
    n_sel = -(-n_keys // NSA_BLOCK)
    per = PAGE // NSA_BLOCK

    @pl.when(qi == 0)
    def _():
        comp_ref[...] = jnp.zeros_like(comp_ref)
        chunks = _kv_chunks(page_refs, kvn_ref, t_new)[:n_cmp // per]
        for part in range(2 * KV_GROUPS):
            cols = slice(part * HEAD_DIM, (part + 1) * HEAD_DIM)
            xw = jnp.concatenate([parts[part] * wexp_ref[:, cols] for parts in chunks], 0)
            comp_ref[0:per * len(chunks), cols] = xw.reshape(per * len(chunks), NSA_BLOCK, HEAD_DIM).sum(1)
        for g in range(KV_GROUPS):
            ck_ref[g] = comp_ref[:, g * HEAD_DIM:(g + 1) * HEAD_DIM].astype(BF16)
            cv_ref[g] = comp_ref[:, (KV_GROUPS + g) * HEAD_DIM:(KV_GROUPS + g + 1) * HEAD_DIM].astype(BF16)

    scale = HEAD_DIM ** -0.5
    rows = GROUP_HEADS * tq
    blk = lax.broadcasted_iota(jnp.int32, (1, LANES), 1)
    q0 = pos0 + qi * tq
    assert tq & (tq - 1) == 0
    qpos_st = q0 + (lax.broadcasted_iota(jnp.int32, (rows, 1), 0) & (tq - 1))
    mask = (qpos_st - ((blk + 1) * NSA_BLOCK - 1) >= 0) & (blk < n_cmp)
    cur = (q0 + lax.broadcasted_iota(jnp.int32, (tq, 1), 0)) // NSA_BLOCK
    scores = []
    for g in range(KV_GROUPS):
        heads = [g * GROUP_HEADS + r for r in range(GROUP_HEADS)]
        bias = bias_ref[g]
        if t_new != tq:
            bias = pltpu.roll(bias, qi * (tq // NSA_BLOCK) + (LANES - CMP_BIAS_LANE0), 1)
        q = jnp.concatenate([q_ref[:, h * HEAD_DIM:(h + 1) * HEAD_DIM] for h in heads], 0).astype(BF16)
        scores.append(_dot_nt(q, ck_ref[g]) * scale + bias)
    probs = []
    for g in range(KV_GROUPS):
        p, l = _softmax_rows(scores[g], mask)
        probs.append(p / jnp.maximum(l, 1e-30))
    for g in range(KV_GROUPS):
        o = jnp.dot(probs[g].astype(BF16), cv_ref[g], preferred_element_type=F32)
        for r in range(GROUP_HEADS):
            h = g * GROUP_HEADS + r
            o_ref[:, h * HEAD_DIM:(h + 1) * HEAD_DIM] = o[r * tq:(r + 1) * tq]
    for g in range(KV_GROUPS):
        imp = probs[g][0:tq]
        for r in range(1, GROUP_HEADS):
            imp = imp + probs[g][r * tq:(r + 1) * tq]
        imp = jnp.where(blk == cur, 2.0, jnp.where(blk > cur, -1.0, imp))
        imp = jnp.where(blk < n_sel, imp, -2.0)
        n_top = min(NSA_TOPN, n_sel)
        cols = slice(g * LANES, (g + 1) * LANES)

        def by_rank(imp=imp, cols=cols):
            rank = jnp.zeros((tq, LANES), F32)
            for i in range(n_sel):
                col = imp[:, i:i + 1]
                ahead = (col > imp) | ((col == imp) & (blk > i))
                rank = rank + jnp.where(ahead, 1.0, 0.0)
            msel_ref[:, cols] = jnp.where((rank < float(n_top)) & (blk < n_sel), 1.0, 0.0)

        def first_blocks(cols=cols):
            msel_ref[:, cols] = jnp.where(blk < n_top, 1.0, 0.0) + jnp.zeros((tq, LANES), F32)

        if t_new == tq:
            if pos0 + tq <= n_top * NSA_BLOCK:
                first_blocks()
            else:
                by_rank()
        else:
            early = pos0 + (qi + 1) * tq <= n_top * NSA_BLOCK
            pl.when(early)(first_blocks)
            pl.when(jnp.logical_not(early))(by_rank)


def _on_causal_width(qi, tq, widths, tile):
    if len(widths) == 1:
        tile(widths[0], True)
        return
    need = (qi * tq + tq - 1) // widths[0]
    for nw, w in enumerate(widths):
        pl.when(need == nw)(functools.partial(tile, w, nw == 0))


def _attn_body(pt_ref, q_ref, kvn_ref, *rest, mode, n_pages, t_new, tq, pos0, widths):
    del pt_ref
    page_refs = rest[:n_pages]
    rest = rest[n_pages:]
    m_ref = None
    if mode in ("sel", "mask"):
        m_ref, rest = rest[0], rest[1:]
    band_ref, o_ref, kc_ref, vc_ref, s_ref, cap_ref = rest[:6]
    p_ref = rest[6] if len(rest) > 6 else s_ref
    qi = pl.program_id(1)
    single = t_new == tq
    n_keys = n_pages * PAGE + t_new
    kbase = pos0 - n_pages * PAGE
    scale = HEAD_DIM ** -0.5
    q0 = pos0 if single else pos0 + qi * tq

    @pl.when(qi == 0)
    def _():
        for c, parts in enumerate(_kv_chunks(page_refs, kvn_ref, t_new)):
            rows = slice(c * PAGE, (c + 1) * PAGE)
            for g in range(KV_GROUPS):
                kc_ref[g, rows, :] = parts[g].astype(BF16)
                vc_ref[g, rows, :] = parts[KV_GROUPS + g].astype(BF16)

    def tile(c0, w, band_at, maybe_first):
        qpos = q0 + lax.broadcasted_iota(jnp.int32, (tq, 1), 0)
        col = c0 + lax.broadcasted_iota(jnp.int32, (1, w), 1)
        dist = qpos - (kbase + col)
        visible = (dist >= 0) & (col < n_keys)
        if mode == "win":
            visible = visible & (dist < NSA_WINDOW)
        if mode == "mask":
            visible = visible & (m_ref[:, 0:w] > 0.5)
        keys = pl.ds(c0, w)
        groups = range(KV_GROUPS)
        for g in groups:
            mask = visible
            if mode == "sel":
                expand = (lax.broadcasted_iota(jnp.int32, (LANES, w), 1) // NSA_BLOCK
                          == lax.broadcasted_iota(jnp.int32, (LANES, w), 0))
                chosen = jnp.dot(m_ref[:, g * LANES:(g + 1) * LANES].astype(BF16),
                                 jnp.where(expand, 1.0, 0.0).astype(BF16), preferred_element_type=F32)
                mask = visible & (chosen > 0.5)
            cap_ref[g, :, 0:w] = jnp.where(mask, jnp.inf, NEG_INF)
            heads = [g * GROUP_HEADS + r for r in range(GROUP_HEADS)]
            q = jnp.concatenate([q_ref[:, h * HEAD_DIM:(h + 1) * HEAD_DIM] for h in heads], 0).astype(BF16)
            s_ref[g, :, 0:w] = _dot_nt(q, kc_ref[g, keys, :]) * (scale * LOG2E)
        for g in groups:
            if band_at is not None:
                s_ref[g, :, band_at:band_at + BAND] += band_ref[g]
            else:
                if maybe_first:
                    @pl.when(qi == 0)
                    def _():
                        s_ref[g, :, 0:PAGE] += band_ref[g, :, PAGE:]

                @pl.when(qi > 0)
                def _():
                    s_ref[g, :, pl.ds(pl.multiple_of(q0 - PAGE - kbase, PAGE), BAND)] += band_ref[g]
        sums, alive = [], []
        for g in groups:
            for r in range(GROUP_HEADS):
                rows = slice(r * tq, (r + 1) * tq)
                m = jnp.max(jnp.minimum(s_ref[g, rows, 0:w], cap_ref[g, :, 0:w]), -1, keepdims=True)
                p = jnp.exp2(jnp.minimum(s_ref[g, rows, 0:w], cap_ref[g, :, 0:w]) - m)
                p_ref[g, rows, 0:w] = p.astype(p_ref.dtype)
                sums.append(jnp.sum(p, -1, keepdims=True))
                alive.append(m > NEG_INF)
        for g in groups:
            o = jnp.dot(p_ref[g, :, 0:w].astype(BF16), vc_ref[g, keys, :], preferred_element_type=F32)
            for r in range(GROUP_HEADS):
                h = g * GROUP_HEADS + r
                o_h = o[r * tq:(r + 1) * tq] / jnp.maximum(sums[h], 1e-30)
                o_ref[:, h * HEAD_DIM:(h + 1) * HEAD_DIM] = jnp.where(alive[h], o_h, 0.0)

    if single:
        tile(0, widths[0], pos0 - PAGE - kbase, False)
    elif mode == "win":
        wch = NSA_WINDOW // PAGE
        pl.when(qi < wch)(functools.partial(tile, 0, NSA_WINDOW, None, True))
        pl.when(qi >= wch)(lambda: tile(pl.multiple_of((qi - wch) * PAGE, PAGE), NSA_WINDOW + PAGE,
                                        NSA_WINDOW - PAGE, False))
    else:
        _on_causal_width(qi, tq, widths, lambda w, first: tile(0, w, None, first))


def _index_body(pt_ref, qidx_ref, miscq_ref, misck_ref, *rest, n_pages, t_new, tq, pos0, widths):
    del pt_ref
    ipage_refs = rest[:n_pages]
    o_ref, kidx_ref = rest[n_pages:]
    qi = pl.program_id(1)
    lk = o_ref.shape[1]
    n_keys = n_pages * PAGE + t_new
    kbase = pos0 - n_pages * PAGE
    q0 = pos0 if t_new == tq else pos0 + qi * tq

    @pl.when(qi == 0)
    def _():
        for c, r in enumerate(ipage_refs):
            kidx_ref[:, c * PAGE:(c + 1) * PAGE] = r[0].astype(BF16)
        for c, x in enumerate(_new_chunks(misck_ref, t_new)):
            cols = slice((n_pages + c) * PAGE, (n_pages + c + 1) * PAGE)
            kidx_ref[:, cols] = x.T[MISC_KI:MISC_KI + IDX_DIM, :].astype(BF16)

    def tile(w, maybe_first):
        del maybe_first
        qpos = q0 + lax.broadcasted_iota(jnp.int32, (tq, 1), 0)
        col = lax.broadcasted_iota(jnp.int32, (1, w), 1)
        visible = (qpos - (kbase + col) >= 0) & (col < n_keys)
        q = jnp.concatenate([qidx_ref[:, hh * IDX_DIM:(hh + 1) * IDX_DIM] for hh in range(IDX_HEADS)], 0)
        sc = jnp.dot(q.astype(BF16), kidx_ref[:, 0:w], preferred_element_type=F32)
        score = jnp.zeros((tq, w), F32)
        for hh in range(IDX_HEADS):
            wi = miscq_ref[:, MISC_WI + hh:MISC_WI + hh + 1] * (IDX_HEADS ** -0.5)
            score = score + jnp.maximum(sc[hh * tq:(hh + 1) * tq] * (IDX_DIM ** -0.5), 0.0) * wi
        o_ref[:, 0:w] = jnp.where(visible, score, NEG_INF)
        if w < lk:
            o_ref[:, w:] = jnp.full((tq, lk - w), NEG_INF, F32)

    _on_causal_width(qi, tq, widths, tile)


def _topk_body(s_ref, m_ref, key_ref, *, k, nq, tr, widths):
    lk = s_ref.shape[1]
    assert lk <= 4096
    neg_key = int(np.array(NEG_INF, np.float32).view(np.int32)) ^ 0x7FFFFFFF
    kf = float(k)

    def tile(w, maybe_first):
        del maybe_first
        bits = lax.bitcast_convert_type(s_ref[:, 0:w] + 0.0, jnp.int32)
        key_ref[:, 0:w] = jnp.where(bits >= 0, bits, bits ^ 0x7FFFFFFF)
        col = lax.broadcasted_iota(jnp.int32, (1, w), 1)
        unseen = float(lk - w)

        groups = [slice(a * (tr // TOPK_ROW_GROUPS), (a + 1) * (tr // TOPK_ROW_GROUPS)) for a in range(TOPK_ROW_GROUPS)]
        zeros = tuple(jnp.zeros((tr // TOPK_ROW_GROUPS, 1), jnp.int32) for _ in groups)

        def thr_step(i, tus):
            out = []
            for rows, tu in zip(groups, tus):
                cand = tu | jnp.left_shift(jnp.int32(1), 31 - i)
                cs = cand ^ INT_MIN
                cnt = jnp.sum(jnp.where(key_ref[rows, 0:w] >= cs, 1.0, 0.0), -1, keepdims=True)
                cnt = cnt + jnp.where(cs <= neg_key, unseen, 0.0)
                out.append(jnp.where(cnt >= kf, cand, tu))
            return tuple(out)

        thr = jnp.concatenate(lax.fori_loop(0, 32, thr_step, zeros, unroll=4), 0) ^ INT_MIN
        key = key_ref[:, 0:w]
        above = key > thr
        tied = key == thr
        need = kf - jnp.sum(jnp.where(above, 1.0, 0.0), -1, keepdims=True)

        def tie_step(i, j0s):
            out = []
            for rows, j0 in zip(groups, j0s):
                cand = j0 | jnp.left_shift(jnp.int32(1), 11 - i)
                hit = (key_ref[rows, 0:w] == thr[rows]) & (col < cand)
                cnt = jnp.sum(jnp.where(hit, 1.0, 0.0), -1, keepdims=True)
                out.append(jnp.where(cnt < need[rows], cand, j0))
            return tuple(out)

        m_ref[:, 0:w] = jnp.where(above | tied, 1.0, 0.0)
        n_tied = jnp.sum(jnp.where(tied, 1.0, 0.0), -1, keepdims=True)
        excess = jnp.max(jnp.where(thr > neg_key, n_tied - need, 0.0))

        @pl.when(excess > 0.0)
        def _():
            j0 = jnp.concatenate(lax.fori_loop(0, 12, tie_step, zeros, unroll=4), 0)
            k2 = key_ref[:, 0:w]
            m_ref[:, 0:w] = jnp.where((k2 > thr) | ((k2 == thr) & (col <= j0)), 1.0, 0.0)
        if w < lk:
            m_ref[:, w:] = jnp.zeros((tr, lk - w), F32)

    _on_causal_width(pl.program_id(0) % nq, tr, widths, tile)


class _Group:
    def __init__(self, row0, n_seq, t_new, tq, pos0, n_pages):
        assert t_new % tq == 0 and row0 % tq == 0 and row0 % t_new == 0
        assert pos0 == n_pages * PAGE or n_pages * PAGE < pos0
        assert t_new == tq or (tq == PAGE and pos0 == 0)
        assert t_new % PAGE == 0 or t_new % PAGE < NSA_BLOCK
        self.row0, self.n_seq, self.t_new, self.tq, self.pos0, self.n_pages = row0, n_seq, t_new, tq, pos0, n_pages
        self.nq = t_new // tq
        self.rows = n_seq * t_new
        self.lk = (n_pages + -(-t_new // PAGE)) * PAGE
        self.n_keys = n_pages * PAGE + t_new

    def q_spec(self, width, col):
        return pl.BlockSpec((self.tq, width), lambda b, qi, pt: (self.row0 // self.tq + b * self.nq + qi, col // width))

    def seq_spec(self, width, col):
        return pl.BlockSpec((self.t_new, width), lambda b, qi, pt: (self.row0 // self.t_new + b, col // width))

    def page_specs(self, shape):
        return [pl.BlockSpec((1,) + shape, lambda b, qi, pt, p=p: (pt[b, p], 0, 0)) for p in range(self.n_pages)]

    def out_spec(self, width):
        return pl.BlockSpec((self.tq, width), lambda b, qi, pt: (b * self.nq + qi, 0))

    def statics(self):
        return dict(n_pages=self.n_pages, t_new=self.t_new, tq=self.tq, pos0=self.pos0)

    def widths(self):
        if self.nq == 1:
            return (self.lk,)
        step = 4 * PAGE
        assert self.lk % step == 0
        return tuple(range(step, self.lk + 1, step))


def _cmp_bias_table(rel_bias, grp):
    lane0 = 0 if grp.nq == 1 else CMP_BIAS_LANE0
    tab = rel_bias[_BUCKETS]
    pieces, n_far = [], 0
    for lane in range(LANES + 1):
        d0 = grp.pos0 - (NSA_BLOCK * (lane - lane0 + 1) - 1)
        plain = lane < LANES and (d0 >= PAGE or d0 + grp.tq - 1 < 0)
        if plain:
            n_far += 1
            continue
        if n_far:
            pieces.append(jnp.broadcast_to(rel_bias[N_BUCKETS - 1], (grp.tq, n_far, rel_bias.shape[1])))
            n_far = 0
        if lane < LANES:
            pieces.append(tab[np.clip(d0 + np.arange(grp.tq), 0, BAND - 1)][:, None, :])
    table = jnp.transpose(jnp.concatenate(pieces, 1), (2, 0, 1))
    return table.reshape(-1, GROUP_HEADS * grp.tq, LANES)


def nsa_compress(grp, z, page_table, pool, wexp, bias):
    kv_w = KV_GROUPS * 2 * HEAD_DIM
    qw = KV_GROUPS * GROUP_HEADS * HEAD_DIM
    const2 = lambda b, qi, pt: (0, 0)
    return pl.pallas_call(
        functools.partial(_cmp_body, **grp.statics()),
        grid_spec=pltpu.PrefetchScalarGridSpec(
            num_scalar_prefetch=1,
            grid=(grp.n_seq, grp.nq),
            in_specs=[grp.q_spec(qw, COL_QN), grp.seq_spec(kv_w, COL_KVC)] + grp.page_specs(KV_PAGE)
            + [pl.BlockSpec((PAGE, kv_w), const2), pl.BlockSpec(bias.shape, lambda b, qi, pt: (0, 0, 0))],
            out_specs=[grp.out_spec(qw), grp.out_spec(KV_GROUPS * LANES)],
            scratch_shapes=[pltpu.VMEM((LANES, kv_w), F32),
                            pltpu.VMEM((KV_GROUPS, LANES, HEAD_DIM), BF16),
                            pltpu.VMEM((KV_GROUPS, LANES, HEAD_DIM), BF16)]),
        out_shape=[jax.ShapeDtypeStruct((grp.rows, qw), F32),
                   jax.ShapeDtypeStruct((grp.rows, KV_GROUPS * LANES), F32)],
        compiler_params=_cparams("parallel", "arbitrary"),
        name="nsa_compress",
    )(page_table, z, z, *([pool] * grp.n_pages), wexp, bias)


def sparse_attention(mode, grp, z, page_table, pool, band, *, q_col, kv_col, mask=None):
    kv_w = KV_GROUPS * 2 * HEAD_DIM
    qw = KV_GROUPS * GROUP_HEADS * HEAD_DIM
    in_specs = [grp.q_spec(qw, q_col), grp.seq_spec(kv_w, kv_col)] + grp.page_specs(KV_PAGE)
    args = [z, z] + [pool] * grp.n_pages
    if mode in ("sel", "mask"):
        in_specs.append(grp.out_spec(mask.shape[1]))
        args.append(mask)
    in_specs.append(pl.BlockSpec((KV_GROUPS, GROUP_HEADS * grp.tq, BAND), lambda b, qi, pt: (0, 0, 0)))
    args.append(band)
    widths = grp.widths()
    s_cols = max(widths) if (mode != "win" or grp.nq == 1) else NSA_WINDOW + PAGE
    rows = GROUP_HEADS * grp.tq
    scratch = [pltpu.VMEM((KV_GROUPS, grp.lk, HEAD_DIM), BF16),
               pltpu.VMEM((KV_GROUPS, grp.lk, HEAD_DIM), BF16),
               pltpu.VMEM((KV_GROUPS, rows, s_cols), F32),
               pltpu.VMEM((KV_GROUPS, grp.tq, s_cols), F32)]
    if grp.tq % BF16_SUBLANES == 0:
        scratch.append(pltpu.VMEM((KV_GROUPS, rows, s_cols), BF16))
    return pl.pallas_call(
        functools.partial(_attn_body, mode=mode, widths=widths, **grp.statics()),
        grid_spec=pltpu.PrefetchScalarGridSpec(
            num_scalar_prefetch=1,
            grid=(grp.n_seq, grp.nq),
            in_specs=in_specs,
            out_specs=grp.out_spec(qw),
            scratch_shapes=scratch),
        out_shape=jax.ShapeDtypeStruct((grp.rows, qw), F32),
        compiler_params=_cparams("parallel", "arbitrary"),
        name="sparse_attention_" + mode,
    )(page_table, *args)


def dsa_index_scores(grp, z, page_table, idx_pool):
    return pl.pallas_call(
        functools.partial(_index_body, widths=grp.widths(), **grp.statics()),
        grid_spec=pltpu.PrefetchScalarGridSpec(
            num_scalar_prefetch=1,
            grid=(grp.n_seq, grp.nq),
            in_specs=[grp.q_spec(IDX_HEADS * IDX_DIM, COL_QI), grp.q_spec(LANES, COL_MISC),
                      grp.seq_spec(LANES, COL_MISC)] + grp.page_specs((IDX_DIM, PAGE)),
            out_specs=grp.out_spec(grp.lk),
            scratch_shapes=[pltpu.VMEM((IDX_DIM, grp.lk), BF16)]),
        out_shape=jax.ShapeDtypeStruct((grp.rows, grp.lk), F32),
        compiler_params=_cparams("parallel", "arbitrary"),
        name="dsa_index_scores",
    )(page_table, z, z, z, *([idx_pool] * grp.n_pages))


def topk_mask(grp, scores, k):
    rows, lk = scores.shape
    tr = PAGE
    assert rows % tr == 0 and (grp.nq == 1 or grp.tq == tr)
    blk = pl.BlockSpec((tr, lk), lambda i: (i, 0))
    return pl.pallas_call(
        functools.partial(_topk_body, k=k, nq=grp.nq, tr=tr, widths=grp.widths()),
        grid=(rows // tr,),
        in_specs=[blk],
        out_specs=blk,
        out_shape=jax.ShapeDtypeStruct((rows, lk), F32),
        scratch_shapes=[pltpu.VMEM((tr, lk), jnp.int32)],
        compiler_params=_cparams("parallel"),
        name="topk_mask",
    )(scores)


def _combine_body(oc_ref, os_ref, ow_ref, od_ref, misc_ref, y_ref):
    n_heads = KV_GROUPS * GROUP_HEADS
    gates = jax.nn.sigmoid(misc_ref[:, MISC_GATES:MISC_GATES + 3 * n_heads])
    for h in range(n_heads):
        hs = slice(h * HEAD_DIM, (h + 1) * HEAD_DIM)
        o = (gates[:, 3 * h:3 * h + 1] * oc_ref[:, hs] + gates[:, 3 * h + 1:3 * h + 2] * os_ref[:, hs]
             + gates[:, 3 * h + 2:3 * h + 3] * ow_ref[:, hs])
        y_ref[:, hs] = o.astype(y_ref.dtype)
    y_ref[:, n_heads * HEAD_DIM:] = od_ref[...].astype(y_ref.dtype)


def nsa_dsa_combine(o_c, o_s, o_w, o_d, z, row0, *, tm):
    m, w = o_c.shape
    assert m % tm == 0 and row0 % tm == 0
    blk = pl.BlockSpec((tm, w), lambda i: (i, 0))
    return pl.pallas_call(
        _combine_body,
        grid=(m // tm,),
        in_specs=[blk, blk, blk, blk, pl.BlockSpec((tm, LANES), lambda i: (row0 // tm + i, COL_MISC // LANES))],
        out_specs=pl.BlockSpec((tm, 2 * w), lambda i: (i, 0)),
        out_shape=jax.ShapeDtypeStruct((m, 2 * w), BF16),
        compiler_params=_cparams("parallel"),
        name="nsa_dsa_combine",
    )(o_c, o_s, o_w, o_d, z)


def _kv_cache_rows_body(z_ref, *o_refs):
    n_parts = 2 * KV_GROUPS
    tm = z_ref.shape[0]
    for seg, o_ref in enumerate(o_refs):
        for part in range(n_parts):
            col = (seg * n_parts + part) * HEAD_DIM
            o_ref[pl.ds(part, tm, stride=n_parts), :] = z_ref[:, col:col + HEAD_DIM]


def kv_cache_rows(z, row0, n_rows, *, tm=512):
    n_parts = 2 * KV_GROUPS
    width = 4 * n_parts * HEAD_DIM
    assert COL_KVS - COL_KVC == COL_KVW - COL_KVS == COL_KVD - COL_KVW == n_parts * HEAD_DIM and COL_KVC % width == 0
    assert n_rows % tm == 0 and row0 % tm == 0
    out_spec = pl.BlockSpec((tm * n_parts, HEAD_DIM), lambda i: (i, 0))
    out_shape = jax.ShapeDtypeStruct((n_rows * n_parts, HEAD_DIM), F32)
    return pl.pallas_call(
        _kv_cache_rows_body,
        grid=(n_rows // tm,),
        in_specs=[pl.BlockSpec((tm, width), lambda i: (row0 // tm + i, COL_KVC // width))],
        out_specs=[out_spec] * 4,
        out_shape=[out_shape] * 4,
        compiler_params=_cparams("parallel"),
        name="kv_cache_rows",
    )(z)


def _band_tiles(rel_bias, tq):
    delta = (rel_bias[_BUCKETS] - rel_bias[N_BUCKETS - 1]).T
    rev = jnp.concatenate([delta[:, ::-1], jnp.zeros((delta.shape[0], PAGE), delta.dtype)], 1)
    tiles = jnp.stack([rev[:, PAGE - 1 - i:PAGE - 1 - i + BAND] for i in range(tq)], 1)
    return tiles.reshape(-1, GROUP_HEADS * tq, BAND) * LOG2E


def _widen_cd_w_in(w):
    sizes = (1024, 512, 512, 512, 24, 1024, 512, 512, 64, 8)
    q_n, kv_c, kv_s, kv_w, gates, q_d, kv_d, q_i, k_i, w_i = jnp.split(w, np.cumsum(sizes)[:-1].tolist(), axis=-1)
    cols = [q_n, q_d, kv_c, kv_s, kv_w, kv_d, q_i, k_i, gates, w_i]
    used = sum(c.shape[-1] for c in cols)
    return jnp.concatenate(cols + [jnp.zeros(w.shape[:-1] + (NZ - used,), w.dtype)], axis=-1)


def kernel(x_prompt, x_sample, state_conv, state_pool, cache_nsa_cmp, cache_nsa_sel, cache_nsa_win, cache_dsa_kv, cache_dsa_idx, page_table, norm_mix, norm_ffn, norm_final, ab_w_in, ab_conv_w, ab_conv_b, ab_ln_g, ab_ln_b, ab_pool_w, ab_pool_scale, ab_w_out, cd_w_in, cd_w_cmp, cd_w_out, rel_bias, ffn_w1, ffn_w2):
    bp, tp, d_model = x_prompt.shape
    bs, ts, _ = x_sample.shape
    mp, ms = bp * tp, bs * ts
    depth = norm_mix.shape[0]
    n_pages = page_table.shape[1]
    n_pool = cache_nsa_cmp.shape[1]
    past_len = n_pages * PAGE
    assert cache_nsa_cmp.shape[2] == PAGE
    win_len = cache_nsa_win.shape[2]
    assert win_len % PAGE == 0 and win_len == NSA_WINDOW and tp >= NSA_WINDOW
    kv_w = KV_GROUPS * 2 * HEAD_DIM

    xs = [x_prompt.reshape(mp, d_model), x_sample.reshape(ms, d_model)]
    grp_p = _Group(0, bp, tp, PAGE, 0, 0)
    grp_s = _Group(mp, bs, ts, ts, past_len, n_pages)
    grp_sw = _Group(mp, bs, ts, ts, past_len, win_len // PAGE)
    no_pages = jnp.zeros((1, 1), jnp.int32)
    win_pages = jnp.arange(bs * (win_len // PAGE), dtype=jnp.int32).reshape(bs, win_len // PAGE)

    outs = {k: [] for k in ("conv_p", "conv_s", "pool_p", "pool_s", "cmp_p", "cmp_s", "sel_p", "sel_s",
                            "win_p", "win_s", "dsa_p", "dsa_s", "idx_p", "idx_s")}
    w_bf16 = {"ab_w_in": ab_w_in, "ab_w_out": ab_w_out, "cd_w_in": _widen_cd_w_in(cd_w_in), "cd_w_out": cd_w_out,
              "ffn_w1": ffn_w1, "ffn_w2": ffn_w2}
    w_bf16 = {k: v.astype(BF16) for k, v in w_bf16.items()}
    y_p = y_s = None
    for i in range(depth):
        j = i // 2
        if i % 2 == 0:
            d_conv = ab_conv_w.shape[2]
            z = norm_matmul(xs, norm_mix[i], w_bf16["ab_w_in"], j)
            mid_p, u_p = ab_mid_prompt(z, bp, tp, ab_conv_w[j], ab_conv_b[j], ab_ln_g[j], ab_ln_b[j],
                                       ab_pool_w[j], ab_pool_scale[j])
            mid_s, conv_s, pool_s = ab_mid_step(z, mp, bs, ts, past_len, state_conv[j], state_pool[j], ab_conv_w[j],
                                                ab_conv_b[j], ab_ln_g[j], ab_ln_b[j], ab_pool_w[j], ab_pool_scale[j])
            xs = [matmul_residual([mid_p, mid_s], w_bf16["ab_w_out"], j, xs)]
            outs["conv_p"].append(u_p.reshape(bp, tp, d_conv)[:, tp - CONV_BUF:])
            outs["conv_s"].append(conv_s)
            outs["pool_p"].append(jnp.stack([z[(b + 1) * tp - POOL_BUF:(b + 1) * tp, 2 * d_conv:] for b in range(bp)]))
            outs["pool_s"].append(pool_s)
        else:
            z = norm_matmul(xs, norm_mix[i], w_bf16["cd_w_in"], j)
            nsa_bias = rel_bias[:, :KV_GROUPS * GROUP_HEADS]
            band_p, band_s = _band_tiles(rel_bias, grp_p.tq), _band_tiles(rel_bias, grp_s.tq)
            wexp = jnp.tile(jnp.repeat(jnp.transpose(cd_w_cmp[j], (1, 0, 2)).reshape(NSA_BLOCK, 2 * KV_GROUPS),
                                       HEAD_DIM, axis=1), (PAGE // NSA_BLOCK, 1))
            pt = page_table + j * n_pool
            pools = [c.reshape((-1,) + KV_PAGE) for c in (cache_nsa_cmp, cache_nsa_sel, cache_dsa_kv)]
            idx_pool = jnp.swapaxes(cache_dsa_idx, 2, 3).reshape(-1, IDX_DIM, PAGE)
            win_pool = cache_nsa_win.reshape((-1,) + KV_PAGE)
            wpt = win_pages + j * bs * (win_len // PAGE)
            mids = []
            for grp, gw, ptab, wtab, band in ((grp_p, grp_p, no_pages, no_pages, band_p),
                                              (grp_s, grp_sw, pt, wpt, band_s)):
                o_c, msel = nsa_compress(grp, z, ptab, pools[0], wexp, _cmp_bias_table(nsa_bias, grp))
                o_s = sparse_attention("sel", grp, z, ptab, pools[1], band[:KV_GROUPS],
                                       q_col=COL_QN, kv_col=COL_KVS, mask=msel)
                o_w = sparse_attention("win", gw, z, wtab, win_pool, band[:KV_GROUPS],
                                       q_col=COL_QN, kv_col=COL_KVW)
                top = topk_mask(grp, dsa_index_scores(grp, z, ptab, idx_pool), min(DSA_TOPK, grp.n_keys // 4))
                o_d = sparse_attention("mask", grp, z, ptab, pools[2], band[KV_GROUPS:],
                                       q_col=COL_QD, kv_col=COL_KVD, mask=top)
                mids.append(nsa_dsa_combine(o_c, o_s, o_w, o_d, z, grp.row0, tm=min(512, grp.rows)))
            xs = [matmul_residual(mids, w_bf16["cd_w_out"], j, xs)]

            kv_tail = (2, KV_GROUPS, HEAD_DIM)
            new_p = [o.reshape((bp, tp) + kv_tail) for o in kv_cache_rows(z, 0, mp)]
            new_s = [o.reshape((bs, ts) + kv_tail) for o in kv_cache_rows(z, mp, ms)]
            for name, k in (("cmp", 0), ("sel", 1), ("dsa", 3)):
                outs[name + "_p"].append(new_p[k])
                outs[name + "_s"].append(new_s[k])
            outs["win_p"].append(new_p[2][:, tp - NSA_WINDOW:])
            outs["win_s"].append(jnp.concatenate([cache_nsa_win[j], new_s[2]], 1)[:, ts:])
            k_idx = z[:, COL_MISC + MISC_KI:COL_MISC + MISC_KI + IDX_DIM]
            outs["idx_p"].append(k_idx[:mp].reshape(bp, tp, IDX_DIM))
            outs["idx_s"].append(k_idx[mp:].reshape(bs, ts, IDX_DIM))
        a = norm_matmul(xs, norm_ffn[i], w_bf16["ffn_w1"], i, relu2=True, out_dtype=BF16)
        if i == depth - 1:
            y_p, y_s = matmul_residual([a], w_bf16["ffn_w2"], i, xs, norm_final, split_out=(mp, ms))
        else:
            xs = [matmul_residual([a], w_bf16["ffn_w2"], i, xs)]

    st = {k: jnp.stack(v) for k, v in outs.items()}
    return (y_p.reshape(bp, tp, d_model), y_s.reshape(bs, ts, d_model),
            st["conv_p"], st["conv_s"], st["pool_p"], st["pool_s"], st["cmp_p"], st["cmp_s"],
            st["sel_p"], st["sel_s"], st["win_p"], st["win_s"], st["dsa_p"], st["dsa_s"],
            st["idx_p"], st["idx_s"])
```

```python
import functools
import math

import numpy as np
import jax
import jax.numpy as jnp
from jax import lax
from jax.experimental import pallas as pl
from jax.experimental.pallas import tpu as pltpu

F32 = jnp.float32
BF16 = jnp.bfloat16

EPS = 1e-6
NEG_INF = -1e30
LOG2E = math.log2(math.e)
HEAD_DIM = 128
LANES = 128
BF16_SUBLANES = 16
CONV_WIDTH = 31
CONV_BUF = CONV_WIDTH - 1
POOL_WINDOWS = (2, 4, 8, 16)
POOL_BUF = max(POOL_WINDOWS) - 1
HALO = 32
VMEM_LIMIT = 56 * 1024 * 1024


def _cparams(*sem):
    return pltpu.CompilerParams(dimension_semantics=sem, vmem_limit_bytes=VMEM_LIMIT)


class _Rows:
    def __init__(self, arrays, tm):
        self.arrays = list(arrays)
        self.tm = tm
        assert all(a.shape[0] % tm == 0 for a in self.arrays)
        self.tiles = [a.shape[0] // tm for a in self.arrays]
        self.n_tiles = sum(self.tiles)
        self.n = len(self.arrays)

    def specs(self, width, col):
        out, t0 = [], 0
        for nt in self.tiles:
            out.append(pl.BlockSpec((self.tm, width), lambda i, j, t0=t0, nt=nt: (jnp.clip(i - t0, 0, nt - 1), col(j))))
            t0 += nt
        return out

    def select(self, i, refs, fn):
        if self.n == 1:
            fn(refs[0])
            return
        t0 = 0
        for nt, ref in zip(self.tiles, refs):
            pl.when((i >= t0) & (i < t0 + nt))(functools.partial(fn, ref))
            t0 += nt


def _rmsnorm_rows(x, g):
    return (x * lax.rsqrt(jnp.mean(x * x, -1, keepdims=True) + EPS)) * g


def _norm_matmul_body(*refs, rows, relu2):
    x_refs = refs[:rows.n]
    g_ref, w_ref, o_ref, h_ref = refs[rows.n:]

    @pl.when(pl.program_id(1) == 0)
    def _():
        def norm(x_ref):
            h_ref[...] = _rmsnorm_rows(x_ref[...], g_ref[...]).astype(BF16)

        rows.select(pl.program_id(0), x_refs, norm)

    y = jnp.dot(h_ref[...], w_ref[...], preferred_element_type=F32)
    if relu2:
        y = jnp.square(jnp.maximum(y, 0.0))
    o_ref[...] = y.astype(o_ref.dtype)


def norm_matmul(xs, g, w, layer, *, relu2=False, out_dtype=F32, tm=1024, tn=1024):
    rows = _Rows(xs, tm)
    _, d, n = w.shape
    assert n % tn == 0
    return pl.pallas_call(
        functools.partial(_norm_matmul_body, rows=rows, relu2=relu2),
        grid=(rows.n_tiles, n // tn),
        in_specs=rows.specs(d, lambda j: 0) + [pl.BlockSpec((1, d), lambda i, j: (0, 0)),
                                               pl.BlockSpec((None, d, tn), lambda i, j: (layer, 0, j))],
        out_specs=pl.BlockSpec((tm, tn), lambda i, j: (i, j)),
        out_shape=jax.ShapeDtypeStruct((rows.n_tiles * tm, n), out_dtype),
        scratch_shapes=[pltpu.VMEM((tm, d), BF16)],
        compiler_params=_cparams("parallel", "arbitrary"),
        name="norm_matmul",
    )(*rows.arrays, g.reshape(1, d), w)


def _matmul_residual_body(*refs, a_rows, r_rows, o_rows, final_norm):
    a_refs, refs = refs[:a_rows.n], refs[a_rows.n:]
    w_ref, refs = refs[0], refs[1:]
    r_refs, refs = refs[:r_rows.n], refs[r_rows.n:]
    if final_norm:
        g_ref, refs = refs[0], refs[1:]
    o_refs, acc_ref = refs[:o_rows.n], refs[o_rows.n]
    i, k = pl.program_id(0), pl.program_id(1)

    @pl.when(k == 0)
    def _():
        acc_ref[...] = jnp.zeros_like(acc_ref)

    def accumulate(a_ref):
        acc_ref[...] += jnp.dot(a_ref[...], w_ref[...], preferred_element_type=F32)

    a_rows.select(i, a_refs, accumulate)

    @pl.when(k == pl.num_programs(1) - 1)
    def _():
        def add_residual(r_ref):
            acc_ref[...] += r_ref[...]

        def write(o_ref):
            o = acc_ref[...]
            o_ref[...] = _rmsnorm_rows(o, g_ref[...]) if final_norm else o

        r_rows.select(i, r_refs, add_residual)
        o_rows.select(i, o_refs, write)


def matmul_residual(a_list, w, layer, r_list, g_final=None, *, split_out=None, tm=512, tk=2048):
    a_rows, r_rows = _Rows(a_list, tm), _Rows(r_list, tm)
    _, kdim, n = w.shape
    m = a_rows.n_tiles * tm
    assert kdim % tk == 0 and r_rows.n_tiles == a_rows.n_tiles
    final_norm = g_final is not None
    o_rows = _Rows([jax.ShapeDtypeStruct((r, n), F32) for r in (split_out or (m,))], tm)
    assert o_rows.n_tiles == a_rows.n_tiles
    in_specs = (a_rows.specs(tk, lambda k: k) + [pl.BlockSpec((None, tk, n), lambda i, k: (layer, k, 0))]
                + r_rows.specs(n, lambda k: 0))
    args = a_rows.arrays + [w] + r_rows.arrays
    if final_norm:
        in_specs.append(pl.BlockSpec((1, n), lambda i, k: (0, 0)))
        args.append(g_final.reshape(1, n))
    out = pl.pallas_call(
        functools.partial(_matmul_residual_body, a_rows=a_rows, r_rows=r_rows, o_rows=o_rows, final_norm=final_norm),
        grid=(a_rows.n_tiles, kdim // tk),
        in_specs=in_specs,
        out_specs=o_rows.specs(n, lambda k: 0),
        out_shape=o_rows.arrays,
        scratch_shapes=[pltpu.VMEM((tm, n), F32)],
        compiler_params=_cparams("parallel", "arbitrary"),
        name="matmul_residual",
    )(*args)
    return out if split_out else out[0]


def _layernorm_silu(c, g, b):
    mu = jnp.mean(c, -1, keepdims=True)
    xc = c - mu
    y = xc * lax.rsqrt(jnp.mean(xc * xc, -1, keepdims=True) + EPS)
    y = y * g + b
    return y * jax.nn.sigmoid(y)


def _ab_mid_body(z_ref, zp_ref, cw_ref, cb_ref, lg_ref, lb_ref, pw_ref, ps_ref, y_ref, u_ref,
                 ext_ref, vext_ref, conv_ref, *, tt, d_conv, d_pool):
    ti = pl.program_id(1)
    keep = (ti > 0).astype(F32)
    a_p = zp_ref[:, 0:d_conv]
    g_p = zp_ref[:, d_conv:2 * d_conv]
    ext_ref[0:HALO, :] = a_p * jax.nn.sigmoid(g_p) * keep
    vext_ref[0:HALO, :] = zp_ref[:, 2 * d_conv:] * keep
    u = z_ref[:, 0:d_conv] * jax.nn.sigmoid(z_ref[:, d_conv:2 * d_conv])
    ext_ref[HALO:, :] = u
    u_ref[...] = u
    vext_ref[HALO:, :] = z_ref[:, 2 * d_conv:]

    off = HALO - CONV_BUF
    sub = 8
    for c in range(d_conv // LANES):
        cs = slice(c * LANES, (c + 1) * LANES)
        acc = jnp.zeros((tt, LANES), F32)
        for s in range(sub):
            n = tt if s == 0 else tt + sub
            part = jnp.zeros((n, LANES), F32)
            for j in range(CONV_WIDTH):
                if (off + j) % sub == s:
                    start = off + j - s
                    part = part + cw_ref[j:j + 1, cs] * ext_ref[start:start + n, cs]
            acc = acc + part[s:s + tt]
        conv_ref[:, cs] = acc + cb_ref[:, cs]
    y_ref[:, 0:d_conv] = _layernorm_silu(conv_ref[...], lg_ref[...], lb_ref[...]).astype(y_ref.dtype)

    pos = ti * tt + lax.broadcasted_iota(jnp.int32, (tt, 1), 0)
    pg = d_pool // len(POOL_WINDOWS)
    for gi, w in enumerate(POOL_WINDOWS):
        gs = slice(gi * pg, (gi + 1) * pg)
        tok = vext_ref[HALO:, gs]
        acc = tok
        for i in range(1, w):
            acc = acc + vext_ref[HALO - i:HALO - i + tt, gs]
        cnt = jnp.minimum(pos + 1, w).astype(F32)
        d = acc / cnt - tok
        yp = jnp.dot(d.astype(BF16), pw_ref[gi], preferred_element_type=F32) * ps_ref[:, gs]
        y_ref[:, d_conv + gi * pg:d_conv + (gi + 1) * pg] = yp.astype(y_ref.dtype)


def ab_mid_prompt(z, n_seq, t_len, conv_w, conv_b, ln_g, ln_b, pool_w, pool_scale, *, tt=256):
    d_conv = conv_w.shape[1]
    d_pool = pool_scale.shape[0]
    nt = t_len // tt
    hb = tt // HALO
    row = lambda b, t: (b * nt + t, 0)
    const = lambda b, t: (0, 0)
    return pl.pallas_call(
        functools.partial(_ab_mid_body, tt=tt, d_conv=d_conv, d_pool=d_pool),
        grid=(n_seq, nt),
        in_specs=[pl.BlockSpec((tt, z.shape[1]), row),
                  pl.BlockSpec((HALO, z.shape[1]), lambda b, t: (jnp.maximum((b * nt + t) * hb - 1, 0), 0)),
                  pl.BlockSpec(conv_w.shape, const),
                  pl.BlockSpec((1, d_conv), const),
                  pl.BlockSpec((1, d_conv), const),
                  pl.BlockSpec((1, d_conv), const),
                  pl.BlockSpec(pool_w.shape, lambda b, t: (0, 0, 0)),
                  pl.BlockSpec((1, d_pool), const)],
        out_specs=[pl.BlockSpec((tt, d_conv + d_pool), row),
                   pl.BlockSpec((tt, d_conv), row)],
        out_shape=[jax.ShapeDtypeStruct((n_seq * t_len, d_conv + d_pool), BF16),
                   jax.ShapeDtypeStruct((n_seq * t_len, d_conv), F32)],
        scratch_shapes=[pltpu.VMEM((HALO + tt, d_conv), F32),
                        pltpu.VMEM((HALO + tt, d_pool), F32),
                        pltpu.VMEM((tt, d_conv), F32)],
        compiler_params=_cparams("parallel", "parallel"),
        name="ab_mid_prompt",
    )(z, z, conv_w, conv_b.reshape(1, -1), ln_g.reshape(1, -1), ln_b.reshape(1, -1),
      pool_w.astype(BF16), pool_scale.reshape(1, -1))


def _ab_mid_step_body(z_ref, sc_ref, sp_ref, cw_ref, cb_ref, lg_ref, lb_ref, pw_ref, ps_ref,
                      y_ref, nc_ref, np_ref, ext_ref, vext_ref, *, nb, t, pos0, d_conv, d_pool):
    e0 = HALO - CONV_BUF
    p0 = 16 - POOL_BUF
    z = z_ref[...].reshape(nb, t, z_ref.shape[1])
    u = z[:, :, 0:d_conv] * jax.nn.sigmoid(z[:, :, d_conv:2 * d_conv])
    ext_ref[:, e0:HALO, :] = sc_ref[...]
    ext_ref[:, HALO:, :] = u
    vext_ref[:, p0:16, :] = sp_ref[...]
    vext_ref[:, 16:, :] = z[:, :, 2 * d_conv:]
    nc_ref[...] = ext_ref[:, HALO + t - CONV_BUF:, :]
    np_ref[...] = vext_ref[:, 16 + t - POOL_BUF:, :]

    acc = jnp.zeros((nb, t, d_conv), F32)
    for j in range(CONV_WIDTH):
        acc = acc + cw_ref[j:j + 1, :][None] * ext_ref[:, e0 + j:e0 + j + t, :]
    c = acc + cb_ref[...][None]
    yc = _layernorm_silu(c, lg_ref[...][None], lb_ref[...][None])
    y_ref[:, 0:d_conv] = yc.reshape(nb * t, d_conv).astype(y_ref.dtype)

    pg = d_pool // len(POOL_WINDOWS)
    for gi, w in enumerate(POOL_WINDOWS):
        gs = slice(gi * pg, (gi + 1) * pg)
        tok = vext_ref[:, 16:, gs]
        acc = tok
        for i in range(1, w):
            acc = acc + vext_ref[:, 16 - i:16 - i + t, gs]
        cnt = jnp.minimum(pos0 + 1 + lax.broadcasted_iota(jnp.int32, (1, t, 1), 1), w).astype(F32)
        d = (acc / cnt - tok).reshape(nb * t, pg)
        yp = jnp.dot(d.astype(BF16), pw_ref[gi], preferred_element_type=F32) * ps_ref[:, gs]
        y_ref[:, d_conv + gi * pg:d_conv + (gi + 1) * pg] = yp.astype(y_ref.dtype)


def ab_mid_step(z, row0, n_seq, t, pos0, state_conv, state_pool, conv_w, conv_b, ln_g, ln_b, pool_w,
                pool_scale, *, nb=16):
    d_conv = conv_w.shape[1]
    d_pool = pool_scale.shape[0]
    rb = nb * t
    assert row0 % rb == 0 and n_seq % nb == 0
    const = lambda i: (0, 0)
    seq3 = lambda i: (i, 0, 0)
    return pl.pallas_call(
        functools.partial(_ab_mid_step_body, nb=nb, t=t, pos0=pos0, d_conv=d_conv, d_pool=d_pool),
        grid=(n_seq // nb,),
        in_specs=[pl.BlockSpec((rb, z.shape[1]), lambda i: (row0 // rb + i, 0)),
                  pl.BlockSpec((nb, CONV_BUF, d_conv), seq3),
                  pl.BlockSpec((nb, POOL_BUF, d_pool), seq3),
                  pl.BlockSpec(conv_w.shape, const),
                  pl.BlockSpec((1, d_conv), const),
                  pl.BlockSpec((1, d_conv), const),
                  pl.BlockSpec((1, d_conv), const),
                  pl.BlockSpec(pool_w.shape, lambda i: (0, 0, 0)),
                  pl.BlockSpec((1, d_pool), const)],
        out_specs=[pl.BlockSpec((rb, d_conv + d_pool), lambda i: (i, 0)),
                   pl.BlockSpec((nb, CONV_BUF, d_conv), seq3),
                   pl.BlockSpec((nb, POOL_BUF, d_pool), seq3)],
        out_shape=[jax.ShapeDtypeStruct((n_seq * t, d_conv + d_pool), BF16),
                   jax.ShapeDtypeStruct((n_seq, CONV_BUF, d_conv), F32),
                   jax.ShapeDtypeStruct((n_seq, POOL_BUF, d_pool), F32)],
        scratch_shapes=[pltpu.VMEM((nb, HALO + t, d_conv), F32),
                        pltpu.VMEM((nb, 16 + t, d_pool), F32)],
        compiler_params=_cparams("parallel"),
        name="ab_mid_step",
    )(z, state_conv, state_pool, conv_w, conv_b.reshape(1, -1), ln_g.reshape(1, -1),
      ln_b.reshape(1, -1), pool_w.astype(BF16), pool_scale.reshape(1, -1))


N_BUCKETS = 32
MAX_DISTANCE = 128
NSA_BLOCK = 64
NSA_TOPN = 16
NSA_WINDOW = 512
DSA_TOPK = 256
IDX_HEADS = 8
IDX_DIM = 64
KV_GROUPS = 2
GROUP_HEADS = 4
PAGE = 128
BAND = 2 * PAGE
INT_MIN = -2 ** 31
KV_PAGE = (2 * KV_GROUPS * PAGE, HEAD_DIM)
TOPK_ROW_GROUPS = 4
SAMPLE_SEQS_PER_STEP = 2
CMP_BIAS_LANE0 = 64

COL_QN, COL_QD, COL_KVC, COL_KVS, COL_KVW, COL_KVD, COL_QI, COL_MISC = 0, 1024, 2048, 2560, 3072, 3584, 4096, 4608
MISC_KI, MISC_GATES, MISC_WI = 0, 64, 88
NZ = 5120


def _bucket_np(n):
    n = np.maximum(np.asarray(n, np.int32), 0)
    exact = N_BUCKETS // 2
    nf = np.maximum(n, 1).astype(np.float32)
    big = exact + (np.log(nf / np.float32(exact)) / np.float32(math.log(MAX_DISTANCE / exact))
                   * np.float32(N_BUCKETS - exact)).astype(np.int32)
    return np.where(n < exact, n, np.minimum(big, N_BUCKETS - 1))


_BUCKETS = _bucket_np(np.arange(BAND))
assert _BUCKETS[PAGE:].min() == N_BUCKETS - 1


def _softmax_rows(s, mask):
    s = jnp.where(mask, s, NEG_INF)
    m = jnp.max(s, -1, keepdims=True)
    p = jnp.where(mask, jnp.exp(s - m), 0.0)
    return p, jnp.sum(p, -1, keepdims=True)


def _dot_nt(a, b):
    return lax.dot_general(a, b, (((1,), (1,)), ((), ())), preferred_element_type=F32)


def _new_chunks(new_ref, t_new):
    chunks = [new_ref[c * PAGE:(c + 1) * PAGE, :] for c in range(t_new // PAGE)]
    rem = t_new % PAGE
    if rem:
        tail = new_ref[(t_new // PAGE) * PAGE:, :]
        chunks.append(jnp.concatenate([tail, jnp.zeros((PAGE - rem, tail.shape[1]), F32)], 0))
    return chunks


def _kv_chunks(page_refs, new_ref, t_new):
    n_parts = 2 * KV_GROUPS
    chunks = [[r[0, pl.ds(part, PAGE, stride=n_parts), :] for part in range(n_parts)] for r in page_refs]
    for x in _new_chunks(new_ref, t_new):
        chunks.append([x[:, part * HEAD_DIM:(part + 1) * HEAD_DIM] for part in range(n_parts)])
    return chunks


def _seq_view(ref, sq, seqs):
    n = ref.shape[0] // seqs
    return ref.at[pl.ds(sq * n, n)]


def _cmp_body(pt_ref, q_ref, kvn_ref, *rest, seqs, n_pages, **statics):
    del pt_ref
    pages, (wexp_ref, bias_ref, o_ref, msel_ref, *scratch) = rest[:seqs * n_pages], rest[seqs * n_pages:]
    for sq in range(seqs):
        _cmp_one(_seq_view(q_ref, sq, seqs), _seq_view(kvn_ref, sq, seqs), pages[sq * n_pages:(sq + 1) * n_pages],
                 wexp_ref, bias_ref, _seq_view(o_ref, sq, seqs), _seq_view(msel_ref, sq, seqs), *scratch,
                 n_pages=n_pages, **statics)


def _cmp_one(q_ref, kvn_ref, page_refs, wexp_ref, bias_ref, o_ref, msel_ref, comp_ref, ck_ref, cv_ref, *,
             n_pages, t_new, tq, pos0):
    qi = pl.program_id(1)
    n_keys = n_pages * PAGE + t_new
    n_cmp = n_keys // NSA_BLOCK
    n_sel = -(-n_keys // NSA_BLOCK)
    per = PAGE // NSA_BLOCK

    @pl.when(qi == 0)
    def _():
        comp_ref[...] = jnp.zeros_like(comp_ref)
        chunks = _kv_chunks(page_refs, kvn_ref, t_new)[:n_cmp // per]
        for part in range(2 * KV_GROUPS):
            cols = slice(part * HEAD_DIM, (part + 1) * HEAD_DIM)
            xw = jnp.concatenate([parts[part] * wexp_ref[:, cols] for parts in chunks], 0)
            comp_ref[0:per * len(chunks), cols] = xw.reshape(per * len(chunks), NSA_BLOCK, HEAD_DIM).sum(1)
        for g in range(KV_GROUPS):
            ck_ref[g] = comp_ref[:, g * HEAD_DIM:(g + 1) * HEAD_DIM].astype(BF16)
            cv_ref[g] = comp_ref[:, (KV_GROUPS + g) * HEAD_DIM:(KV_GROUPS + g + 1) * HEAD_DIM].astype(BF16)

    scale = HEAD_DIM ** -0.5
    rows = GROUP_HEADS * tq
    blk = lax.broadcasted_iota(jnp.int32, (1, LANES), 1)
    q0 = pos0 + qi * tq
    assert tq & (tq - 1) == 0
    qpos_st = q0 + (lax.broadcasted_iota(jnp.int32, (rows, 1), 0) & (tq - 1))
    mask = (qpos_st - ((blk + 1) * NSA_BLOCK - 1) >= 0) & (blk < n_cmp)
    cur = (q0 + lax.broadcasted_iota(jnp.int32, (tq, 1), 0)) // NSA_BLOCK
    scores = []
    for g in range(KV_GROUPS):
        heads = [g * GROUP_HEADS + r for r in range(GROUP_HEADS)]
        bias = bias_ref[g]
        if t_new != tq:
            bias = pltpu.roll(bias, qi * (tq // NSA_BLOCK) + (LANES - CMP_BIAS_LANE0), 1)
        q = jnp.concatenate([q_ref[:, h * HEAD_DIM:(h + 1) * HEAD_DIM] for h in heads], 0).astype(BF16)
        scores.append(_dot_nt(q, ck_ref[g]) * scale + bias)
    probs = []
    for g in range(KV_GROUPS):
        p, l = _softmax_rows(scores[g], mask)
        probs.append(p / jnp.maximum(l, 1e-30))
    for g in range(KV_GROUPS):
        o = jnp.dot(probs[g].astype(BF16), cv_ref[g], preferred_element_type=F32)
        for r in range(GROUP_HEADS):
            h = g * GROUP_HEADS + r
            o_ref[:, h * HEAD_DIM:(h + 1) * HEAD_DIM] = o[r * tq:(r + 1) * tq]
    for g in range(KV_GROUPS):
        imp = probs[g][0:tq]
        for r in range(1, GROUP_HEADS):
            imp = imp + probs[g][r * tq:(r + 1) * tq]
        imp = jnp.where(blk == cur, 2.0, jnp.where(blk > cur, -1.0, imp))
        imp = jnp.where(blk < n_sel, imp, -2.0)
        n_top = min(NSA_TOPN, n_sel)
        cols = slice(g * LANES, (g + 1) * LANES)

        def by_rank(imp=imp, cols=cols):
            rank = jnp.zeros((tq, LANES), F32)
            for i in range(n_sel):
                col = imp[:, i:i + 1]
                ahead = (col > imp) | ((col == imp) & (blk > i))
                rank = rank + jnp.where(ahead, 1.0, 0.0)
            msel_ref[:, cols] = jnp.where((rank < float(n_top)) & (blk < n_sel), 1.0, 0.0)

        def first_blocks(cols=cols):
            msel_ref[:, cols] = jnp.where(blk < n_top, 1.0, 0.0) + jnp.zeros((tq, LANES), F32)

        if t_new == tq:
            if pos0 + tq <= n_top * NSA_BLOCK:
                first_blocks()
            else:
                by_rank()
        else:
            early = pos0 + (qi + 1) * tq <= n_top * NSA_BLOCK
            pl.when(early)(first_blocks)
            pl.when(jnp.logical_not(early))(by_rank)


def _on_causal_width(qi, tq, widths, tile):
    if len(widths) == 1:
        tile(widths[0], True)
        return
    need = (qi * tq + tq - 1) // widths[0]
    for nw, w in enumerate(widths):
        pl.when(need == nw)(functools.partial(tile, w, nw == 0))


def _attn_body(pt_ref, q_ref, kvn_ref, *rest, mode, seqs, n_pages, **statics):
    del pt_ref
    pages, rest = rest[:seqs * n_pages], rest[seqs * n_pages:]
    m_ref = None
    if mode in ("sel", "mask"):
        m_ref, rest = rest[0], rest[1:]
    band_ref, o_ref, *scratch = rest
    for sq in range(seqs):
        _attn_one(_seq_view(q_ref, sq, seqs), _seq_view(kvn_ref, sq, seqs), pages[sq * n_pages:(sq + 1) * n_pages],
                  None if m_ref is None else _seq_view(m_ref, sq, seqs), band_ref, _seq_view(o_ref, sq, seqs),
                  *scratch, mode=mode, n_pages=n_pages, **statics)


def _attn_one(q_ref, kvn_ref, page_refs, m_ref, band_ref, o_ref, kc_ref, vc_ref, s_ref, cap_ref, *maybe_p_ref,
              mode, n_pages, t_new, tq, pos0, widths):
    p_ref = maybe_p_ref[0] if maybe_p_ref else s_ref
    qi = pl.program_id(1)
    single = t_new == tq
    n_keys = n_pages * PAGE + t_new
    kbase = pos0 - n_pages * PAGE
    scale = HEAD_DIM ** -0.5
    q0 = pos0 if single else pos0 + qi * tq

    @pl.when(qi == 0)
    def _():
        for c, parts in enumerate(_kv_chunks(page_refs, kvn_ref, t_new)):
            rows = slice(c * PAGE, (c + 1) * PAGE)
            for g in range(KV_GROUPS):
                kc_ref[g, rows, :] = parts[g].astype(BF16)
                vc_ref[g, rows, :] = parts[KV_GROUPS + g].astype(BF16)

    def tile(c0, w, band_at, maybe_first):
        qpos = q0 + lax.broadcasted_iota(jnp.int32, (tq, 1), 0)
        col = c0 + lax.broadcasted_iota(jnp.int32, (1, w), 1)
        dist = qpos - (kbase + col)
        visible = (dist >= 0) & (col < n_keys)
        if mode == "win":
            visible = visible & (dist < NSA_WINDOW)
        if mode == "mask":
            visible = visible & (m_ref[:, 0:w] > 0.5)
        keys = pl.ds(c0, w)
        groups = range(KV_GROUPS)
        for g in groups:
            mask = visible
            if mode == "sel":
                expand = (lax.broadcasted_iota(jnp.int32, (LANES, w), 1) // NSA_BLOCK
                          == lax.broadcasted_iota(jnp.int32, (LANES, w), 0))
                chosen = jnp.dot(m_ref[:, g * LANES:(g + 1) * LANES].astype(BF16),
                                 jnp.where(expand, 1.0, 0.0).astype(BF16), preferred_element_type=F32)
                mask = visible & (chosen > 0.5)
            cap_ref[g, :, 0:w] = jnp.where(mask, jnp.inf, NEG_INF)
            heads = [g * GROUP_HEADS + r for r in range(GROUP_HEADS)]
            q = jnp.concatenate([q_ref[:, h * HEAD_DIM:(h + 1) * HEAD_DIM] for h in heads], 0).astype(BF16)
            s_ref[g, :, 0:w] = _dot_nt(q, kc_ref[g, keys, :]) * (scale * LOG2E)
        for g in groups:
            if band_at is not None:
                s_ref[g, :, band_at:band_at + BAND] += band_ref[g]
            else:
                if maybe_first:
                    @pl.when(qi == 0)
                    def _():
                        s_ref[g, :, 0:PAGE] += band_ref[g, :, PAGE:]

                @pl.when(qi > 0)
                def _():
                    s_ref[g, :, pl.ds(pl.multiple_of(q0 - PAGE - kbase, PAGE), BAND)] += band_ref[g]
        sums, alive = [], []
        for g in groups:
            for r in range(GROUP_HEADS):
                rows = slice(r * tq, (r + 1) * tq)
                m = jnp.max(jnp.minimum(s_ref[g, rows, 0:w], cap_ref[g, :, 0:w]), -1, keepdims=True)
                p = jnp.exp2(jnp.minimum(s_ref[g, rows, 0:w], cap_ref[g, :, 0:w]) - m)
                p_ref[g, rows, 0:w] = p.astype(p_ref.dtype)
                sums.append(jnp.sum(p, -1, keepdims=True))
                alive.append(m > NEG_INF)
        for g in groups:
            o = jnp.dot(p_ref[g, :, 0:w].astype(BF16), vc_ref[g, keys, :], preferred_element_type=F32)
            for r in range(GROUP_HEADS):
                h = g * GROUP_HEADS + r
                o_h = o[r * tq:(r + 1) * tq] / jnp.maximum(sums[h], 1e-30)
                o_ref[:, h * HEAD_DIM:(h + 1) * HEAD_DIM] = jnp.where(alive[h], o_h, 0.0)

    if single:
        tile(0, widths[0], pos0 - PAGE - kbase, False)
    elif mode == "win":
        wch = NSA_WINDOW // PAGE
        pl.when(qi < wch)(functools.partial(tile, 0, NSA_WINDOW, None, True))
        pl.when(qi >= wch)(lambda: tile(pl.multiple_of((qi - wch) * PAGE, PAGE), NSA_WINDOW + PAGE,
                                        NSA_WINDOW - PAGE, False))
    else:
        _on_causal_width(qi, tq, widths, lambda w, first: tile(0, w, None, first))


def _index_body(pt_ref, qidx_ref, miscq_ref, misck_ref, *rest, seqs, n_pages, **statics):
    del pt_ref
    pages, (o_ref, kidx_ref) = rest[:seqs * n_pages], rest[seqs * n_pages:]
    for sq in range(seqs):
        _index_one(_seq_view(qidx_ref, sq, seqs), _seq_view(miscq_ref, sq, seqs), _seq_view(misck_ref, sq, seqs),
                   pages[sq * n_pages:(sq + 1) * n_pages], _seq_view(o_ref, sq, seqs), kidx_ref,
                   n_pages=n_pages, **statics)


def _index_one(qidx_ref, miscq_ref, misck_ref, ipage_refs, o_ref, kidx_ref, *, n_pages, t_new, tq, pos0, widths):
    qi = pl.program_id(1)
    lk = o_ref.shape[1]
    n_keys = n_pages * PAGE + t_new
    kbase = pos0 - n_pages * PAGE
    q0 = pos0 if t_new == tq else pos0 + qi * tq

    @pl.when(qi == 0)
    def _():
        for c, r in enumerate(ipage_refs):
            kidx_ref[:, c * PAGE:(c + 1) * PAGE] = r[0].astype(BF16)
        for c, x in enumerate(_new_chunks(misck_ref, t_new)):
            cols = slice((n_pages + c) * PAGE, (n_pages + c + 1) * PAGE)
            kidx_ref[:, cols] = x.T[MISC_KI:MISC_KI + IDX_DIM, :].astype(BF16)

    def tile(w, maybe_first):
        del maybe_first
        qpos = q0 + lax.broadcasted_iota(jnp.int32, (tq, 1), 0)
        col = lax.broadcasted_iota(jnp.int32, (1, w), 1)
        visible = (qpos - (kbase + col) >= 0) & (col < n_keys)
        q = jnp.concatenate([qidx_ref[:, hh * IDX_DIM:(hh + 1) * IDX_DIM] for hh in range(IDX_HEADS)], 0)
        sc = jnp.dot(q.astype(BF16), kidx_ref[:, 0:w], preferred_element_type=F32)
        score = jnp.zeros((tq, w), F32)
        for hh in range(IDX_HEADS):
            wi = miscq_ref[:, MISC_WI + hh:MISC_WI + hh + 1] * (IDX_HEADS ** -0.5)
            score = score + jnp.maximum(sc[hh * tq:(hh + 1) * tq] * (IDX_DIM ** -0.5), 0.0) * wi
        o_ref[:, 0:w] = jnp.where(visible, score, NEG_INF)
        if w < lk:
            o_ref[:, w:] = jnp.full((tq, lk - w), NEG_INF, F32)

    _on_causal_width(qi, tq, widths, tile)


def _topk_body(s_ref, m_ref, key_ref, *, k, nq, tr, widths):
    lk = s_ref.shape[1]
    assert lk <= 4096
    neg_key = int(np.array(NEG_INF, np.float32).view(np.int32)) ^ 0x7FFFFFFF
    kf = float(k)

    def tile(w, maybe_first):
        del maybe_first
        bits = lax.bitcast_convert_type(s_ref[:, 0:w] + 0.0, jnp.int32)
        key_ref[:, 0:w] = jnp.where(bits >= 0, bits, bits ^ 0x7FFFFFFF)
        col = lax.broadcasted_iota(jnp.int32, (1, w), 1)
        unseen = float(lk - w)

        groups = [slice(a * (tr // TOPK_ROW_GROUPS), (a + 1) * (tr // TOPK_ROW_GROUPS)) for a in range(TOPK_ROW_GROUPS)]
        zeros = tuple(jnp.zeros((tr // TOPK_ROW_GROUPS, 1), jnp.int32) for _ in groups)

        def thr_step(i, tus):
            out = []
            for rows, tu in zip(groups, tus):
                cand = tu | jnp.left_shift(jnp.int32(1), 31 - i)
                cs = cand ^ INT_MIN
                cnt = jnp.sum(jnp.where(key_ref[rows, 0:w] >= cs, 1.0, 0.0), -1, keepdims=True)
                cnt = cnt + jnp.where(cs <= neg_key, unseen, 0.0)
                out.append(jnp.where(cnt >= kf, cand, tu))
            return tuple(out)

        thr = jnp.concatenate(lax.fori_loop(0, 32, thr_step, zeros, unroll=4), 0) ^ INT_MIN
        key = key_ref[:, 0:w]
        above = key > thr
        tied = key == thr
        need = kf - jnp.sum(jnp.where(above, 1.0, 0.0), -1, keepdims=True)

        def tie_step(i, j0s):
            out = []
            for rows, j0 in zip(groups, j0s):
                cand = j0 | jnp.left_shift(jnp.int32(1), 11 - i)
                hit = (key_ref[rows, 0:w] == thr[rows]) & (col < cand)
                cnt = jnp.sum(jnp.where(hit, 1.0, 0.0), -1, keepdims=True)
                out.append(jnp.where(cnt < need[rows], cand, j0))
            return tuple(out)

        m_ref[:, 0:w] = jnp.where(above | tied, 1.0, 0.0)
        n_tied = jnp.sum(jnp.where(tied, 1.0, 0.0), -1, keepdims=True)
        excess = jnp.max(jnp.where(thr > neg_key, n_tied - need, 0.0))

        @pl.when(excess > 0.0)
        def _():
            j0 = jnp.concatenate(lax.fori_loop(0, 12, tie_step, zeros, unroll=4), 0)
            k2 = key_ref[:, 0:w]
            m_ref[:, 0:w] = jnp.where((k2 > thr) | ((k2 == thr) & (col <= j0)), 1.0, 0.0)
        if w < lk:
            m_ref[:, w:] = jnp.zeros((tr, lk - w), F32)

    _on_causal_width(pl.program_id(0) % nq, tr, widths, tile)


class _Group:
    def __init__(self, row0, n_seq, t_new, tq, pos0, n_pages, seqs=1):
        assert t_new % tq == 0 and row0 % (seqs * tq) == 0 and row0 % (seqs * t_new) == 0
        assert n_seq % seqs == 0 and (seqs == 1 or t_new == tq)
        self.seqs = seqs
        assert pos0 == n_pages * PAGE or n_pages * PAGE < pos0
        assert t_new == tq or (tq == PAGE and pos0 == 0)
        assert t_new % PAGE == 0 or t_new % PAGE < NSA_BLOCK
        self.row0, self.n_seq, self.t_new, self.tq, self.pos0, self.n_pages = row0, n_seq, t_new, tq, pos0, n_pages
        self.nq = t_new // tq
        self.rows = n_seq * t_new
        self.lk = (n_pages + -(-t_new // PAGE)) * PAGE
        self.n_keys = n_pages * PAGE + t_new

    def grid(self):
        return (self.n_seq // self.seqs, self.nq)

    def q_spec(self, width, col):
        rows = self.seqs * self.tq
        return pl.BlockSpec((rows, width), lambda b, qi, pt: (self.row0 // rows + b * self.nq + qi, col // width))

    def seq_spec(self, width, col):
        rows = self.seqs * self.t_new
        return pl.BlockSpec((rows, width), lambda b, qi, pt: (self.row0 // rows + b, col // width))

    def page_specs(self, shape):
        return [pl.BlockSpec((1,) + shape, lambda b, qi, pt, sq=sq, p=p: (pt[b * self.seqs + sq, p], 0, 0))
                for sq in range(self.seqs) for p in range(self.n_pages)]

    def page_args(self, pool):
        return [pool] * (self.seqs * self.n_pages)

    def out_spec(self, width):
        return pl.BlockSpec((self.seqs * self.tq, width), lambda b, qi, pt: (b * self.nq + qi, 0))

    def statics(self):
        return dict(seqs=self.seqs, n_pages=self.n_pages, t_new=self.t_new, tq=self.tq, pos0=self.pos0)

    def widths(self):
        if self.nq == 1:
            return (self.lk,)
        step = 4 * PAGE
        assert self.lk % step == 0
        return tuple(range(step, self.lk + 1, step))


def _cmp_bias_table(rel_bias, grp):
    lane0 = 0 if grp.nq == 1 else CMP_BIAS_LANE0
    tab = rel_bias[_BUCKETS]
    pieces, n_far = [], 0
    for lane in range(LANES + 1):
        d0 = grp.pos0 - (NSA_BLOCK * (lane - lane0 + 1) - 1)
        plain = lane < LANES and (d0 >= PAGE or d0 + grp.tq - 1 < 0)
        if plain:
            n_far += 1
            continue
        if n_far:
            pieces.append(jnp.broadcast_to(rel_bias[N_BUCKETS - 1], (grp.tq, n_far, rel_bias.shape[1])))
            n_far = 0
        if lane < LANES:
            pieces.append(tab[np.clip(d0 + np.arange(grp.tq), 0, BAND - 1)][:, None, :])
    table = jnp.transpose(jnp.concatenate(pieces, 1), (2, 0, 1))
    return table.reshape(-1, GROUP_HEADS * grp.tq, LANES)


def nsa_compress(grp, z, page_table, pool, wexp, bias):
    kv_w = KV_GROUPS * 2 * HEAD_DIM
    qw = KV_GROUPS * GROUP_HEADS * HEAD_DIM
    const2 = lambda b, qi, pt: (0, 0)
    return pl.pallas_call(
        functools.partial(_cmp_body, **grp.statics()),
        grid_spec=pltpu.PrefetchScalarGridSpec(
            num_scalar_prefetch=1,
            grid=grp.grid(),
            in_specs=[grp.q_spec(qw, COL_QN), grp.seq_spec(kv_w, COL_KVC)] + grp.page_specs(KV_PAGE)
            + [pl.BlockSpec((PAGE, kv_w), const2), pl.BlockSpec(bias.shape, lambda b, qi, pt: (0, 0, 0))],
            out_specs=[grp.out_spec(qw), grp.out_spec(KV_GROUPS * LANES)],
            scratch_shapes=[pltpu.VMEM((LANES, kv_w), F32),
                            pltpu.VMEM((KV_GROUPS, LANES, HEAD_DIM), BF16),
                            pltpu.VMEM((KV_GROUPS, LANES, HEAD_DIM), BF16)]),
        out_shape=[jax.ShapeDtypeStruct((grp.rows, qw), F32),
                   jax.ShapeDtypeStruct((grp.rows, KV_GROUPS * LANES), F32)],
        compiler_params=_cparams("parallel", "arbitrary"),
        name="nsa_compress",
    )(page_table, z, z, *grp.page_args(pool), wexp, bias)


def sparse_attention(mode, grp, z, page_table, pool, band, *, q_col, kv_col, mask=None):
    kv_w = KV_GROUPS * 2 * HEAD_DIM
    qw = KV_GROUPS * GROUP_HEADS * HEAD_DIM
    in_specs = [grp.q_spec(qw, q_col), grp.seq_spec(kv_w, kv_col)] + grp.page_specs(KV_PAGE)
    args = [z, z] + grp.page_args(pool)
    if mode in ("sel", "mask"):
        in_specs.append(grp.out_spec(mask.shape[1]))
        args.append(mask)
    in_specs.append(pl.BlockSpec((KV_GROUPS, GROUP_HEADS * grp.tq, BAND), lambda b, qi, pt: (0, 0, 0)))
    args.append(band)
    widths = grp.widths()
    s_cols = max(widths) if (mode != "win" or grp.nq == 1) else NSA_WINDOW + PAGE
    rows = GROUP_HEADS * grp.tq
    scratch = [pltpu.VMEM((KV_GROUPS, grp.lk, HEAD_DIM), BF16),
               pltpu.VMEM((KV_GROUPS, grp.lk, HEAD_DIM), BF16),
               pltpu.VMEM((KV_GROUPS, rows, s_cols), F32),
               pltpu.VMEM((KV_GROUPS, grp.tq, s_cols), F32)]
    if grp.tq % BF16_SUBLANES == 0:
        scratch.append(pltpu.VMEM((KV_GROUPS, rows, s_cols), BF16))
    return pl.pallas_call(
        functools.partial(_attn_body, mode=mode, widths=widths, **grp.statics()),
        grid_spec=pltpu.PrefetchScalarGridSpec(
            num_scalar_prefetch=1,
            grid=grp.grid(),
            in_specs=in_specs,
            out_specs=grp.out_spec(qw),
            scratch_shapes=scratch),
        out_shape=jax.ShapeDtypeStruct((grp.rows, qw), F32),
        compiler_params=_cparams("parallel", "arbitrary"),
        name="sparse_attention_" + mode,
    )(page_table, *args)


def dsa_index_scores(grp, z, page_table, idx_pool):
    return pl.pallas_call(
        functools.partial(_index_body, widths=grp.widths(), **grp.statics()),
        grid_spec=pltpu.PrefetchScalarGridSpec(
            num_scalar_prefetch=1,
            grid=grp.grid(),
            in_specs=[grp.q_spec(IDX_HEADS * IDX_DIM, COL_QI), grp.q_spec(LANES, COL_MISC),
                      grp.seq_spec(LANES, COL_MISC)] + grp.page_specs((IDX_DIM, PAGE)),
            out_specs=grp.out_spec(grp.lk),
            scratch_shapes=[pltpu.VMEM((IDX_DIM, grp.lk), BF16)]),
        out_shape=jax.ShapeDtypeStruct((grp.rows, grp.lk), F32),
        compiler_params=_cparams("parallel", "arbitrary"),
        name="dsa_index_scores",
    )(page_table, z, z, z, *grp.page_args(idx_pool))


def topk_mask(grp, scores, k):
    rows, lk = scores.shape
    tr = PAGE
    assert rows % tr == 0 and (grp.nq == 1 or grp.tq == tr)
    blk = pl.BlockSpec((tr, lk), lambda i: (i, 0))
    return pl.pallas_call(
        functools.partial(_topk_body, k=k, nq=grp.nq, tr=tr, widths=grp.widths()),
        grid=(rows // tr,),
        in_specs=[blk],
        out_specs=blk,
        out_shape=jax.ShapeDtypeStruct((rows, lk), F32),
        scratch_shapes=[pltpu.VMEM((tr, lk), jnp.int32)],
        compiler_params=_cparams("parallel"),
        name="topk_mask",
    )(scores)


def _combine_body(oc_ref, os_ref, ow_ref, od_ref, misc_ref, y_ref):
    n_heads = KV_GROUPS * GROUP_HEADS
    gates = jax.nn.sigmoid(misc_ref[:, MISC_GATES:MISC_GATES + 3 * n_heads])
    for h in range(n_heads):
        hs = slice(h * HEAD_DIM, (h + 1) * HEAD_DIM)
        o = (gates[:, 3 * h:3 * h + 1] * oc_ref[:, hs] + gates[:, 3 * h + 1:3 * h + 2] * os_ref[:, hs]
             + gates[:, 3 * h + 2:3 * h + 3] * ow_ref[:, hs])
        y_ref[:, hs] = o.astype(y_ref.dtype)
    y_ref[:, n_heads * HEAD_DIM:] = od_ref[...].astype(y_ref.dtype)


def nsa_dsa_combine(o_c, o_s, o_w, o_d, z, row0, *, tm):
    m, w = o_c.shape
    assert m % tm == 0 and row0 % tm == 0
    blk = pl.BlockSpec((tm, w), lambda i: (i, 0))
    return pl.pallas_call(
        _combine_body,
        grid=(m // tm,),
        in_specs=[blk, blk, blk, blk, pl.BlockSpec((tm, LANES), lambda i: (row0 // tm + i, COL_MISC // LANES))],
        out_specs=pl.BlockSpec((tm, 2 * w), lambda i: (i, 0)),
        out_shape=jax.ShapeDtypeStruct((m, 2 * w), BF16),
        compiler_params=_cparams("parallel"),
        name="nsa_dsa_combine",
    )(o_c, o_s, o_w, o_d, z)


def _band_tiles(rel_bias, tq):
    delta = (rel_bias[_BUCKETS] - rel_bias[N_BUCKETS - 1]).T
    rev = jnp.concatenate([delta[:, ::-1], jnp.zeros((delta.shape[0], PAGE), delta.dtype)], 1)
    tiles = jnp.stack([rev[:, PAGE - 1 - i:PAGE - 1 - i + BAND] for i in range(tq)], 1)
    return tiles.reshape(-1, GROUP_HEADS * tq, BAND) * LOG2E


def _widen_cd_w_in(w):
    sizes = (1024, 512, 512, 512, 24, 1024, 512, 512, 64, 8)
    q_n, kv_c, kv_s, kv_w, gates, q_d, kv_d, q_i, k_i, w_i = jnp.split(w, np.cumsum(sizes)[:-1].tolist(), axis=-1)
    cols = [q_n, q_d, kv_c, kv_s, kv_w, kv_d, q_i, k_i, gates, w_i]
    used = sum(c.shape[-1] for c in cols)
    return jnp.concatenate(cols + [jnp.zeros(w.shape[:-1] + (NZ - used,), w.dtype)], axis=-1)


def kernel(x_prompt, x_sample, state_conv, state_pool, cache_nsa_cmp, cache_nsa_sel, cache_nsa_win, cache_dsa_kv, cache_dsa_idx, page_table, norm_mix, norm_ffn, norm_final, ab_w_in, ab_conv_w, ab_conv_b, ab_ln_g, ab_ln_b, ab_pool_w, ab_pool_scale, ab_w_out, cd_w_in, cd_w_cmp, cd_w_out, rel_bias, ffn_w1, ffn_w2):
    bp, tp, d_model = x_prompt.shape
    bs, ts, _ = x_sample.shape
    mp, ms = bp * tp, bs * ts
    depth = norm_mix.shape[0]
    n_pages = page_table.shape[1]
    n_pool = cache_nsa_cmp.shape[1]
    past_len = n_pages * PAGE
    assert cache_nsa_cmp.shape[2] == PAGE
    win_len = cache_nsa_win.shape[2]
    assert win_len % PAGE == 0 and win_len == NSA_WINDOW and tp >= NSA_WINDOW
    kv_w = KV_GROUPS * 2 * HEAD_DIM

    xs = [x_prompt.reshape(mp, d_model), x_sample.reshape(ms, d_model)]
    grp_p = _Group(0, bp, tp, PAGE, 0, 0)
    grp_s = _Group(mp, bs, ts, ts, past_len, n_pages)
    grp_sc = _Group(mp, bs, ts, ts, past_len, n_pages, SAMPLE_SEQS_PER_STEP)
    grp_sw = _Group(mp, bs, ts, ts, past_len, win_len // PAGE, SAMPLE_SEQS_PER_STEP)
    no_pages = jnp.zeros((1, 1), jnp.int32)
    win_pages = jnp.arange(bs * (win_len // PAGE), dtype=jnp.int32).reshape(bs, win_len // PAGE)

    outs = {k: [] for k in ("conv_p", "conv_s", "pool_p", "pool_s", "cmp_p", "cmp_s", "sel_p", "sel_s",
                            "win_p", "win_s", "dsa_p", "dsa_s", "idx_p", "idx_s")}
    w_bf16 = {"ab_w_in": ab_w_in, "ab_w_out": ab_w_out, "cd_w_in": _widen_cd_w_in(cd_w_in), "cd_w_out": cd_w_out,
              "ffn_w1": ffn_w1, "ffn_w2": ffn_w2}
    w_bf16 = {k: v.astype(BF16) for k, v in w_bf16.items()}
    y_p = y_s = None
    for i in range(depth):
        j = i // 2
        if i % 2 == 0:
            d_conv = ab_conv_w.shape[2]
            z = norm_matmul(xs, norm_mix[i], w_bf16["ab_w_in"], j)
            mid_p, u_p = ab_mid_prompt(z, bp, tp, ab_conv_w[j], ab_conv_b[j], ab_ln_g[j], ab_ln_b[j],
                                       ab_pool_w[j], ab_pool_scale[j])
            mid_s, conv_s, pool_s = ab_mid_step(z, mp, bs, ts, past_len, state_conv[j], state_pool[j], ab_conv_w[j],
                                                ab_conv_b[j], ab_ln_g[j], ab_ln_b[j], ab_pool_w[j], ab_pool_scale[j])
            xs = [matmul_residual([mid_p, mid_s], w_bf16["ab_w_out"], j, xs)]
            outs["conv_p"].append(u_p.reshape(bp, tp, d_conv)[:, tp - CONV_BUF:])
            outs["conv_s"].append(conv_s)
            outs["pool_p"].append(jnp.stack([z[(b + 1) * tp - POOL_BUF:(b + 1) * tp, 2 * d_conv:] for b in range(bp)]))
            outs["pool_s"].append(pool_s)
        else:
            z = norm_matmul(xs, norm_mix[i], w_bf16["cd_w_in"], j)
            nsa_bias = rel_bias[:, :KV_GROUPS * GROUP_HEADS]
            band_p, band_s = _band_tiles(rel_bias, grp_p.tq), _band_tiles(rel_bias, grp_s.tq)
            wexp = jnp.tile(jnp.repeat(jnp.transpose(cd_w_cmp[j], (1, 0, 2)).reshape(NSA_BLOCK, 2 * KV_GROUPS),
                                       HEAD_DIM, axis=1), (PAGE // NSA_BLOCK, 1))
            pt = page_table + j * n_pool
            pools = [c.reshape((-1,) + KV_PAGE) for c in (cache_nsa_cmp, cache_nsa_sel, cache_dsa_kv)]
            idx_pool = jnp.swapaxes(cache_dsa_idx, 2, 3).reshape(-1, IDX_DIM, PAGE)
            win_pool = cache_nsa_win.reshape((-1,) + KV_PAGE)
            wpt = win_pages + j * bs * (win_len // PAGE)
            mids = []
            for grp, gc, gw, ptab, wtab, band in ((grp_p, grp_p, grp_p, no_pages, no_pages, band_p),
                                                  (grp_s, grp_sc, grp_sw, pt, wpt, band_s)):
                o_c, msel = nsa_compress(gc, z, ptab, pools[0], wexp, _cmp_bias_table(nsa_bias, gc))
                o_s = sparse_attention("sel", grp, z, ptab, pools[1], band[:KV_GROUPS],
                                       q_col=COL_QN, kv_col=COL_KVS, mask=msel)
                o_w = sparse_attention("win", gw, z, wtab, win_pool, band[:KV_GROUPS],
                                       q_col=COL_QN, kv_col=COL_KVW)
                top = topk_mask(grp, dsa_index_scores(grp, z, ptab, idx_pool), min(DSA_TOPK, grp.n_keys // 4))
                o_d = sparse_attention("mask", grp, z, ptab, pools[2], band[KV_GROUPS:],
                                       q_col=COL_QD, kv_col=COL_KVD, mask=top)
                mids.append(nsa_dsa_combine(o_c, o_s, o_w, o_d, z, grp.row0, tm=min(512, grp.rows)))
            xs = [matmul_residual(mids, w_bf16["cd_w_out"], j, xs)]

            def kv_out(col, width, tail):
                seg = z[:, col:col + width]
                return seg[:mp].reshape((bp, tp) + tail), seg[mp:].reshape((bs, ts) + tail)

            kv_tail = (2, KV_GROUPS, HEAD_DIM)
            for name, col in (("cmp", COL_KVC), ("sel", COL_KVS), ("dsa", COL_KVD)):
                p_new, s_new = kv_out(col, kv_w, kv_tail)
                outs[name + "_p"].append(p_new)
                outs[name + "_s"].append(s_new)
            w_p, w_s = kv_out(COL_KVW, kv_w, kv_tail)
            outs["win_p"].append(w_p[:, tp - NSA_WINDOW:])
            outs["win_s"].append(jnp.concatenate([cache_nsa_win[j], w_s], 1)[:, ts:])
            i_p, i_s = kv_out(COL_MISC + MISC_KI, IDX_DIM, (IDX_DIM,))
            outs["idx_p"].append(i_p)
            outs["idx_s"].append(i_s)
        a = norm_matmul(xs, norm_ffn[i], w_bf16["ffn_w1"], i, relu2=True, out_dtype=BF16)
        if i == depth - 1:
            y_p, y_s = matmul_residual([a], w_bf16["ffn_w2"], i, xs, norm_final, split_out=(mp, ms))
        else:
            xs = [matmul_residual([a], w_bf16["ffn_w2"], i, xs)]

    st = {k: jnp.stack(v) for k, v in outs.items()}
    return (y_p.reshape(bp, tp, d_model), y_s.reshape(bs, ts, d_model),
            st["conv_p"], st["conv_s"], st["pool_p"], st["pool_s"], st["cmp_p"], st["cmp_s"],
            st["sel_p"], st["sel_s"], st["win_p"], st["win_s"], st["dsa_p"], st["dsa_s"],
            st["idx_p"], st["idx_s"])
```

```python
import functools
import math

import numpy as np
import jax
import jax.numpy as jnp
from jax import lax
from jax.experimental import pallas as pl
from jax.experimental.pallas import tpu as pltpu

F32 = jnp.float32
BF16 = jnp.bfloat16

EPS = 1e-6
NEG_INF = -1e30
LOG2E = math.log2(math.e)
HEAD_DIM = 128
LANES = 128
BF16_SUBLANES = 16
CONV_WIDTH = 31
CONV_BUF = CONV_WIDTH - 1
POOL_WINDOWS = (2, 4, 8, 16)
POOL_BUF = max(POOL_WINDOWS) - 1
HALO = 32
VMEM_LIMIT = 56 * 1024 * 1024


def _cparams(*sem):
    return pltpu.CompilerParams(dimension_semantics=sem, vmem_limit_bytes=VMEM_LIMIT)


class _Rows:
    def __init__(self, arrays, tm):
        self.arrays = list(arrays)
        self.tm = tm
        assert all(a.shape[0] % tm == 0 for a in self.arrays)
        self.tiles = [a.shape[0] // tm for a in self.arrays]
        self.n_tiles = sum(self.tiles)
        self.n = len(self.arrays)

    def specs(self, width, col):
        out, t0 = [], 0
        for nt in self.tiles:
            out.append(pl.BlockSpec((self.tm, width), lambda i, j, t0=t0, nt=nt: (jnp.clip(i - t0, 0, nt - 1), col(j))))
            t0 += nt
        return out

    def select(self, i, refs, fn):
        if self.n == 1:
            fn(refs[0])
            return
        t0 = 0
        for nt, ref in zip(self.tiles, refs):
            pl.when((i >= t0) & (i < t0 + nt))(functools.partial(fn, ref))
            t0 += nt


def _rmsnorm_rows(x, g):
    return (x * lax.rsqrt(jnp.mean(x * x, -1, keepdims=True) + EPS)) * g


def _norm_matmul_body(*refs, rows, relu2):
    x_refs = refs[:rows.n]
    g_ref, w_ref, o_ref, h_ref = refs[rows.n:]

    @pl.when(pl.program_id(1) == 0)
    def _():
        def norm(x_ref):
            h_ref[...] = _rmsnorm_rows(x_ref[...], g_ref[...]).astype(BF16)

        rows.select(pl.program_id(0), x_refs, norm)

    y = jnp.dot(h_ref[...], w_ref[...], preferred_element_type=F32)
    if relu2:
        y = jnp.square(jnp.maximum(y, 0.0))
    o_ref[...] = y.astype(o_ref.dtype)


def norm_matmul(xs, g, w, layer, *, relu2=False, out_dtype=F32, tm=1024, tn=1024):
    rows = _Rows(xs, tm)
    _, d, n = w.shape
    assert n % tn == 0
    return pl.pallas_call(
        functools.partial(_norm_matmul_body, rows=rows, relu2=relu2),
        grid=(rows.n_tiles, n // tn),
        in_specs=rows.specs(d, lambda j: 0) + [pl.BlockSpec((1, d), lambda i, j: (0, 0)),
                                               pl.BlockSpec((None, d, tn), lambda i, j: (layer, 0, j))],
        out_specs=pl.BlockSpec((tm, tn), lambda i, j: (i, j)),
        out_shape=jax.ShapeDtypeStruct((rows.n_tiles * tm, n), out_dtype),
        scratch_shapes=[pltpu.VMEM((tm, d), BF16)],
        compiler_params=_cparams("parallel", "arbitrary"),
        name="norm_matmul",
    )(*rows.arrays, g.reshape(1, d), w)


def _matmul_residual_body(*refs, a_rows, r_rows, o_rows, final_norm):
    a_refs, refs = refs[:a_rows.n], refs[a_rows.n:]
    w_ref, refs = refs[0], refs[1:]
    r_refs, refs = refs[:r_rows.n], refs[r_rows.n:]
    if final_norm:
        g_ref, refs = refs[0], refs[1:]
    o_refs, acc_ref = refs[:o_rows.n], refs[o_rows.n]
    i, k = pl.program_id(0), pl.program_id(1)

    @pl.when(k == 0)
    def _():
        acc_ref[...] = jnp.zeros_like(acc_ref)

    def accumulate(a_ref):
        acc_ref[...] += jnp.dot(a_ref[...], w_ref[...], preferred_element_type=F32)

    a_rows.select(i, a_refs, accumulate)

    @pl.when(k == pl.num_programs(1) - 1)
    def _():
        def add_residual(r_ref):
            acc_ref[...] += r_ref[...]

        def write(o_ref):
            o = acc_ref[...]
            o_ref[...] = _rmsnorm_rows(o, g_ref[...]) if final_norm else o

        r_rows.select(i, r_refs, add_residual)
        o_rows.select(i, o_refs, write)


def matmul_residual(a_list, w, layer, r_list, g_final=None, *, split_out=None, tm=512, tk=2048):
    a_rows, r_rows = _Rows(a_list, tm), _Rows(r_list, tm)
    _, kdim, n = w.shape
    m = a_rows.n_tiles * tm
    assert kdim % tk == 0 and r_rows.n_tiles == a_rows.n_tiles
    final_norm = g_final is not None
    o_rows = _Rows([jax.ShapeDtypeStruct((r, n), F32) for r in (split_out or (m,))], tm)
    assert o_rows.n_tiles == a_rows.n_tiles
    in_specs = (a_rows.specs(tk, lambda k: k) + [pl.BlockSpec((None, tk, n), lambda i, k: (layer, k, 0))]
                + r_rows.specs(n, lambda k: 0))
    args = a_rows.arrays + [w] + r_rows.arrays
    if final_norm:
        in_specs.append(pl.BlockSpec((1, n), lambda i, k: (0, 0)))
        args.append(g_final.reshape(1, n))
    out = pl.pallas_call(
        functools.partial(_matmul_residual_body, a_rows=a_rows, r_rows=r_rows, o_rows=o_rows, final_norm=final_norm),
        grid=(a_rows.n_tiles, kdim // tk),
        in_specs=in_specs,
        out_specs=o_rows.specs(n, lambda k: 0),
        out_shape=o_rows.arrays,
        scratch_shapes=[pltpu.VMEM((tm, n), F32)],
        compiler_params=_cparams("parallel", "arbitrary"),
        name="matmul_residual",
    )(*args)
    return out if split_out else out[0]


def _layernorm_silu(c, g, b):
    mu = jnp.mean(c, -1, keepdims=True)
    xc = c - mu
    y = xc * lax.rsqrt(jnp.mean(xc * xc, -1, keepdims=True) + EPS)
    y = y * g + b
    return y * jax.nn.sigmoid(y)


def _ab_mid_body(z_ref, zp_ref, cw_ref, cb_ref, lg_ref, lb_ref, pw_ref, ps_ref, y_ref, u_ref,
                 ext_ref, vext_ref, conv_ref, *, tt, d_conv, d_pool):
    ti = pl.program_id(1)
    keep = (ti > 0).astype(F32)
    a_p = zp_ref[:, 0:d_conv]
    g_p = zp_ref[:, d_conv:2 * d_conv]
    ext_ref[0:HALO, :] = a_p * jax.nn.sigmoid(g_p) * keep
    vext_ref[0:HALO, :] = zp_ref[:, 2 * d_conv:] * keep
    u = z_ref[:, 0:d_conv] * jax.nn.sigmoid(z_ref[:, d_conv:2 * d_conv])
    ext_ref[HALO:, :] = u
    u_ref[...] = u
    vext_ref[HALO:, :] = z_ref[:, 2 * d_conv:]

    off = HALO - CONV_BUF
    sub = 8
    for c in range(d_conv // LANES):
        cs = slice(c * LANES, (c + 1) * LANES)
        acc = jnp.zeros((tt, LANES), F32)
        for s in range(sub):
            n = tt if s == 0 else tt + sub
            part = jnp.zeros((n, LANES), F32)
            for j in range(CONV_WIDTH):
                if (off + j) % sub == s:
                    start = off + j - s
                    part = part + cw_ref[j:j + 1, cs] * ext_ref[start:start + n, cs]
            acc = acc + part[s:s + tt]
        conv_ref[:, cs] = acc + cb_ref[:, cs]
    y_ref[:, 0:d_conv] = _layernorm_silu(conv_ref[...], lg_ref[...], lb_ref[...]).astype(y_ref.dtype)

    pos = ti * tt + lax.broadcasted_iota(jnp.int32, (tt, 1), 0)
    pg = d_pool // len(POOL_WINDOWS)
    for gi, w in enumerate(POOL_WINDOWS):
        gs = slice(gi * pg, (gi + 1) * pg)
        tok = vext_ref[HALO:, gs]
        acc = tok
        for i in range(1, w):
            acc = acc + vext_ref[HALO - i:HALO - i + tt, gs]
        cnt = jnp.minimum(pos + 1, w).astype(F32)
        d = acc / cnt - tok
        yp = jnp.dot(d.astype(BF16), pw_ref[gi], preferred_element_type=F32) * ps_ref[:, gs]
        y_ref[:, d_conv + gi * pg:d_conv + (gi + 1) * pg] = yp.astype(y_ref.dtype)


def ab_mid_prompt(z, n_seq, t_len, conv_w, conv_b, ln_g, ln_b, pool_w, pool_scale, *, tt=256):
    d_conv = conv_w.shape[1]
    d_pool = pool_scale.shape[0]
    nt = t_len // tt
    hb = tt // HALO
    row = lambda b, t: (b * nt + t, 0)
    const = lambda b, t: (0, 0)
    return pl.pallas_call(
        functools.partial(_ab_mid_body, tt=tt, d_conv=d_conv, d_pool=d_pool),
        grid=(n_seq, nt),
        in_specs=[pl.BlockSpec((tt, z.shape[1]), row),
                  pl.BlockSpec((HALO, z.shape[1]), lambda b, t: (jnp.maximum((b * nt + t) * hb - 1, 0), 0)),
                  pl.BlockSpec(conv_w.shape, const),
                  pl.BlockSpec((1, d_conv), const),
                  pl.BlockSpec((1, d_conv), const),
                  pl.BlockSpec((1, d_conv), const),
                  pl.BlockSpec(pool_w.shape, lambda b, t: (0, 0, 0)),
                  pl.BlockSpec((1, d_pool), const)],
        out_specs=[pl.BlockSpec((tt, d_conv + d_pool), row),
                   pl.BlockSpec((tt, d_conv), row)],
        out_shape=[jax.ShapeDtypeStruct((n_seq * t_len, d_conv + d_pool), BF16),
                   jax.ShapeDtypeStruct((n_seq * t_len, d_conv), F32)],
        scratch_shapes=[pltpu.VMEM((HALO + tt, d_conv), F32),
                        pltpu.VMEM((HALO + tt, d_pool), F32),
                        pltpu.VMEM((tt, d_conv), F32)],
        compiler_params=_cparams("parallel", "parallel"),
        name="ab_mid_prompt",
    )(z, z, conv_w, conv_b.reshape(1, -1), ln_g.reshape(1, -1), ln_b.reshape(1, -1),
      pool_w.astype(BF16), pool_scale.reshape(1, -1))


def _ab_mid_step_body(z_ref, sc_ref, sp_ref, cw_ref, cb_ref, lg_ref, lb_ref, pw_ref, ps_ref,
                      y_ref, nc_ref, np_ref, ext_ref, vext_ref, *, nb, t, pos0, d_conv, d_pool):
    e0 = HALO - CONV_BUF
    p0 = 16 - POOL_BUF
    z = z_ref[...].reshape(nb, t, z_ref.shape[1])
    u = z[:, :, 0:d_conv] * jax.nn.sigmoid(z[:, :, d_conv:2 * d_conv])
    ext_ref[:, e0:HALO, :] = sc_ref[...]
    ext_ref[:, HALO:, :] = u
    vext_ref[:, p0:16, :] = sp_ref[...]
    vext_ref[:, 16:, :] = z[:, :, 2 * d_conv:]
    nc_ref[...] = ext_ref[:, HALO + t - CONV_BUF:, :]
    np_ref[...] = vext_ref[:, 16 + t - POOL_BUF:, :]

    acc = jnp.zeros((nb, t, d_conv), F32)
    for j in range(CONV_WIDTH):
        acc = acc + cw_ref[j:j + 1, :][None] * ext_ref[:, e0 + j:e0 + j + t, :]
    c = acc + cb_ref[...][None]
    yc = _layernorm_silu(c, lg_ref[...][None], lb_ref[...][None])
    y_ref[:, 0:d_conv] = yc.reshape(nb * t, d_conv).astype(y_ref.dtype)

    pg = d_pool // len(POOL_WINDOWS)
    for gi, w in enumerate(POOL_WINDOWS):
        gs = slice(gi * pg, (gi + 1) * pg)
        tok = vext_ref[:, 16:, gs]
        acc = tok
        for i in range(1, w):
            acc = acc + vext_ref[:, 16 - i:16 - i + t, gs]
        cnt = jnp.minimum(pos0 + 1 + lax.broadcasted_iota(jnp.int32, (1, t, 1), 1), w).astype(F32)
        d = (acc / cnt - tok).reshape(nb * t, pg)
        yp = jnp.dot(d.astype(BF16), pw_ref[gi], preferred_element_type=F32) * ps_ref[:, gs]
        y_ref[:, d_conv + gi * pg:d_conv + (gi + 1) * pg] = yp.astype(y_ref.dtype)


def ab_mid_step(z, row0, n_seq, t, pos0, state_conv, state_pool, conv_w, conv_b, ln_g, ln_b, pool_w,
                pool_scale, *, nb=16):
    d_conv = conv_w.shape[1]
    d_pool = pool_scale.shape[0]
    rb = nb * t
    assert row0 % rb == 0 and n_seq % nb == 0
    const = lambda i: (0, 0)
    seq3 = lambda i: (i, 0, 0)
    return pl.pallas_call(
        functools.partial(_ab_mid_step_body, nb=nb, t=t, pos0=pos0, d_conv=d_conv, d_pool=d_pool),
        grid=(n_seq // nb,),
        in_specs=[pl.BlockSpec((rb, z.shape[1]), lambda i: (row0 // rb + i, 0)),
                  pl.BlockSpec((nb, CONV_BUF, d_conv), seq3),
                  pl.BlockSpec((nb, POOL_BUF, d_pool), seq3),
                  pl.BlockSpec(conv_w.shape, const),
                  pl.BlockSpec((1, d_conv), const),
                  pl.BlockSpec((1, d_conv), const),
                  pl.BlockSpec((1, d_conv), const),
                  pl.BlockSpec(pool_w.shape, lambda i: (0, 0, 0)),
                  pl.BlockSpec((1, d_pool), const)],
        out_specs=[pl.BlockSpec((rb, d_conv + d_pool), lambda i: (i, 0)),
                   pl.BlockSpec((nb, CONV_BUF, d_conv), seq3),
                   pl.BlockSpec((nb, POOL_BUF, d_pool), seq3)],
        out_shape=[jax.ShapeDtypeStruct((n_seq * t, d_conv + d_pool), BF16),
                   jax.ShapeDtypeStruct((n_seq, CONV_BUF, d_conv), F32),
                   jax.ShapeDtypeStruct((n_seq, POOL_BUF, d_pool), F32)],
        scratch_shapes=[pltpu.VMEM((nb, HALO + t, d_conv), F32),
                        pltpu.VMEM((nb, 16 + t, d_pool), F32)],
        compiler_params=_cparams("parallel"),
        name="ab_mid_step",
    )(z, state_conv, state_pool, conv_w, conv_b.reshape(1, -1), ln_g.reshape(1, -1),
      ln_b.reshape(1, -1), pool_w.astype(BF16), pool_scale.reshape(1, -1))


N_BUCKETS = 32
MAX_DISTANCE = 128
NSA_BLOCK = 64
NSA_TOPN = 16
NSA_WINDOW = 512
DSA_TOPK = 256
IDX_HEADS = 8
IDX_DIM = 64
KV_GROUPS = 2
GROUP_HEADS = 4
PAGE = 128
BAND = 2 * PAGE
INT_MIN = -2 ** 31
KV_PAGE = (2 * KV_GROUPS * PAGE, HEAD_DIM)
TOPK_ROW_GROUPS = 4
SAMPLE_SEQS_PER_STEP = 2
CMP_BIAS_LANE0 = 64

COL_QN, COL_QD, COL_KVC, COL_KVS, COL_KVW, COL_KVD, COL_QI, COL_MISC = 0, 1024, 2048, 2560, 3072, 3584, 4096, 4608
MISC_KI, MISC_GATES, MISC_WI = 0, 64, 88
NZ = 5120


def _bucket_np(n):
    n = np.maximum(np.asarray(n, np.int32), 0)
    exact = N_BUCKETS // 2
    nf = np.maximum(n, 1).astype(np.float32)
    big = exact + (np.log(nf / np.float32(exact)) / np.float32(math.log(MAX_DISTANCE / exact))
                   * np.float32(N_BUCKETS - exact)).astype(np.int32)
    return np.where(n < exact, n, np.minimum(big, N_BUCKETS - 1))


_BUCKETS = _bucket_np(np.arange(BAND))
assert _BUCKETS[PAGE:].min() == N_BUCKETS - 1


def _softmax_rows(s, mask):
    s = jnp.where(mask, s, NEG_INF)
    m = jnp.max(s, -1, keepdims=True)
    p = jnp.where(mask, jnp.exp(s - m), 0.0)
    return p, jnp.sum(p, -1, keepdims=True)


def _dot_nt(a, b):
    return lax.dot_general(a, b, (((1,), (1,)), ((), ())), preferred_element_type=F32)


def _new_chunks(new_ref, t_new):
    chunks = [new_ref[c * PAGE:(c + 1) * PAGE, :] for c in range(t_new // PAGE)]
    rem = t_new % PAGE
    if rem:
        tail = new_ref[(t_new // PAGE) * PAGE:, :]
        chunks.append(jnp.concatenate([tail, jnp.zeros((PAGE - rem, tail.shape[1]), F32)], 0))
    return chunks


def _kv_chunks(page_refs, new_ref, t_new):
    n_parts = 2 * KV_GROUPS
    chunks = [[r[0, pl.ds(part, PAGE, stride=n_parts), :] for part in range(n_parts)] for r in page_refs]
    for x in _new_chunks(new_ref, t_new):
        chunks.append([x[:, part * HEAD_DIM:(part + 1) * HEAD_DIM] for part in range(n_parts)])
    return chunks


def _seq_view(ref, sq, seqs):
    n = ref.shape[0] // seqs
    return ref.at[pl.ds(sq * n, n)]


def _cmp_body(pt_ref, q_ref, kvn_ref, *rest, seqs, n_pages, **statics):
    del pt_ref
    pages, (wexp_ref, bias_ref, o_ref, msel_ref, *scratch) = rest[:seqs * n_pages], rest[seqs * n_pages:]
    for sq in range(seqs):
        _cmp_one(_seq_view(q_ref, sq, seqs), _seq_view(kvn_ref, sq, seqs), pages[sq * n_pages:(sq + 1) * n_pages],
                 wexp_ref, bias_ref, _seq_view(o_ref, sq, seqs), _seq_view(msel_ref, sq, seqs), *scratch,
                 n_pages=n_pages, **statics)


def _cmp_one(q_ref, kvn_ref, page_refs, wexp_ref, bias_ref, o_ref, msel_ref, comp_ref, ck_ref, cv_ref, *,
             n_pages, t_new, tq, pos0):
    qi = pl.program_id(1)
    n_keys = n_pages * PAGE + t_new
    n_cmp = n_keys // NSA_BLOCK
    n_sel = -(-n_keys // NSA_BLOCK)
    per = PAGE // NSA_BLOCK

    @pl.when(qi == 0)
    def _():
        comp_ref[...] = jnp.zeros_like(comp_ref)
        chunks = _kv_chunks(page_refs, kvn_ref, t_new)[:n_cmp // per]
        for part in range(2 * KV_GROUPS):
            cols = slice(part * HEAD_DIM, (part + 1) * HEAD_DIM)
            xw = jnp.concatenate([parts[part] * wexp_ref[:, cols] for parts in chunks], 0)
            comp_ref[0:per * len(chunks), cols] = xw.reshape(per * len(chunks), NSA_BLOCK, HEAD_DIM).sum(1)
        for g in range(KV_GROUPS):
            ck_ref[g] = comp_ref[:, g * HEAD_DIM:(g + 1) * HEAD_DIM].astype(BF16)
            cv_ref[g] = comp_ref[:, (KV_GROUPS + g) * HEAD_DIM:(KV_GROUPS + g + 1) * HEAD_DIM].astype(BF16)

    scale = HEAD_DIM ** -0.5
    rows = GROUP_HEADS * tq
    blk = lax.broadcasted_iota(jnp.int32, (1, LANES), 1)
    q0 = pos0 + qi * tq
    assert tq & (tq - 1) == 0
    qpos_st = q0 + (lax.broadcasted_iota(jnp.int32, (rows, 1), 0) & (tq - 1))
    mask = (qpos_st - ((blk + 1) * NSA_BLOCK - 1) >= 0) & (blk < n_cmp)
    cur = (q0 + lax.broadcasted_iota(jnp.int32, (tq, 1), 0)) // NSA_BLOCK
    scores = []
    for g in range(KV_GROUPS):
        heads = [g * GROUP_HEADS + r for r in range(GROUP_HEADS)]
        bias = bias_ref[g]
        if t_new != tq:
            bias = pltpu.roll(bias, qi * (tq // NSA_BLOCK) + (LANES - CMP_BIAS_LANE0), 1)
        q = jnp.concatenate([q_ref[:, h * HEAD_DIM:(h + 1) * HEAD_DIM] for h in heads], 0).astype(BF16)
        scores.append(_dot_nt(q, ck_ref[g]) * scale + bias)
    probs = []
    for g in range(KV_GROUPS):
        p, l = _softmax_rows(scores[g], mask)
        probs.append(p / jnp.maximum(l, 1e-30))
    for g in range(KV_GROUPS):
        o = jnp.dot(probs[g].astype(BF16), cv_ref[g], preferred_element_type=F32)
        for r in range(GROUP_HEADS):
            h = g * GROUP_HEADS + r
            o_ref[:, h * HEAD_DIM:(h + 1) * HEAD_DIM] = o[r * tq:(r + 1) * tq]
    for g in range(KV_GROUPS):
        imp = probs[g][0:tq]
        for r in range(1, GROUP_HEADS):
            imp = imp + probs[g][r * tq:(r + 1) * tq]
        imp = jnp.where(blk == cur, 2.0, jnp.where(blk > cur, -1.0, imp))
        imp = jnp.where(blk < n_sel, imp, -2.0)
        n_top = min(NSA_TOPN, n_sel)
        cols = slice(g * LANES, (g + 1) * LANES)

        def by_rank(imp=imp, cols=cols):
            rank = jnp.zeros((tq, LANES), F32)
            for i in range(n_sel):
                col = imp[:, i:i + 1]
                ahead = (col > imp) | ((col == imp) & (blk > i))
                rank = rank + jnp.where(ahead, 1.0, 0.0)
            msel_ref[:, cols] = jnp.where((rank < float(n_top)) & (blk < n_sel), 1.0, 0.0)

        def first_blocks(cols=cols):
            msel_ref[:, cols] = jnp.where(blk < n_top, 1.0, 0.0) + jnp.zeros((tq, LANES), F32)

        if t_new == tq:
            if pos0 + tq <= n_top * NSA_BLOCK:
                first_blocks()
            else:
                by_rank()
        else:
            early = pos0 + (qi + 1) * tq <= n_top * NSA_BLOCK
            pl.when(early)(first_blocks)
            pl.when(jnp.logical_not(early))(by_rank)


def _on_causal_width(qi, tq, widths, tile):
    if len(widths) == 1:
        tile(widths[0], True)
        return
    need = (qi * tq + tq - 1) // widths[0]
    for nw, w in enumerate(widths):
        pl.when(need == nw)(functools.partial(tile, w, nw == 0))


def _attn_body(pt_ref, q_ref, kvn_ref, *rest, mode, seqs, n_pages, **statics):
    del pt_ref
    pages, rest = rest[:seqs * n_pages], rest[seqs * n_pages:]
    m_ref = None
    if mode in ("sel", "mask"):
        m_ref, rest = rest[0], rest[1:]
    band_ref, o_ref, *scratch = rest
    for sq in range(seqs):
        _attn_one(_seq_view(q_ref, sq, seqs), _seq_view(kvn_ref, sq, seqs), pages[sq * n_pages:(sq + 1) * n_pages],
                  None if m_ref is None else _seq_view(m_ref, sq, seqs), band_ref, _seq_view(o_ref, sq, seqs),
                  *scratch, mode=mode, n_pages=n_pages, **statics)


def _attn_one(q_ref, kvn_ref, page_refs, m_ref, band_ref, o_ref, kc_ref, vc_ref, s_ref, cap_ref, *maybe_p_ref,
              mode, n_pages, t_new, tq, pos0, widths):
    p_ref = maybe_p_ref[0] if maybe_p_ref else s_ref
    qi = pl.program_id(1)
    single = t_new == tq
    n_keys = n_pages * PAGE + t_new
    kbase = pos0 - n_pages * PAGE
    scale = HEAD_DIM ** -0.5
    q0 = pos0 if single else pos0 + qi * tq

    @pl.when(qi == 0)
    def _():
        for c, parts in enumerate(_kv_chunks(page_refs, kvn_ref, t_new)):
            rows = slice(c * PAGE, (c + 1) * PAGE)
            for g in range(KV_GROUPS):
                kc_ref[g, rows, :] = parts[g].astype(BF16)
                vc_ref[g, rows, :] = parts[KV_GROUPS + g].astype(BF16)

    def tile(c0, w, band_at, maybe_first):
        qpos = q0 + lax.broadcasted_iota(jnp.int32, (tq, 1), 0)
        col = c0 + lax.broadcasted_iota(jnp.int32, (1, w), 1)
        dist = qpos - (kbase + col)
        visible = (dist >= 0) & (col < n_keys)
        if mode == "win":
            visible = visible & (dist < NSA_WINDOW)
        if mode == "mask":
            visible = visible & (m_ref[:, 0:w] > 0.5)
        keys = pl.ds(c0, w)
        groups = range(KV_GROUPS)
        for g in groups:
            mask = visible
            if mode == "sel":
                expand = (lax.broadcasted_iota(jnp.int32, (LANES, w), 1) // NSA_BLOCK
                          == lax.broadcasted_iota(jnp.int32, (LANES, w), 0))
                chosen = jnp.dot(m_ref[:, g * LANES:(g + 1) * LANES].astype(BF16),
                                 jnp.where(expand, 1.0, 0.0).astype(BF16), preferred_element_type=F32)
                mask = visible & (chosen > 0.5)
            cap_ref[g, :, 0:w] = jnp.where(mask, jnp.inf, NEG_INF)
            heads = [g * GROUP_HEADS + r for r in range(GROUP_HEADS)]
            q = jnp.concatenate([q_ref[:, h * HEAD_DIM:(h + 1) * HEAD_DIM] for h in heads], 0).astype(BF16)
            s_ref[g, :, 0:w] = _dot_nt(q, kc_ref[g, keys, :]) * (scale * LOG2E)
        for g in groups:
            if band_at is not None:
                s_ref[g, :, band_at:band_at + BAND] += band_ref[g]
            else:
                if maybe_first:
                    @pl.when(qi == 0)
                    def _():
                        s_ref[g, :, 0:PAGE] += band_ref[g, :, PAGE:]

                @pl.when(qi > 0)
                def _():
                    s_ref[g, :, pl.ds(pl.multiple_of(q0 - PAGE - kbase, PAGE), BAND)] += band_ref[g]
        sums, alive = [], []
        for g in groups:
            for r in range(GROUP_HEADS):
                rows = slice(r * tq, (r + 1) * tq)
                m = jnp.max(jnp.minimum(s_ref[g, rows, 0:w], cap_ref[g, :, 0:w]), -1, keepdims=True)
                p = jnp.exp2(jnp.minimum(s_ref[g, rows, 0:w], cap_ref[g, :, 0:w]) - m)
                p_ref[g, rows, 0:w] = p.astype(p_ref.dtype)
                sums.append(jnp.sum(p, -1, keepdims=True))
                alive.append(m > NEG_INF)
        for g in groups:
            o = jnp.dot(p_ref[g, :, 0:w].astype(BF16), vc_ref[g, keys, :], preferred_element_type=F32)
            for r in range(GROUP_HEADS):
                h = g * GROUP_HEADS + r
                o_h = o[r * tq:(r + 1) * tq] / jnp.maximum(sums[h], 1e-30)
                o_ref[:, h * HEAD_DIM:(h + 1) * HEAD_DIM] = jnp.where(alive[h], o_h, 0.0)

    if single:
        tile(0, widths[0], pos0 - PAGE - kbase, False)
    elif mode == "win":
        wch = NSA_WINDOW // PAGE
        pl.when(qi < wch)(functools.partial(tile, 0, NSA_WINDOW, None, True))
        pl.when(qi >= wch)(lambda: tile(pl.multiple_of((qi - wch) * PAGE, PAGE), NSA_WINDOW + PAGE,
                                        NSA_WINDOW - PAGE, False))
    else:
        _on_causal_width(qi, tq, widths, lambda w, first: tile(0, w, None, first))


def _index_body(pt_ref, qidx_ref, miscq_ref, misck_ref, *rest, seqs, n_pages, **statics):
    del pt_ref
    pages, (o_ref, kidx_ref) = rest[:seqs * n_pages], rest[seqs * n_pages:]
    for sq in range(seqs):
        _index_one(_seq_view(qidx_ref, sq, seqs), _seq_view(miscq_ref, sq, seqs), _seq_view(misck_ref, sq, seqs),
                   pages[sq * n_pages:(sq + 1) * n_pages], _seq_view(o_ref, sq, seqs), kidx_ref,
                   n_pages=n_pages, **statics)


def _index_one(qidx_ref, miscq_ref, misck_ref, ipage_refs, o_ref, kidx_ref, *, n_pages, t_new, tq, pos0, widths):
    qi = pl.program_id(1)
    lk = o_ref.shape[1]
    n_keys = n_pages * PAGE + t_new
    kbase = pos0 - n_pages * PAGE
    q0 = pos0 if t_new == tq else pos0 + qi * tq

    @pl.when(qi == 0)
    def _():
        for c, r in enumerate(ipage_refs):
            kidx_ref[:, c * PAGE:(c + 1) * PAGE] = r[0].astype(BF16)
        for c, x in enumerate(_new_chunks(misck_ref, t_new)):
            cols = slice((n_pages + c) * PAGE, (n_pages + c + 1) * PAGE)
            kidx_ref[:, cols] = x.T[MISC_KI:MISC_KI + IDX_DIM, :].astype(BF16)

    def tile(w, maybe_first):
        del maybe_first
        qpos = q0 + lax.broadcasted_iota(jnp.int32, (tq, 1), 0)
        col = lax.broadcasted_iota(jnp.int32, (1, w), 1)
        visible = (qpos - (kbase + col) >= 0) & (col < n_keys)
        q = jnp.concatenate([qidx_ref[:, hh * IDX_DIM:(hh + 1) * IDX_DIM] for hh in range(IDX_HEADS)], 0)
        sc = jnp.dot(q.astype(BF16), kidx_ref[:, 0:w], preferred_element_type=F32)
        score = jnp.zeros((tq, w), F32)
        for hh in range(IDX_HEADS):
            wi = miscq_ref[:, MISC_WI + hh:MISC_WI + hh + 1] * (IDX_HEADS ** -0.5)
            score = score + jnp.maximum(sc[hh * tq:(hh + 1) * tq] * (IDX_DIM ** -0.5), 0.0) * wi
        o_ref[:, 0:w] = jnp.where(visible, score, NEG_INF)
        if w < lk:
            o_ref[:, w:] = jnp.full((tq, lk - w), NEG_INF, F32)

    _on_causal_width(qi, tq, widths, tile)


def _topk_body(s_ref, m_ref, key_ref, *, k, nq, tr, widths):
    lk = s_ref.shape[1]
    assert lk <= 4096
    neg_key = int(np.array(NEG_INF, np.float32).view(np.int32)) ^ 0x7FFFFFFF
    kf = float(k)

    def tile(w, maybe_first):
        del maybe_first
        bits = lax.bitcast_convert_type(s_ref[:, 0:w] + 0.0, jnp.int32)
        key_ref[:, 0:w] = jnp.where(bits >= 0, bits, bits ^ 0x7FFFFFFF)
        col = lax.broadcasted_iota(jnp.int32, (1, w), 1)
        unseen = float(lk - w)

        groups = [slice(a * (tr // TOPK_ROW_GROUPS), (a + 1) * (tr // TOPK_ROW_GROUPS)) for a in range(TOPK_ROW_GROUPS)]
        zeros = tuple(jnp.zeros((tr // TOPK_ROW_GROUPS, 1), jnp.int32) for _ in groups)

        def thr_step(i, tus):
            out = []
            for rows, tu in zip(groups, tus):
                cand = tu | jnp.left_shift(jnp.int32(1), 31 - i)
                cs = cand ^ INT_MIN
                cnt = jnp.sum(jnp.where(key_ref[rows, 0:w] >= cs, 1.0, 0.0), -1, keepdims=True)
                cnt = cnt + jnp.where(cs <= neg_key, unseen, 0.0)
                out.append(jnp.where(cnt >= kf, cand, tu))
            return tuple(out)

        thr = jnp.concatenate(lax.fori_loop(0, 32, thr_step, zeros, unroll=4), 0) ^ INT_MIN
        key = key_ref[:, 0:w]
        above = key > thr
        tied = key == thr
        need = kf - jnp.sum(jnp.where(above, 1.0, 0.0), -1, keepdims=True)

        def tie_step(i, j0s):
            out = []
            for rows, j0 in zip(groups, j0s):
                cand = j0 | jnp.left_shift(jnp.int32(1), 11 - i)
                hit = (key_ref[rows, 0:w] == thr[rows]) & (col < cand)
                cnt = jnp.sum(jnp.where(hit, 1.0, 0.0), -1, keepdims=True)
                out.append(jnp.where(cnt < need[rows], cand, j0))
            return tuple(out)

        m_ref[:, 0:w] = jnp.where(above | tied, 1.0, 0.0)
        n_tied = jnp.sum(jnp.where(tied, 1.0, 0.0), -1, keepdims=True)
        excess = jnp.max(jnp.where(thr > neg_key, n_tied - need, 0.0))

        @pl.when(excess > 0.0)
        def _():
            j0 = jnp.concatenate(lax.fori_loop(0, 12, tie_step, zeros, unroll=4), 0)
            k2 = key_ref[:, 0:w]
            m_ref[:, 0:w] = jnp.where((k2 > thr) | ((k2 == thr) & (col <= j0)), 1.0, 0.0)
        if w < lk:
            m_ref[:, w:] = jnp.zeros((tr, lk - w), F32)

    _on_causal_width(pl.program_id(0) % nq, tr, widths, tile)


class _Group:
    def __init__(self, row0, n_seq, t_new, tq, pos0, n_pages, seqs=1):
        assert t_new % tq == 0 and row0 % (seqs * tq) == 0 and row0 % (seqs * t_new) == 0
        assert n_seq % seqs == 0 and (seqs == 1 or t_new == tq)
        self.seqs = seqs
        assert pos0 == n_pages * PAGE or n_pages * PAGE < pos0
        assert t_new == tq or (tq == PAGE and pos0 == 0)
        assert t_new % PAGE == 0 or t_new % PAGE < NSA_BLOCK
        self.row0, self.n_seq, self.t_new, self.tq, self.pos0, self.n_pages = row0, n_seq, t_new, tq, pos0, n_pages
        self.nq = t_new // tq
        self.rows = n_seq * t_new
        self.lk = (n_pages + -(-t_new // PAGE)) * PAGE
        self.n_keys = n_pages * PAGE + t_new

    def grid(self):
        return (self.n_seq // self.seqs, self.nq)

    def q_spec(self, width, col):
        rows = self.seqs * self.tq
        return pl.BlockSpec((rows, width), lambda b, qi, pt: (self.row0 // rows + b * self.nq + qi, col // width))

    def seq_spec(self, width, col):
        rows = self.seqs * self.t_new
        return pl.BlockSpec((rows, width), lambda b, qi, pt: (self.row0 // rows + b, col // width))

    def page_specs(self, shape):
        return [pl.BlockSpec((1,) + shape, lambda b, qi, pt, sq=sq, p=p: (pt[b * self.seqs + sq, p], 0, 0))
                for sq in range(self.seqs) for p in range(self.n_pages)]

    def page_args(self, pool):
        return [pool] * (self.seqs * self.n_pages)

    def out_spec(self, width):
        return pl.BlockSpec((self.seqs * self.tq, width), lambda b, qi, pt: (b * self.nq + qi, 0))

    def statics(self):
        return dict(seqs=self.seqs, n_pages=self.n_pages, t_new=self.t_new, tq=self.tq, pos0=self.pos0)

    def widths(self):
        if self.nq == 1:
            return (self.lk,)
        step = 4 * PAGE
        assert self.lk % step == 0
        return tuple(range(step, self.lk + 1, step))


def _cmp_bias_table(rel_bias, grp):
    lane0 = 0 if grp.nq == 1 else CMP_BIAS_LANE0
    tab = rel_bias[_BUCKETS]
    pieces, n_far = [], 0
    for lane in range(LANES + 1):
        d0 = grp.pos0 - (NSA_BLOCK * (lane - lane0 + 1) - 1)
        plain = lane < LANES and (d0 >= PAGE or d0 + grp.tq - 1 < 0)
        if plain:
            n_far += 1
            continue
        if n_far:
            pieces.append(jnp.broadcast_to(rel_bias[N_BUCKETS - 1], (grp.tq, n_far, rel_bias.shape[1])))
            n_far = 0
        if lane < LANES:
            pieces.append(tab[np.clip(d0 + np.arange(grp.tq), 0, BAND - 1)][:, None, :])
    table = jnp.transpose(jnp.concatenate(pieces, 1), (2, 0, 1))
    return table.reshape(-1, GROUP_HEADS * grp.tq, LANES)


def nsa_compress(grp, z, page_table, pool, wexp, bias):
    kv_w = KV_GROUPS * 2 * HEAD_DIM
    qw = KV_GROUPS * GROUP_HEADS * HEAD_DIM
    const2 = lambda b, qi, pt: (0, 0)
    return pl.pallas_call(
        functools.partial(_cmp_body, **grp.statics()),
        grid_spec=pltpu.PrefetchScalarGridSpec(
            num_scalar_prefetch=1,
            grid=grp.grid(),
            in_specs=[grp.q_spec(qw, COL_QN), grp.seq_spec(kv_w, COL_KVC)] + grp.page_specs(KV_PAGE)
            + [pl.BlockSpec((PAGE, kv_w), const2), pl.BlockSpec(bias.shape, lambda b, qi, pt: (0, 0, 0))],
            out_specs=[grp.out_spec(qw), grp.out_spec(KV_GROUPS * LANES)],
            scratch_shapes=[pltpu.VMEM((LANES, kv_w), F32),
                            pltpu.VMEM((KV_GROUPS, LANES, HEAD_DIM), BF16),
                            pltpu.VMEM((KV_GROUPS, LANES, HEAD_DIM), BF16)]),
        out_shape=[jax.ShapeDtypeStruct((grp.rows, qw), F32),
                   jax.ShapeDtypeStruct((grp.rows, KV_GROUPS * LANES), F32)],
        compiler_params=_cparams("parallel", "arbitrary"),
        name="nsa_compress",
    )(page_table, z, z, *grp.page_args(pool), wexp, bias)


def sparse_attention(mode, grp, z, page_table, pool, band, *, q_col, kv_col, mask=None):
    kv_w = KV_GROUPS * 2 * HEAD_DIM
    qw = KV_GROUPS * GROUP_HEADS * HEAD_DIM
    in_specs = [grp.q_spec(qw, q_col), grp.seq_spec(kv_w, kv_col)] + grp.page_specs(KV_PAGE)
    args = [z, z] + grp.page_args(pool)
    if mode in ("sel", "mask"):
        in_specs.append(grp.out_spec(mask.shape[1]))
        args.append(mask)
    in_specs.append(pl.BlockSpec((KV_GROUPS, GROUP_HEADS * grp.tq, BAND), lambda b, qi, pt: (0, 0, 0)))
    args.append(band)
    widths = grp.widths()
    s_cols = max(widths) if (mode != "win" or grp.nq == 1) else NSA_WINDOW + PAGE
    rows = GROUP_HEADS * grp.tq
    scratch = [pltpu.VMEM((KV_GROUPS, grp.lk, HEAD_DIM), BF16),
               pltpu.VMEM((KV_GROUPS, grp.lk, HEAD_DIM), BF16),
               pltpu.VMEM((KV_GROUPS, rows, s_cols), F32),
               pltpu.VMEM((KV_GROUPS, grp.tq, s_cols), F32)]
    if grp.tq % BF16_SUBLANES == 0:
        scratch.append(pltpu.VMEM((KV_GROUPS, rows, s_cols), BF16))
    return pl.pallas_call(
        functools.partial(_attn_body, mode=mode, widths=widths, **grp.statics()),
        grid_spec=pltpu.PrefetchScalarGridSpec(
            num_scalar_prefetch=1,
            grid=grp.grid(),
            in_specs=in_specs,
            out_specs=grp.out_spec(qw),
            scratch_shapes=scratch),
        out_shape=jax.ShapeDtypeStruct((grp.rows, qw), F32),
        compiler_params=_cparams("parallel", "arbitrary"),
        name="sparse_attention_" + mode,
    )(page_table, *args)


def dsa_index_scores(grp, z, page_table, idx_pool):
    return pl.pallas_call(
        functools.partial(_index_body, widths=grp.widths(), **grp.statics()),
        grid_spec=pltpu.PrefetchScalarGridSpec(
            num_scalar_prefetch=1,
            grid=grp.grid(),
            in_specs=[grp.q_spec(IDX_HEADS * IDX_DIM, COL_QI), grp.q_spec(LANES, COL_MISC),
                      grp.seq_spec(LANES, COL_MISC)] + grp.page_specs((IDX_DIM, PAGE)),
            out_specs=grp.out_spec(grp.lk),
            scratch_shapes=[pltpu.VMEM((IDX_DIM, grp.lk), BF16)]),
        out_shape=jax.ShapeDtypeStruct((grp.rows, grp.lk), F32),
        compiler_params=_cparams("parallel", "arbitrary"),
        name="dsa_index_scores",
    )(page_table, z, z, z, *grp.page_args(idx_pool))


def topk_mask(grp, scores, k):
    rows, lk = scores.shape
    tr = PAGE
    assert rows % tr == 0 and (grp.nq == 1 or grp.tq == tr)
    blk = pl.BlockSpec((tr, lk), lambda i: (i, 0))
    return pl.pallas_call(
        functools.partial(_topk_body, k=k, nq=grp.nq, tr=tr, widths=grp.widths()),
        grid=(rows // tr,),
        in_specs=[blk],
        out_specs=blk,
        out_shape=jax.ShapeDtypeStruct((rows, lk), F32),
        scratch_shapes=[pltpu.VMEM((tr, lk), jnp.int32)],
        compiler_params=_cparams("parallel"),
        name="topk_mask",
    )(scores)


def _combine_body(oc_ref, os_ref, ow_ref, od_ref, misc_ref, y_ref):
    n_heads = KV_GROUPS * GROUP_HEADS
    gates = jax.nn.sigmoid(misc_ref[:, MISC_GATES:MISC_GATES + 3 * n_heads])
    for h in range(n_heads):
        hs = slice(h * HEAD_DIM, (h + 1) * HEAD_DIM)
        o = (gates[:, 3 * h:3 * h + 1] * oc_ref[:, hs] + gates[:, 3 * h + 1:3 * h + 2] * os_ref[:, hs]
             + gates[:, 3 * h + 2:3 * h + 3] * ow_ref[:, hs])
        y_ref[:, hs] = o.astype(y_ref.dtype)
    y_ref[:, n_heads * HEAD_DIM:] = od_ref[...].astype(y_ref.dtype)


def nsa_dsa_combine(o_c, o_s, o_w, o_d, z, row0, *, tm):
    m, w = o_c.shape
    assert m % tm == 0 and row0 % tm == 0
    blk = pl.BlockSpec((tm, w), lambda i: (i, 0))
    return pl.pallas_call(
        _combine_body,
        grid=(m // tm,),
        in_specs=[blk, blk, blk, blk, pl.BlockSpec((tm, LANES), lambda i: (row0 // tm + i, COL_MISC // LANES))],
        out_specs=pl.BlockSpec((tm, 2 * w), lambda i: (i, 0)),
        out_shape=jax.ShapeDtypeStruct((m, 2 * w), BF16),
        compiler_params=_cparams("parallel"),
        name="nsa_dsa_combine",
    )(o_c, o_s, o_w, o_d, z)


def _band_tiles(rel_bias, tq):
    delta = (rel_bias[_BUCKETS] - rel_bias[N_BUCKETS - 1]).T
    rev = jnp.concatenate([delta[:, ::-1], jnp.zeros((delta.shape[0], PAGE), delta.dtype)], 1)
    tiles = jnp.stack([rev[:, PAGE - 1 - i:PAGE - 1 - i + BAND] for i in range(tq)], 1)
    return tiles.reshape(-1, GROUP_HEADS * tq, BAND) * LOG2E


def _widen_cd_w_in(w):
    sizes = (1024, 512, 512, 512, 24, 1024, 512, 512, 64, 8)
    q_n, kv_c, kv_s, kv_w, gates, q_d, kv_d, q_i, k_i, w_i = jnp.split(w, np.cumsum(sizes)[:-1].tolist(), axis=-1)
    cols = [q_n, q_d, kv_c, kv_s, kv_w, kv_d, q_i, k_i, gates, w_i]
    used = sum(c.shape[-1] for c in cols)
    return jnp.concatenate(cols + [jnp.zeros(w.shape[:-1] + (NZ - used,), w.dtype)], axis=-1)


def kernel(x_prompt, x_sample, state_conv, state_pool, cache_nsa_cmp, cache_nsa_sel, cache_nsa_win, cache_dsa_kv, cache_dsa_idx, page_table, norm_mix, norm_ffn, norm_final, ab_w_in, ab_conv_w, ab_conv_b, ab_ln_g, ab_ln_b, ab_pool_w, ab_pool_scale, ab_w_out, cd_w_in, cd_w_cmp, cd_w_out, rel_bias, ffn_w1, ffn_w2):
    bp, tp, d_model = x_prompt.shape
    bs, ts, _ = x_sample.shape
    mp, ms = bp * tp, bs * ts
    depth = norm_mix.shape[0]
    n_pages = page_table.shape[1]
    n_pool = cache_nsa_cmp.shape[1]
    past_len = n_pages * PAGE
    assert cache_nsa_cmp.shape[2] == PAGE
    win_len = cache_nsa_win.shape[2]
    assert win_len % PAGE == 0 and win_len == NSA_WINDOW and tp >= NSA_WINDOW
    kv_w = KV_GROUPS * 2 * HEAD_DIM

    xs = [x_prompt.reshape(mp, d_model), x_sample.reshape(ms, d_model)]
    grp_p = _Group(0, bp, tp, PAGE, 0, 0)
    grp_s = _Group(mp, bs, ts, ts, past_len, n_pages)
    grp_sc = _Group(mp, bs, ts, ts, past_len, n_pages, SAMPLE_SEQS_PER_STEP)
    grp_sw = _Group(mp, bs, ts, ts, past_len, win_len // PAGE, SAMPLE_SEQS_PER_STEP)
    no_pages = jnp.zeros((1, 1), jnp.int32)
    win_pages = jnp.arange(bs * (win_len // PAGE), dtype=jnp.int32).reshape(bs, win_len // PAGE)

    outs = {k: [] for k in ("conv_p", "conv_s", "pool_p", "pool_s", "cmp_p", "cmp_s", "sel_p", "sel_s",
                            "win_p", "win_s", "dsa_p", "dsa_s", "idx_p", "idx_s")}
    w_bf16 = {"ab_w_in": ab_w_in, "ab_w_out": ab_w_out, "cd_w_in": _widen_cd_w_in(cd_w_in), "cd_w_out": cd_w_out,
              "ffn_w1": ffn_w1, "ffn_w2": ffn_w2}
    w_bf16 = {k: v.astype(BF16) for k, v in w_bf16.items()}
    y_p = y_s = None
    for i in range(depth):
        j = i // 2
        if i % 2 == 0:
            d_conv = ab_conv_w.shape[2]
            z = norm_matmul(xs, norm_mix[i], w_bf16["ab_w_in"], j)
            mid_p, u_p = ab_mid_prompt(z, bp, tp, ab_conv_w[j], ab_conv_b[j], ab_ln_g[j], ab_ln_b[j],
                                       ab_pool_w[j], ab_pool_scale[j])
            mid_s, conv_s, pool_s = ab_mid_step(z, mp, bs, ts, past_len, state_conv[j], state_pool[j], ab_conv_w[j],
                                                ab_conv_b[j], ab_ln_g[j], ab_ln_b[j], ab_pool_w[j], ab_pool_scale[j])
            xs = [matmul_residual([mid_p, mid_s], w_bf16["ab_w_out"], j, xs)]
            outs["conv_p"].append(u_p.reshape(bp, tp, d_conv)[:, tp - CONV_BUF:])
            outs["conv_s"].append(conv_s)
            outs["pool_p"].append(jnp.stack([z[(b + 1) * tp - POOL_BUF:(b + 1) * tp, 2 * d_conv:] for b in range(bp)]))
            outs["pool_s"].append(pool_s)
        else:
            z = norm_matmul(xs, norm_mix[i], w_bf16["cd_w_in"], j)
            nsa_bias = rel_bias[:, :KV_GROUPS * GROUP_HEADS]
            band_p, band_s = _band_tiles(rel_bias, grp_p.tq), _band_tiles(rel_bias, grp_s.tq)
            wexp = jnp.tile(jnp.repeat(jnp.transpose(cd_w_cmp[j], (1, 0, 2)).reshape(NSA_BLOCK, 2 * KV_GROUPS),
                                       HEAD_DIM, axis=1), (PAGE // NSA_BLOCK, 1))
            pt = page_table + j * n_pool
            pools = [c.reshape((-1,) + KV_PAGE) for c in (cache_nsa_cmp, cache_nsa_sel, cache_dsa_kv)]
            idx_pool = jnp.swapaxes(cache_dsa_idx, 2, 3).reshape(-1, IDX_DIM, PAGE)
            win_pool = cache_nsa_win.reshape((-1,) + KV_PAGE)
            wpt = win_pages + j * bs * (win_len // PAGE)
            mids = []
            for grp, gc, gw, ptab, wtab, band in ((grp_p, grp_p, grp_p, no_pages, no_pages, band_p),
                                                  (grp_s, grp_sc, grp_sw, pt, wpt, band_s)):
                o_c, msel = nsa_compress(gc, z, ptab, pools[0], wexp, _cmp_bias_table(nsa_bias, gc))
                o_s = sparse_attention("sel", gc, z, ptab, pools[1], band[:KV_GROUPS],
                                       q_col=COL_QN, kv_col=COL_KVS, mask=msel)
                o_w = sparse_attention("win", gw, z, wtab, win_pool, band[:KV_GROUPS],
                                       q_col=COL_QN, kv_col=COL_KVW)
                top = topk_mask(grp, dsa_index_scores(grp, z, ptab, idx_pool), min(DSA_TOPK, grp.n_keys // 4))
                o_d = sparse_attention("mask", gc, z, ptab, pools[2], band[KV_GROUPS:],
                                       q_col=COL_QD, kv_col=COL_KVD, mask=top)
                mids.append(nsa_dsa_combine(o_c, o_s, o_w, o_d, z, grp.row0, tm=min(512, grp.rows)))
            xs = [matmul_residual(mids, w_bf16["cd_w_out"], j, xs)]

            def kv_out(col, width, tail):
                seg = z[:, col:col + width]
                return seg[:mp].reshape((bp, tp) + tail), seg[mp:].reshape((bs, ts) + tail)

            kv_tail = (2, KV_GROUPS, HEAD_DIM)
            for name, col in (("cmp", COL_KVC), ("sel", COL_KVS), ("dsa", COL_KVD)):
                p_new, s_new = kv_out(col, kv_w, kv_tail)
                outs[name + "_p"].append(p_new)
                outs[name + "_s"].append(s_new)
            w_p, w_s = kv_out(COL_KVW, kv_w, kv_tail)
            outs["win_p"].append(w_p[:, tp - NSA_WINDOW:])
            outs["win_s"].append(jnp.concatenate([cache_nsa_win[j], w_s], 1)[:, ts:])
            i_p, i_s = kv_out(COL_MISC + MISC_KI, IDX_DIM, (IDX_DIM,))
            outs["idx_p"].append(i_p)
            outs["idx_s"].append(i_s)
        a = norm_matmul(xs, norm_ffn[i], w_bf16["ffn_w1"], i, relu2=True, out_dtype=BF16)
        if i == depth - 1:
            y_p, y_s = matmul_residual([a], w_bf16["ffn_w2"], i, xs, norm_final, split_out=(mp, ms))
        else:
            xs = [matmul_residual([a], w_bf16["ffn_w2"], i, xs)]

    st = {k: jnp.stack(v) for k, v in outs.items()}
    return (y_p.reshape(bp, tp, d_model), y_s.reshape(bs, ts, d_model),
            st["conv_p"], st["conv_s"], st["pool_p"], st["pool_s"], st["cmp_p"], st["cmp_s"],
            st["sel_p"], st["sel_s"], st["win_p"], st["win_s"], st["dsa_p"], st["dsa_s"],
            st["idx_p"], st["idx_s"])
```

```python
import functools
import math

import numpy as np
import jax
import jax.numpy as jnp
from jax import lax
from jax.experimental import pallas as pl
from jax.experimental.pallas import tpu as pltpu

F32 = jnp.float32
BF16 = jnp.bfloat16

EPS = 1e-6
NEG_INF = -1e30
LOG2E = math.log2(math.e)
HEAD_DIM = 128
LANES = 128
BF16_SUBLANES = 16
CONV_WIDTH = 31
CONV_BUF = CONV_WIDTH - 1
POOL_WINDOWS = (2, 4, 8, 16)
POOL_BUF = max(POOL_WINDOWS) - 1
HALO = 32
VMEM_LIMIT = 56 * 1024 * 1024


def _cparams(*sem):
    return pltpu.CompilerParams(dimension_semantics=sem, vmem_limit_bytes=VMEM_LIMIT)


class _Rows:
    def __init__(self, arrays, tm):
        self.arrays = list(arrays)
        self.tm = tm
        assert all(a.shape[0] % tm == 0 for a in self.arrays)
        self.tiles = [a.shape[0] // tm for a in self.arrays]
        self.n_tiles = sum(self.tiles)
        self.n = len(self.arrays)

    def specs(self, width, col):
        out, t0 = [], 0
        for nt in self.tiles:
            out.append(pl.BlockSpec((self.tm, width), lambda i, j, t0=t0, nt=nt: (jnp.clip(i - t0, 0, nt - 1), col(j))))
            t0 += nt
        return out

    def select(self, i, refs, fn):
        if self.n == 1:
            fn(refs[0])
            return
        t0 = 0
        for nt, ref in zip(self.tiles, refs):
            pl.when((i >= t0) & (i < t0 + nt))(functools.partial(fn, ref))
            t0 += nt


def _rmsnorm_rows(x, g):
    return (x * lax.rsqrt(jnp.mean(x * x, -1, keepdims=True) + EPS)) * g


def _norm_matmul_body(*refs, rows, relu2):
    x_refs = refs[:rows.n]
    g_ref, w_ref, o_ref, h_ref = refs[rows.n:]

    @pl.when(pl.program_id(1) == 0)
    def _():
        def norm(x_ref):
            h_ref[...] = _rmsnorm_rows(x_ref[...], g_ref[...]).astype(BF16)

        rows.select(pl.program_id(0), x_refs, norm)

    y = jnp.dot(h_ref[...], w_ref[...], preferred_element_type=F32)
    if relu2:
        y = jnp.square(jnp.maximum(y, 0.0))
    o_ref[...] = y.astype(o_ref.dtype)


def norm_matmul(xs, g, w, layer, *, relu2=False, out_dtype=F32, tm=1024, tn=1024):
    rows = _Rows(xs, tm)
    _, d, n = w.shape
    assert n % tn == 0
    return pl.pallas_call(
        functools.partial(_norm_matmul_body, rows=rows, relu2=relu2),
        grid=(rows.n_tiles, n // tn),
        in_specs=rows.specs(d, lambda j: 0) + [pl.BlockSpec((1, d), lambda i, j: (0, 0)),
                                               pl.BlockSpec((None, d, tn), lambda i, j: (layer, 0, j))],
        out_specs=pl.BlockSpec((tm, tn), lambda i, j: (i, j)),
        out_shape=jax.ShapeDtypeStruct((rows.n_tiles * tm, n), out_dtype),
        scratch_shapes=[pltpu.VMEM((tm, d), BF16)],
        compiler_params=_cparams("parallel", "arbitrary"),
        name="norm_matmul",
    )(*rows.arrays, g.reshape(1, d), w)


def _matmul_residual_body(*refs, a_rows, r_rows, o_rows, final_norm):
    a_refs, refs = refs[:a_rows.n], refs[a_rows.n:]
    w_ref, refs = refs[0], refs[1:]
    r_refs, refs = refs[:r_rows.n], refs[r_rows.n:]
    if final_norm:
        g_ref, refs = refs[0], refs[1:]
    o_refs, acc_ref = refs[:o_rows.n], refs[o_rows.n]
    i, k = pl.program_id(0), pl.program_id(1)

    @pl.when(k == 0)
    def _():
        acc_ref[...] = jnp.zeros_like(acc_ref)

    def accumulate(a_ref):
        acc_ref[...] += jnp.dot(a_ref[...], w_ref[...], preferred_element_type=F32)

    a_rows.select(i, a_refs, accumulate)

    @pl.when(k == pl.num_programs(1) - 1)
    def _():
        def add_residual(r_ref):
            acc_ref[...] += r_ref[...]

        def write(o_ref):
            o = acc_ref[...]
            o_ref[...] = _rmsnorm_rows(o, g_ref[...]) if final_norm else o

        r_rows.select(i, r_refs, add_residual)
        o_rows.select(i, o_refs, write)


def matmul_residual(a_list, w, layer, r_list, g_final=None, *, split_out=None, tm=512, tk=2048):
    a_rows, r_rows = _Rows(a_list, tm), _Rows(r_list, tm)
    _, kdim, n = w.shape
    m = a_rows.n_tiles * tm
    assert kdim % tk == 0 and r_rows.n_tiles == a_rows.n_tiles
    final_norm = g_final is not None
    o_rows = _Rows([jax.ShapeDtypeStruct((r, n), F32) for r in (split_out or (m,))], tm)
    assert o_rows.n_tiles == a_rows.n_tiles
    in_specs = (a_rows.specs(tk, lambda k: k) + [pl.BlockSpec((None, tk, n), lambda i, k: (layer, k, 0))]
                + r_rows.specs(n, lambda k: 0))
    args = a_rows.arrays + [w] + r_rows.arrays
    if final_norm:
        in_specs.append(pl.BlockSpec((1, n), lambda i, k: (0, 0)))
        args.append(g_final.reshape(1, n))
    out = pl.pallas_call(
        functools.partial(_matmul_residual_body, a_rows=a_rows, r_rows=r_rows, o_rows=o_rows, final_norm=final_norm),
        grid=(a_rows.n_tiles, kdim // tk),
        in_specs=in_specs,
        out_specs=o_rows.specs(n, lambda k: 0),
        out_shape=o_rows.arrays,
        scratch_shapes=[pltpu.VMEM((tm, n), F32)],
        compiler_params=_cparams("parallel", "arbitrary"),
        name="matmul_residual",
    )(*args)
    return out if split_out else out[0]


def _layernorm_silu(c, g, b):
    mu = jnp.mean(c, -1, keepdims=True)
    xc = c - mu
    y = xc * lax.rsqrt(jnp.mean(xc * xc, -1, keepdims=True) + EPS)
    y = y * g + b
    return y * jax.nn.sigmoid(y)


def _ab_mid_body(z_ref, zp_ref, cw_ref, cb_ref, lg_ref, lb_ref, pw_ref, ps_ref, y_ref, u_ref,
                 ext_ref, vext_ref, conv_ref, *, tt, d_conv, d_pool):
    ti = pl.program_id(1)
    keep = (ti > 0).astype(F32)
    a_p = zp_ref[:, 0:d_conv]
    g_p = zp_ref[:, d_conv:2 * d_conv]
    ext_ref[0:HALO, :] = a_p * jax.nn.sigmoid(g_p) * keep
    vext_ref[0:HALO, :] = zp_ref[:, 2 * d_conv:] * keep
    u = z_ref[:, 0:d_conv] * jax.nn.sigmoid(z_ref[:, d_conv:2 * d_conv])
    ext_ref[HALO:, :] = u
    u_ref[...] = u
    vext_ref[HALO:, :] = z_ref[:, 2 * d_conv:]

    off = HALO - CONV_BUF
    sub = 8
    for c in range(d_conv // LANES):
        cs = slice(c * LANES, (c + 1) * LANES)
        acc = jnp.zeros((tt, LANES), F32)
        for s in range(sub):
            n = tt if s == 0 else tt + sub
            part = jnp.zeros((n, LANES), F32)
            for j in range(CONV_WIDTH):
                if (off + j) % sub == s:
                    start = off + j - s
                    part = part + cw_ref[j:j + 1, cs] * ext_ref[start:start + n, cs]
            acc = acc + part[s:s + tt]
        conv_ref[:, cs] = acc + cb_ref[:, cs]
    y_ref[:, 0:d_conv] = _layernorm_silu(conv_ref[...], lg_ref[...], lb_ref[...]).astype(y_ref.dtype)

    pos = ti * tt + lax.broadcasted_iota(jnp.int32, (tt, 1), 0)
    pg = d_pool // len(POOL_WINDOWS)
    for gi, w in enumerate(POOL_WINDOWS):
        gs = slice(gi * pg, (gi + 1) * pg)
        tok = vext_ref[HALO:, gs]
        acc = tok
        for i in range(1, w):
            acc = acc + vext_ref[HALO - i:HALO - i + tt, gs]
        cnt = jnp.minimum(pos + 1, w).astype(F32)
        d = acc / cnt - tok
        yp = jnp.dot(d.astype(BF16), pw_ref[gi], preferred_element_type=F32) * ps_ref[:, gs]
        y_ref[:, d_conv + gi * pg:d_conv + (gi + 1) * pg] = yp.astype(y_ref.dtype)


def ab_mid_prompt(z, n_seq, t_len, conv_w, conv_b, ln_g, ln_b, pool_w, pool_scale, *, tt=256):
    d_conv = conv_w.shape[1]
    d_pool = pool_scale.shape[0]
    nt = t_len // tt
    hb = tt // HALO
    row = lambda b, t: (b * nt + t, 0)
    const = lambda b, t: (0, 0)
    return pl.pallas_call(
        functools.partial(_ab_mid_body, tt=tt, d_conv=d_conv, d_pool=d_pool),
        grid=(n_seq, nt),
        in_specs=[pl.BlockSpec((tt, z.shape[1]), row),
                  pl.BlockSpec((HALO, z.shape[1]), lambda b, t: (jnp.maximum((b * nt + t) * hb - 1, 0), 0)),
                  pl.BlockSpec(conv_w.shape, const),
                  pl.BlockSpec((1, d_conv), const),
                  pl.BlockSpec((1, d_conv), const),
                  pl.BlockSpec((1, d_conv), const),
                  pl.BlockSpec(pool_w.shape, lambda b, t: (0, 0, 0)),
                  pl.BlockSpec((1, d_pool), const)],
        out_specs=[pl.BlockSpec((tt, d_conv + d_pool), row),
                   pl.BlockSpec((tt, d_conv), row)],
        out_shape=[jax.ShapeDtypeStruct((n_seq * t_len, d_conv + d_pool), BF16),
                   jax.ShapeDtypeStruct((n_seq * t_len, d_conv), F32)],
        scratch_shapes=[pltpu.VMEM((HALO + tt, d_conv), F32),
                        pltpu.VMEM((HALO + tt, d_pool), F32),
                        pltpu.VMEM((tt, d_conv), F32)],
        compiler_params=_cparams("parallel", "parallel"),
        name="ab_mid_prompt",
    )(z, z, conv_w, conv_b.reshape(1, -1), ln_g.reshape(1, -1), ln_b.reshape(1, -1),
      pool_w.astype(BF16), pool_scale.reshape(1, -1))


def _ab_mid_step_body(z_ref, sc_ref, sp_ref, cw_ref, cb_ref, lg_ref, lb_ref, pw_ref, ps_ref,
                      y_ref, nc_ref, np_ref, ext_ref, vext_ref, *, nb, t, pos0, d_conv, d_pool):
    e0 = HALO - CONV_BUF
    p0 = 16 - POOL_BUF
    z = z_ref[...].reshape(nb, t, z_ref.shape[1])
    u = z[:, :, 0:d_conv] * jax.nn.sigmoid(z[:, :, d_conv:2 * d_conv])
    ext_ref[:, e0:HALO, :] = sc_ref[...]
    ext_ref[:, HALO:, :] = u
    vext_ref[:, p0:16, :] = sp_ref[...]
    vext_ref[:, 16:, :] = z[:, :, 2 * d_conv:]
    nc_ref[...] = ext_ref[:, HALO + t - CONV_BUF:, :]
    np_ref[...] = vext_ref[:, 16 + t - POOL_BUF:, :]

    acc = jnp.zeros((nb, t, d_conv), F32)
    for j in range(CONV_WIDTH):
        acc = acc + cw_ref[j:j + 1, :][None] * ext_ref[:, e0 + j:e0 + j + t, :]
    c = acc + cb_ref[...][None]
    yc = _layernorm_silu(c, lg_ref[...][None], lb_ref[...][None])
    y_ref[:, 0:d_conv] = yc.reshape(nb * t, d_conv).astype(y_ref.dtype)

    pg = d_pool // len(POOL_WINDOWS)
    for gi, w in enumerate(POOL_WINDOWS):
        gs = slice(gi * pg, (gi + 1) * pg)
        tok = vext_ref[:, 16:, gs]
        acc = tok
        for i in range(1, w):
            acc = acc + vext_ref[:, 16 - i:16 - i + t, gs]
        cnt = jnp.minimum(pos0 + 1 + lax.broadcasted_iota(jnp.int32, (1, t, 1), 1), w).astype(F32)
        d = (acc / cnt - tok).reshape(nb * t, pg)
        yp = jnp.dot(d.astype(BF16), pw_ref[gi], preferred_element_type=F32) * ps_ref[:, gs]
        y_ref[:, d_conv + gi * pg:d_conv + (gi + 1) * pg] = yp.astype(y_ref.dtype)


def ab_mid_step(z, row0, n_seq, t, pos0, state_conv, state_pool, conv_w, conv_b, ln_g, ln_b, pool_w,
                pool_scale, *, nb=16):
    d_conv = conv_w.shape[1]
    d_pool = pool_scale.shape[0]
    rb = nb * t
    assert row0 % rb == 0 and n_seq % nb == 0
    const = lambda i: (0, 0)
    seq3 = lambda i: (i, 0, 0)
    return pl.pallas_call(
        functools.partial(_ab_mid_step_body, nb=nb, t=t, pos0=pos0, d_conv=d_conv, d_pool=d_pool),
        grid=(n_seq // nb,),
        in_specs=[pl.BlockSpec((rb, z.shape[1]), lambda i: (row0 // rb + i, 0)),
                  pl.BlockSpec((nb, CONV_BUF, d_conv), seq3),
                  pl.BlockSpec((nb, POOL_BUF, d_pool), seq3),
                  pl.BlockSpec(conv_w.shape, const),
                  pl.BlockSpec((1, d_conv), const),
                  pl.BlockSpec((1, d_conv), const),
                  pl.BlockSpec((1, d_conv), const),
                  pl.BlockSpec(pool_w.shape, lambda i: (0, 0, 0)),
                  pl.BlockSpec((1, d_pool), const)],
        out_specs=[pl.BlockSpec((rb, d_conv + d_pool), lambda i: (i, 0)),
                   pl.BlockSpec((nb, CONV_BUF, d_conv), seq3),
                   pl.BlockSpec((nb, POOL_BUF, d_pool), seq3)],
        out_shape=[jax.ShapeDtypeStruct((n_seq * t, d_conv + d_pool), BF16),
                   jax.ShapeDtypeStruct((n_seq, CONV_BUF, d_conv), F32),
                   jax.ShapeDtypeStruct((n_seq, POOL_BUF, d_pool), F32)],
        scratch_shapes=[pltpu.VMEM((nb, HALO + t, d_conv), F32),
                        pltpu.VMEM((nb, 16 + t, d_pool), F32)],
        compiler_params=_cparams("parallel"),
        name="ab_mid_step",
    )(z, state_conv, state_pool, conv_w, conv_b.reshape(1, -1), ln_g.reshape(1, -1),
      ln_b.reshape(1, -1), pool_w.astype(BF16), pool_scale.reshape(1, -1))


N_BUCKETS = 32
MAX_DISTANCE = 128
NSA_BLOCK = 64
NSA_TOPN = 16
NSA_WINDOW = 512
DSA_TOPK = 256
IDX_HEADS = 8
IDX_DIM = 64
KV_GROUPS = 2
GROUP_HEADS = 4
PAGE = 128
BAND = 2 * PAGE
INT_MIN = -2 ** 31
KV_PAGE = (2 * KV_GROUPS * PAGE, HEAD_DIM)
TOPK_ROW_GROUPS = 4
SAMPLE_SEQS_PER_STEP = 4
CAUSAL_WIDTH_STEP = 2 * PAGE
CMP_BIAS_LANE0 = 64

COL_QN, COL_QD, COL_KVC, COL_KVS, COL_KVW, COL_KVD, COL_QI, COL_MISC = 0, 1024, 2048, 2560, 3072, 3584, 4096, 4608
MISC_KI, MISC_GATES, MISC_WI = 0, 64, 88
NZ = 5120


def _bucket_np(n):
    n = np.maximum(np.asarray(n, np.int32), 0)
    exact = N_BUCKETS // 2
    nf = np.maximum(n, 1).astype(np.float32)
    big = exact + (np.log(nf / np.float32(exact)) / np.float32(math.log(MAX_DISTANCE / exact))
                   * np.float32(N_BUCKETS - exact)).astype(np.int32)
    return np.where(n < exact, n, np.minimum(big, N_BUCKETS - 1))


_BUCKETS = _bucket_np(np.arange(BAND))
assert _BUCKETS[PAGE:].min() == N_BUCKETS - 1


def _softmax_rows(s, mask):
    s = jnp.where(mask, s, NEG_INF)
    m = jnp.max(s, -1, keepdims=True)
    p = jnp.where(mask, jnp.exp(s - m), 0.0)
    return p, jnp.sum(p, -1, keepdims=True)


def _dot_nt(a, b):
    return lax.dot_general(a, b, (((1,), (1,)), ((), ())), preferred_element_type=F32)


def _new_chunks(new_ref, t_new):
    chunks = [new_ref[c * PAGE:(c + 1) * PAGE, :] for c in range(t_new // PAGE)]
    rem = t_new % PAGE
    if rem:
        tail = new_ref[(t_new // PAGE) * PAGE:, :]
        chunks.append(jnp.concatenate([tail, jnp.zeros((PAGE - rem, tail.shape[1]), F32)], 0))
    return chunks


def _kv_chunks(page_refs, new_ref, t_new):
    n_parts = 2 * KV_GROUPS
    chunks = [[r[0, pl.ds(part, PAGE, stride=n_parts), :] for part in range(n_parts)] for r in page_refs]
    for x in _new_chunks(new_ref, t_new):
        chunks.append([x[:, part * HEAD_DIM:(part + 1) * HEAD_DIM] for part in range(n_parts)])
    return chunks


def _seq_view(ref, sq, seqs):
    n = ref.shape[0] // seqs
    return ref.at[pl.ds(sq * n, n)]


def _cmp_body(pt_ref, q_ref, kvn_ref, *rest, seqs, n_pages, **statics):
    del pt_ref
    pages, (wexp_ref, bias_ref, o_ref, msel_ref, *scratch) = rest[:seqs * n_pages], rest[seqs * n_pages:]
    for sq in range(seqs):
        _cmp_one(_seq_view(q_ref, sq, seqs), _seq_view(kvn_ref, sq, seqs), pages[sq * n_pages:(sq + 1) * n_pages],
                 wexp_ref, bias_ref, _seq_view(o_ref, sq, seqs), _seq_view(msel_ref, sq, seqs), *scratch,
                 n_pages=n_pages, **statics)


def _cmp_one(q_ref, kvn_ref, page_refs, wexp_ref, bias_ref, o_ref, msel_ref, comp_ref, ck_ref, cv_ref, *,
             n_pages, t_new, tq, pos0):
    qi = pl.program_id(1)
    n_keys = n_pages * PAGE + t_new
    n_cmp = n_keys // NSA_BLOCK
    n_sel = -(-n_keys // NSA_BLOCK)
    per = PAGE // NSA_BLOCK

    @pl.when(qi == 0)
    def _():
        comp_ref[...] = jnp.zeros_like(comp_ref)
        chunks = _kv_chunks(page_refs, kvn_ref, t_new)[:n_cmp // per]
        for part in range(2 * KV_GROUPS):
            cols = slice(part * HEAD_DIM, (part + 1) * HEAD_DIM)
            xw = jnp.concatenate([parts[part] * wexp_ref[:, cols] for parts in chunks], 0)
            comp_ref[0:per * len(chunks), cols] = xw.reshape(per * len(chunks), NSA_BLOCK, HEAD_DIM).sum(1)
        for g in range(KV_GROUPS):
            ck_ref[g] = comp_ref[:, g * HEAD_DIM:(g + 1) * HEAD_DIM].astype(BF16)
            cv_ref[g] = comp_ref[:, (KV_GROUPS + g) * HEAD_DIM:(KV_GROUPS + g + 1) * HEAD_DIM].astype(BF16)

    scale = HEAD_DIM ** -0.5
    rows = GROUP_HEADS * tq
    blk = lax.broadcasted_iota(jnp.int32, (1, LANES), 1)
    q0 = pos0 + qi * tq
    assert tq & (tq - 1) == 0
    qpos_st = q0 + (lax.broadcasted_iota(jnp.int32, (rows, 1), 0) & (tq - 1))
    mask = (qpos_st - ((blk + 1) * NSA_BLOCK - 1) >= 0) & (blk < n_cmp)
    cur = (q0 + lax.broadcasted_iota(jnp.int32, (tq, 1), 0)) // NSA_BLOCK
    scores = []
    for g in range(KV_GROUPS):
        heads = [g * GROUP_HEADS + r for r in range(GROUP_HEADS)]
        bias = bias_ref[g]
        if t_new != tq:
            bias = pltpu.roll(bias, qi * (tq // NSA_BLOCK) + (LANES - CMP_BIAS_LANE0), 1)
        q = jnp.concatenate([q_ref[:, h * HEAD_DIM:(h + 1) * HEAD_DIM] for h in heads], 0).astype(BF16)
        scores.append(_dot_nt(q, ck_ref[g]) * scale + bias)
    probs = []
    for g in range(KV_GROUPS):
        p, l = _softmax_rows(scores[g], mask)
        probs.append(p / jnp.maximum(l, 1e-30))
    for g in range(KV_GROUPS):
        o = jnp.dot(probs[g].astype(BF16), cv_ref[g], preferred_element_type=F32)
        for r in range(GROUP_HEADS):
            h = g * GROUP_HEADS + r
            o_ref[:, h * HEAD_DIM:(h + 1) * HEAD_DIM] = o[r * tq:(r + 1) * tq]
    for g in range(KV_GROUPS):
        imp = probs[g][0:tq]
        for r in range(1, GROUP_HEADS):
            imp = imp + probs[g][r * tq:(r + 1) * tq]
        imp = jnp.where(blk == cur, 2.0, jnp.where(blk > cur, -1.0, imp))
        imp = jnp.where(blk < n_sel, imp, -2.0)
        n_top = min(NSA_TOPN, n_sel)
        cols = slice(g * LANES, (g + 1) * LANES)

        def by_rank(imp=imp, cols=cols):
            rank = jnp.zeros((tq, LANES), F32)
            for i in range(n_sel):
                col = imp[:, i:i + 1]
                ahead = (col > imp) | ((col == imp) & (blk > i))
                rank = rank + jnp.where(ahead, 1.0, 0.0)
            msel_ref[:, cols] = jnp.where((rank < float(n_top)) & (blk < n_sel), 1.0, 0.0)

        def first_blocks(cols=cols):
            msel_ref[:, cols] = jnp.where(blk < n_top, 1.0, 0.0) + jnp.zeros((tq, LANES), F32)

        if t_new == tq:
            if pos0 + tq <= n_top * NSA_BLOCK:
                first_blocks()
            else:
                by_rank()
        else:
            early = pos0 + (qi + 1) * tq <= n_top * NSA_BLOCK
            pl.when(early)(first_blocks)
            pl.when(jnp.logical_not(early))(by_rank)


def _on_causal_width(qi, tq, widths, tile):
    if len(widths) == 1:
        tile(widths[0], True)
        return
    need = (qi * tq + tq - 1) // widths[0]
    for nw, w in enumerate(widths):
        pl.when(need == nw)(functools.partial(tile, w, nw == 0))


def _attn_body(pt_ref, q_ref, kvn_ref, *rest, mode, seqs, n_pages, **statics):
    del pt_ref
    pages, rest = rest[:seqs * n_pages], rest[seqs * n_pages:]
    m_ref = None
    if mode in ("sel", "mask"):
        m_ref, rest = rest[0], rest[1:]
    band_ref, o_ref, *scratch = rest
    for sq in range(seqs):
        _attn_one(_seq_view(q_ref, sq, seqs), _seq_view(kvn_ref, sq, seqs), pages[sq * n_pages:(sq + 1) * n_pages],
                  None if m_ref is None else _seq_view(m_ref, sq, seqs), band_ref, _seq_view(o_ref, sq, seqs),
                  *scratch, mode=mode, n_pages=n_pages, **statics)


def _attn_one(q_ref, kvn_ref, page_refs, m_ref, band_ref, o_ref, kc_ref, vc_ref, s_ref, cap_ref, *maybe_p_ref,
              mode, n_pages, t_new, tq, pos0, widths):
    p_ref = maybe_p_ref[0] if maybe_p_ref else s_ref
    qi = pl.program_id(1)
    single = t_new == tq
    n_keys = n_pages * PAGE + t_new
    kbase = pos0 - n_pages * PAGE
    scale = HEAD_DIM ** -0.5
    q0 = pos0 if single else pos0 + qi * tq

    @pl.when(qi == 0)
    def _():
        for c, parts in enumerate(_kv_chunks(page_refs, kvn_ref, t_new)):
            rows = slice(c * PAGE, (c + 1) * PAGE)
            for g in range(KV_GROUPS):
                kc_ref[g, rows, :] = parts[g].astype(BF16)
                vc_ref[g, rows, :] = parts[KV_GROUPS + g].astype(BF16)

    def tile(c0, w, band_at, maybe_first):
        qpos = q0 + lax.broadcasted_iota(jnp.int32, (tq, 1), 0)
        col = c0 + lax.broadcasted_iota(jnp.int32, (1, w), 1)
        dist = qpos - (kbase + col)
        visible = (dist >= 0) & (col < n_keys)
        if mode == "win":
            visible = visible & (dist < NSA_WINDOW)
        if mode == "mask":
            visible = visible & (m_ref[:, 0:w] > 0.5)
        keys = pl.ds(c0, w)
        groups = range(KV_GROUPS)
        for g in groups:
            mask = visible
            if mode == "sel":
                expand = (lax.broadcasted_iota(jnp.int32, (LANES, w), 1) // NSA_BLOCK
                          == lax.broadcasted_iota(jnp.int32, (LANES, w), 0))
                chosen = jnp.dot(m_ref[:, g * LANES:(g + 1) * LANES].astype(BF16),
                                 jnp.where(expand, 1.0, 0.0).astype(BF16), preferred_element_type=F32)
                mask = visible & (chosen > 0.5)
            cap_ref[g, :, 0:w] = jnp.where(mask, jnp.inf, NEG_INF)
            heads = [g * GROUP_HEADS + r for r in range(GROUP_HEADS)]
            q = jnp.concatenate([q_ref[:, h * HEAD_DIM:(h + 1) * HEAD_DIM] for h in heads], 0).astype(BF16)
            s_ref[g, :, 0:w] = _dot_nt(q, kc_ref[g, keys, :]) * (scale * LOG2E)
        for g in groups:
            if band_at is not None:
                s_ref[g, :, band_at:band_at + BAND] += band_ref[g]
            else:
                if maybe_first:
                    @pl.when(qi == 0)
                    def _():
                        s_ref[g, :, 0:PAGE] += band_ref[g, :, PAGE:]

                @pl.when(qi > 0)
                def _():
                    s_ref[g, :, pl.ds(pl.multiple_of(q0 - PAGE - kbase, PAGE), BAND)] += band_ref[g]
        sums, alive = [], []
        for g in groups:
            for r in range(GROUP_HEADS):
                rows = slice(r * tq, (r + 1) * tq)
                m = jnp.max(jnp.minimum(s_ref[g, rows, 0:w], cap_ref[g, :, 0:w]), -1, keepdims=True)
                p = jnp.exp2(jnp.minimum(s_ref[g, rows, 0:w], cap_ref[g, :, 0:w]) - m)
                p_ref[g, rows, 0:w] = p.astype(p_ref.dtype)
                sums.append(jnp.sum(p, -1, keepdims=True))
                alive.append(m > NEG_INF)
        for g in groups:
            o = jnp.dot(p_ref[g, :, 0:w].astype(BF16), vc_ref[g, keys, :], preferred_element_type=F32)
            for r in range(GROUP_HEADS):
                h = g * GROUP_HEADS + r
                o_h = o[r * tq:(r + 1) * tq] / jnp.maximum(sums[h], 1e-30)
                o_ref[:, h * HEAD_DIM:(h + 1) * HEAD_DIM] = jnp.where(alive[h], o_h, 0.0)

    if single:
        tile(0, widths[0], pos0 - PAGE - kbase, False)
    elif mode == "win":
        wch = NSA_WINDOW // PAGE
        pl.when(qi < wch)(functools.partial(tile, 0, NSA_WINDOW, None, True))
        pl.when(qi >= wch)(lambda: tile(pl.multiple_of((qi - wch) * PAGE, PAGE), NSA_WINDOW + PAGE,
                                        NSA_WINDOW - PAGE, False))
    else:
        _on_causal_width(qi, tq, widths, lambda w, first: tile(0, w, None, first))


def _index_body(pt_ref, qidx_ref, miscq_ref, misck_ref, *rest, seqs, n_pages, **statics):
    del pt_ref
    pages, (o_ref, kidx_ref) = rest[:seqs * n_pages], rest[seqs * n_pages:]
    for sq in range(seqs):
        _index_one(_seq_view(qidx_ref, sq, seqs), _seq_view(miscq_ref, sq, seqs), _seq_view(misck_ref, sq, seqs),
                   pages[sq * n_pages:(sq + 1) * n_pages], _seq_view(o_ref, sq, seqs), kidx_ref,
                   n_pages=n_pages, **statics)


def _index_one(qidx_ref, miscq_ref, misck_ref, ipage_refs, o_ref, kidx_ref, *, n_pages, t_new, tq, pos0, widths):
    qi = pl.program_id(1)
    lk = o_ref.shape[1]
    n_keys = n_pages * PAGE + t_new
    kbase = pos0 - n_pages * PAGE
    q0 = pos0 if t_new == tq else pos0 + qi * tq

    @pl.when(qi == 0)
    def _():
        for c, r in enumerate(ipage_refs):
            kidx_ref[:, c * PAGE:(c + 1) * PAGE] = r[0].astype(BF16)
        for c, x in enumerate(_new_chunks(misck_ref, t_new)):
            cols = slice((n_pages + c) * PAGE, (n_pages + c + 1) * PAGE)
            kidx_ref[:, cols] = x.T[MISC_KI:MISC_KI + IDX_DIM, :].astype(BF16)

    def tile(w, maybe_first):
        del maybe_first
        qpos = q0 + lax.broadcasted_iota(jnp.int32, (tq, 1), 0)
        col = lax.broadcasted_iota(jnp.int32, (1, w), 1)
        visible = (qpos - (kbase + col) >= 0) & (col < n_keys)
        q = jnp.concatenate([qidx_ref[:, hh * IDX_DIM:(hh + 1) * IDX_DIM] for hh in range(IDX_HEADS)], 0)
        sc = jnp.dot(q.astype(BF16), kidx_ref[:, 0:w], preferred_element_type=F32)
        score = jnp.zeros((tq, w), F32)
        for hh in range(IDX_HEADS):
            wi = miscq_ref[:, MISC_WI + hh:MISC_WI + hh + 1] * (IDX_HEADS ** -0.5)
            score = score + jnp.maximum(sc[hh * tq:(hh + 1) * tq] * (IDX_DIM ** -0.5), 0.0) * wi
        o_ref[:, 0:w] = jnp.where(visible, score, NEG_INF)
        if w < lk:
            o_ref[:, w:] = jnp.full((tq, lk - w), NEG_INF, F32)

    _on_causal_width(qi, tq, widths, tile)


def _topk_body(s_ref, m_ref, key_ref, *, k, nq, tr, widths):
    lk = s_ref.shape[1]
    assert lk <= 4096
    neg_key = int(np.array(NEG_INF, np.float32).view(np.int32)) ^ 0x7FFFFFFF
    kf = float(k)

    def tile(w, maybe_first):
        del maybe_first
        bits = lax.bitcast_convert_type(s_ref[:, 0:w] + 0.0, jnp.int32)
        key_ref[:, 0:w] = jnp.where(bits >= 0, bits, bits ^ 0x7FFFFFFF)
        col = lax.broadcasted_iota(jnp.int32, (1, w), 1)
        unseen = float(lk - w)

        groups = [slice(a * (tr // TOPK_ROW_GROUPS), (a + 1) * (tr // TOPK_ROW_GROUPS)) for a in range(TOPK_ROW_GROUPS)]
        zeros = tuple(jnp.zeros((tr // TOPK_ROW_GROUPS, 1), jnp.int32) for _ in groups)

        def thr_step(i, tus):
            out = []
            for rows, tu in zip(groups, tus):
                cand = tu | jnp.left_shift(jnp.int32(1), 31 - i)
                cs = cand ^ INT_MIN
                cnt = jnp.sum(jnp.where(key_ref[rows, 0:w] >= cs, 1.0, 0.0), -1, keepdims=True)
                cnt = cnt + jnp.where(cs <= neg_key, unseen, 0.0)
                out.append(jnp.where(cnt >= kf, cand, tu))
            return tuple(out)

        thr = jnp.concatenate(lax.fori_loop(0, 32, thr_step, zeros, unroll=4), 0) ^ INT_MIN
        key = key_ref[:, 0:w]
        above = key > thr
        tied = key == thr
        need = kf - jnp.sum(jnp.where(above, 1.0, 0.0), -1, keepdims=True)

        def tie_step(i, j0s):
            out = []
            for rows, j0 in zip(groups, j0s):
                cand = j0 | jnp.left_shift(jnp.int32(1), 11 - i)
                hit = (key_ref[rows, 0:w] == thr[rows]) & (col < cand)
                cnt = jnp.sum(jnp.where(hit, 1.0, 0.0), -1, keepdims=True)
                out.append(jnp.where(cnt < need[rows], cand, j0))
            return tuple(out)

        m_ref[:, 0:w] = jnp.where(above | tied, 1.0, 0.0)
        n_tied = jnp.sum(jnp.where(tied, 1.0, 0.0), -1, keepdims=True)
        excess = jnp.max(jnp.where(thr > neg_key, n_tied - need, 0.0))

        @pl.when(excess > 0.0)
        def _():
            j0 = jnp.concatenate(lax.fori_loop(0, 12, tie_step, zeros, unroll=4), 0)
            k2 = key_ref[:, 0:w]
            m_ref[:, 0:w] = jnp.where((k2 > thr) | ((k2 == thr) & (col <= j0)), 1.0, 0.0)
        if w < lk:
            m_ref[:, w:] = jnp.zeros((tr, lk - w), F32)

    _on_causal_width(pl.program_id(0) % nq, tr, widths, tile)


class _Group:
    def __init__(self, row0, n_seq, t_new, tq, pos0, n_pages, seqs=1):
        assert t_new % tq == 0 and row0 % (seqs * tq) == 0 and row0 % (seqs * t_new) == 0
        assert n_seq % seqs == 0 and (seqs == 1 or t_new == tq)
        self.seqs = seqs
        assert pos0 == n_pages * PAGE or n_pages * PAGE < pos0
        assert t_new == tq or (tq == PAGE and pos0 == 0)
        assert t_new % PAGE == 0 or t_new % PAGE < NSA_BLOCK
        self.row0, self.n_seq, self.t_new, self.tq, self.pos0, self.n_pages = row0, n_seq, t_new, tq, pos0, n_pages
        self.nq = t_new // tq
        self.rows = n_seq * t_new
        self.lk = (n_pages + -(-t_new // PAGE)) * PAGE
        self.n_keys = n_pages * PAGE + t_new

    def grid(self):
        return (self.n_seq // self.seqs, self.nq)

    def q_spec(self, width, col):
        rows = self.seqs * self.tq
        return pl.BlockSpec((rows, width), lambda b, qi, pt: (self.row0 // rows + b * self.nq + qi, col // width))

    def seq_spec(self, width, col):
        rows = self.seqs * self.t_new
        return pl.BlockSpec((rows, width), lambda b, qi, pt: (self.row0 // rows + b, col // width))

    def page_specs(self, shape):
        return [pl.BlockSpec((1,) + shape, lambda b, qi, pt, sq=sq, p=p: (pt[b * self.seqs + sq, p], 0, 0))
                for sq in range(self.seqs) for p in range(self.n_pages)]

    def page_args(self, pool):
        return [pool] * (self.seqs * self.n_pages)

    def out_spec(self, width):
        return pl.BlockSpec((self.seqs * self.tq, width), lambda b, qi, pt: (b * self.nq + qi, 0))

    def statics(self):
        return dict(seqs=self.seqs, n_pages=self.n_pages, t_new=self.t_new, tq=self.tq, pos0=self.pos0)

    def widths(self):
        if self.nq == 1:
            return (self.lk,)
        step = CAUSAL_WIDTH_STEP
        assert self.lk % step == 0
        return tuple(range(step, self.lk + 1, step))


def _cmp_bias_table(rel_bias, grp):
    lane0 = 0 if grp.nq == 1 else CMP_BIAS_LANE0
    tab = rel_bias[_BUCKETS]
    pieces, n_far = [], 0
    for lane in range(LANES + 1):
        d0 = grp.pos0 - (NSA_BLOCK * (lane - lane0 + 1) - 1)
        plain = lane < LANES and (d0 >= PAGE or d0 + grp.tq - 1 < 0)
        if plain:
            n_far += 1
            continue
        if n_far:
            pieces.append(jnp.broadcast_to(rel_bias[N_BUCKETS - 1], (grp.tq, n_far, rel_bias.shape[1])))
            n_far = 0
        if lane < LANES:
            pieces.append(tab[np.clip(d0 + np.arange(grp.tq), 0, BAND - 1)][:, None, :])
    table = jnp.transpose(jnp.concatenate(pieces, 1), (2, 0, 1))
    return table.reshape(-1, GROUP_HEADS * grp.tq, LANES)


def nsa_compress(grp, z, page_table, pool, wexp, bias):
    kv_w = KV_GROUPS * 2 * HEAD_DIM
    qw = KV_GROUPS * GROUP_HEADS * HEAD_DIM
    const2 = lambda b, qi, pt: (0, 0)
    return pl.pallas_call(
        functools.partial(_cmp_body, **grp.statics()),
        grid_spec=pltpu.PrefetchScalarGridSpec(
            num_scalar_prefetch=1,
            grid=grp.grid(),
            in_specs=[grp.q_spec(qw, COL_QN), grp.seq_spec(kv_w, COL_KVC)] + grp.page_specs(KV_PAGE)
            + [pl.BlockSpec((PAGE, kv_w), const2), pl.BlockSpec(bias.shape, lambda b, qi, pt: (0, 0, 0))],
            out_specs=[grp.out_spec(qw), grp.out_spec(KV_GROUPS * LANES)],
            scratch_shapes=[pltpu.VMEM((LANES, kv_w), F32),
                            pltpu.VMEM((KV_GROUPS, LANES, HEAD_DIM), BF16),
                            pltpu.VMEM((KV_GROUPS, LANES, HEAD_DIM), BF16)]),
        out_shape=[jax.ShapeDtypeStruct((grp.rows, qw), F32),
                   jax.ShapeDtypeStruct((grp.rows, KV_GROUPS * LANES), F32)],
        compiler_params=_cparams("parallel", "arbitrary"),
        name="nsa_compress",
    )(page_table, z, z, *grp.page_args(pool), wexp, bias)


def sparse_attention(mode, grp, z, page_table, pool, band, *, q_col, kv_col, mask=None):
    kv_w = KV_GROUPS * 2 * HEAD_DIM
    qw = KV_GROUPS * GROUP_HEADS * HEAD_DIM
    in_specs = [grp.q_spec(qw, q_col), grp.seq_spec(kv_w, kv_col)] + grp.page_specs(KV_PAGE)
    args = [z, z] + grp.page_args(pool)
    if mode in ("sel", "mask"):
        in_specs.append(grp.out_spec(mask.shape[1]))
        args.append(mask)
    in_specs.append(pl.BlockSpec((KV_GROUPS, GROUP_HEADS * grp.tq, BAND), lambda b, qi, pt: (0, 0, 0)))
    args.append(band)
    widths = grp.widths()
    s_cols = max(widths) if (mode != "win" or grp.nq == 1) else NSA_WINDOW + PAGE
    rows = GROUP_HEADS * grp.tq
    scratch = [pltpu.VMEM((KV_GROUPS, grp.lk, HEAD_DIM), BF16),
               pltpu.VMEM((KV_GROUPS, grp.lk, HEAD_DIM), BF16),
               pltpu.VMEM((KV_GROUPS, rows, s_cols), F32),
               pltpu.VMEM((KV_GROUPS, grp.tq, s_cols), F32)]
    if grp.tq % BF16_SUBLANES == 0:
        scratch.append(pltpu.VMEM((KV_GROUPS, rows, s_cols), BF16))
    return pl.pallas_call(
        functools.partial(_attn_body, mode=mode, widths=widths, **grp.statics()),
        grid_spec=pltpu.PrefetchScalarGridSpec(
            num_scalar_prefetch=1,
            grid=grp.grid(),
            in_specs=in_specs,
            out_specs=grp.out_spec(qw),
            scratch_shapes=scratch),
        out_shape=jax.ShapeDtypeStruct((grp.rows, qw), F32),
        compiler_params=_cparams("parallel", "arbitrary"),
        name="sparse_attention_" + mode,
    )(page_table, *args)


def dsa_index_scores(grp, z, page_table, idx_pool):
    return pl.pallas_call(
        functools.partial(_index_body, widths=grp.widths(), **grp.statics()),
        grid_spec=pltpu.PrefetchScalarGridSpec(
            num_scalar_prefetch=1,
            grid=grp.grid(),
            in_specs=[grp.q_spec(IDX_HEADS * IDX_DIM, COL_QI), grp.q_spec(LANES, COL_MISC),
                      grp.seq_spec(LANES, COL_MISC)] + grp.page_specs((IDX_DIM, PAGE)),
            out_specs=grp.out_spec(grp.lk),
            scratch_shapes=[pltpu.VMEM((IDX_DIM, grp.lk), BF16)]),
        out_shape=jax.ShapeDtypeStruct((grp.rows, grp.lk), F32),
        compiler_params=_cparams("parallel", "arbitrary"),
        name="dsa_index_scores",
    )(page_table, z, z, z, *grp.page_args(idx_pool))


def topk_mask(grp, scores, k):
    rows, lk = scores.shape
    tr = PAGE
    assert rows % tr == 0 and (grp.nq == 1 or grp.tq == tr)
    blk = pl.BlockSpec((tr, lk), lambda i: (i, 0))
    return pl.pallas_call(
        functools.partial(_topk_body, k=k, nq=grp.nq, tr=tr, widths=grp.widths()),
        grid=(rows // tr,),
        in_specs=[blk],
        out_specs=blk,
        out_shape=jax.ShapeDtypeStruct((rows, lk), F32),
        scratch_shapes=[pltpu.VMEM((tr, lk), jnp.int32)],
        compiler_params=_cparams("parallel"),
        name="topk_mask",
    )(scores)


def _combine_body(oc_ref, os_ref, ow_ref, od_ref, misc_ref, y_ref):
    n_heads = KV_GROUPS * GROUP_HEADS
    gates = jax.nn.sigmoid(misc_ref[:, MISC_GATES:MISC_GATES + 3 * n_heads])
    for h in range(n_heads):
        hs = slice(h * HEAD_DIM, (h + 1) * HEAD_DIM)
        o = (gates[:, 3 * h:3 * h + 1] * oc_ref[:, hs] + gates[:, 3 * h + 1:3 * h + 2] * os_ref[:, hs]
             + gates[:, 3 * h + 2:3 * h + 3] * ow_ref[:, hs])
        y_ref[:, hs] = o.astype(y_ref.dtype)
    y_ref[:, n_heads * HEAD_DIM:] = od_ref[...].astype(y_ref.dtype)


def nsa_dsa_combine(o_c, o_s, o_w, o_d, z, row0, *, tm):
    m, w = o_c.shape
    assert m % tm == 0 and row0 % tm == 0
    blk = pl.BlockSpec((tm, w), lambda i: (i, 0))
    return pl.pallas_call(
        _combine_body,
        grid=(m // tm,),
        in_specs=[blk, blk, blk, blk, pl.BlockSpec((tm, LANES), lambda i: (row0 // tm + i, COL_MISC // LANES))],
        out_specs=pl.BlockSpec((tm, 2 * w), lambda i: (i, 0)),
        out_shape=jax.ShapeDtypeStruct((m, 2 * w), BF16),
        compiler_params=_cparams("parallel"),
        name="nsa_dsa_combine",
    )(o_c, o_s, o_w, o_d, z)


def _band_tiles(rel_bias, tq):
    delta = (rel_bias[_BUCKETS] - rel_bias[N_BUCKETS - 1]).T
    rev = jnp.concatenate([delta[:, ::-1], jnp.zeros((delta.shape[0], PAGE), delta.dtype)], 1)
    tiles = jnp.stack([rev[:, PAGE - 1 - i:PAGE - 1 - i + BAND] for i in range(tq)], 1)
    return tiles.reshape(-1, GROUP_HEADS * tq, BAND) * LOG2E


def _widen_cd_w_in(w):
    sizes = (1024, 512, 512, 512, 24, 1024, 512, 512, 64, 8)
    q_n, kv_c, kv_s, kv_w, gates, q_d, kv_d, q_i, k_i, w_i = jnp.split(w, np.cumsum(sizes)[:-1].tolist(), axis=-1)
    cols = [q_n, q_d, kv_c, kv_s, kv_w, kv_d, q_i, k_i, gates, w_i]
    used = sum(c.shape[-1] for c in cols)
    return jnp.concatenate(cols + [jnp.zeros(w.shape[:-1] + (NZ - used,), w.dtype)], axis=-1)


def kernel(x_prompt, x_sample, state_conv, state_pool, cache_nsa_cmp, cache_nsa_sel, cache_nsa_win, cache_dsa_kv, cache_dsa_idx, page_table, norm_mix, norm_ffn, norm_final, ab_w_in, ab_conv_w, ab_conv_b, ab_ln_g, ab_ln_b, ab_pool_w, ab_pool_scale, ab_w_out, cd_w_in, cd_w_cmp, cd_w_out, rel_bias, ffn_w1, ffn_w2):
    bp, tp, d_model = x_prompt.shape
    bs, ts, _ = x_sample.shape
    mp, ms = bp * tp, bs * ts
    depth = norm_mix.shape[0]
    n_pages = page_table.shape[1]
    n_pool = cache_nsa_cmp.shape[1]
    past_len = n_pages * PAGE
    assert cache_nsa_cmp.shape[2] == PAGE
    win_len = cache_nsa_win.shape[2]
    assert win_len % PAGE == 0 and win_len == NSA_WINDOW and tp >= NSA_WINDOW
    kv_w = KV_GROUPS * 2 * HEAD_DIM

    xs = [x_prompt.reshape(mp, d_model), x_sample.reshape(ms, d_model)]
    grp_p = _Group(0, bp, tp, PAGE, 0, 0)
    grp_s = _Group(mp, bs, ts, ts, past_len, n_pages)
    grp_sc = _Group(mp, bs, ts, ts, past_len, n_pages, SAMPLE_SEQS_PER_STEP)
    grp_sw = _Group(mp, bs, ts, ts, past_len, win_len // PAGE, SAMPLE_SEQS_PER_STEP)
    no_pages = jnp.zeros((1, 1), jnp.int32)
    win_pages = jnp.arange(bs * (win_len // PAGE), dtype=jnp.int32).reshape(bs, win_len // PAGE)

    outs = {k: [] for k in ("conv_p", "conv_s", "pool_p", "pool_s", "cmp_p", "cmp_s", "sel_p", "sel_s",
                            "win_p", "win_s", "dsa_p", "dsa_s", "idx_p", "idx_s")}
    w_bf16 = {"ab_w_in": ab_w_in, "ab_w_out": ab_w_out, "cd_w_in": _widen_cd_w_in(cd_w_in), "cd_w_out": cd_w_out,
              "ffn_w1": ffn_w1, "ffn_w2": ffn_w2}
    w_bf16 = {k: v.astype(BF16) for k, v in w_bf16.items()}
    y_p = y_s = None
    for i in range(depth):
        j = i // 2
        if i % 2 == 0:
            d_conv = ab_conv_w.shape[2]
            z = norm_matmul(xs, norm_mix[i], w_bf16["ab_w_in"], j)
            mid_p, u_p = ab_mid_prompt(z, bp, tp, ab_conv_w[j], ab_conv_b[j], ab_ln_g[j], ab_ln_b[j],
                                       ab_pool_w[j], ab_pool_scale[j])
            mid_s, conv_s, pool_s = ab_mid_step(z, mp, bs, ts, past_len, state_conv[j], state_pool[j], ab_conv_w[j],
                                                ab_conv_b[j], ab_ln_g[j], ab_ln_b[j], ab_pool_w[j], ab_pool_scale[j])
            xs = [matmul_residual([mid_p, mid_s], w_bf16["ab_w_out"], j, xs)]
            outs["conv_p"].append(u_p.reshape(bp, tp, d_conv)[:, tp - CONV_BUF:])
            outs["conv_s"].append(conv_s)
            outs["pool_p"].append(jnp.stack([z[(b + 1) * tp - POOL_BUF:(b + 1) * tp, 2 * d_conv:] for b in range(bp)]))
            outs["pool_s"].append(pool_s)
        else:
            z = norm_matmul(xs, norm_mix[i], w_bf16["cd_w_in"], j)
            nsa_bias = rel_bias[:, :KV_GROUPS * GROUP_HEADS]
            band_p, band_s = _band_tiles(rel_bias, grp_p.tq), _band_tiles(rel_bias, grp_s.tq)
            wexp = jnp.tile(jnp.repeat(jnp.transpose(cd_w_cmp[j], (1, 0, 2)).reshape(NSA_BLOCK, 2 * KV_GROUPS),
                                       HEAD_DIM, axis=1), (PAGE // NSA_BLOCK, 1))
            pt = page_table + j * n_pool
            pools = [c.reshape((-1,) + KV_PAGE) for c in (cache_nsa_cmp, cache_nsa_sel, cache_dsa_kv)]
            idx_pool = jnp.swapaxes(cache_dsa_idx, 2, 3).reshape(-1, IDX_DIM, PAGE)
            win_pool = cache_nsa_win.reshape((-1,) + KV_PAGE)
            wpt = win_pages + j * bs * (win_len // PAGE)
            mids = []
            for grp, gc, gw, ptab, wtab, band in ((grp_p, grp_p, grp_p, no_pages, no_pages, band_p),
                                                  (grp_s, grp_sc, grp_sw, pt, wpt, band_s)):
                o_c, msel = nsa_compress(gc, z, ptab, pools[0], wexp, _cmp_bias_table(nsa_bias, gc))
                o_s = sparse_attention("sel", gc, z, ptab, pools[1], band[:KV_GROUPS],
                                       q_col=COL_QN, kv_col=COL_KVS, mask=msel)
                o_w = sparse_attention("win", gw, z, wtab, win_pool, band[:KV_GROUPS],
                                       q_col=COL_QN, kv_col=COL_KVW)
                top = topk_mask(grp, dsa_index_scores(grp, z, ptab, idx_pool), min(DSA_TOPK, grp.n_keys // 4))
                o_d = sparse_attention("mask", gc, z, ptab, pools[2], band[KV_GROUPS:],
                                       q_col=COL_QD, kv_col=COL_KVD, mask=top)
                mids.append(nsa_dsa_combine(o_c, o_s, o_w, o_d, z, grp.row0, tm=min(512, grp.rows)))
            xs = [matmul_residual(mids, w_bf16["cd_w_out"], j, xs)]

            def kv_out(col, width, tail):
                seg = z[:, col:col + width]
                return seg[:mp].reshape((bp, tp) + tail), seg[mp:].reshape((bs, ts) + tail)

            kv_tail = (2, KV_GROUPS, HEAD_DIM)
            for name, col in (("cmp", COL_KVC), ("sel", COL_KVS), ("dsa", COL_KVD)):
                p_new, s_new = kv_out(col, kv_w, kv_tail)
                outs[name + "_p"].append(p_new)
                outs[name + "_s"].append(s_new)
            w_p, w_s = kv_out(COL_KVW, kv_w, kv_tail)
            outs["win_p"].append(w_p[:, tp - NSA_WINDOW:])
            outs["win_s"].append(jnp.concatenate([cache_nsa_win[j], w_s], 1)[:, ts:])
            i_p, i_s = kv_out(COL_MISC + MISC_KI, IDX_DIM, (IDX_DIM,))
            outs["idx_p"].append(i_p)
            outs["idx_s"].append(i_s)
        a = norm_matmul(xs, norm_ffn[i], w_bf16["ffn_w1"], i, relu2=True, out_dtype=BF16)
        if i == depth - 1:
            y_p, y_s = matmul_residual([a], w_bf16["ffn_w2"], i, xs, norm_final, split_out=(mp, ms))
        else:
            xs = [matmul_residual([a], w_bf16["ffn_w2"], i, xs)]

    st = {k: jnp.stack(v) for k, v in outs.items()}
    return (y_p.reshape(bp, tp, d_model), y_s.reshape(bs, ts, d_model),
            st["conv_p"], st["conv_s"], st["pool_p"], st["pool_s"], st["cmp_p"], st["cmp_s"],
            st["sel_p"], st["sel_s"], st["win_p"], st["win_s"], st["dsa_p"], st["dsa_s"],
            st["idx_p"], st["idx_s"])
```

```python
import functools
import math

import numpy as np
import jax
import jax.numpy as jnp
from jax import lax
from jax.experimental import pallas as pl
from jax.experimental.pallas import tpu as pltpu

F32 = jnp.float32
BF16 = jnp.bfloat16

EPS = 1e-6
NEG_INF = -1e30
LOG2E = math.log2(math.e)
HEAD_DIM = 128
LANES = 128
BF16_SUBLANES = 16
CONV_WIDTH = 31
CONV_BUF = CONV_WIDTH - 1
POOL_WINDOWS = (2, 4, 8, 16)
POOL_BUF = max(POOL_WINDOWS) - 1
HALO = 32
VMEM_LIMIT = 56 * 1024 * 1024


def _cparams(*sem):
    return pltpu.CompilerParams(dimension_semantics=sem, vmem_limit_bytes=VMEM_LIMIT)


class _Rows:
    def __init__(self, arrays, tm):
        self.arrays = list(arrays)
        self.tm = tm
        assert all(a.shape[0] % tm == 0 for a in self.arrays)
        self.tiles = [a.shape[0] // tm for a in self.arrays]
        self.n_tiles = sum(self.tiles)
        self.n = len(self.arrays)

    def specs(self, width, col):
        out, t0 = [], 0
        for nt in self.tiles:
            out.append(pl.BlockSpec((self.tm, width), lambda i, j, t0=t0, nt=nt: (jnp.clip(i - t0, 0, nt - 1), col(j))))
            t0 += nt
        return out

    def select(self, i, refs, fn):
        if self.n == 1:
            fn(refs[0])
            return
        t0 = 0
        for nt, ref in zip(self.tiles, refs):
            pl.when((i >= t0) & (i < t0 + nt))(functools.partial(fn, ref))
            t0 += nt


def _rmsnorm_rows(x, g):
    return (x * lax.rsqrt(jnp.mean(x * x, -1, keepdims=True) + EPS)) * g


def _norm_matmul_body(*refs, rows, relu2):
    x_refs = refs[:rows.n]
    g_ref, w_ref, o_ref, h_ref = refs[rows.n:]

    @pl.when(pl.program_id(1) == 0)
    def _():
        def norm(x_ref):
            h_ref[...] = _rmsnorm_rows(x_ref[...], g_ref[...]).astype(BF16)

        rows.select(pl.program_id(0), x_refs, norm)

    y = jnp.dot(h_ref[...], w_ref[...].astype(BF16), preferred_element_type=F32)
    if relu2:
        y = jnp.square(jnp.maximum(y, 0.0))
    o_ref[...] = y.astype(o_ref.dtype)


def norm_matmul(xs, g, w, layer, *, relu2=False, out_dtype=F32, tm=1024, tn=1024):
    rows = _Rows(xs, tm)
    _, d, n = w.shape
    assert n % tn == 0
    return pl.pallas_call(
        functools.partial(_norm_matmul_body, rows=rows, relu2=relu2),
        grid=(rows.n_tiles, n // tn),
        in_specs=rows.specs(d, lambda j: 0) + [pl.BlockSpec((1, d), lambda i, j: (0, 0)),
                                               pl.BlockSpec((None, d, tn), lambda i, j: (layer, 0, j))],
        out_specs=pl.BlockSpec((tm, tn), lambda i, j: (i, j)),
        out_shape=jax.ShapeDtypeStruct((rows.n_tiles * tm, n), out_dtype),
        scratch_shapes=[pltpu.VMEM((tm, d), BF16)],
        compiler_params=_cparams("parallel", "arbitrary"),
        name="norm_matmul",
    )(*rows.arrays, g.reshape(1, d), w)


def _matmul_residual_body(*refs, a_rows, r_rows, o_rows, final_norm):
    a_refs, refs = refs[:a_rows.n], refs[a_rows.n:]
    w_ref, refs = refs[0], refs[1:]
    r_refs, refs = refs[:r_rows.n], refs[r_rows.n:]
    if final_norm:
        g_ref, refs = refs[0], refs[1:]
    o_refs, acc_ref = refs[:o_rows.n], refs[o_rows.n]
    i, k = pl.program_id(0), pl.program_id(1)

    @pl.when(k == 0)
    def _():
        acc_ref[...] = jnp.zeros_like(acc_ref)

    def accumulate(a_ref):
        acc_ref[...] += jnp.dot(a_ref[...], w_ref[...], preferred_element_type=F32)

    a_rows.select(i, a_refs, accumulate)

    @pl.when(k == pl.num_programs(1) - 1)
    def _():
        def add_residual(r_ref):
            acc_ref[...] += r_ref[...]

        def write(o_ref):
            o = acc_ref[...]
            o_ref[...] = _rmsnorm_rows(o, g_ref[...]) if final_norm else o

        r_rows.select(i, r_refs, add_residual)
        o_rows.select(i, o_refs, write)


def matmul_residual(a_list, w, layer, r_list, g_final=None, *, split_out=None, tm=512, tk=2048):
    a_rows, r_rows = _Rows(a_list, tm), _Rows(r_list, tm)
    _, kdim, n = w.shape
    m = a_rows.n_tiles * tm
    assert kdim % tk == 0 and r_rows.n_tiles == a_rows.n_tiles
    final_norm = g_final is not None
    o_rows = _Rows([jax.ShapeDtypeStruct((r, n), F32) for r in (split_out or (m,))], tm)
    assert o_rows.n_tiles == a_rows.n_tiles
    in_specs = (a_rows.specs(tk, lambda k: k) + [pl.BlockSpec((None, tk, n), lambda i, k: (layer, k, 0))]
                + r_rows.specs(n, lambda k: 0))
    args = a_rows.arrays + [w] + r_rows.arrays
    if final_norm:
        in_specs.append(pl.BlockSpec((1, n), lambda i, k: (0, 0)))
        args.append(g_final.reshape(1, n))
    out = pl.pallas_call(
        functools.partial(_matmul_residual_body, a_rows=a_rows, r_rows=r_rows, o_rows=o_rows, final_norm=final_norm),
        grid=(a_rows.n_tiles, kdim // tk),
        in_specs=in_specs,
        out_specs=o_rows.specs(n, lambda k: 0),
        out_shape=o_rows.arrays,
        scratch_shapes=[pltpu.VMEM((tm, n), F32)],
        compiler_params=_cparams("parallel", "arbitrary"),
        name="matmul_residual",
    )(*args)
    return out if split_out else out[0]


def _layernorm_silu(c, g, b):
    mu = jnp.mean(c, -1, keepdims=True)
    xc = c - mu
    y = xc * lax.rsqrt(jnp.mean(xc * xc, -1, keepdims=True) + EPS)
    y = y * g + b
    return y * jax.nn.sigmoid(y)


def _ab_mid_body(z_ref, zp_ref, cw_ref, cb_ref, lg_ref, lb_ref, pw_ref, ps_ref, y_ref, u_ref,
                 ext_ref, vext_ref, conv_ref, *, tt, d_conv, d_pool):
    ti = pl.program_id(1)
    keep = (ti > 0).astype(F32)
    a_p = zp_ref[:, 0:d_conv]
    g_p = zp_ref[:, d_conv:2 * d_conv]
    ext_ref[0:HALO, :] = a_p * jax.nn.sigmoid(g_p) * keep
    vext_ref[0:HALO, :] = zp_ref[:, 2 * d_conv:] * keep
    u = z_ref[:, 0:d_conv] * jax.nn.sigmoid(z_ref[:, d_conv:2 * d_conv])
    ext_ref[HALO:, :] = u
    u_ref[...] = u
    vext_ref[HALO:, :] = z_ref[:, 2 * d_conv:]

    off = HALO - CONV_BUF
    sub = 8
    for c in range(d_conv // LANES):
        cs = slice(c * LANES, (c + 1) * LANES)
        acc = jnp.zeros((tt, LANES), F32)
        for s in range(sub):
            n = tt if s == 0 else tt + sub
            part = jnp.zeros((n, LANES), F32)
            for j in range(CONV_WIDTH):
                if (off + j) % sub == s:
                    start = off + j - s
                    part = part + cw_ref[j:j + 1, cs] * ext_ref[start:start + n, cs]
            acc = acc + part[s:s + tt]
        conv_ref[:, cs] = acc + cb_ref[:, cs]
    y_ref[:, 0:d_conv] = _layernorm_silu(conv_ref[...], lg_ref[...], lb_ref[...]).astype(y_ref.dtype)

    pos = ti * tt + lax.broadcasted_iota(jnp.int32, (tt, 1), 0)
    pg = d_pool // len(POOL_WINDOWS)
    for gi, w in enumerate(POOL_WINDOWS):
        gs = slice(gi * pg, (gi + 1) * pg)
        tok = vext_ref[HALO:, gs]
        acc = tok
        for i in range(1, w):
            acc = acc + vext_ref[HALO - i:HALO - i + tt, gs]
        cnt = jnp.minimum(pos + 1, w).astype(F32)
        d = acc / cnt - tok
        yp = jnp.dot(d.astype(BF16), pw_ref[gi], preferred_element_type=F32) * ps_ref[:, gs]
        y_ref[:, d_conv + gi * pg:d_conv + (gi + 1) * pg] = yp.astype(y_ref.dtype)


def ab_mid_prompt(z, n_seq, t_len, conv_w, conv_b, ln_g, ln_b, pool_w, pool_scale, *, tt=256):
    d_conv = conv_w.shape[1]
    d_pool = pool_scale.shape[0]
    nt = t_len // tt
    hb = tt // HALO
    row = lambda b, t: (b * nt + t, 0)
    const = lambda b, t: (0, 0)
    return pl.pallas_call(
        functools.partial(_ab_mid_body, tt=tt, d_conv=d_conv, d_pool=d_pool),
        grid=(n_seq, nt),
        in_specs=[pl.BlockSpec((tt, z.shape[1]), row),
                  pl.BlockSpec((HALO, z.shape[1]), lambda b, t: (jnp.maximum((b * nt + t) * hb - 1, 0), 0)),
                  pl.BlockSpec(conv_w.shape, const),
                  pl.BlockSpec((1, d_conv), const),
                  pl.BlockSpec((1, d_conv), const),
                  pl.BlockSpec((1, d_conv), const),
                  pl.BlockSpec(pool_w.shape, lambda b, t: (0, 0, 0)),
                  pl.BlockSpec((1, d_pool), const)],
        out_specs=[pl.BlockSpec((tt, d_conv + d_pool), row),
                   pl.BlockSpec((tt, d_conv), row)],
        out_shape=[jax.ShapeDtypeStruct((n_seq * t_len, d_conv + d_pool), BF16),
                   jax.ShapeDtypeStruct((n_seq * t_len, d_conv), F32)],
        scratch_shapes=[pltpu.VMEM((HALO + tt, d_conv), F32),
                        pltpu.VMEM((HALO + tt, d_pool), F32),
                        pltpu.VMEM((tt, d_conv), F32)],
        compiler_params=_cparams("parallel", "parallel"),
        name="ab_mid_prompt",
    )(z, z, conv_w, conv_b.reshape(1, -1), ln_g.reshape(1, -1), ln_b.reshape(1, -1),
      pool_w.astype(BF16), pool_scale.reshape(1, -1))


def _ab_mid_step_body(z_ref, sc_ref, sp_ref, cw_ref, cb_ref, lg_ref, lb_ref, pw_ref, ps_ref,
                      y_ref, nc_ref, np_ref, ext_ref, vext_ref, *, nb, t, pos0, d_conv, d_pool):
    e0 = HALO - CONV_BUF
    p0 = 16 - POOL_BUF
    z = z_ref[...].reshape(nb, t, z_ref.shape[1])
    u = z[:, :, 0:d_conv] * jax.nn.sigmoid(z[:, :, d_conv:2 * d_conv])
    ext_ref[:, e0:HALO, :] = sc_ref[...]
    ext_ref[:, HALO:, :] = u
    vext_ref[:, p0:16, :] = sp_ref[...]
    vext_ref[:, 16:, :] = z[:, :, 2 * d_conv:]
    nc_ref[...] = ext_ref[:, HALO + t - CONV_BUF:, :]
    np_ref[...] = vext_ref[:, 16 + t - POOL_BUF:, :]

    acc = jnp.zeros((nb, t, d_conv), F32)
    for j in range(CONV_WIDTH):
        acc = acc + cw_ref[j:j + 1, :][None] * ext_ref[:, e0 + j:e0 + j + t, :]
    c = acc + cb_ref[...][None]
    yc = _layernorm_silu(c, lg_ref[...][None], lb_ref[...][None])
    y_ref[:, 0:d_conv] = yc.reshape(nb * t, d_conv).astype(y_ref.dtype)

    pg = d_pool // len(POOL_WINDOWS)
    for gi, w in enumerate(POOL_WINDOWS):
        gs = slice(gi * pg, (gi + 1) * pg)
        tok = vext_ref[:, 16:, gs]
        acc = tok
        for i in range(1, w):
            acc = acc + vext_ref[:, 16 - i:16 - i + t, gs]
        cnt = jnp.minimum(pos0 + 1 + lax.broadcasted_iota(jnp.int32, (1, t, 1), 1), w).astype(F32)
        d = (acc / cnt - tok).reshape(nb * t, pg)
        yp = jnp.dot(d.astype(BF16), pw_ref[gi], preferred_element_type=F32) * ps_ref[:, gs]
        y_ref[:, d_conv + gi * pg:d_conv + (gi + 1) * pg] = yp.astype(y_ref.dtype)


def ab_mid_step(z, row0, n_seq, t, pos0, state_conv, state_pool, conv_w, conv_b, ln_g, ln_b, pool_w,
                pool_scale, *, nb=16):
    d_conv = conv_w.shape[1]
    d_pool = pool_scale.shape[0]
    rb = nb * t
    assert row0 % rb == 0 and n_seq % nb == 0
    const = lambda i: (0, 0)
    seq3 = lambda i: (i, 0, 0)
    return pl.pallas_call(
        functools.partial(_ab_mid_step_body, nb=nb, t=t, pos0=pos0, d_conv=d_conv, d_pool=d_pool),
        grid=(n_seq // nb,),
        in_specs=[pl.BlockSpec((rb, z.shape[1]), lambda i: (row0 // rb + i, 0)),
                  pl.BlockSpec((nb, CONV_BUF, d_conv), seq3),
                  pl.BlockSpec((nb, POOL_BUF, d_pool), seq3),
                  pl.BlockSpec(conv_w.shape, const),
                  pl.BlockSpec((1, d_conv), const),
                  pl.BlockSpec((1, d_conv), const),
                  pl.BlockSpec((1, d_conv), const),
                  pl.BlockSpec(pool_w.shape, lambda i: (0, 0, 0)),
                  pl.BlockSpec((1, d_pool), const)],
        out_specs=[pl.BlockSpec((rb, d_conv + d_pool), lambda i: (i, 0)),
                   pl.BlockSpec((nb, CONV_BUF, d_conv), seq3),
                   pl.BlockSpec((nb, POOL_BUF, d_pool), seq3)],
        out_shape=[jax.ShapeDtypeStruct((n_seq * t, d_conv + d_pool), BF16),
                   jax.ShapeDtypeStruct((n_seq, CONV_BUF, d_conv), F32),
                   jax.ShapeDtypeStruct((n_seq, POOL_BUF, d_pool), F32)],
        scratch_shapes=[pltpu.VMEM((nb, HALO + t, d_conv), F32),
                        pltpu.VMEM((nb, 16 + t, d_pool), F32)],
        compiler_params=_cparams("parallel"),
        name="ab_mid_step",
    )(z, state_conv, state_pool, conv_w, conv_b.reshape(1, -1), ln_g.reshape(1, -1),
      ln_b.reshape(1, -1), pool_w.astype(BF16), pool_scale.reshape(1, -1))


N_BUCKETS = 32
MAX_DISTANCE = 128
NSA_BLOCK = 64
NSA_TOPN = 16
NSA_WINDOW = 512
DSA_TOPK = 256
IDX_HEADS = 8
IDX_DIM = 64
KV_GROUPS = 2
GROUP_HEADS = 4
PAGE = 128
BAND = 2 * PAGE
INT_MIN = -2 ** 31
KV_PAGE = (2 * KV_GROUPS * PAGE, HEAD_DIM)
TOPK_ROW_GROUPS = 4
SAMPLE_SEQS_PER_STEP = 4
CAUSAL_WIDTH_STEP = 2 * PAGE
CMP_BIAS_LANE0 = 64

COL_QN, COL_QD, COL_KVC, COL_KVS, COL_KVW, COL_KVD, COL_QI, COL_MISC = 0, 1024, 2048, 2560, 3072, 3584, 4096, 4608
MISC_KI, MISC_GATES, MISC_WI = 0, 64, 88
NZ = 5120


def _bucket_np(n):
    n = np.maximum(np.asarray(n, np.int32), 0)
    exact = N_BUCKETS // 2
    nf = np.maximum(n, 1).astype(np.float32)
    big = exact + (np.log(nf / np.float32(exact)) / np.float32(math.log(MAX_DISTANCE / exact))
                   * np.float32(N_BUCKETS - exact)).astype(np.int32)
    return np.where(n < exact, n, np.minimum(big, N_BUCKETS - 1))


_BUCKETS = _bucket_np(np.arange(BAND))
assert _BUCKETS[PAGE:].min() == N_BUCKETS - 1


def _softmax_rows(s, mask):
    s = jnp.where(mask, s, NEG_INF)
    m = jnp.max(s, -1, keepdims=True)
    p = jnp.where(mask, jnp.exp(s - m), 0.0)
    return p, jnp.sum(p, -1, keepdims=True)


def _dot_nt(a, b):
    return lax.dot_general(a, b, (((1,), (1,)), ((), ())), preferred_element_type=F32)


def _new_chunks(new_ref, t_new):
    chunks = [new_ref[c * PAGE:(c + 1) * PAGE, :] for c in range(t_new // PAGE)]
    rem = t_new % PAGE
    if rem:
        tail = new_ref[(t_new // PAGE) * PAGE:, :]
        chunks.append(jnp.concatenate([tail, jnp.zeros((PAGE - rem, tail.shape[1]), F32)], 0))
    return chunks


def _kv_chunks(page_refs, new_ref, t_new):
    n_parts = 2 * KV_GROUPS
    chunks = [[r[0, pl.ds(part, PAGE, stride=n_parts), :] for part in range(n_parts)] for r in page_refs]
    for x in _new_chunks(new_ref, t_new):
        chunks.append([x[:, part * HEAD_DIM:(part + 1) * HEAD_DIM] for part in range(n_parts)])
    return chunks


def _seq_view(ref, sq, seqs):
    n = ref.shape[0] // seqs
    return ref.at[pl.ds(sq * n, n)]


def _cmp_body(pt_ref, q_ref, kvn_ref, *rest, seqs, n_pages, **statics):
    del pt_ref
    pages, (wexp_ref, bias_ref, o_ref, msel_ref, *scratch) = rest[:seqs * n_pages], rest[seqs * n_pages:]
    for sq in range(seqs):
        _cmp_one(_seq_view(q_ref, sq, seqs), _seq_view(kvn_ref, sq, seqs), pages[sq * n_pages:(sq + 1) * n_pages],
                 wexp_ref, bias_ref, _seq_view(o_ref, sq, seqs), _seq_view(msel_ref, sq, seqs), *scratch,
                 n_pages=n_pages, **statics)


def _cmp_one(q_ref, kvn_ref, page_refs, wexp_ref, bias_ref, o_ref, msel_ref, comp_ref, ck_ref, cv_ref, *,
             n_pages, t_new, tq, pos0):
    qi = pl.program_id(1)
    n_keys = n_pages * PAGE + t_new
    n_cmp = n_keys // NSA_BLOCK
    n_sel = -(-n_keys // NSA_BLOCK)
    per = PAGE // NSA_BLOCK

    @pl.when(qi == 0)
    def _():
        comp_ref[...] = jnp.zeros_like(comp_ref)
        chunks = _kv_chunks(page_refs, kvn_ref, t_new)[:n_cmp // per]
        for part in range(2 * KV_GROUPS):
            cols = slice(part * HEAD_DIM, (part + 1) * HEAD_DIM)
            xw = jnp.concatenate([parts[part] * wexp_ref[:, cols] for parts in chunks], 0)
            comp_ref[0:per * len(chunks), cols] = xw.reshape(per * len(chunks), NSA_BLOCK, HEAD_DIM).sum(1)
        for g in range(KV_GROUPS):
            ck_ref[g] = comp_ref[:, g * HEAD_DIM:(g + 1) * HEAD_DIM].astype(BF16)
            cv_ref[g] = comp_ref[:, (KV_GROUPS + g) * HEAD_DIM:(KV_GROUPS + g + 1) * HEAD_DIM].astype(BF16)

    scale = HEAD_DIM ** -0.5
    rows = GROUP_HEADS * tq
    blk = lax.broadcasted_iota(jnp.int32, (1, LANES), 1)
    q0 = pos0 + qi * tq
    assert tq & (tq - 1) == 0
    qpos_st = q0 + (lax.broadcasted_iota(jnp.int32, (rows, 1), 0) & (tq - 1))
    mask = (qpos_st - ((blk + 1) * NSA_BLOCK - 1) >= 0) & (blk < n_cmp)
    cur = (q0 + lax.broadcasted_iota(jnp.int32, (tq, 1), 0)) // NSA_BLOCK
    scores = []
    for g in range(KV_GROUPS):
        heads = [g * GROUP_HEADS + r for r in range(GROUP_HEADS)]
        bias = bias_ref[g]
        if t_new != tq:
            bias = pltpu.roll(bias, qi * (tq // NSA_BLOCK) + (LANES - CMP_BIAS_LANE0), 1)
        q = jnp.concatenate([q_ref[:, h * HEAD_DIM:(h + 1) * HEAD_DIM] for h in heads], 0).astype(BF16)
        scores.append(_dot_nt(q, ck_ref[g]) * scale + bias)
    probs = []
    for g in range(KV_GROUPS):
        p, l = _softmax_rows(scores[g], mask)
        probs.append(p / jnp.maximum(l, 1e-30))
    for g in range(KV_GROUPS):
        o = jnp.dot(probs[g].astype(BF16), cv_ref[g], preferred_element_type=F32)
        for r in range(GROUP_HEADS):
            h = g * GROUP_HEADS + r
            o_ref[:, h * HEAD_DIM:(h + 1) * HEAD_DIM] = o[r * tq:(r + 1) * tq]
    for g in range(KV_GROUPS):
        imp = probs[g][0:tq]
        for r in range(1, GROUP_HEADS):
            imp = imp + probs[g][r * tq:(r + 1) * tq]
        imp = jnp.where(blk == cur, 2.0, jnp.where(blk > cur, -1.0, imp))
        imp = jnp.where(blk < n_sel, imp, -2.0)
        n_top = min(NSA_TOPN, n_sel)
        cols = slice(g * LANES, (g + 1) * LANES)

        def by_rank(imp=imp, cols=cols):
            rank = jnp.zeros((tq, LANES), F32)
            for i in range(n_sel):
                col = imp[:, i:i + 1]
                ahead = (col > imp) | ((col == imp) & (blk > i))
                rank = rank + jnp.where(ahead, 1.0, 0.0)
            msel_ref[:, cols] = jnp.where((rank < float(n_top)) & (blk < n_sel), 1.0, 0.0)

        def first_blocks(cols=cols):
            msel_ref[:, cols] = jnp.where(blk < n_top, 1.0, 0.0) + jnp.zeros((tq, LANES), F32)

        if t_new == tq:
            if pos0 + tq <= n_top * NSA_BLOCK:
                first_blocks()
            else:
                by_rank()
        else:
            early = pos0 + (qi + 1) * tq <= n_top * NSA_BLOCK
            pl.when(early)(first_blocks)
            pl.when(jnp.logical_not(early))(by_rank)


def _on_causal_width(qi, tq, widths, tile):
    if len(widths) == 1:
        tile(widths[0], True)
        return
    need = (qi * tq + tq - 1) // widths[0]
    for nw, w in enumerate(widths):
        pl.when(need == nw)(functools.partial(tile, w, nw == 0))


def _attn_body(pt_ref, q_ref, kvn_ref, *rest, mode, seqs, n_pages, **statics):
    del pt_ref
    pages, rest = rest[:seqs * n_pages], rest[seqs * n_pages:]
    m_ref = None
    if mode in ("sel", "mask"):
        m_ref, rest = rest[0], rest[1:]
    band_ref, o_ref, *scratch = rest
    for sq in range(seqs):
        _attn_one(_seq_view(q_ref, sq, seqs), _seq_view(kvn_ref, sq, seqs), pages[sq * n_pages:(sq + 1) * n_pages],
                  None if m_ref is None else _seq_view(m_ref, sq, seqs), band_ref, _seq_view(o_ref, sq, seqs),
                  *scratch, mode=mode, n_pages=n_pages, **statics)


def _attn_one(q_ref, kvn_ref, page_refs, m_ref, band_ref, o_ref, kc_ref, vc_ref, s_ref, cap_ref, *maybe_p_ref,
              mode, n_pages, t_new, tq, pos0, widths):
    p_ref = maybe_p_ref[0] if maybe_p_ref else s_ref
    qi = pl.program_id(1)
    single = t_new == tq
    n_keys = n_pages * PAGE + t_new
    kbase = pos0 - n_pages * PAGE
    scale = HEAD_DIM ** -0.5
    q0 = pos0 if single else pos0 + qi * tq

    @pl.when(qi == 0)
    def _():
        for c, parts in enumerate(_kv_chunks(page_refs, kvn_ref, t_new)):
            rows = slice(c * PAGE, (c + 1) * PAGE)
            for g in range(KV_GROUPS):
                kc_ref[g, rows, :] = parts[g].astype(BF16)
                vc_ref[g, rows, :] = parts[KV_GROUPS + g].astype(BF16)

    def tile(c0, w, band_at, maybe_first):
        qpos = q0 + lax.broadcasted_iota(jnp.int32, (tq, 1), 0)
        col = c0 + lax.broadcasted_iota(jnp.int32, (1, w), 1)
        dist = qpos - (kbase + col)
        visible = (dist >= 0) & (col < n_keys)
        if mode == "win":
            visible = visible & (dist < NSA_WINDOW)
        if mode == "mask":
            visible = visible & (m_ref[:, 0:w] > 0.5)
        keys = pl.ds(c0, w)
        groups = range(KV_GROUPS)
        for g in groups:
            mask = visible
            if mode == "sel":
                expand = (lax.broadcasted_iota(jnp.int32, (LANES, w), 1) // NSA_BLOCK
                          == lax.broadcasted_iota(jnp.int32, (LANES, w), 0))
                chosen = jnp.dot(m_ref[:, g * LANES:(g + 1) * LANES].astype(BF16),
                                 jnp.where(expand, 1.0, 0.0).astype(BF16), preferred_element_type=F32)
                mask = visible & (chosen > 0.5)
            cap_ref[g, :, 0:w] = jnp.where(mask, jnp.inf, NEG_INF)
            heads = [g * GROUP_HEADS + r for r in range(GROUP_HEADS)]
            q = jnp.concatenate([q_ref[:, h * HEAD_DIM:(h + 1) * HEAD_DIM] for h in heads], 0).astype(BF16)
            s_ref[g, :, 0:w] = _dot_nt(q, kc_ref[g, keys, :]) * (scale * LOG2E)
        for g in groups:
            if band_at is not None:
                s_ref[g, :, band_at:band_at + BAND] += band_ref[g]
            else:
                if maybe_first:
                    @pl.when(qi == 0)
                    def _():
                        s_ref[g, :, 0:PAGE] += band_ref[g, :, PAGE:]

                @pl.when(qi > 0)
                def _():
                    s_ref[g, :, pl.ds(pl.multiple_of(q0 - PAGE - kbase, PAGE), BAND)] += band_ref[g]
        sums, alive = [], []
        for g in groups:
            for r in range(GROUP_HEADS):
                rows = slice(r * tq, (r + 1) * tq)
                m = jnp.max(jnp.minimum(s_ref[g, rows, 0:w], cap_ref[g, :, 0:w]), -1, keepdims=True)
                p = jnp.exp2(jnp.minimum(s_ref[g, rows, 0:w], cap_ref[g, :, 0:w]) - m)
                p_ref[g, rows, 0:w] = p.astype(p_ref.dtype)
                sums.append(jnp.sum(p, -1, keepdims=True))
                alive.append(m > NEG_INF)
        for g in groups:
            o = jnp.dot(p_ref[g, :, 0:w].astype(BF16), vc_ref[g, keys, :], preferred_element_type=F32)
            for r in range(GROUP_HEADS):
                h = g * GROUP_HEADS + r
                o_h = o[r * tq:(r + 1) * tq] / jnp.maximum(sums[h], 1e-30)
                o_ref[:, h * HEAD_DIM:(h + 1) * HEAD_DIM] = jnp.where(alive[h], o_h, 0.0)

    if single:
        tile(0, widths[0], pos0 - PAGE - kbase, False)
    elif mode == "win":
        wch = NSA_WINDOW // PAGE
        pl.when(qi < wch)(functools.partial(tile, 0, NSA_WINDOW, None, True))
        pl.when(qi >= wch)(lambda: tile(pl.multiple_of((qi - wch) * PAGE, PAGE), NSA_WINDOW + PAGE,
                                        NSA_WINDOW - PAGE, False))
    else:
        _on_causal_width(qi, tq, widths, lambda w, first: tile(0, w, None, first))


def _index_body(pt_ref, qidx_ref, miscq_ref, misck_ref, *rest, seqs, n_pages, **statics):
    del pt_ref
    pages, (o_ref, kidx_ref) = rest[:seqs * n_pages], rest[seqs * n_pages:]
    for sq in range(seqs):
        _index_one(_seq_view(qidx_ref, sq, seqs), _seq_view(miscq_ref, sq, seqs), _seq_view(misck_ref, sq, seqs),
                   pages[sq * n_pages:(sq + 1) * n_pages], _seq_view(o_ref, sq, seqs), kidx_ref,
                   n_pages=n_pages, **statics)


def _index_one(qidx_ref, miscq_ref, misck_ref, ipage_refs, o_ref, kidx_ref, *, n_pages, t_new, tq, pos0, widths):
    qi = pl.program_id(1)
    lk = o_ref.shape[1]
    n_keys = n_pages * PAGE + t_new
    kbase = pos0 - n_pages * PAGE
    q0 = pos0 if t_new == tq else pos0 + qi * tq

    @pl.when(qi == 0)
    def _():
        for c, r in enumerate(ipage_refs):
            kidx_ref[:, c * PAGE:(c + 1) * PAGE] = r[0].astype(BF16)
        for c, x in enumerate(_new_chunks(misck_ref, t_new)):
            cols = slice((n_pages + c) * PAGE, (n_pages + c + 1) * PAGE)
            kidx_ref[:, cols] = x.T[MISC_KI:MISC_KI + IDX_DIM, :].astype(BF16)

    def tile(w, maybe_first):
        del maybe_first
        qpos = q0 + lax.broadcasted_iota(jnp.int32, (tq, 1), 0)
        col = lax.broadcasted_iota(jnp.int32, (1, w), 1)
        visible = (qpos - (kbase + col) >= 0) & (col < n_keys)
        q = jnp.concatenate([qidx_ref[:, hh * IDX_DIM:(hh + 1) * IDX_DIM] for hh in range(IDX_HEADS)], 0)
        sc = jnp.dot(q.astype(BF16), kidx_ref[:, 0:w], preferred_element_type=F32)
        score = jnp.zeros((tq, w), F32)
        for hh in range(IDX_HEADS):
            wi = miscq_ref[:, MISC_WI + hh:MISC_WI + hh + 1] * (IDX_HEADS ** -0.5)
            score = score + jnp.maximum(sc[hh * tq:(hh + 1) * tq] * (IDX_DIM ** -0.5), 0.0) * wi
        o_ref[:, 0:w] = jnp.where(visible, score, NEG_INF)
        if w < lk:
            o_ref[:, w:] = jnp.full((tq, lk - w), NEG_INF, F32)

    _on_causal_width(qi, tq, widths, tile)


def _topk_body(s_ref, m_ref, key_ref, *, k, nq, tr, widths):
    lk = s_ref.shape[1]
    assert lk <= 4096
    neg_key = int(np.array(NEG_INF, np.float32).view(np.int32)) ^ 0x7FFFFFFF
    kf = float(k)

    def tile(w, maybe_first):
        del maybe_first
        bits = lax.bitcast_convert_type(s_ref[:, 0:w] + 0.0, jnp.int32)
        key_ref[:, 0:w] = jnp.where(bits >= 0, bits, bits ^ 0x7FFFFFFF)
        col = lax.broadcasted_iota(jnp.int32, (1, w), 1)
        unseen = float(lk - w)

        groups = [slice(a * (tr // TOPK_ROW_GROUPS), (a + 1) * (tr // TOPK_ROW_GROUPS)) for a in range(TOPK_ROW_GROUPS)]
        zeros = tuple(jnp.zeros((tr // TOPK_ROW_GROUPS, 1), jnp.int32) for _ in groups)

        def thr_step(i, tus):
            out = []
            for rows, tu in zip(groups, tus):
                cand = tu | jnp.left_shift(jnp.int32(1), 31 - i)
                cs = cand ^ INT_MIN
                cnt = jnp.sum(jnp.where(key_ref[rows, 0:w] >= cs, 1.0, 0.0), -1, keepdims=True)
                cnt = cnt + jnp.where(cs <= neg_key, unseen, 0.0)
                out.append(jnp.where(cnt >= kf, cand, tu))
            return tuple(out)

        thr = jnp.concatenate(lax.fori_loop(0, 32, thr_step, zeros, unroll=4), 0) ^ INT_MIN
        key = key_ref[:, 0:w]
        above = key > thr
        tied = key == thr
        need = kf - jnp.sum(jnp.where(above, 1.0, 0.0), -1, keepdims=True)

        def tie_step(i, j0s):
            out = []
            for rows, j0 in zip(groups, j0s):
                cand = j0 | jnp.left_shift(jnp.int32(1), 11 - i)
                hit = (key_ref[rows, 0:w] == thr[rows]) & (col < cand)
                cnt = jnp.sum(jnp.where(hit, 1.0, 0.0), -1, keepdims=True)
                out.append(jnp.where(cnt < need[rows], cand, j0))
            return tuple(out)

        m_ref[:, 0:w] = jnp.where(above | tied, 1.0, 0.0)
        n_tied = jnp.sum(jnp.where(tied, 1.0, 0.0), -1, keepdims=True)
        excess = jnp.max(jnp.where(thr > neg_key, n_tied - need, 0.0))

        @pl.when(excess > 0.0)
        def _():
            j0 = jnp.concatenate(lax.fori_loop(0, 12, tie_step, zeros, unroll=4), 0)
            k2 = key_ref[:, 0:w]
            m_ref[:, 0:w] = jnp.where((k2 > thr) | ((k2 == thr) & (col <= j0)), 1.0, 0.0)
        if w < lk:
            m_ref[:, w:] = jnp.zeros((tr, lk - w), F32)

    _on_causal_width(pl.program_id(0) % nq, tr, widths, tile)


class _Group:
    def __init__(self, row0, n_seq, t_new, tq, pos0, n_pages, seqs=1):
        assert t_new % tq == 0 and row0 % (seqs * tq) == 0 and row0 % (seqs * t_new) == 0
        assert n_seq % seqs == 0 and (seqs == 1 or t_new == tq)
        self.seqs = seqs
        assert pos0 == n_pages * PAGE or n_pages * PAGE < pos0
        assert t_new == tq or (tq == PAGE and pos0 == 0)
        assert t_new % PAGE == 0 or t_new % PAGE < NSA_BLOCK
        self.row0, self.n_seq, self.t_new, self.tq, self.pos0, self.n_pages = row0, n_seq, t_new, tq, pos0, n_pages
        self.nq = t_new // tq
        self.rows = n_seq * t_new
        self.lk = (n_pages + -(-t_new // PAGE)) * PAGE
        self.n_keys = n_pages * PAGE + t_new

    def grid(self):
        return (self.n_seq // self.seqs, self.nq)

    def q_spec(self, width, col):
        rows = self.seqs * self.tq
        return pl.BlockSpec((rows, width), lambda b, qi, pt: (self.row0 // rows + b * self.nq + qi, col // width))

    def seq_spec(self, width, col):
        rows = self.seqs * self.t_new
        return pl.BlockSpec((rows, width), lambda b, qi, pt: (self.row0 // rows + b, col // width))

    def page_specs(self, shape):
        return [pl.BlockSpec((1,) + shape, lambda b, qi, pt, sq=sq, p=p: (pt[b * self.seqs + sq, p], 0, 0))
                for sq in range(self.seqs) for p in range(self.n_pages)]

    def page_args(self, pool):
        return [pool] * (self.seqs * self.n_pages)

    def out_spec(self, width):
        return pl.BlockSpec((self.seqs * self.tq, width), lambda b, qi, pt: (b * self.nq + qi, 0))

    def statics(self):
        return dict(seqs=self.seqs, n_pages=self.n_pages, t_new=self.t_new, tq=self.tq, pos0=self.pos0)

    def widths(self):
        if self.nq == 1:
            return (self.lk,)
        step = CAUSAL_WIDTH_STEP
        assert self.lk % step == 0
        return tuple(range(step, self.lk + 1, step))


def _cmp_bias_table(rel_bias, grp):
    lane0 = 0 if grp.nq == 1 else CMP_BIAS_LANE0
    tab = rel_bias[_BUCKETS]
    pieces, n_far = [], 0
    for lane in range(LANES + 1):
        d0 = grp.pos0 - (NSA_BLOCK * (lane - lane0 + 1) - 1)
        plain = lane < LANES and (d0 >= PAGE or d0 + grp.tq - 1 < 0)
        if plain:
            n_far += 1
            continue
        if n_far:
            pieces.append(jnp.broadcast_to(rel_bias[N_BUCKETS - 1], (grp.tq, n_far, rel_bias.shape[1])))
            n_far = 0
        if lane < LANES:
            pieces.append(tab[np.clip(d0 + np.arange(grp.tq), 0, BAND - 1)][:, None, :])
    table = jnp.transpose(jnp.concatenate(pieces, 1), (2, 0, 1))
    return table.reshape(-1, GROUP_HEADS * grp.tq, LANES)


def nsa_compress(grp, z, page_table, pool, wexp, bias):
    kv_w = KV_GROUPS * 2 * HEAD_DIM
    qw = KV_GROUPS * GROUP_HEADS * HEAD_DIM
    const2 = lambda b, qi, pt: (0, 0)
    return pl.pallas_call(
        functools.partial(_cmp_body, **grp.statics()),
        grid_spec=pltpu.PrefetchScalarGridSpec(
            num_scalar_prefetch=1,
            grid=grp.grid(),
            in_specs=[grp.q_spec(qw, COL_QN), grp.seq_spec(kv_w, COL_KVC)] + grp.page_specs(KV_PAGE)
            + [pl.BlockSpec((PAGE, kv_w), const2), pl.BlockSpec(bias.shape, lambda b, qi, pt: (0, 0, 0))],
            out_specs=[grp.out_spec(qw), grp.out_spec(KV_GROUPS * LANES)],
            scratch_shapes=[pltpu.VMEM((LANES, kv_w), F32),
                            pltpu.VMEM((KV_GROUPS, LANES, HEAD_DIM), BF16),
                            pltpu.VMEM((KV_GROUPS, LANES, HEAD_DIM), BF16)]),
        out_shape=[jax.ShapeDtypeStruct((grp.rows, qw), F32),
                   jax.ShapeDtypeStruct((grp.rows, KV_GROUPS * LANES), F32)],
        compiler_params=_cparams("parallel", "arbitrary"),
        name="nsa_compress",
    )(page_table, z, z, *grp.page_args(pool), wexp, bias)


def sparse_attention(mode, grp, z, page_table, pool, band, *, q_col, kv_col, mask=None):
    kv_w = KV_GROUPS * 2 * HEAD_DIM
    qw = KV_GROUPS * GROUP_HEADS * HEAD_DIM
    in_specs = [grp.q_spec(qw, q_col), grp.seq_spec(kv_w, kv_col)] + grp.page_specs(KV_PAGE)
    args = [z, z] + grp.page_args(pool)
    if mode in ("sel", "mask"):
        in_specs.append(grp.out_spec(mask.shape[1]))
        args.append(mask)
    in_specs.append(pl.BlockSpec((KV_GROUPS, GROUP_HEADS * grp.tq, BAND), lambda b, qi, pt: (0, 0, 0)))
    args.append(band)
    widths = grp.widths()
    s_cols = max(widths) if (mode != "win" or grp.nq == 1) else NSA_WINDOW + PAGE
    rows = GROUP_HEADS * grp.tq
    scratch = [pltpu.VMEM((KV_GROUPS, grp.lk, HEAD_DIM), BF16),
               pltpu.VMEM((KV_GROUPS, grp.lk, HEAD_DIM), BF16),
               pltpu.VMEM((KV_GROUPS, rows, s_cols), F32),
               pltpu.VMEM((KV_GROUPS, grp.tq, s_cols), F32)]
    if grp.tq % BF16_SUBLANES == 0:
        scratch.append(pltpu.VMEM((KV_GROUPS, rows, s_cols), BF16))
    return pl.pallas_call(
        functools.partial(_attn_body, mode=mode, widths=widths, **grp.statics()),
        grid_spec=pltpu.PrefetchScalarGridSpec(
            num_scalar_prefetch=1,
            grid=grp.grid(),
            in_specs=in_specs,
            out_specs=grp.out_spec(qw),
            scratch_shapes=scratch),
        out_shape=jax.ShapeDtypeStruct((grp.rows, qw), F32),
        compiler_params=_cparams("parallel", "arbitrary"),
        name="sparse_attention_" + mode,
    )(page_table, *args)


def dsa_index_scores(grp, z, page_table, idx_pool):
    return pl.pallas_call(
        functools.partial(_index_body, widths=grp.widths(), **grp.statics()),
        grid_spec=pltpu.PrefetchScalarGridSpec(
            num_scalar_prefetch=1,
            grid=grp.grid(),
            in_specs=[grp.q_spec(IDX_HEADS * IDX_DIM, COL_QI), grp.q_spec(LANES, COL_MISC),
                      grp.seq_spec(LANES, COL_MISC)] + grp.page_specs((IDX_DIM, PAGE)),
            out_specs=grp.out_spec(grp.lk),
            scratch_shapes=[pltpu.VMEM((IDX_DIM, grp.lk), BF16)]),
        out_shape=jax.ShapeDtypeStruct((grp.rows, grp.lk), F32),
        compiler_params=_cparams("parallel", "arbitrary"),
        name="dsa_index_scores",
    )(page_table, z, z, z, *grp.page_args(idx_pool))


def topk_mask(grp, scores, k):
    rows, lk = scores.shape
    tr = PAGE
    assert rows % tr == 0 and (grp.nq == 1 or grp.tq == tr)
    blk = pl.BlockSpec((tr, lk), lambda i: (i, 0))
    return pl.pallas_call(
        functools.partial(_topk_body, k=k, nq=grp.nq, tr=tr, widths=grp.widths()),
        grid=(rows // tr,),
        in_specs=[blk],
        out_specs=blk,
        out_shape=jax.ShapeDtypeStruct((rows, lk), F32),
        scratch_shapes=[pltpu.VMEM((tr, lk), jnp.int32)],
        compiler_params=_cparams("parallel"),
        name="topk_mask",
    )(scores)


def _combine_body(oc_ref, os_ref, ow_ref, od_ref, misc_ref, y_ref):
    n_heads = KV_GROUPS * GROUP_HEADS
    gates = jax.nn.sigmoid(misc_ref[:, MISC_GATES:MISC_GATES + 3 * n_heads])
    for h in range(n_heads):
        hs = slice(h * HEAD_DIM, (h + 1) * HEAD_DIM)
        o = (gates[:, 3 * h:3 * h + 1] * oc_ref[:, hs] + gates[:, 3 * h + 1:3 * h + 2] * os_ref[:, hs]
             + gates[:, 3 * h + 2:3 * h + 3] * ow_ref[:, hs])
        y_ref[:, hs] = o.astype(y_ref.dtype)
    y_ref[:, n_heads * HEAD_DIM:] = od_ref[...].astype(y_ref.dtype)


def nsa_dsa_combine(o_c, o_s, o_w, o_d, z, row0, *, tm):
    m, w = o_c.shape
    assert m % tm == 0 and row0 % tm == 0
    blk = pl.BlockSpec((tm, w), lambda i: (i, 0))
    return pl.pallas_call(
        _combine_body,
        grid=(m // tm,),
        in_specs=[blk, blk, blk, blk, pl.BlockSpec((tm, LANES), lambda i: (row0 // tm + i, COL_MISC // LANES))],
        out_specs=pl.BlockSpec((tm, 2 * w), lambda i: (i, 0)),
        out_shape=jax.ShapeDtypeStruct((m, 2 * w), BF16),
        compiler_params=_cparams("parallel"),
        name="nsa_dsa_combine",
    )(o_c, o_s, o_w, o_d, z)


def _band_tiles(rel_bias, tq):
    delta = (rel_bias[_BUCKETS] - rel_bias[N_BUCKETS - 1]).T
    rev = jnp.concatenate([delta[:, ::-1], jnp.zeros((delta.shape[0], PAGE), delta.dtype)], 1)
    tiles = jnp.stack([rev[:, PAGE - 1 - i:PAGE - 1 - i + BAND] for i in range(tq)], 1)
    return tiles.reshape(-1, GROUP_HEADS * tq, BAND) * LOG2E


def _widen_cd_w_in(w):
    sizes = (1024, 512, 512, 512, 24, 1024, 512, 512, 64, 8)
    q_n, kv_c, kv_s, kv_w, gates, q_d, kv_d, q_i, k_i, w_i = jnp.split(w, np.cumsum(sizes)[:-1].tolist(), axis=-1)
    cols = [q_n, q_d, kv_c, kv_s, kv_w, kv_d, q_i, k_i, gates, w_i]
    used = sum(c.shape[-1] for c in cols)
    return jnp.concatenate(cols + [jnp.zeros(w.shape[:-1] + (NZ - used,), w.dtype)], axis=-1)


def kernel(x_prompt, x_sample, state_conv, state_pool, cache_nsa_cmp, cache_nsa_sel, cache_nsa_win, cache_dsa_kv, cache_dsa_idx, page_table, norm_mix, norm_ffn, norm_final, ab_w_in, ab_conv_w, ab_conv_b, ab_ln_g, ab_ln_b, ab_pool_w, ab_pool_scale, ab_w_out, cd_w_in, cd_w_cmp, cd_w_out, rel_bias, ffn_w1, ffn_w2):
    bp, tp, d_model = x_prompt.shape
    bs, ts, _ = x_sample.shape
    mp, ms = bp * tp, bs * ts
    depth = norm_mix.shape[0]
    n_pages = page_table.shape[1]
    n_pool = cache_nsa_cmp.shape[1]
    past_len = n_pages * PAGE
    assert cache_nsa_cmp.shape[2] == PAGE
    win_len = cache_nsa_win.shape[2]
    assert win_len % PAGE == 0 and win_len == NSA_WINDOW and tp >= NSA_WINDOW
    kv_w = KV_GROUPS * 2 * HEAD_DIM

    xs = [x_prompt.reshape(mp, d_model), x_sample.reshape(ms, d_model)]
    grp_p = _Group(0, bp, tp, PAGE, 0, 0)
    grp_s = _Group(mp, bs, ts, ts, past_len, n_pages)
    grp_sc = _Group(mp, bs, ts, ts, past_len, n_pages, SAMPLE_SEQS_PER_STEP)
    grp_sw = _Group(mp, bs, ts, ts, past_len, win_len // PAGE, SAMPLE_SEQS_PER_STEP)
    no_pages = jnp.zeros((1, 1), jnp.int32)
    win_pages = jnp.arange(bs * (win_len // PAGE), dtype=jnp.int32).reshape(bs, win_len // PAGE)

    outs = {k: [] for k in ("conv_p", "conv_s", "pool_p", "pool_s", "cmp_p", "cmp_s", "sel_p", "sel_s",
                            "win_p", "win_s", "dsa_p", "dsa_s", "idx_p", "idx_s")}
    weights = {"ab_w_in": ab_w_in.astype(BF16), "ab_w_out": ab_w_out.astype(BF16),
              "cd_w_in": _widen_cd_w_in(cd_w_in), "cd_w_out": cd_w_out.astype(BF16),
              "ffn_w1": ffn_w1, "ffn_w2": ffn_w2.astype(BF16)}
    y_p = y_s = None
    for i in range(depth):
        j = i // 2
        if i % 2 == 0:
            d_conv = ab_conv_w.shape[2]
            z = norm_matmul(xs, norm_mix[i], weights["ab_w_in"], j)
            mid_p, u_p = ab_mid_prompt(z, bp, tp, ab_conv_w[j], ab_conv_b[j], ab_ln_g[j], ab_ln_b[j],
                                       ab_pool_w[j], ab_pool_scale[j])
            mid_s, conv_s, pool_s = ab_mid_step(z, mp, bs, ts, past_len, state_conv[j], state_pool[j], ab_conv_w[j],
                                                ab_conv_b[j], ab_ln_g[j], ab_ln_b[j], ab_pool_w[j], ab_pool_scale[j])
            xs = [matmul_residual([mid_p, mid_s], weights["ab_w_out"], j, xs)]
            outs["conv_p"].append(u_p.reshape(bp, tp, d_conv)[:, tp - CONV_BUF:])
            outs["conv_s"].append(conv_s)
            outs["pool_p"].append(jnp.stack([z[(b + 1) * tp - POOL_BUF:(b + 1) * tp, 2 * d_conv:] for b in range(bp)]))
            outs["pool_s"].append(pool_s)
        else:
            z = norm_matmul(xs, norm_mix[i], weights["cd_w_in"], j)
            nsa_bias = rel_bias[:, :KV_GROUPS * GROUP_HEADS]
            band_p, band_s = _band_tiles(rel_bias, grp_p.tq), _band_tiles(rel_bias, grp_s.tq)
            wexp = jnp.tile(jnp.repeat(jnp.transpose(cd_w_cmp[j], (1, 0, 2)).reshape(NSA_BLOCK, 2 * KV_GROUPS),
                                       HEAD_DIM, axis=1), (PAGE // NSA_BLOCK, 1))
            pt = page_table + j * n_pool
            pools = [c.reshape((-1,) + KV_PAGE) for c in (cache_nsa_cmp, cache_nsa_sel, cache_dsa_kv)]
            idx_pool = jnp.swapaxes(cache_dsa_idx, 2, 3).reshape(-1, IDX_DIM, PAGE)
            win_pool = cache_nsa_win.reshape((-1,) + KV_PAGE)
            wpt = win_pages + j * bs * (win_len // PAGE)
            mids = []
            for grp, gc, gw, ptab, wtab, band in ((grp_p, grp_p, grp_p, no_pages, no_pages, band_p),
                                                  (grp_s, grp_sc, grp_sw, pt, wpt, band_s)):
                o_c, msel = nsa_compress(gc, z, ptab, pools[0], wexp, _cmp_bias_table(nsa_bias, gc))
                o_s = sparse_attention("sel", gc, z, ptab, pools[1], band[:KV_GROUPS],
                                       q_col=COL_QN, kv_col=COL_KVS, mask=msel)
                o_w = sparse_attention("win", gw, z, wtab, win_pool, band[:KV_GROUPS],
                                       q_col=COL_QN, kv_col=COL_KVW)
                top = topk_mask(grp, dsa_index_scores(grp, z, ptab, idx_pool), min(DSA_TOPK, grp.n_keys // 4))
                o_d = sparse_attention("mask", gc, z, ptab, pools[2], band[KV_GROUPS:],
                                       q_col=COL_QD, kv_col=COL_KVD, mask=top)
                mids.append(nsa_dsa_combine(o_c, o_s, o_w, o_d, z, grp.row0, tm=min(512, grp.rows)))
            xs = [matmul_residual(mids, weights["cd_w_out"], j, xs)]

            def kv_out(col, width, tail):
                seg = z[:, col:col + width]
                return seg[:mp].reshape((bp, tp) + tail), seg[mp:].reshape((bs, ts) + tail)

            kv_tail = (2, KV_GROUPS, HEAD_DIM)
            for name, col in (("cmp", COL_KVC), ("sel", COL_KVS), ("dsa", COL_KVD)):
                p_new, s_new = kv_out(col, kv_w, kv_tail)
                outs[name + "_p"].append(p_new)
                outs[name + "_s"].append(s_new)
            w_p, w_s = kv_out(COL_KVW, kv_w, kv_tail)
            outs["win_p"].append(w_p[:, tp - NSA_WINDOW:])
            outs["win_s"].append(jnp.concatenate([cache_nsa_win[j], w_s], 1)[:, ts:])
            i_p, i_s = kv_out(COL_MISC + MISC_KI, IDX_DIM, (IDX_DIM,))
            outs["idx_p"].append(i_p)
            outs["idx_s"].append(i_s)
        a = norm_matmul(xs, norm_ffn[i], weights["ffn_w1"], i, relu2=True, out_dtype=BF16)
        if i == depth - 1:
            y_p, y_s = matmul_residual([a], weights["ffn_w2"], i, xs, norm_final, split_out=(mp, ms))
        else:
            xs = [matmul_residual([a], weights["ffn_w2"], i, xs)]

    st = {k: jnp.stack(v) for k, v in outs.items()}
    return (y_p.reshape(bp, tp, d_model), y_s.reshape(bs, ts, d_model),
            st["conv_p"], st["conv_s"], st["pool_p"], st["pool_s"], st["cmp_p"], st["cmp_s"],
            st["sel_p"], st["sel_s"], st["win_p"], st["win_s"], st["dsa_p"], st["dsa_s"],
            st["idx_p"], st["idx_s"])
```

```python
import functools
import math

import numpy as np
import jax
import jax.numpy as jnp
from jax import lax
from jax.experimental import pallas as pl
from jax.experimental.pallas import tpu as pltpu

F32 = jnp.float32
BF16 = jnp.bfloat16

EPS = 1e-6
NEG_INF = -1e30
LOG2E = math.log2(math.e)
HEAD_DIM = 128
LANES = 128
SUBLANES = 8
BF16_SUBLANES = 16
CONV_WIDTH = 31
CONV_BUF = CONV_WIDTH - 1
POOL_WINDOWS = (2, 4, 8, 16)
POOL_BUF = max(POOL_WINDOWS) - 1
HALO = 32
POOL_HALO = 16
VMEM_LIMIT = 56 * 1024 * 1024


def _cparams(*sem):
    return pltpu.CompilerParams(dimension_semantics=sem, vmem_limit_bytes=VMEM_LIMIT)


class _Rows:
    def __init__(self, arrays, tm):
        self.arrays = list(arrays)
        self.tm = tm
        assert all(a.shape[0] % tm == 0 for a in self.arrays)
        self.tiles = [a.shape[0] // tm for a in self.arrays]
        self.n_tiles = sum(self.tiles)
        self.n = len(self.arrays)

    def specs(self, width, col):
        out, t0 = [], 0
        for nt in self.tiles:
            out.append(pl.BlockSpec((self.tm, width), lambda i, j, t0=t0, nt=nt: (jnp.clip(i - t0, 0, nt - 1), col(j))))
            t0 += nt
        return out

    def select(self, i, refs, fn):
        if self.n == 1:
            fn(refs[0])
            return
        t0 = 0
        for nt, ref in zip(self.tiles, refs):
            pl.when((i >= t0) & (i < t0 + nt))(functools.partial(fn, ref))
            t0 += nt


def _rmsnorm_rows(x, g):
    return (x * lax.rsqrt(jnp.mean(x * x, -1, keepdims=True) + EPS)) * g


def _norm_matmul_body(*refs, rows, relu2):
    x_refs = refs[:rows.n]
    g_ref, w_ref, o_ref, h_ref = refs[rows.n:]

    @pl.when(pl.program_id(1) == 0)
    def _():
        def norm(x_ref):
            h_ref[...] = _rmsnorm_rows(x_ref[...], g_ref[...]).astype(BF16)

        rows.select(pl.program_id(0), x_refs, norm)

    y = jnp.dot(h_ref[...], w_ref[...].astype(BF16), preferred_element_type=F32)
    if relu2:
        y = jnp.square(jnp.maximum(y, 0.0))
    o_ref[...] = y.astype(o_ref.dtype)


def norm_matmul(xs, g, w, layer, *, relu2=False, out_dtype=F32, tm=1024, tn=1024):
    rows = _Rows(xs, tm)
    _, d, n = w.shape
    assert n % tn == 0
    return pl.pallas_call(
        functools.partial(_norm_matmul_body, rows=rows, relu2=relu2),
        grid=(rows.n_tiles, n // tn),
        in_specs=rows.specs(d, lambda j: 0) + [pl.BlockSpec((1, d), lambda i, j: (0, 0)),
                                               pl.BlockSpec((None, d, tn), lambda i, j: (layer, 0, j))],
        out_specs=pl.BlockSpec((tm, tn), lambda i, j: (i, j)),
        out_shape=jax.ShapeDtypeStruct((rows.n_tiles * tm, n), out_dtype),
        scratch_shapes=[pltpu.VMEM((tm, d), BF16)],
        compiler_params=_cparams("parallel", "arbitrary"),
        name="norm_matmul",
    )(*rows.arrays, g.reshape(1, d), w)


def _matmul_residual_body(*refs, a_rows, r_rows, o_rows, final_norm, single_k):
    a_refs, refs = refs[:a_rows.n], refs[a_rows.n:]
    w_ref, refs = refs[0], refs[1:]
    r_refs, refs = refs[:r_rows.n], refs[r_rows.n:]
    if final_norm:
        g_ref, refs = refs[0], refs[1:]
    o_refs, acc_ref = refs[:o_rows.n], refs[o_rows.n]
    i, k = pl.program_id(0), pl.program_id(1)

    def product(a_ref):
        return jnp.dot(a_ref[...], w_ref[...], preferred_element_type=F32)

    def assign(a_ref):
        acc_ref[...] = product(a_ref)

    def accumulate(a_ref):
        acc_ref[...] += product(a_ref)

    if single_k:
        a_rows.select(i, a_refs, assign)
    else:
        @pl.when(k == 0)
        def _():
            acc_ref[...] = jnp.zeros_like(acc_ref)

        a_rows.select(i, a_refs, accumulate)

    @pl.when(k == pl.num_programs(1) - 1)
    def _():
        def add_residual(r_ref):
            acc_ref[...] += r_ref[...]

        def write(o_ref):
            o = acc_ref[...]
            o_ref[...] = _rmsnorm_rows(o, g_ref[...]) if final_norm else o

        r_rows.select(i, r_refs, add_residual)
        o_rows.select(i, o_refs, write)


def matmul_residual(a_list, w, layer, r_list, g_final=None, *, split_out=None, tm=512, tk=2048):
    a_rows, r_rows = _Rows(a_list, tm), _Rows(r_list, tm)
    _, kdim, n = w.shape
    m = a_rows.n_tiles * tm
    assert kdim % tk == 0 and r_rows.n_tiles == a_rows.n_tiles
    final_norm = g_final is not None
    o_rows = _Rows([jax.ShapeDtypeStruct((r, n), F32) for r in (split_out or (m,))], tm)
    assert o_rows.n_tiles == a_rows.n_tiles
    in_specs = (a_rows.specs(tk, lambda k: k) + [pl.BlockSpec((None, tk, n), lambda i, k: (layer, k, 0))]
                + r_rows.specs(n, lambda k: 0))
    args = a_rows.arrays + [w] + r_rows.arrays
    if final_norm:
        in_specs.append(pl.BlockSpec((1, n), lambda i, k: (0, 0)))
        args.append(g_final.reshape(1, n))
    out = pl.pallas_call(
        functools.partial(_matmul_residual_body, a_rows=a_rows, r_rows=r_rows, o_rows=o_rows, final_norm=final_norm,
                          single_k=kdim == tk),
        grid=(a_rows.n_tiles, kdim // tk),
        in_specs=in_specs,
        out_specs=o_rows.specs(n, lambda k: 0),
        out_shape=o_rows.arrays,
        scratch_shapes=[pltpu.VMEM((tm, n), F32)],
        compiler_params=_cparams("parallel", "arbitrary"),
        name="matmul_residual",
    )(*args)
    return out if split_out else out[0]


def _layernorm_silu(c, g, b):
    mu = jnp.mean(c, -1, keepdims=True)
    xc = c - mu
    y = xc * lax.rsqrt(jnp.mean(xc * xc, -1, keepdims=True) + EPS)
    y = y * g + b
    return y * jax.nn.sigmoid(y)


def _ab_mid_body(z_ref, zp_ref, cw_ref, cb_ref, lg_ref, lb_ref, pw_ref, ps_ref, y_ref, u_ref,
                 ext_ref, vext_ref, conv_ref, *, tt, d_conv, d_pool):
    ti = pl.program_id(1)
    keep = (ti > 0).astype(F32)
    a_p = zp_ref[:, 0:d_conv]
    g_p = zp_ref[:, d_conv:2 * d_conv]
    ext_ref[0:HALO, :] = a_p * jax.nn.sigmoid(g_p) * keep
    vext_ref[0:HALO, :] = zp_ref[:, 2 * d_conv:] * keep
    u = z_ref[:, 0:d_conv] * jax.nn.sigmoid(z_ref[:, d_conv:2 * d_conv])
    ext_ref[HALO:, :] = u
    u_ref[...] = u
    vext_ref[HALO:, :] = z_ref[:, 2 * d_conv:]

    off = HALO - CONV_BUF
    sub = SUBLANES
    for c in range(d_conv // LANES):
        cs = slice(c * LANES, (c + 1) * LANES)
        acc = jnp.zeros((tt, LANES), F32)
        for s in range(sub):
            n = tt if s == 0 else tt + sub
            part = jnp.zeros((n, LANES), F32)
            for j in range(CONV_WIDTH):
                if (off + j) % sub == s:
                    start = off + j - s
                    part = part + cw_ref[j:j + 1, cs] * ext_ref[start:start + n, cs]
            acc = acc + part[s:s + tt]
        conv_ref[:, cs] = acc + cb_ref[:, cs]
    y_ref[:, 0:d_conv] = _layernorm_silu(conv_ref[...], lg_ref[...], lb_ref[...]).astype(y_ref.dtype)

    pos = ti * tt + lax.broadcasted_iota(jnp.int32, (tt, 1), 0)
    pg = d_pool // len(POOL_WINDOWS)
    for gi, w in enumerate(POOL_WINDOWS):
        gs = slice(gi * pg, (gi + 1) * pg)
        tok = vext_ref[HALO:, gs]
        acc = tok
        for i in range(1, w):
            acc = acc + vext_ref[HALO - i:HALO - i + tt, gs]
        cnt = jnp.minimum(pos + 1, w).astype(F32)
        d = acc / cnt - tok
        yp = jnp.dot(d.astype(BF16), pw_ref[gi], preferred_element_type=F32) * ps_ref[:, gs]
        y_ref[:, d_conv + gi * pg:d_conv + (gi + 1) * pg] = yp.astype(y_ref.dtype)


def ab_mid_prompt(z, n_seq, t_len, conv_w, conv_b, ln_g, ln_b, pool_w, pool_scale, *, tt=256):
    d_conv = conv_w.shape[1]
    d_pool = pool_scale.shape[0]
    nt = t_len // tt
    hb = tt // HALO
    row = lambda b, t: (b * nt + t, 0)
    const = lambda b, t: (0, 0)
    return pl.pallas_call(
        functools.partial(_ab_mid_body, tt=tt, d_conv=d_conv, d_pool=d_pool),
        grid=(n_seq, nt),
        in_specs=[pl.BlockSpec((tt, z.shape[1]), row),
                  pl.BlockSpec((HALO, z.shape[1]), lambda b, t: (jnp.maximum((b * nt + t) * hb - 1, 0), 0)),
                  pl.BlockSpec(conv_w.shape, const),
                  pl.BlockSpec((1, d_conv), const),
                  pl.BlockSpec((1, d_conv), const),
                  pl.BlockSpec((1, d_conv), const),
                  pl.BlockSpec(pool_w.shape, lambda b, t: (0, 0, 0)),
                  pl.BlockSpec((1, d_pool), const)],
        out_specs=[pl.BlockSpec((tt, d_conv + d_pool), row),
                   pl.BlockSpec((tt, d_conv), row)],
        out_shape=[jax.ShapeDtypeStruct((n_seq * t_len, d_conv + d_pool), BF16),
                   jax.ShapeDtypeStruct((n_seq * t_len, d_conv), F32)],
        scratch_shapes=[pltpu.VMEM((HALO + tt, d_conv), F32),
                        pltpu.VMEM((HALO + tt, d_pool), F32),
                        pltpu.VMEM((tt, d_conv), F32)],
        compiler_params=_cparams("parallel", "parallel"),
        name="ab_mid_prompt",
    )(z, z, conv_w, conv_b.reshape(1, -1), ln_g.reshape(1, -1), ln_b.reshape(1, -1),
      pool_w.astype(BF16), pool_scale.reshape(1, -1))


def _ab_mid_step_body(z_ref, sc_ref, sp_ref, cw_ref, cb_ref, lg_ref, lb_ref, pw_ref, ps_ref,
                      y_ref, nc_ref, np_ref, ext_ref, vext_ref, *, nb, t, pos0, d_conv, d_pool):
    e0 = HALO - CONV_BUF
    p0 = POOL_HALO - POOL_BUF
    z = z_ref[...].reshape(nb, t, z_ref.shape[1])
    u = z[:, :, 0:d_conv] * jax.nn.sigmoid(z[:, :, d_conv:2 * d_conv])
    ext_ref[:, e0:HALO, :] = sc_ref[...]
    ext_ref[:, HALO:, :] = u
    vext_ref[:, p0:POOL_HALO, :] = sp_ref[...]
    vext_ref[:, POOL_HALO:, :] = z[:, :, 2 * d_conv:]
    nc_ref[...] = ext_ref[:, HALO + t - CONV_BUF:, :]
    np_ref[...] = vext_ref[:, POOL_HALO + t - POOL_BUF:, :]

    acc = jnp.zeros((nb, t, d_conv), F32)
    for j in range(CONV_WIDTH):
        acc = acc + cw_ref[j:j + 1, :][None] * ext_ref[:, e0 + j:e0 + j + t, :]
    c = acc + cb_ref[...][None]
    yc = _layernorm_silu(c, lg_ref[...][None], lb_ref[...][None])
    y_ref[:, 0:d_conv] = yc.reshape(nb * t, d_conv).astype(y_ref.dtype)

    pg = d_pool // len(POOL_WINDOWS)
    for gi, w in enumerate(POOL_WINDOWS):
        gs = slice(gi * pg, (gi + 1) * pg)
        tok = vext_ref[:, POOL_HALO:, gs]
        acc = tok
        for i in range(1, w):
            acc = acc + vext_ref[:, POOL_HALO - i:POOL_HALO - i + t, gs]
        cnt = jnp.minimum(pos0 + 1 + lax.broadcasted_iota(jnp.int32, (1, t, 1), 1), w).astype(F32)
        d = (acc / cnt - tok).reshape(nb * t, pg)
        yp = jnp.dot(d.astype(BF16), pw_ref[gi], preferred_element_type=F32) * ps_ref[:, gs]
        y_ref[:, d_conv + gi * pg:d_conv + (gi + 1) * pg] = yp.astype(y_ref.dtype)


def ab_mid_step(z, row0, n_seq, t, pos0, state_conv, state_pool, conv_w, conv_b, ln_g, ln_b, pool_w,
                pool_scale, *, nb=16):
    d_conv = conv_w.shape[1]
    d_pool = pool_scale.shape[0]
    rb = nb * t
    assert row0 % rb == 0 and n_seq % nb == 0
    const = lambda i: (0, 0)
    seq3 = lambda i: (i, 0, 0)
    return pl.pallas_call(
        functools.partial(_ab_mid_step_body, nb=nb, t=t, pos0=pos0, d_conv=d_conv, d_pool=d_pool),
        grid=(n_seq // nb,),
        in_specs=[pl.BlockSpec((rb, z.shape[1]), lambda i: (row0 // rb + i, 0)),
                  pl.BlockSpec((nb, CONV_BUF, d_conv), seq3),
                  pl.BlockSpec((nb, POOL_BUF, d_pool), seq3),
                  pl.BlockSpec(conv_w.shape, const),
                  pl.BlockSpec((1, d_conv), const),
                  pl.BlockSpec((1, d_conv), const),
                  pl.BlockSpec((1, d_conv), const),
                  pl.BlockSpec(pool_w.shape, lambda i: (0, 0, 0)),
                  pl.BlockSpec((1, d_pool), const)],
        out_specs=[pl.BlockSpec((rb, d_conv + d_pool), lambda i: (i, 0)),
                   pl.BlockSpec((nb, CONV_BUF, d_conv), seq3),
                   pl.BlockSpec((nb, POOL_BUF, d_pool), seq3)],
        out_shape=[jax.ShapeDtypeStruct((n_seq * t, d_conv + d_pool), BF16),
                   jax.ShapeDtypeStruct((n_seq, CONV_BUF, d_conv), F32),
                   jax.ShapeDtypeStruct((n_seq, POOL_BUF, d_pool), F32)],
        scratch_shapes=[pltpu.VMEM((nb, HALO + t, d_conv), F32),
                        pltpu.VMEM((nb, POOL_HALO + t, d_pool), F32)],
        compiler_params=_cparams("parallel"),
        name="ab_mid_step",
    )(z, state_conv, state_pool, conv_w, conv_b.reshape(1, -1), ln_g.reshape(1, -1),
      ln_b.reshape(1, -1), pool_w.astype(BF16), pool_scale.reshape(1, -1))


N_BUCKETS = 32
MAX_DISTANCE = 128
NSA_BLOCK = 64
NSA_TOPN = 16
NSA_WINDOW = 512
DSA_TOPK = 256
IDX_HEADS = 8
IDX_DIM = 64
KV_GROUPS = 2
GROUP_HEADS = 4
PAGE = 128
BAND = 2 * PAGE
INT_MIN = -2 ** 31
KV_PAGE = (2 * KV_GROUPS * PAGE, HEAD_DIM)
TOPK_ROW_GROUPS = 4
TOPK_KEY_BITS = 32
TOPK_INDEX_BITS = 12
SAMPLE_SEQS_PER_STEP = 4
CAUSAL_WIDTH_STEP = 2 * PAGE
CMP_BIAS_LANE0 = 64

COL_QN, COL_QD, COL_KVC, COL_KVS, COL_KVW, COL_KVD, COL_QI, COL_MISC = 0, 1024, 2048, 2560, 3072, 3584, 4096, 4608
MISC_KI, MISC_GATES, MISC_WI = 0, 64, 88
NZ = 5120


def _bucket_np(n):
    n = np.maximum(np.asarray(n, np.int32), 0)
    exact = N_BUCKETS // 2
    nf = np.maximum(n, 1).astype(np.float32)
    big = exact + (np.log(nf / np.float32(exact)) / np.float32(math.log(MAX_DISTANCE / exact))
                   * np.float32(N_BUCKETS - exact)).astype(np.int32)
    return np.where(n < exact, n, np.minimum(big, N_BUCKETS - 1))


_BUCKETS = _bucket_np(np.arange(BAND))
assert _BUCKETS[PAGE:].min() == N_BUCKETS - 1


def _softmax_rows(s, mask):
    s = jnp.where(mask, s, NEG_INF)
    m = jnp.max(s, -1, keepdims=True)
    p = jnp.where(mask, jnp.exp(s - m), 0.0)
    return p, jnp.sum(p, -1, keepdims=True)


def _dot_nt(a, b):
    return lax.dot_general(a, b, (((1,), (1,)), ((), ())), preferred_element_type=F32)


def _new_chunks(new_ref, t_new):
    chunks = [new_ref[c * PAGE:(c + 1) * PAGE, :] for c in range(t_new // PAGE)]
    rem = t_new % PAGE
    if rem:
        tail = new_ref[(t_new // PAGE) * PAGE:, :]
        chunks.append(jnp.concatenate([tail, jnp.zeros((PAGE - rem, tail.shape[1]), F32)], 0))
    return chunks


def _kv_chunks(page_refs, new_ref, t_new):
    n_parts = 2 * KV_GROUPS
    chunks = [[r[0, pl.ds(part, PAGE, stride=n_parts), :] for part in range(n_parts)] for r in page_refs]
    for x in _new_chunks(new_ref, t_new):
        chunks.append([x[:, part * HEAD_DIM:(part + 1) * HEAD_DIM] for part in range(n_parts)])
    return chunks


def _seq_view(ref, sq, seqs):
    n = ref.shape[0] // seqs
    return ref.at[pl.ds(sq * n, n)]


def _cmp_body(pt_ref, q_ref, kvn_ref, *rest, seqs, n_pages, **statics):
    del pt_ref
    pages, (wexp_ref, bias_ref, o_ref, msel_ref, *scratch) = rest[:seqs * n_pages], rest[seqs * n_pages:]
    for sq in range(seqs):
        _cmp_one(_seq_view(q_ref, sq, seqs), _seq_view(kvn_ref, sq, seqs), pages[sq * n_pages:(sq + 1) * n_pages],
                 wexp_ref, bias_ref, _seq_view(o_ref, sq, seqs), _seq_view(msel_ref, sq, seqs), *scratch,
                 n_pages=n_pages, **statics)


def _cmp_one(q_ref, kvn_ref, page_refs, wexp_ref, bias_ref, o_ref, msel_ref, comp_ref, ck_ref, cv_ref, *,
             n_pages, t_new, tq, pos0):
    qi = pl.program_id(1)
    n_keys = n_pages * PAGE + t_new
    n_cmp = n_keys // NSA_BLOCK
    n_sel = -(-n_keys // NSA_BLOCK)
    per = PAGE // NSA_BLOCK

    @pl.when(qi == 0)
    def _():
        comp_ref[...] = jnp.zeros_like(comp_ref)
        chunks = _kv_chunks(page_refs, kvn_ref, t_new)[:n_cmp // per]
        for part in range(2 * KV_GROUPS):
            cols = slice(part * HEAD_DIM, (part + 1) * HEAD_DIM)
            xw = jnp.concatenate([parts[part] * wexp_ref[:, cols] for parts in chunks], 0)
            comp_ref[0:per * len(chunks), cols] = xw.reshape(per * len(chunks), NSA_BLOCK, HEAD_DIM).sum(1)
        for g in range(KV_GROUPS):
            ck_ref[g] = comp_ref[:, g * HEAD_DIM:(g + 1) * HEAD_DIM].astype(BF16)
            cv_ref[g] = comp_ref[:, (KV_GROUPS + g) * HEAD_DIM:(KV_GROUPS + g + 1) * HEAD_DIM].astype(BF16)

    scale = HEAD_DIM ** -0.5
    rows = GROUP_HEADS * tq
    blk = lax.broadcasted_iota(jnp.int32, (1, LANES), 1)
    q0 = pos0 + qi * tq
    assert tq & (tq - 1) == 0
    qpos_st = q0 + (lax.broadcasted_iota(jnp.int32, (rows, 1), 0) & (tq - 1))
    mask = (qpos_st - ((blk + 1) * NSA_BLOCK - 1) >= 0) & (blk < n_cmp)
    cur = (q0 + lax.broadcasted_iota(jnp.int32, (tq, 1), 0)) // NSA_BLOCK
    scores = []
    for g in range(KV_GROUPS):
        heads = [g * GROUP_HEADS + r for r in range(GROUP_HEADS)]
        bias = bias_ref[g]
        if t_new != tq:
            bias = pltpu.roll(bias, qi * (tq // NSA_BLOCK) + (LANES - CMP_BIAS_LANE0), 1)
        q = jnp.concatenate([q_ref[:, h * HEAD_DIM:(h + 1) * HEAD_DIM] for h in heads], 0).astype(BF16)
        scores.append(_dot_nt(q, ck_ref[g]) * scale + bias)
    probs = []
    for g in range(KV_GROUPS):
        p, l = _softmax_rows(scores[g], mask)
        probs.append(p / jnp.maximum(l, 1e-30))
    for g in range(KV_GROUPS):
        o = jnp.dot(probs[g].astype(BF16), cv_ref[g], preferred_element_type=F32)
        for r in range(GROUP_HEADS):
            h = g * GROUP_HEADS + r
            o_ref[:, h * HEAD_DIM:(h + 1) * HEAD_DIM] = o[r * tq:(r + 1) * tq]
    for g in range(KV_GROUPS):
        imp = probs[g][0:tq]
        for r in range(1, GROUP_HEADS):
            imp = imp + probs[g][r * tq:(r + 1) * tq]
        imp = jnp.where(blk == cur, 2.0, jnp.where(blk > cur, -1.0, imp))
        imp = jnp.where(blk < n_sel, imp, -2.0)
        n_top = min(NSA_TOPN, n_sel)
        cols = slice(g * LANES, (g + 1) * LANES)

        def by_rank(imp=imp, cols=cols):
            rank = jnp.zeros((tq, LANES), F32)
            for i in range(n_sel):
                col = imp[:, i:i + 1]
                ahead = (col > imp) | ((col == imp) & (blk > i))
                rank = rank + jnp.where(ahead, 1.0, 0.0)
            msel_ref[:, cols] = jnp.where((rank < float(n_top)) & (blk < n_sel), 1.0, 0.0)

        def first_blocks(cols=cols):
            msel_ref[:, cols] = jnp.where(blk < n_top, 1.0, 0.0) + jnp.zeros((tq, LANES), F32)

        if t_new == tq:
            if pos0 + tq <= n_top * NSA_BLOCK:
                first_blocks()
            else:
                by_rank()
        else:
            early = pos0 + (qi + 1) * tq <= n_top * NSA_BLOCK
            pl.when(early)(first_blocks)
            pl.when(jnp.logical_not(early))(by_rank)


def _on_causal_width(qi, tq, widths, tile):
    if len(widths) == 1:
        tile(widths[0], True)
        return
    need = (qi * tq + tq - 1) // widths[0]
    for nw, w in enumerate(widths):
        pl.when(need == nw)(functools.partial(tile, w, nw == 0))


def _attn_body(pt_ref, q_ref, kvn_ref, *rest, mode, seqs, n_pages, **statics):
    del pt_ref
    pages, rest = rest[:seqs * n_pages], rest[seqs * n_pages:]
    m_ref = None
    if mode in ("sel", "mask"):
        m_ref, rest = rest[0], rest[1:]
    band_ref, o_ref, *scratch = rest
    for sq in range(seqs):
        _attn_one(_seq_view(q_ref, sq, seqs), _seq_view(kvn_ref, sq, seqs), pages[sq * n_pages:(sq + 1) * n_pages],
                  None if m_ref is None else _seq_view(m_ref, sq, seqs), band_ref, _seq_view(o_ref, sq, seqs),
                  *scratch, mode=mode, n_pages=n_pages, **statics)


def _attn_one(q_ref, kvn_ref, page_refs, m_ref, band_ref, o_ref, kc_ref, vc_ref, s_ref, cap_ref, *maybe_p_ref,
              mode, n_pages, t_new, tq, pos0, widths):
    p_ref = maybe_p_ref[0] if maybe_p_ref else s_ref
    qi = pl.program_id(1)
    single = t_new == tq
    n_keys = n_pages * PAGE + t_new
    kbase = pos0 - n_pages * PAGE
    scale = HEAD_DIM ** -0.5
    q0 = pos0 if single else pos0 + qi * tq

    @pl.when(qi == 0)
    def _():
        for c, parts in enumerate(_kv_chunks(page_refs, kvn_ref, t_new)):
            rows = slice(c * PAGE, (c + 1) * PAGE)
            for g in range(KV_GROUPS):
                kc_ref[g, rows, :] = parts[g].astype(BF16)
                vc_ref[g, rows, :] = parts[KV_GROUPS + g].astype(BF16)

    def tile(c0, w, band_at, maybe_first):
        qpos = q0 + lax.broadcasted_iota(jnp.int32, (tq, 1), 0)
        col = c0 + lax.broadcasted_iota(jnp.int32, (1, w), 1)
        dist = qpos - (kbase + col)
        visible = (dist >= 0) & (col < n_keys)
        if mode == "win":
            visible = visible & (dist < NSA_WINDOW)
        if mode == "mask":
            visible = visible & (m_ref[:, 0:w] > 0.5)
        keys = pl.ds(c0, w)
        groups = range(KV_GROUPS)
        for g in groups:
            mask = visible
            if mode == "sel":
                expand = (lax.broadcasted_iota(jnp.int32, (LANES, w), 1) // NSA_BLOCK
                          == lax.broadcasted_iota(jnp.int32, (LANES, w), 0))
                chosen = jnp.dot(m_ref[:, g * LANES:(g + 1) * LANES].astype(BF16),
                                 jnp.where(expand, 1.0, 0.0).astype(BF16), preferred_element_type=F32)
                mask = visible & (chosen > 0.5)
            cap_ref[g, :, 0:w] = jnp.where(mask, jnp.inf, NEG_INF)
            heads = [g * GROUP_HEADS + r for r in range(GROUP_HEADS)]
            q = jnp.concatenate([q_ref[:, h * HEAD_DIM:(h + 1) * HEAD_DIM] for h in heads], 0).astype(BF16)
            s_ref[g, :, 0:w] = _dot_nt(q, kc_ref[g, keys, :]) * (scale * LOG2E)
        for g in groups:
            if band_at is not None:
                s_ref[g, :, band_at:band_at + BAND] += band_ref[g]
            else:
                if maybe_first:
                    @pl.when(qi == 0)
                    def _():
                        s_ref[g, :, 0:PAGE] += band_ref[g, :, PAGE:]

                @pl.when(qi > 0)
                def _():
                    s_ref[g, :, pl.ds(pl.multiple_of(q0 - PAGE - kbase, PAGE), BAND)] += band_ref[g]
        sums, alive = [], []
        for g in groups:
            for r in range(GROUP_HEADS):
                rows = slice(r * tq, (r + 1) * tq)
                m = jnp.max(jnp.minimum(s_ref[g, rows, 0:w], cap_ref[g, :, 0:w]), -1, keepdims=True)
                p = jnp.exp2(jnp.minimum(s_ref[g, rows, 0:w], cap_ref[g, :, 0:w]) - m)
                p_ref[g, rows, 0:w] = p.astype(p_ref.dtype)
                sums.append(jnp.sum(p, -1, keepdims=True))
                alive.append(m > NEG_INF)
        for g in groups:
            o = jnp.dot(p_ref[g, :, 0:w].astype(BF16), vc_ref[g, keys, :], preferred_element_type=F32)
            for r in range(GROUP_HEADS):
                h = g * GROUP_HEADS + r
                o_h = o[r * tq:(r + 1) * tq] / jnp.maximum(sums[h], 1e-30)
                o_ref[:, h * HEAD_DIM:(h + 1) * HEAD_DIM] = jnp.where(alive[h], o_h, 0.0)

    if single:
        tile(0, widths[0], pos0 - PAGE - kbase, False)
    elif mode == "win":
        wch = NSA_WINDOW // PAGE
        pl.when(qi < wch)(functools.partial(tile, 0, NSA_WINDOW, None, True))
        pl.when(qi >= wch)(lambda: tile(pl.multiple_of((qi - wch) * PAGE, PAGE), NSA_WINDOW + PAGE,
                                        NSA_WINDOW - PAGE, False))
    else:
        _on_causal_width(qi, tq, widths, lambda w, first: tile(0, w, None, first))


def _index_body(pt_ref, qidx_ref, miscq_ref, misck_ref, *rest, seqs, n_pages, **statics):
    del pt_ref
    pages, (o_ref, kidx_ref) = rest[:seqs * n_pages], rest[seqs * n_pages:]
    for sq in range(seqs):
        _index_one(_seq_view(qidx_ref, sq, seqs), _seq_view(miscq_ref, sq, seqs), _seq_view(misck_ref, sq, seqs),
                   pages[sq * n_pages:(sq + 1) * n_pages], _seq_view(o_ref, sq, seqs), kidx_ref,
                   n_pages=n_pages, **statics)


def _index_one(qidx_ref, miscq_ref, misck_ref, ipage_refs, o_ref, kidx_ref, *, n_pages, t_new, tq, pos0, widths):
    qi = pl.program_id(1)
    lk = o_ref.shape[1]
    n_keys = n_pages * PAGE + t_new
    kbase = pos0 - n_pages * PAGE
    q0 = pos0 if t_new == tq else pos0 + qi * tq

    @pl.when(qi == 0)
    def _():
        for c, r in enumerate(ipage_refs):
            kidx_ref[:, c * PAGE:(c + 1) * PAGE] = r[0].astype(BF16)
        for c, x in enumerate(_new_chunks(misck_ref, t_new)):
            cols = slice((n_pages + c) * PAGE, (n_pages + c + 1) * PAGE)
            kidx_ref[:, cols] = x.T[MISC_KI:MISC_KI + IDX_DIM, :].astype(BF16)

    def tile(w, maybe_first):
        del maybe_first
        qpos = q0 + lax.broadcasted_iota(jnp.int32, (tq, 1), 0)
        col = lax.broadcasted_iota(jnp.int32, (1, w), 1)
        visible = (qpos - (kbase + col) >= 0) & (col < n_keys)
        q = jnp.concatenate([qidx_ref[:, hh * IDX_DIM:(hh + 1) * IDX_DIM] for hh in range(IDX_HEADS)], 0)
        sc = jnp.dot(q.astype(BF16), kidx_ref[:, 0:w], preferred_element_type=F32)
        score = jnp.zeros((tq, w), F32)
        for hh in range(IDX_HEADS):
            wi = miscq_ref[:, MISC_WI + hh:MISC_WI + hh + 1] * (IDX_HEADS ** -0.5)
            score = score + jnp.maximum(sc[hh * tq:(hh + 1) * tq] * (IDX_DIM ** -0.5), 0.0) * wi
        o_ref[:, 0:w] = jnp.where(visible, score, NEG_INF)
        if w < lk:
            o_ref[:, w:] = jnp.full((tq, lk - w), NEG_INF, F32)

    _on_causal_width(qi, tq, widths, tile)


def _topk_body(s_ref, m_ref, key_ref, *, k, nq, tr, widths):
    lk = s_ref.shape[1]
    assert lk <= 1 << TOPK_INDEX_BITS
    neg_key = int(np.array(NEG_INF, np.float32).view(np.int32)) ^ 0x7FFFFFFF
    kf = float(k)

    def tile(w, maybe_first):
        del maybe_first
        bits = lax.bitcast_convert_type(s_ref[:, 0:w] + 0.0, jnp.int32)
        key_ref[:, 0:w] = jnp.where(bits >= 0, bits, bits ^ 0x7FFFFFFF)
        col = lax.broadcasted_iota(jnp.int32, (1, w), 1)
        unseen = float(lk - w)

        groups = [slice(a * (tr // TOPK_ROW_GROUPS), (a + 1) * (tr // TOPK_ROW_GROUPS)) for a in range(TOPK_ROW_GROUPS)]
        zeros = tuple(jnp.zeros((tr // TOPK_ROW_GROUPS, 1), jnp.int32) for _ in groups)

        def thr_step(i, tus):
            out = []
            for rows, tu in zip(groups, tus):
                cand = tu | jnp.left_shift(jnp.int32(1), TOPK_KEY_BITS - 1 - i)
                cs = cand ^ INT_MIN
                cnt = jnp.sum(jnp.where(key_ref[rows, 0:w] >= cs, 1.0, 0.0), -1, keepdims=True)
                cnt = cnt + jnp.where(cs <= neg_key, unseen, 0.0)
                out.append(jnp.where(cnt >= kf, cand, tu))
            return tuple(out)

        thr = jnp.concatenate(lax.fori_loop(0, TOPK_KEY_BITS, thr_step, zeros, unroll=8), 0) ^ INT_MIN
        key = key_ref[:, 0:w]
        above = key > thr
        tied = key == thr
        need = kf - jnp.sum(jnp.where(above, 1.0, 0.0), -1, keepdims=True)

        def tie_step(i, j0s):
            out = []
            for rows, j0 in zip(groups, j0s):
                cand = j0 | jnp.left_shift(jnp.int32(1), TOPK_INDEX_BITS - 1 - i)
                hit = (key_ref[rows, 0:w] == thr[rows]) & (col < cand)
                cnt = jnp.sum(jnp.where(hit, 1.0, 0.0), -1, keepdims=True)
                out.append(jnp.where(cnt < need[rows], cand, j0))
            return tuple(out)

        m_ref[:, 0:w] = jnp.where(above | tied, 1.0, 0.0)
        n_tied = jnp.sum(jnp.where(tied, 1.0, 0.0), -1, keepdims=True)
        excess = jnp.max(jnp.where(thr > neg_key, n_tied - need, 0.0))

        @pl.when(excess > 0.0)
        def _():
            j0 = jnp.concatenate(lax.fori_loop(0, TOPK_INDEX_BITS, tie_step, zeros, unroll=4), 0)
            k2 = key_ref[:, 0:w]
            m_ref[:, 0:w] = jnp.where((k2 > thr) | ((k2 == thr) & (col <= j0)), 1.0, 0.0)
        if w < lk:
            m_ref[:, w:] = jnp.zeros((tr, lk - w), F32)

    _on_causal_width(pl.program_id(0) % nq, tr, widths, tile)


class _Group:
    def __init__(self, row0, n_seq, t_new, tq, pos0, n_pages, seqs=1):
        assert t_new % tq == 0 and row0 % (seqs * tq) == 0 and row0 % (seqs * t_new) == 0
        assert n_seq % seqs == 0 and (seqs == 1 or t_new == tq)
        self.seqs = seqs
        assert pos0 == n_pages * PAGE or n_pages * PAGE < pos0
        assert t_new == tq or (tq == PAGE and pos0 == 0)
        assert t_new % PAGE == 0 or t_new % PAGE < NSA_BLOCK
        self.row0, self.n_seq, self.t_new, self.tq, self.pos0, self.n_pages = row0, n_seq, t_new, tq, pos0, n_pages
        self.nq = t_new // tq
        self.rows = n_seq * t_new
        self.lk = (n_pages + -(-t_new // PAGE)) * PAGE
        self.n_keys = n_pages * PAGE + t_new

    def grid(self):
        return (self.n_seq // self.seqs, self.nq)

    def q_spec(self, width, col):
        rows = self.seqs * self.tq
        return pl.BlockSpec((rows, width), lambda b, qi, pt: (self.row0 // rows + b * self.nq + qi, col // width))

    def seq_spec(self, width, col):
        rows = self.seqs * self.t_new
        return pl.BlockSpec((rows, width), lambda b, qi, pt: (self.row0 // rows + b, col // width))

    def page_specs(self, shape):
        return [pl.BlockSpec((1,) + shape, lambda b, qi, pt, sq=sq, p=p: (pt[b * self.seqs + sq, p], 0, 0))
                for sq in range(self.seqs) for p in range(self.n_pages)]

    def page_args(self, pool):
        return [pool] * (self.seqs * self.n_pages)

    def out_spec(self, width):
        return pl.BlockSpec((self.seqs * self.tq, width), lambda b, qi, pt: (b * self.nq + qi, 0))

    def statics(self):
        return dict(seqs=self.seqs, n_pages=self.n_pages, t_new=self.t_new, tq=self.tq, pos0=self.pos0)

    def widths(self):
        if self.nq == 1:
            return (self.lk,)
        step = CAUSAL_WIDTH_STEP
        assert self.lk % step == 0
        return tuple(range(step, self.lk + 1, step))


def _cmp_bias_table(rel_bias, grp):
    lane0 = 0 if grp.nq == 1 else CMP_BIAS_LANE0
    tab = rel_bias[_BUCKETS]
    pieces, n_far = [], 0
    for lane in range(LANES + 1):
        d0 = grp.pos0 - (NSA_BLOCK * (lane - lane0 + 1) - 1)
        plain = lane < LANES and (d0 >= PAGE or d0 + grp.tq - 1 < 0)
        if plain:
            n_far += 1
            continue
        if n_far:
            pieces.append(jnp.broadcast_to(rel_bias[N_BUCKETS - 1], (grp.tq, n_far, rel_bias.shape[1])))
            n_far = 0
        if lane < LANES:
            pieces.append(tab[np.clip(d0 + np.arange(grp.tq), 0, BAND - 1)][:, None, :])
    table = jnp.transpose(jnp.concatenate(pieces, 1), (2, 0, 1))
    return table.reshape(-1, GROUP_HEADS * grp.tq, LANES)


def nsa_compress(grp, z, page_table, pool, wexp, bias):
    kv_w = KV_GROUPS * 2 * HEAD_DIM
    qw = KV_GROUPS * GROUP_HEADS * HEAD_DIM
    const2 = lambda b, qi, pt: (0, 0)
    return pl.pallas_call(
        functools.partial(_cmp_body, **grp.statics()),
        grid_spec=pltpu.PrefetchScalarGridSpec(
            num_scalar_prefetch=1,
            grid=grp.grid(),
            in_specs=[grp.q_spec(qw, COL_QN), grp.seq_spec(kv_w, COL_KVC)] + grp.page_specs(KV_PAGE)
            + [pl.BlockSpec((PAGE, kv_w), const2), pl.BlockSpec(bias.shape, lambda b, qi, pt: (0, 0, 0))],
            out_specs=[grp.out_spec(qw), grp.out_spec(KV_GROUPS * LANES)],
            scratch_shapes=[pltpu.VMEM((LANES, kv_w), F32),
                            pltpu.VMEM((KV_GROUPS, LANES, HEAD_DIM), BF16),
                            pltpu.VMEM((KV_GROUPS, LANES, HEAD_DIM), BF16)]),
        out_shape=[jax.ShapeDtypeStruct((grp.rows, qw), F32),
                   jax.ShapeDtypeStruct((grp.rows, KV_GROUPS * LANES), F32)],
        compiler_params=_cparams("parallel", "arbitrary"),
        name="nsa_compress",
    )(page_table, z, z, *grp.page_args(pool), wexp, bias)


def sparse_attention(mode, grp, z, page_table, pool, band, *, q_col, kv_col, mask=None):
    kv_w = KV_GROUPS * 2 * HEAD_DIM
    qw = KV_GROUPS * GROUP_HEADS * HEAD_DIM
    in_specs = [grp.q_spec(qw, q_col), grp.seq_spec(kv_w, kv_col)] + grp.page_specs(KV_PAGE)
    args = [z, z] + grp.page_args(pool)
    if mode in ("sel", "mask"):
        in_specs.append(grp.out_spec(mask.shape[1]))
        args.append(mask)
    in_specs.append(pl.BlockSpec((KV_GROUPS, GROUP_HEADS * grp.tq, BAND), lambda b, qi, pt: (0, 0, 0)))
    args.append(band)
    widths = grp.widths()
    s_cols = max(widths) if (mode != "win" or grp.nq == 1) else NSA_WINDOW + PAGE
    rows = GROUP_HEADS * grp.tq
    scratch = [pltpu.VMEM((KV_GROUPS, grp.lk, HEAD_DIM), BF16),
               pltpu.VMEM((KV_GROUPS, grp.lk, HEAD_DIM), BF16),
               pltpu.VMEM((KV_GROUPS, rows, s_cols), F32),
               pltpu.VMEM((KV_GROUPS, grp.tq, s_cols), F32)]
    if grp.tq % BF16_SUBLANES == 0:
        scratch.append(pltpu.VMEM((KV_GROUPS, rows, s_cols), BF16))
    return pl.pallas_call(
        functools.partial(_attn_body, mode=mode, widths=widths, **grp.statics()),
        grid_spec=pltpu.PrefetchScalarGridSpec(
            num_scalar_prefetch=1,
            grid=grp.grid(),
            in_specs=in_specs,
            out_specs=grp.out_spec(qw),
            scratch_shapes=scratch),
        out_shape=jax.ShapeDtypeStruct((grp.rows, qw), F32),
        compiler_params=_cparams("parallel", "arbitrary"),
        name="sparse_attention_" + mode,
    )(page_table, *args)


def dsa_index_scores(grp, z, page_table, idx_pool):
    return pl.pallas_call(
        functools.partial(_index_body, widths=grp.widths(), **grp.statics()),
        grid_spec=pltpu.PrefetchScalarGridSpec(
            num_scalar_prefetch=1,
            grid=grp.grid(),
            in_specs=[grp.q_spec(IDX_HEADS * IDX_DIM, COL_QI), grp.q_spec(LANES, COL_MISC),
                      grp.seq_spec(LANES, COL_MISC)] + grp.page_specs((IDX_DIM, PAGE)),
            out_specs=grp.out_spec(grp.lk),
            scratch_shapes=[pltpu.VMEM((IDX_DIM, grp.lk), BF16)]),
        out_shape=jax.ShapeDtypeStruct((grp.rows, grp.lk), F32),
        compiler_params=_cparams("parallel", "arbitrary"),
        name="dsa_index_scores",
    )(page_table, z, z, z, *grp.page_args(idx_pool))


def topk_mask(grp, scores, k):
    rows, lk = scores.shape
    tr = PAGE
    assert rows % tr == 0 and (grp.nq == 1 or grp.tq == tr)
    blk = pl.BlockSpec((tr, lk), lambda i: (i, 0))
    return pl.pallas_call(
        functools.partial(_topk_body, k=k, nq=grp.nq, tr=tr, widths=grp.widths()),
        grid=(rows // tr,),
        in_specs=[blk],
        out_specs=blk,
        out_shape=jax.ShapeDtypeStruct((rows, lk), F32),
        scratch_shapes=[pltpu.VMEM((tr, lk), jnp.int32)],
        compiler_params=_cparams("parallel"),
        name="topk_mask",
    )(scores)


def _combine_body(oc_ref, os_ref, ow_ref, od_ref, misc_ref, y_ref):
    n_heads = KV_GROUPS * GROUP_HEADS
    gates = jax.nn.sigmoid(misc_ref[:, MISC_GATES:MISC_GATES + 3 * n_heads])
    for h in range(n_heads):
        hs = slice(h * HEAD_DIM, (h + 1) * HEAD_DIM)
        o = (gates[:, 3 * h:3 * h + 1] * oc_ref[:, hs] + gates[:, 3 * h + 1:3 * h + 2] * os_ref[:, hs]
             + gates[:, 3 * h + 2:3 * h + 3] * ow_ref[:, hs])
        y_ref[:, hs] = o.astype(y_ref.dtype)
    y_ref[:, n_heads * HEAD_DIM:] = od_ref[...].astype(y_ref.dtype)


def nsa_dsa_combine(o_c, o_s, o_w, o_d, z, row0, *, tm):
    m, w = o_c.shape
    assert m % tm == 0 and row0 % tm == 0
    blk = pl.BlockSpec((tm, w), lambda i: (i, 0))
    return pl.pallas_call(
        _combine_body,
        grid=(m // tm,),
        in_specs=[blk, blk, blk, blk, pl.BlockSpec((tm, LANES), lambda i: (row0 // tm + i, COL_MISC // LANES))],
        out_specs=pl.BlockSpec((tm, 2 * w), lambda i: (i, 0)),
        out_shape=jax.ShapeDtypeStruct((m, 2 * w), BF16),
        compiler_params=_cparams("parallel"),
        name="nsa_dsa_combine",
    )(o_c, o_s, o_w, o_d, z)


def _band_tiles(rel_bias, tq):
    delta = (rel_bias[_BUCKETS] - rel_bias[N_BUCKETS - 1]).T
    rev = jnp.concatenate([delta[:, ::-1], jnp.zeros((delta.shape[0], PAGE), delta.dtype)], 1)
    tiles = jnp.stack([rev[:, PAGE - 1 - i:PAGE - 1 - i + BAND] for i in range(tq)], 1)
    return tiles.reshape(-1, GROUP_HEADS * tq, BAND) * LOG2E


def _widen_cd_w_in(w):
    sizes = (1024, 512, 512, 512, 24, 1024, 512, 512, 64, 8)
    q_n, kv_c, kv_s, kv_w, gates, q_d, kv_d, q_i, k_i, w_i = jnp.split(w, np.cumsum(sizes)[:-1].tolist(), axis=-1)
    cols = [q_n, q_d, kv_c, kv_s, kv_w, kv_d, q_i, k_i, gates, w_i]
    used = sum(c.shape[-1] for c in cols)
    return jnp.concatenate(cols + [jnp.zeros(w.shape[:-1] + (NZ - used,), w.dtype)], axis=-1)


def kernel(x_prompt, x_sample, state_conv, state_pool, cache_nsa_cmp, cache_nsa_sel, cache_nsa_win, cache_dsa_kv, cache_dsa_idx, page_table, norm_mix, norm_ffn, norm_final, ab_w_in, ab_conv_w, ab_conv_b, ab_ln_g, ab_ln_b, ab_pool_w, ab_pool_scale, ab_w_out, cd_w_in, cd_w_cmp, cd_w_out, rel_bias, ffn_w1, ffn_w2):
    bp, tp, d_model = x_prompt.shape
    bs, ts, _ = x_sample.shape
    mp, ms = bp * tp, bs * ts
    depth = norm_mix.shape[0]
    n_pages = page_table.shape[1]
    n_pool = cache_nsa_cmp.shape[1]
    past_len = n_pages * PAGE
    assert cache_nsa_cmp.shape[2] == PAGE
    win_len = cache_nsa_win.shape[2]
    assert win_len % PAGE == 0 and win_len == NSA_WINDOW and tp >= NSA_WINDOW
    kv_w = KV_GROUPS * 2 * HEAD_DIM

    xs = [x_prompt.reshape(mp, d_model), x_sample.reshape(ms, d_model)]
    grp_p = _Group(0, bp, tp, PAGE, 0, 0)
    grp_s = _Group(mp, bs, ts, ts, past_len, n_pages)
    grp_sc = _Group(mp, bs, ts, ts, past_len, n_pages, SAMPLE_SEQS_PER_STEP)
    grp_sw = _Group(mp, bs, ts, ts, past_len, win_len // PAGE, SAMPLE_SEQS_PER_STEP)
    no_pages = jnp.zeros((1, 1), jnp.int32)
    win_pages = jnp.arange(bs * (win_len // PAGE), dtype=jnp.int32).reshape(bs, win_len // PAGE)

    outs = {k: [] for k in ("conv_p", "conv_s", "pool_p", "pool_s", "cmp_p", "cmp_s", "sel_p", "sel_s",
                            "win_p", "win_s", "dsa_p", "dsa_s", "idx_p", "idx_s")}
    weights = {"ab_w_in": ab_w_in.astype(BF16), "ab_w_out": ab_w_out.astype(BF16),
               "cd_w_in": _widen_cd_w_in(cd_w_in).astype(BF16), "cd_w_out": cd_w_out.astype(BF16),
               "ffn_w1": ffn_w1, "ffn_w2": ffn_w2.astype(BF16)}
    y_p = y_s = None
    for i in range(depth):
        j = i // 2
        if i % 2 == 0:
            d_conv = ab_conv_w.shape[2]
            z = norm_matmul(xs, norm_mix[i], weights["ab_w_in"], j)
            mid_p, u_p = ab_mid_prompt(z, bp, tp, ab_conv_w[j], ab_conv_b[j], ab_ln_g[j], ab_ln_b[j],
                                       ab_pool_w[j], ab_pool_scale[j])
            mid_s, conv_s, pool_s = ab_mid_step(z, mp, bs, ts, past_len, state_conv[j], state_pool[j], ab_conv_w[j],
                                                ab_conv_b[j], ab_ln_g[j], ab_ln_b[j], ab_pool_w[j], ab_pool_scale[j])
            xs = [matmul_residual([mid_p, mid_s], weights["ab_w_out"], j, xs)]
            outs["conv_p"].append(u_p.reshape(bp, tp, d_conv)[:, tp - CONV_BUF:])
            outs["conv_s"].append(conv_s)
            outs["pool_p"].append(jnp.stack([z[(b + 1) * tp - POOL_BUF:(b + 1) * tp, 2 * d_conv:] for b in range(bp)]))
            outs["pool_s"].append(pool_s)
        else:
            z = norm_matmul(xs, norm_mix[i], weights["cd_w_in"], j)
            nsa_bias = rel_bias[:, :KV_GROUPS * GROUP_HEADS]
            band_p, band_s = _band_tiles(rel_bias, grp_p.tq), _band_tiles(rel_bias, grp_s.tq)
            wexp = jnp.tile(jnp.repeat(jnp.transpose(cd_w_cmp[j], (1, 0, 2)).reshape(NSA_BLOCK, 2 * KV_GROUPS),
                                       HEAD_DIM, axis=1), (PAGE // NSA_BLOCK, 1))
            pt = page_table + j * n_pool
            pools = [c.reshape((-1,) + KV_PAGE) for c in (cache_nsa_cmp, cache_nsa_sel, cache_dsa_kv)]
            idx_pool = jnp.swapaxes(cache_dsa_idx, 2, 3).reshape(-1, IDX_DIM, PAGE)
            win_pool = cache_nsa_win.reshape((-1,) + KV_PAGE)
            wpt = win_pages + j * bs * (win_len // PAGE)
            mids = []
            for grp, gc, gw, ptab, wtab, band in ((grp_p, grp_p, grp_p, no_pages, no_pages, band_p),
                                                  (grp_s, grp_sc, grp_sw, pt, wpt, band_s)):
                o_c, msel = nsa_compress(gc, z, ptab, pools[0], wexp, _cmp_bias_table(nsa_bias, gc))
                o_s = sparse_attention("sel", gc, z, ptab, pools[1], band[:KV_GROUPS],
                                       q_col=COL_QN, kv_col=COL_KVS, mask=msel)
                o_w = sparse_attention("win", gw, z, wtab, win_pool, band[:KV_GROUPS],
                                       q_col=COL_QN, kv_col=COL_KVW)
                top = topk_mask(grp, dsa_index_scores(grp, z, ptab, idx_pool), min(DSA_TOPK, grp.n_keys // 4))
                o_d = sparse_attention("mask", gc, z, ptab, pools[2], band[KV_GROUPS:],
                                       q_col=COL_QD, kv_col=COL_KVD, mask=top)
                mids.append(nsa_dsa_combine(o_c, o_s, o_w, o_d, z, grp.row0, tm=min(512, grp.rows)))
            xs = [matmul_residual(mids, weights["cd_w_out"], j, xs)]

            def kv_out(col, width, tail):
                seg = z[:, col:col + width]
                return seg[:mp].reshape((bp, tp) + tail), seg[mp:].reshape((bs, ts) + tail)

            kv_tail = (2, KV_GROUPS, HEAD_DIM)
            for name, col in (("cmp", COL_KVC), ("sel", COL_KVS), ("dsa", COL_KVD)):
                p_new, s_new = kv_out(col, kv_w, kv_tail)
                outs[name + "_p"].append(p_new)
                outs[name + "_s"].append(s_new)
            w_p, w_s = kv_out(COL_KVW, kv_w, kv_tail)
            outs["win_p"].append(w_p[:, tp - NSA_WINDOW:])
            outs["win_s"].append(jnp.concatenate([cache_nsa_win[j], w_s], 1)[:, ts:])
            i_p, i_s = kv_out(COL_MISC + MISC_KI, IDX_DIM, (IDX_DIM,))
            outs["idx_p"].append(i_p)
            outs["idx_s"].append(i_s)
        a = norm_matmul(xs, norm_ffn[i], weights["ffn_w1"], i, relu2=True, out_dtype=BF16)
        if i == depth - 1:
            y_p, y_s = matmul_residual([a], weights["ffn_w2"], i, xs, norm_final, split_out=(mp, ms))
        else:
            xs = [matmul_residual([a], weights["ffn_w2"], i, xs)]

    st = {k: jnp.stack(v) for k, v in outs.items()}
    return (y_p.reshape(bp, tp, d_model), y_s.reshape(bs, ts, d_model),
            st["conv_p"], st["conv_s"], st["pool_p"], st["pool_s"], st["cmp_p"], st["cmp_s"],
            st["sel_p"], st["sel_s"], st["win_p"], st["win_s"], st["dsa_p"], st["dsa_s"],
            st["idx_p"], st["idx_s"])
```

```python
import functools
import math

import numpy as np
import jax
import jax.numpy as jnp
from jax import lax
from jax.experimental import pallas as pl
from jax.experimental.pallas import tpu as pltpu

F32 = jnp.float32
BF16 = jnp.bfloat16

EPS = 1e-6
NEG_INF = -1e30
LOG2E = math.log2(math.e)
HEAD_DIM = 128
LANES = 128
SUBLANES = 8
BF16_SUBLANES = 16
CONV_WIDTH = 31
CONV_BUF = CONV_WIDTH - 1
POOL_WINDOWS = (2, 4, 8, 16)
POOL_BUF = max(POOL_WINDOWS) - 1
HALO = 32
POOL_HALO = 16
VMEM_LIMIT = 56 * 1024 * 1024


def _cparams(*sem):
    return pltpu.CompilerParams(dimension_semantics=sem, vmem_limit_bytes=VMEM_LIMIT)


class _Rows:
    def __init__(self, arrays, tm):
        self.arrays = list(arrays)
        self.tm = tm
        assert all(a.shape[0] % tm == 0 for a in self.arrays)
        self.tiles = [a.shape[0] // tm for a in self.arrays]
        self.n_tiles = sum(self.tiles)
        self.n = len(self.arrays)

    def specs(self, width, col):
        out, t0 = [], 0
        for nt in self.tiles:
            out.append(pl.BlockSpec((self.tm, width), lambda i, j, t0=t0, nt=nt: (jnp.clip(i - t0, 0, nt - 1), col(j))))
            t0 += nt
        return out

    def select(self, i, refs, fn):
        if self.n == 1:
            fn(refs[0])
            return
        t0 = 0
        for nt, ref in zip(self.tiles, refs):
            pl.when((i >= t0) & (i < t0 + nt))(functools.partial(fn, ref))
            t0 += nt


def _rmsnorm_rows(x, g):
    return (x * lax.rsqrt(jnp.mean(x * x, -1, keepdims=True) + EPS)) * g


def _norm_matmul_body(*refs, rows, relu2):
    x_refs = refs[:rows.n]
    g_ref, w_ref, o_ref, h_ref = refs[rows.n:]

    @pl.when(pl.program_id(1) == 0)
    def _():
        def norm(x_ref):
            h_ref[...] = _rmsnorm_rows(x_ref[...], g_ref[...]).astype(BF16)

        rows.select(pl.program_id(0), x_refs, norm)

    y = jnp.dot(h_ref[...], w_ref[...].astype(BF16), preferred_element_type=F32)
    if relu2:
        y = jnp.square(jnp.maximum(y, 0.0))
    o_ref[...] = y.astype(o_ref.dtype)


def norm_matmul(xs, g, w, layer, *, relu2=False, out_dtype=F32, tm=1024, tn=1024):
    rows = _Rows(xs, tm)
    _, d, n = w.shape
    assert n % tn == 0
    return pl.pallas_call(
        functools.partial(_norm_matmul_body, rows=rows, relu2=relu2),
        grid=(rows.n_tiles, n // tn),
        in_specs=rows.specs(d, lambda j: 0) + [pl.BlockSpec((1, d), lambda i, j: (0, 0)),
                                               pl.BlockSpec((None, d, tn), lambda i, j: (layer, 0, j))],
        out_specs=pl.BlockSpec((tm, tn), lambda i, j: (i, j)),
        out_shape=jax.ShapeDtypeStruct((rows.n_tiles * tm, n), out_dtype),
        scratch_shapes=[pltpu.VMEM((tm, d), BF16)],
        compiler_params=_cparams("parallel", "arbitrary"),
        name="norm_matmul",
    )(*rows.arrays, g.reshape(1, d), w)


def _matmul_residual_body(*refs, a_rows, r_rows, o_rows, final_norm, single_k):
    a_refs, refs = refs[:a_rows.n], refs[a_rows.n:]
    w_ref, refs = refs[0], refs[1:]
    r_refs, refs = refs[:r_rows.n], refs[r_rows.n:]
    if final_norm:
        g_ref, refs = refs[0], refs[1:]
    o_refs, acc_ref = refs[:o_rows.n], refs[o_rows.n]
    i, k = pl.program_id(0), pl.program_id(1)

    def product(a_ref):
        return jnp.dot(a_ref[...], w_ref[...], preferred_element_type=F32)

    def assign(a_ref):
        acc_ref[...] = product(a_ref)

    def accumulate(a_ref):
        acc_ref[...] += product(a_ref)

    if single_k:
        a_rows.select(i, a_refs, assign)
    else:
        @pl.when(k == 0)
        def _():
            acc_ref[...] = jnp.zeros_like(acc_ref)

        a_rows.select(i, a_refs, accumulate)

    @pl.when(k == pl.num_programs(1) - 1)
    def _():
        def add_residual(r_ref):
            acc_ref[...] += r_ref[...]

        def write(o_ref):
            o = acc_ref[...]
            o_ref[...] = _rmsnorm_rows(o, g_ref[...]) if final_norm else o

        r_rows.select(i, r_refs, add_residual)
        o_rows.select(i, o_refs, write)


def matmul_residual(a_list, w, layer, r_list, g_final=None, *, split_out=None, tm=512, tk=2048):
    a_rows, r_rows = _Rows(a_list, tm), _Rows(r_list, tm)
    _, kdim, n = w.shape
    m = a_rows.n_tiles * tm
    assert kdim % tk == 0 and r_rows.n_tiles == a_rows.n_tiles
    final_norm = g_final is not None
    o_rows = _Rows([jax.ShapeDtypeStruct((r, n), F32) for r in (split_out or (m,))], tm)
    assert o_rows.n_tiles == a_rows.n_tiles
    in_specs = (a_rows.specs(tk, lambda k: k) + [pl.BlockSpec((None, tk, n), lambda i, k: (layer, k, 0))]
                + r_rows.specs(n, lambda k: 0))
    args = a_rows.arrays + [w] + r_rows.arrays
    if final_norm:
        in_specs.append(pl.BlockSpec((1, n), lambda i, k: (0, 0)))
        args.append(g_final.reshape(1, n))
    out = pl.pallas_call(
        functools.partial(_matmul_residual_body, a_rows=a_rows, r_rows=r_rows, o_rows=o_rows, final_norm=final_norm,
                          single_k=kdim == tk),
        grid=(a_rows.n_tiles, kdim // tk),
        in_specs=in_specs,
        out_specs=o_rows.specs(n, lambda k: 0),
        out_shape=o_rows.arrays,
        scratch_shapes=[pltpu.VMEM((tm, n), F32)],
        compiler_params=_cparams("parallel", "arbitrary"),
        name="matmul_residual",
    )(*args)
    return out if split_out else out[0]


def _layernorm_silu(c, g, b):
    mu = jnp.mean(c, -1, keepdims=True)
    xc = c - mu
    y = xc * lax.rsqrt(jnp.mean(xc * xc, -1, keepdims=True) + EPS)
    y = y * g + b
    return y * jax.nn.sigmoid(y)


def _ab_mid_body(z_ref, zp_ref, cw_ref, cb_ref, lg_ref, lb_ref, pw_ref, ps_ref, y_ref, u_ref,
                 ext_ref, vext_ref, conv_ref, *, tt, d_conv, d_pool):
    ti = pl.program_id(1)
    keep = (ti > 0).astype(F32)
    a_p = zp_ref[:, 0:d_conv]
    g_p = zp_ref[:, d_conv:2 * d_conv]
    ext_ref[0:HALO, :] = a_p * jax.nn.sigmoid(g_p) * keep
    vext_ref[0:HALO, :] = zp_ref[:, 2 * d_conv:] * keep
    u = z_ref[:, 0:d_conv] * jax.nn.sigmoid(z_ref[:, d_conv:2 * d_conv])
    ext_ref[HALO:, :] = u
    u_ref[...] = u
    vext_ref[HALO:, :] = z_ref[:, 2 * d_conv:]

    off = HALO - CONV_BUF
    sub = SUBLANES
    for c in range(d_conv // LANES):
        cs = slice(c * LANES, (c + 1) * LANES)
        acc = jnp.zeros((tt, LANES), F32)
        for s in range(sub):
            n = tt if s == 0 else tt + sub
            part = jnp.zeros((n, LANES), F32)
            for j in range(CONV_WIDTH):
                if (off + j) % sub == s:
                    start = off + j - s
                    part = part + cw_ref[j:j + 1, cs] * ext_ref[start:start + n, cs]
            acc = acc + part[s:s + tt]
        conv_ref[:, cs] = acc + cb_ref[:, cs]
    y_ref[:, 0:d_conv] = _layernorm_silu(conv_ref[...], lg_ref[...], lb_ref[...]).astype(y_ref.dtype)

    pos = ti * tt + lax.broadcasted_iota(jnp.int32, (tt, 1), 0)
    pg = d_pool // len(POOL_WINDOWS)
    for gi, w in enumerate(POOL_WINDOWS):
        gs = slice(gi * pg, (gi + 1) * pg)
        tok = vext_ref[HALO:, gs]
        acc = tok
        for i in range(1, w):
            acc = acc + vext_ref[HALO - i:HALO - i + tt, gs]
        cnt = jnp.minimum(pos + 1, w).astype(F32)
        d = acc / cnt - tok
        yp = jnp.dot(d.astype(BF16), pw_ref[gi], preferred_element_type=F32) * ps_ref[:, gs]
        y_ref[:, d_conv + gi * pg:d_conv + (gi + 1) * pg] = yp.astype(y_ref.dtype)


def ab_mid_prompt(z, n_seq, t_len, conv_w, conv_b, ln_g, ln_b, pool_w, pool_scale, *, tt=256):
    d_conv = conv_w.shape[1]
    d_pool = pool_scale.shape[0]
    nt = t_len // tt
    hb = tt // HALO
    row = lambda b, t: (b * nt + t, 0)
    const = lambda b, t: (0, 0)
    return pl.pallas_call(
        functools.partial(_ab_mid_body, tt=tt, d_conv=d_conv, d_pool=d_pool),
        grid=(n_seq, nt),
        in_specs=[pl.BlockSpec((tt, z.shape[1]), row),
                  pl.BlockSpec((HALO, z.shape[1]), lambda b, t: (jnp.maximum((b * nt + t) * hb - 1, 0), 0)),
                  pl.BlockSpec(conv_w.shape, const),
                  pl.BlockSpec((1, d_conv), const),
                  pl.BlockSpec((1, d_conv), const),
                  pl.BlockSpec((1, d_conv), const),
                  pl.BlockSpec(pool_w.shape, lambda b, t: (0, 0, 0)),
                  pl.BlockSpec((1, d_pool), const)],
        out_specs=[pl.BlockSpec((tt, d_conv + d_pool), row),
                   pl.BlockSpec((tt, d_conv), row)],
        out_shape=[jax.ShapeDtypeStruct((n_seq * t_len, d_conv + d_pool), BF16),
                   jax.ShapeDtypeStruct((n_seq * t_len, d_conv), F32)],
        scratch_shapes=[pltpu.VMEM((HALO + tt, d_conv), F32),
                        pltpu.VMEM((HALO + tt, d_pool), F32),
                        pltpu.VMEM((tt, d_conv), F32)],
        compiler_params=_cparams("parallel", "parallel"),
        name="ab_mid_prompt",
    )(z, z, conv_w, conv_b.reshape(1, -1), ln_g.reshape(1, -1), ln_b.reshape(1, -1),
      pool_w.astype(BF16), pool_scale.reshape(1, -1))


def _ab_mid_step_body(z_ref, sc_ref, sp_ref, cw_ref, cb_ref, lg_ref, lb_ref, pw_ref, ps_ref,
                      y_ref, nc_ref, np_ref, ext_ref, vext_ref, *, nb, t, pos0, d_conv, d_pool):
    e0 = HALO - CONV_BUF
    p0 = POOL_HALO - POOL_BUF
    z = z_ref[...].reshape(nb, t, z_ref.shape[1])
    u = z[:, :, 0:d_conv] * jax.nn.sigmoid(z[:, :, d_conv:2 * d_conv])
    ext_ref[:, e0:HALO, :] = sc_ref[...]
    ext_ref[:, HALO:, :] = u
    vext_ref[:, p0:POOL_HALO, :] = sp_ref[...]
    vext_ref[:, POOL_HALO:, :] = z[:, :, 2 * d_conv:]
    nc_ref[...] = ext_ref[:, HALO + t - CONV_BUF:, :]
    np_ref[...] = vext_ref[:, POOL_HALO + t - POOL_BUF:, :]

    acc = jnp.zeros((nb, t, d_conv), F32)
    for j in range(CONV_WIDTH):
        acc = acc + cw_ref[j:j + 1, :][None] * ext_ref[:, e0 + j:e0 + j + t, :]
    c = acc + cb_ref[...][None]
    yc = _layernorm_silu(c, lg_ref[...][None], lb_ref[...][None])
    y_ref[:, 0:d_conv] = yc.reshape(nb * t, d_conv).astype(y_ref.dtype)

    pg = d_pool // len(POOL_WINDOWS)
    for gi, w in enumerate(POOL_WINDOWS):
        gs = slice(gi * pg, (gi + 1) * pg)
        tok = vext_ref[:, POOL_HALO:, gs]
        acc = tok
        for i in range(1, w):
            acc = acc + vext_ref[:, POOL_HALO - i:POOL_HALO - i + t, gs]
        cnt = jnp.minimum(pos0 + 1 + lax.broadcasted_iota(jnp.int32, (1, t, 1), 1), w).astype(F32)
        d = (acc / cnt - tok).reshape(nb * t, pg)
        yp = jnp.dot(d.astype(BF16), pw_ref[gi], preferred_element_type=F32) * ps_ref[:, gs]
        y_ref[:, d_conv + gi * pg:d_conv + (gi + 1) * pg] = yp.astype(y_ref.dtype)


def ab_mid_step(z, row0, n_seq, t, pos0, state_conv, state_pool, conv_w, conv_b, ln_g, ln_b, pool_w,
                pool_scale, *, nb=16):
    d_conv = conv_w.shape[1]
    d_pool = pool_scale.shape[0]
    rb = nb * t
    assert row0 % rb == 0 and n_seq % nb == 0
    const = lambda i: (0, 0)
    seq3 = lambda i: (i, 0, 0)
    return pl.pallas_call(
        functools.partial(_ab_mid_step_body, nb=nb, t=t, pos0=pos0, d_conv=d_conv, d_pool=d_pool),
        grid=(n_seq // nb,),
        in_specs=[pl.BlockSpec((rb, z.shape[1]), lambda i: (row0 // rb + i, 0)),
                  pl.BlockSpec((nb, CONV_BUF, d_conv), seq3),
                  pl.BlockSpec((nb, POOL_BUF, d_pool), seq3),
                  pl.BlockSpec(conv_w.shape, const),
                  pl.BlockSpec((1, d_conv), const),
                  pl.BlockSpec((1, d_conv), const),
                  pl.BlockSpec((1, d_conv), const),
                  pl.BlockSpec(pool_w.shape, lambda i: (0, 0, 0)),
                  pl.BlockSpec((1, d_pool), const)],
        out_specs=[pl.BlockSpec((rb, d_conv + d_pool), lambda i: (i, 0)),
                   pl.BlockSpec((nb, CONV_BUF, d_conv), seq3),
                   pl.BlockSpec((nb, POOL_BUF, d_pool), seq3)],
        out_shape=[jax.ShapeDtypeStruct((n_seq * t, d_conv + d_pool), BF16),
                   jax.ShapeDtypeStruct((n_seq, CONV_BUF, d_conv), F32),
                   jax.ShapeDtypeStruct((n_seq, POOL_BUF, d_pool), F32)],
        scratch_shapes=[pltpu.VMEM((nb, HALO + t, d_conv), F32),
                        pltpu.VMEM((nb, POOL_HALO + t, d_pool), F32)],
        compiler_params=_cparams("parallel"),
        name="ab_mid_step",
    )(z, state_conv, state_pool, conv_w, conv_b.reshape(1, -1), ln_g.reshape(1, -1),
      ln_b.reshape(1, -1), pool_w.astype(BF16), pool_scale.reshape(1, -1))


N_BUCKETS = 32
MAX_DISTANCE = 128
NSA_BLOCK = 64
NSA_TOPN = 16
NSA_WINDOW = 512
DSA_TOPK = 256
IDX_HEADS = 8
IDX_DIM = 64
KV_GROUPS = 2
GROUP_HEADS = 4
PAGE = 128
BAND = 2 * PAGE
INT_MIN = -2 ** 31
KV_PAGE = (2 * KV_GROUPS * PAGE, HEAD_DIM)
TOPK_ROW_GROUPS = 4
TOPK_KEY_BITS = 32
TOPK_INDEX_BITS = 12
SAMPLE_SEQS_PER_STEP = 4
CAUSAL_WIDTH_STEP = PAGE
CMP_BIAS_LANE0 = 64

COL_QN, COL_QD, COL_KVC, COL_KVS, COL_KVW, COL_KVD, COL_QI, COL_MISC = 0, 1024, 2048, 2560, 3072, 3584, 4096, 4608
MISC_KI, MISC_GATES, MISC_WI = 0, 64, 88
NZ = 5120


def _bucket_np(n):
    n = np.maximum(np.asarray(n, np.int32), 0)
    exact = N_BUCKETS // 2
    nf = np.maximum(n, 1).astype(np.float32)
    big = exact + (np.log(nf / np.float32(exact)) / np.float32(math.log(MAX_DISTANCE / exact))
                   * np.float32(N_BUCKETS - exact)).astype(np.int32)
    return np.where(n < exact, n, np.minimum(big, N_BUCKETS - 1))


_BUCKETS = _bucket_np(np.arange(BAND))
assert _BUCKETS[PAGE:].min() == N_BUCKETS - 1


def _softmax_rows(s, mask):
    s = jnp.where(mask, s, NEG_INF)
    m = jnp.max(s, -1, keepdims=True)
    p = jnp.where(mask, jnp.exp(s - m), 0.0)
    return p, jnp.sum(p, -1, keepdims=True)


def _dot_nt(a, b):
    return lax.dot_general(a, b, (((1,), (1,)), ((), ())), preferred_element_type=F32)


def _new_chunks(new_ref, t_new):
    chunks = [new_ref[c * PAGE:(c + 1) * PAGE, :] for c in range(t_new // PAGE)]
    rem = t_new % PAGE
    if rem:
        tail = new_ref[(t_new // PAGE) * PAGE:, :]
        chunks.append(jnp.concatenate([tail, jnp.zeros((PAGE - rem, tail.shape[1]), F32)], 0))
    return chunks


def _kv_chunks(page_refs, new_ref, t_new):
    n_parts = 2 * KV_GROUPS
    chunks = [[r[0, pl.ds(part, PAGE, stride=n_parts), :] for part in range(n_parts)] for r in page_refs]
    for x in _new_chunks(new_ref, t_new):
        chunks.append([x[:, part * HEAD_DIM:(part + 1) * HEAD_DIM] for part in range(n_parts)])
    return chunks


def _seq_view(ref, sq, seqs):
    n = ref.shape[0] // seqs
    return ref.at[pl.ds(sq * n, n)]


def _cmp_body(pt_ref, q_ref, kvn_ref, *rest, seqs, n_pages, **statics):
    del pt_ref
    pages, (wexp_ref, bias_ref, o_ref, msel_ref, *scratch) = rest[:seqs * n_pages], rest[seqs * n_pages:]
    for sq in range(seqs):
        _cmp_one(_seq_view(q_ref, sq, seqs), _seq_view(kvn_ref, sq, seqs), pages[sq * n_pages:(sq + 1) * n_pages],
                 wexp_ref, bias_ref, _seq_view(o_ref, sq, seqs), _seq_view(msel_ref, sq, seqs), *scratch,
                 n_pages=n_pages, **statics)


def _cmp_one(q_ref, kvn_ref, page_refs, wexp_ref, bias_ref, o_ref, msel_ref, comp_ref, ck_ref, cv_ref, *,
             n_pages, t_new, tq, pos0):
    qi = pl.program_id(1)
    n_keys = n_pages * PAGE + t_new
    n_cmp = n_keys // NSA_BLOCK
    n_sel = -(-n_keys // NSA_BLOCK)
    per = PAGE // NSA_BLOCK

    @pl.when(qi == 0)
    def _():
        comp_ref[...] = jnp.zeros_like(comp_ref)
        chunks = _kv_chunks(page_refs, kvn_ref, t_new)[:n_cmp // per]
        for part in range(2 * KV_GROUPS):
            cols = slice(part * HEAD_DIM, (part + 1) * HEAD_DIM)
            xw = jnp.concatenate([parts[part] * wexp_ref[:, cols] for parts in chunks], 0)
            comp_ref[0:per * len(chunks), cols] = xw.reshape(per * len(chunks), NSA_BLOCK, HEAD_DIM).sum(1)
        for g in range(KV_GROUPS):
            ck_ref[g] = comp_ref[:, g * HEAD_DIM:(g + 1) * HEAD_DIM].astype(BF16)
            cv_ref[g] = comp_ref[:, (KV_GROUPS + g) * HEAD_DIM:(KV_GROUPS + g + 1) * HEAD_DIM].astype(BF16)

    scale = HEAD_DIM ** -0.5
    rows = GROUP_HEADS * tq
    blk = lax.broadcasted_iota(jnp.int32, (1, LANES), 1)
    q0 = pos0 + qi * tq
    assert tq & (tq - 1) == 0
    qpos_st = q0 + (lax.broadcasted_iota(jnp.int32, (rows, 1), 0) & (tq - 1))
    mask = (qpos_st - ((blk + 1) * NSA_BLOCK - 1) >= 0) & (blk < n_cmp)
    cur = (q0 + lax.broadcasted_iota(jnp.int32, (tq, 1), 0)) // NSA_BLOCK
    scores = []
    for g in range(KV_GROUPS):
        heads = [g * GROUP_HEADS + r for r in range(GROUP_HEADS)]
        bias = bias_ref[g]
        if t_new != tq:
            bias = pltpu.roll(bias, qi * (tq // NSA_BLOCK) + (LANES - CMP_BIAS_LANE0), 1)
        q = jnp.concatenate([q_ref[:, h * HEAD_DIM:(h + 1) * HEAD_DIM] for h in heads], 0).astype(BF16)
        scores.append(_dot_nt(q, ck_ref[g]) * scale + bias)
    probs = []
    for g in range(KV_GROUPS):
        p, l = _softmax_rows(scores[g], mask)
        probs.append(p / jnp.maximum(l, 1e-30))
    for g in range(KV_GROUPS):
        o = jnp.dot(probs[g].astype(BF16), cv_ref[g], preferred_element_type=F32)
        for r in range(GROUP_HEADS):
            h = g * GROUP_HEADS + r
            o_ref[:, h * HEAD_DIM:(h + 1) * HEAD_DIM] = o[r * tq:(r + 1) * tq]
    for g in range(KV_GROUPS):
        imp = probs[g][0:tq]
        for r in range(1, GROUP_HEADS):
            imp = imp + probs[g][r * tq:(r + 1) * tq]
        imp = jnp.where(blk == cur, 2.0, jnp.where(blk > cur, -1.0, imp))
        imp = jnp.where(blk < n_sel, imp, -2.0)
        n_top = min(NSA_TOPN, n_sel)
        cols = slice(g * LANES, (g + 1) * LANES)

        def by_rank(imp=imp, cols=cols):
            rank = jnp.zeros((tq, LANES), F32)
            for i in range(n_sel):
                col = imp[:, i:i + 1]
                ahead = (col > imp) | ((col == imp) & (blk > i))
                rank = rank + jnp.where(ahead, 1.0, 0.0)
            msel_ref[:, cols] = jnp.where((rank < float(n_top)) & (blk < n_sel), 1.0, 0.0)

        def first_blocks(cols=cols):
            msel_ref[:, cols] = jnp.where(blk < n_top, 1.0, 0.0) + jnp.zeros((tq, LANES), F32)

        if t_new == tq:
            if pos0 + tq <= n_top * NSA_BLOCK:
                first_blocks()
            else:
                by_rank()
        else:
            early = pos0 + (qi + 1) * tq <= n_top * NSA_BLOCK
            pl.when(early)(first_blocks)
            pl.when(jnp.logical_not(early))(by_rank)


def _on_causal_width(qi, tq, widths, tile):
    if len(widths) == 1:
        tile(widths[0], True)
        return
    need = (qi * tq + tq - 1) // widths[0]
    for nw, w in enumerate(widths):
        pl.when(need == nw)(functools.partial(tile, w, nw == 0))


def _attn_body(pt_ref, q_ref, kvn_ref, *rest, mode, seqs, n_pages, **statics):
    del pt_ref
    pages, rest = rest[:seqs * n_pages], rest[seqs * n_pages:]
    m_ref = None
    if mode in ("sel", "mask"):
        m_ref, rest = rest[0], rest[1:]
    band_ref, o_ref, *scratch = rest
    for sq in range(seqs):
        _attn_one(_seq_view(q_ref, sq, seqs), _seq_view(kvn_ref, sq, seqs), pages[sq * n_pages:(sq + 1) * n_pages],
                  None if m_ref is None else _seq_view(m_ref, sq, seqs), band_ref, _seq_view(o_ref, sq, seqs),
                  *scratch, mode=mode, n_pages=n_pages, **statics)


def _attn_one(q_ref, kvn_ref, page_refs, m_ref, band_ref, o_ref, kc_ref, vc_ref, s_ref, cap_ref, *maybe_p_ref,
              mode, n_pages, t_new, tq, pos0, widths):
    p_ref = maybe_p_ref[0] if maybe_p_ref else s_ref
    qi = pl.program_id(1)
    single = t_new == tq
    n_keys = n_pages * PAGE + t_new
    kbase = pos0 - n_pages * PAGE
    scale = HEAD_DIM ** -0.5
    q0 = pos0 if single else pos0 + qi * tq

    @pl.when(qi == 0)
    def _():
        for c, parts in enumerate(_kv_chunks(page_refs, kvn_ref, t_new)):
            rows = slice(c * PAGE, (c + 1) * PAGE)
            for g in range(KV_GROUPS):
                kc_ref[g, rows, :] = parts[g].astype(BF16)
                vc_ref[g, rows, :] = parts[KV_GROUPS + g].astype(BF16)

    def tile(c0, w, band_at, maybe_first):
        qpos = q0 + lax.broadcasted_iota(jnp.int32, (tq, 1), 0)
        col = c0 + lax.broadcasted_iota(jnp.int32, (1, w), 1)
        dist = qpos - (kbase + col)
        visible = (dist >= 0) & (col < n_keys)
        if mode == "win":
            visible = visible & (dist < NSA_WINDOW)
        if mode == "mask":
            visible = visible & (m_ref[:, 0:w] > 0.5)
        keys = pl.ds(c0, w)
        groups = range(KV_GROUPS)
        for g in groups:
            mask = visible
            if mode == "sel":
                expand = (lax.broadcasted_iota(jnp.int32, (LANES, w), 1) // NSA_BLOCK
                          == lax.broadcasted_iota(jnp.int32, (LANES, w), 0))
                chosen = jnp.dot(m_ref[:, g * LANES:(g + 1) * LANES].astype(BF16),
                                 jnp.where(expand, 1.0, 0.0).astype(BF16), preferred_element_type=F32)
                mask = visible & (chosen > 0.5)
            cap_ref[g, :, 0:w] = jnp.where(mask, jnp.inf, NEG_INF)
            heads = [g * GROUP_HEADS + r for r in range(GROUP_HEADS)]
            q = jnp.concatenate([q_ref[:, h * HEAD_DIM:(h + 1) * HEAD_DIM] for h in heads], 0).astype(BF16)
            s_ref[g, :, 0:w] = _dot_nt(q, kc_ref[g, keys, :]) * (scale * LOG2E)
        for g in groups:
            if band_at is not None:
                s_ref[g, :, band_at:band_at + BAND] += band_ref[g]
            else:
                if maybe_first:
                    @pl.when(qi == 0)
                    def _():
                        s_ref[g, :, 0:PAGE] += band_ref[g, :, PAGE:]

                @pl.when(qi > 0)
                def _():
                    s_ref[g, :, pl.ds(pl.multiple_of(q0 - PAGE - kbase, PAGE), BAND)] += band_ref[g]
        sums, alive = [], []
        for g in groups:
            for r in range(GROUP_HEADS):
                rows = slice(r * tq, (r + 1) * tq)
                m = jnp.max(jnp.minimum(s_ref[g, rows, 0:w], cap_ref[g, :, 0:w]), -1, keepdims=True)
                p = jnp.exp2(jnp.minimum(s_ref[g, rows, 0:w], cap_ref[g, :, 0:w]) - m)
                p_ref[g, rows, 0:w] = p.astype(p_ref.dtype)
                sums.append(jnp.sum(p, -1, keepdims=True))
                alive.append(m > NEG_INF)
        for g in groups:
            o = jnp.dot(p_ref[g, :, 0:w].astype(BF16), vc_ref[g, keys, :], preferred_element_type=F32)
            for r in range(GROUP_HEADS):
                h = g * GROUP_HEADS + r
                o_h = o[r * tq:(r + 1) * tq] / jnp.maximum(sums[h], 1e-30)
                o_ref[:, h * HEAD_DIM:(h + 1) * HEAD_DIM] = jnp.where(alive[h], o_h, 0.0)

    if single:
        tile(0, widths[0], pos0 - PAGE - kbase, False)
    elif mode == "win":
        wch = NSA_WINDOW // PAGE
        pl.when(qi < wch)(functools.partial(tile, 0, NSA_WINDOW, None, True))
        pl.when(qi >= wch)(lambda: tile(pl.multiple_of((qi - wch) * PAGE, PAGE), NSA_WINDOW + PAGE,
                                        NSA_WINDOW - PAGE, False))
    else:
        _on_causal_width(qi, tq, widths, lambda w, first: tile(0, w, None, first))


def _index_body(pt_ref, qidx_ref, miscq_ref, misck_ref, *rest, seqs, n_pages, **statics):
    del pt_ref
    pages, (o_ref, kidx_ref) = rest[:seqs * n_pages], rest[seqs * n_pages:]
    for sq in range(seqs):
        _index_one(_seq_view(qidx_ref, sq, seqs), _seq_view(miscq_ref, sq, seqs), _seq_view(misck_ref, sq, seqs),
                   pages[sq * n_pages:(sq + 1) * n_pages], _seq_view(o_ref, sq, seqs), kidx_ref,
                   n_pages=n_pages, **statics)


def _index_one(qidx_ref, miscq_ref, misck_ref, ipage_refs, o_ref, kidx_ref, *, n_pages, t_new, tq, pos0, widths):
    qi = pl.program_id(1)
    lk = o_ref.shape[1]
    n_keys = n_pages * PAGE + t_new
    kbase = pos0 - n_pages * PAGE
    q0 = pos0 if t_new == tq else pos0 + qi * tq

    @pl.when(qi == 0)
    def _():
        for c, r in enumerate(ipage_refs):
            kidx_ref[:, c * PAGE:(c + 1) * PAGE] = r[0].astype(BF16)
        for c, x in enumerate(_new_chunks(misck_ref, t_new)):
            cols = slice((n_pages + c) * PAGE, (n_pages + c + 1) * PAGE)
            kidx_ref[:, cols] = x.T[MISC_KI:MISC_KI + IDX_DIM, :].astype(BF16)

    def tile(w, maybe_first):
        del maybe_first
        qpos = q0 + lax.broadcasted_iota(jnp.int32, (tq, 1), 0)
        col = lax.broadcasted_iota(jnp.int32, (1, w), 1)
        visible = (qpos - (kbase + col) >= 0) & (col < n_keys)
        q = jnp.concatenate([qidx_ref[:, hh * IDX_DIM:(hh + 1) * IDX_DIM] for hh in range(IDX_HEADS)], 0)
        sc = jnp.dot(q.astype(BF16), kidx_ref[:, 0:w], preferred_element_type=F32)
        score = jnp.zeros((tq, w), F32)
        for hh in range(IDX_HEADS):
            wi = miscq_ref[:, MISC_WI + hh:MISC_WI + hh + 1] * (IDX_HEADS ** -0.5)
            score = score + jnp.maximum(sc[hh * tq:(hh + 1) * tq] * (IDX_DIM ** -0.5), 0.0) * wi
        o_ref[:, 0:w] = jnp.where(visible, score, NEG_INF)
        if w < lk:
            o_ref[:, w:] = jnp.full((tq, lk - w), NEG_INF, F32)

    _on_causal_width(qi, tq, widths, tile)


def _topk_body(s_ref, m_ref, key_ref, *, k, nq, tr, widths):
    lk = s_ref.shape[1]
    assert lk <= 1 << TOPK_INDEX_BITS
    neg_key = int(np.array(NEG_INF, np.float32).view(np.int32)) ^ 0x7FFFFFFF
    kf = float(k)

    def tile(w, maybe_first):
        del maybe_first
        bits = lax.bitcast_convert_type(s_ref[:, 0:w] + 0.0, jnp.int32)
        key_ref[:, 0:w] = jnp.where(bits >= 0, bits, bits ^ 0x7FFFFFFF)
        col = lax.broadcasted_iota(jnp.int32, (1, w), 1)
        unseen = float(lk - w)

        groups = [slice(a * (tr // TOPK_ROW_GROUPS), (a + 1) * (tr // TOPK_ROW_GROUPS)) for a in range(TOPK_ROW_GROUPS)]
        zeros = tuple(jnp.zeros((tr // TOPK_ROW_GROUPS, 1), jnp.int32) for _ in groups)

        def thr_step(i, tus):
            out = []
            for rows, tu in zip(groups, tus):
                cand = tu | jnp.left_shift(jnp.int32(1), TOPK_KEY_BITS - 1 - i)
                cs = cand ^ INT_MIN
                cnt = jnp.sum(jnp.where(key_ref[rows, 0:w] >= cs, 1.0, 0.0), -1, keepdims=True)
                cnt = cnt + jnp.where(cs <= neg_key, unseen, 0.0)
                out.append(jnp.where(cnt >= kf, cand, tu))
            return tuple(out)

        thr = jnp.concatenate(lax.fori_loop(0, TOPK_KEY_BITS, thr_step, zeros, unroll=8), 0) ^ INT_MIN
        key = key_ref[:, 0:w]
        above = key > thr
        tied = key == thr
        need = kf - jnp.sum(jnp.where(above, 1.0, 0.0), -1, keepdims=True)

        def tie_step(i, j0s):
            out = []
            for rows, j0 in zip(groups, j0s):
                cand = j0 | jnp.left_shift(jnp.int32(1), TOPK_INDEX_BITS - 1 - i)
                hit = (key_ref[rows, 0:w] == thr[rows]) & (col < cand)
                cnt = jnp.sum(jnp.where(hit, 1.0, 0.0), -1, keepdims=True)
                out.append(jnp.where(cnt < need[rows], cand, j0))
            return tuple(out)

        m_ref[:, 0:w] = jnp.where(above | tied, 1.0, 0.0)
        n_tied = jnp.sum(jnp.where(tied, 1.0, 0.0), -1, keepdims=True)
        excess = jnp.max(jnp.where(thr > neg_key, n_tied - need, 0.0))

        @pl.when(excess > 0.0)
        def _():
            j0 = jnp.concatenate(lax.fori_loop(0, TOPK_INDEX_BITS, tie_step, zeros, unroll=4), 0)
            k2 = key_ref[:, 0:w]
            m_ref[:, 0:w] = jnp.where((k2 > thr) | ((k2 == thr) & (col <= j0)), 1.0, 0.0)
        if w < lk:
            m_ref[:, w:] = jnp.zeros((tr, lk - w), F32)

    _on_causal_width(pl.program_id(0) % nq, tr, widths, tile)


class _Group:
    def __init__(self, row0, n_seq, t_new, tq, pos0, n_pages, seqs=1):
        assert t_new % tq == 0 and row0 % (seqs * tq) == 0 and row0 % (seqs * t_new) == 0
        assert n_seq % seqs == 0 and (seqs == 1 or t_new == tq)
        self.seqs = seqs
        assert pos0 == n_pages * PAGE or n_pages * PAGE < pos0
        assert t_new == tq or (tq == PAGE and pos0 == 0)
        assert t_new % PAGE == 0 or t_new % PAGE < NSA_BLOCK
        self.row0, self.n_seq, self.t_new, self.tq, self.pos0, self.n_pages = row0, n_seq, t_new, tq, pos0, n_pages
        self.nq = t_new // tq
        self.rows = n_seq * t_new
        self.lk = (n_pages + -(-t_new // PAGE)) * PAGE
        self.n_keys = n_pages * PAGE + t_new

    def grid(self):
        return (self.n_seq // self.seqs, self.nq)

    def q_spec(self, width, col):
        rows = self.seqs * self.tq
        return pl.BlockSpec((rows, width), lambda b, qi, pt: (self.row0 // rows + b * self.nq + qi, col // width))

    def seq_spec(self, width, col):
        rows = self.seqs * self.t_new
        return pl.BlockSpec((rows, width), lambda b, qi, pt: (self.row0 // rows + b, col // width))

    def page_specs(self, shape):
        return [pl.BlockSpec((1,) + shape, lambda b, qi, pt, sq=sq, p=p: (pt[b * self.seqs + sq, p], 0, 0))
                for sq in range(self.seqs) for p in range(self.n_pages)]

    def page_args(self, pool):
        return [pool] * (self.seqs * self.n_pages)

    def out_spec(self, width):
        return pl.BlockSpec((self.seqs * self.tq, width), lambda b, qi, pt: (b * self.nq + qi, 0))

    def statics(self):
        return dict(seqs=self.seqs, n_pages=self.n_pages, t_new=self.t_new, tq=self.tq, pos0=self.pos0)

    def widths(self):
        if self.nq == 1:
            return (self.lk,)
        step = CAUSAL_WIDTH_STEP
        assert self.lk % step == 0
        return tuple(range(step, self.lk + 1, step))


def _cmp_bias_table(rel_bias, grp):
    lane0 = 0 if grp.nq == 1 else CMP_BIAS_LANE0
    tab = rel_bias[_BUCKETS]
    pieces, n_far = [], 0
    for lane in range(LANES + 1):
        d0 = grp.pos0 - (NSA_BLOCK * (lane - lane0 + 1) - 1)
        plain = lane < LANES and (d0 >= PAGE or d0 + grp.tq - 1 < 0)
        if plain:
            n_far += 1
            continue
        if n_far:
            pieces.append(jnp.broadcast_to(rel_bias[N_BUCKETS - 1], (grp.tq, n_far, rel_bias.shape[1])))
            n_far = 0
        if lane < LANES:
            pieces.append(tab[np.clip(d0 + np.arange(grp.tq), 0, BAND - 1)][:, None, :])
    table = jnp.transpose(jnp.concatenate(pieces, 1), (2, 0, 1))
    return table.reshape(-1, GROUP_HEADS * grp.tq, LANES)


def nsa_compress(grp, z, page_table, pool, wexp, bias):
    kv_w = KV_GROUPS * 2 * HEAD_DIM
    qw = KV_GROUPS * GROUP_HEADS * HEAD_DIM
    const2 = lambda b, qi, pt: (0, 0)
    return pl.pallas_call(
        functools.partial(_cmp_body, **grp.statics()),
        grid_spec=pltpu.PrefetchScalarGridSpec(
            num_scalar_prefetch=1,
            grid=grp.grid(),
            in_specs=[grp.q_spec(qw, COL_QN), grp.seq_spec(kv_w, COL_KVC)] + grp.page_specs(KV_PAGE)
            + [pl.BlockSpec((PAGE, kv_w), const2), pl.BlockSpec(bias.shape, lambda b, qi, pt: (0, 0, 0))],
            out_specs=[grp.out_spec(qw), grp.out_spec(KV_GROUPS * LANES)],
            scratch_shapes=[pltpu.VMEM((LANES, kv_w), F32),
                            pltpu.VMEM((KV_GROUPS, LANES, HEAD_DIM), BF16),
                            pltpu.VMEM((KV_GROUPS, LANES, HEAD_DIM), BF16)]),
        out_shape=[jax.ShapeDtypeStruct((grp.rows, qw), F32),
                   jax.ShapeDtypeStruct((grp.rows, KV_GROUPS * LANES), F32)],
        compiler_params=_cparams("parallel", "arbitrary"),
        name="nsa_compress",
    )(page_table, z, z, *grp.page_args(pool), wexp, bias)


def sparse_attention(mode, grp, z, page_table, pool, band, *, q_col, kv_col, mask=None):
    kv_w = KV_GROUPS * 2 * HEAD_DIM
    qw = KV_GROUPS * GROUP_HEADS * HEAD_DIM
    in_specs = [grp.q_spec(qw, q_col), grp.seq_spec(kv_w, kv_col)] + grp.page_specs(KV_PAGE)
    args = [z, z] + grp.page_args(pool)
    if mode in ("sel", "mask"):
        in_specs.append(grp.out_spec(mask.shape[1]))
        args.append(mask)
    in_specs.append(pl.BlockSpec((KV_GROUPS, GROUP_HEADS * grp.tq, BAND), lambda b, qi, pt: (0, 0, 0)))
    args.append(band)
    widths = grp.widths()
    s_cols = max(widths) if (mode != "win" or grp.nq == 1) else NSA_WINDOW + PAGE
    rows = GROUP_HEADS * grp.tq
    scratch = [pltpu.VMEM((KV_GROUPS, grp.lk, HEAD_DIM), BF16),
               pltpu.VMEM((KV_GROUPS, grp.lk, HEAD_DIM), BF16),
               pltpu.VMEM((KV_GROUPS, rows, s_cols), F32),
               pltpu.VMEM((KV_GROUPS, grp.tq, s_cols), F32)]
    if grp.tq % BF16_SUBLANES == 0:
        scratch.append(pltpu.VMEM((KV_GROUPS, rows, s_cols), BF16))
    return pl.pallas_call(
        functools.partial(_attn_body, mode=mode, widths=widths, **grp.statics()),
        grid_spec=pltpu.PrefetchScalarGridSpec(
            num_scalar_prefetch=1,
            grid=grp.grid(),
            in_specs=in_specs,
            out_specs=grp.out_spec(qw),
            scratch_shapes=scratch),
        out_shape=jax.ShapeDtypeStruct((grp.rows, qw), F32),
        compiler_params=_cparams("parallel", "arbitrary"),
        name="sparse_attention_" + mode,
    )(page_table, *args)


def dsa_index_scores(grp, z, page_table, idx_pool):
    return pl.pallas_call(
        functools.partial(_index_body, widths=grp.widths(), **grp.statics()),
        grid_spec=pltpu.PrefetchScalarGridSpec(
            num_scalar_prefetch=1,
            grid=grp.grid(),
            in_specs=[grp.q_spec(IDX_HEADS * IDX_DIM, COL_QI), grp.q_spec(LANES, COL_MISC),
                      grp.seq_spec(LANES, COL_MISC)] + grp.page_specs((IDX_DIM, PAGE)),
            out_specs=grp.out_spec(grp.lk),
            scratch_shapes=[pltpu.VMEM((IDX_DIM, grp.lk), BF16)]),
        out_shape=jax.ShapeDtypeStruct((grp.rows, grp.lk), F32),
        compiler_params=_cparams("parallel", "arbitrary"),
        name="dsa_index_scores",
    )(page_table, z, z, z, *grp.page_args(idx_pool))


def topk_mask(grp, scores, k):
    rows, lk = scores.shape
    tr = PAGE
    assert rows % tr == 0 and (grp.nq == 1 or grp.tq == tr)
    blk = pl.BlockSpec((tr, lk), lambda i: (i, 0))
    return pl.pallas_call(
        functools.partial(_topk_body, k=k, nq=grp.nq, tr=tr, widths=grp.widths()),
        grid=(rows // tr,),
        in_specs=[blk],
        out_specs=blk,
        out_shape=jax.ShapeDtypeStruct((rows, lk), F32),
        scratch_shapes=[pltpu.VMEM((tr, lk), jnp.int32)],
        compiler_params=_cparams("parallel"),
        name="topk_mask",
    )(scores)


def _combine_body(oc_ref, os_ref, ow_ref, od_ref, misc_ref, y_ref):
    n_heads = KV_GROUPS * GROUP_HEADS
    gates = jax.nn.sigmoid(misc_ref[:, MISC_GATES:MISC_GATES + 3 * n_heads])
    for h in range(n_heads):
        hs = slice(h * HEAD_DIM, (h + 1) * HEAD_DIM)
        o = (gates[:, 3 * h:3 * h + 1] * oc_ref[:, hs] + gates[:, 3 * h + 1:3 * h + 2] * os_ref[:, hs]
             + gates[:, 3 * h + 2:3 * h + 3] * ow_ref[:, hs])
        y_ref[:, hs] = o.astype(y_ref.dtype)
    y_ref[:, n_heads * HEAD_DIM:] = od_ref[...].astype(y_ref.dtype)


def nsa_dsa_combine(o_c, o_s, o_w, o_d, z, row0, *, tm):
    m, w = o_c.shape
    assert m % tm == 0 and row0 % tm == 0
    blk = pl.BlockSpec((tm, w), lambda i: (i, 0))
    return pl.pallas_call(
        _combine_body,
        grid=(m // tm,),
        in_specs=[blk, blk, blk, blk, pl.BlockSpec((tm, LANES), lambda i: (row0 // tm + i, COL_MISC // LANES))],
        out_specs=pl.BlockSpec((tm, 2 * w), lambda i: (i, 0)),
        out_shape=jax.ShapeDtypeStruct((m, 2 * w), BF16),
        compiler_params=_cparams("parallel"),
        name="nsa_dsa_combine",
    )(o_c, o_s, o_w, o_d, z)


def _band_tiles(rel_bias, tq):
    delta = (rel_bias[_BUCKETS] - rel_bias[N_BUCKETS - 1]).T
    rev = jnp.concatenate([delta[:, ::-1], jnp.zeros((delta.shape[0], PAGE), delta.dtype)], 1)
    tiles = jnp.stack([rev[:, PAGE - 1 - i:PAGE - 1 - i + BAND] for i in range(tq)], 1)
    return tiles.reshape(-1, GROUP_HEADS * tq, BAND) * LOG2E


def _widen_cd_w_in(w):
    sizes = (1024, 512, 512, 512, 24, 1024, 512, 512, 64, 8)
    q_n, kv_c, kv_s, kv_w, gates, q_d, kv_d, q_i, k_i, w_i = jnp.split(w, np.cumsum(sizes)[:-1].tolist(), axis=-1)
    cols = [q_n, q_d, kv_c, kv_s, kv_w, kv_d, q_i, k_i, gates, w_i]
    used = sum(c.shape[-1] for c in cols)
    return jnp.concatenate(cols + [jnp.zeros(w.shape[:-1] + (NZ - used,), w.dtype)], axis=-1)


def kernel(x_prompt, x_sample, state_conv, state_pool, cache_nsa_cmp, cache_nsa_sel, cache_nsa_win, cache_dsa_kv, cache_dsa_idx, page_table, norm_mix, norm_ffn, norm_final, ab_w_in, ab_conv_w, ab_conv_b, ab_ln_g, ab_ln_b, ab_pool_w, ab_pool_scale, ab_w_out, cd_w_in, cd_w_cmp, cd_w_out, rel_bias, ffn_w1, ffn_w2):
    bp, tp, d_model = x_prompt.shape
    bs, ts, _ = x_sample.shape
    mp, ms = bp * tp, bs * ts
    depth = norm_mix.shape[0]
    n_pages = page_table.shape[1]
    n_pool = cache_nsa_cmp.shape[1]
    past_len = n_pages * PAGE
    assert cache_nsa_cmp.shape[2] == PAGE
    win_len = cache_nsa_win.shape[2]
    assert win_len % PAGE == 0 and win_len == NSA_WINDOW and tp >= NSA_WINDOW
    kv_w = KV_GROUPS * 2 * HEAD_DIM

    xs = [x_prompt.reshape(mp, d_model), x_sample.reshape(ms, d_model)]
    grp_p = _Group(0, bp, tp, PAGE, 0, 0)
    grp_s = _Group(mp, bs, ts, ts, past_len, n_pages)
    grp_sc = _Group(mp, bs, ts, ts, past_len, n_pages, SAMPLE_SEQS_PER_STEP)
    grp_sw = _Group(mp, bs, ts, ts, past_len, win_len // PAGE, SAMPLE_SEQS_PER_STEP)
    no_pages = jnp.zeros((1, 1), jnp.int32)
    win_pages = jnp.arange(bs * (win_len // PAGE), dtype=jnp.int32).reshape(bs, win_len // PAGE)

    outs = {k: [] for k in ("conv_p", "conv_s", "pool_p", "pool_s", "cmp_p", "cmp_s", "sel_p", "sel_s",
                            "win_p", "win_s", "dsa_p", "dsa_s", "idx_p", "idx_s")}
    weights = {"ab_w_in": ab_w_in.astype(BF16), "ab_w_out": ab_w_out.astype(BF16),
               "cd_w_in": _widen_cd_w_in(cd_w_in).astype(BF16), "cd_w_out": cd_w_out.astype(BF16),
               "ffn_w1": ffn_w1, "ffn_w2": ffn_w2.astype(BF16)}
    y_p = y_s = None
    for i in range(depth):
        j = i // 2
        if i % 2 == 0:
            d_conv = ab_conv_w.shape[2]
            z = norm_matmul(xs, norm_mix[i], weights["ab_w_in"], j)
            mid_p, u_p = ab_mid_prompt(z, bp, tp, ab_conv_w[j], ab_conv_b[j], ab_ln_g[j], ab_ln_b[j],
                                       ab_pool_w[j], ab_pool_scale[j])
            mid_s, conv_s, pool_s = ab_mid_step(z, mp, bs, ts, past_len, state_conv[j], state_pool[j], ab_conv_w[j],
                                                ab_conv_b[j], ab_ln_g[j], ab_ln_b[j], ab_pool_w[j], ab_pool_scale[j])
            xs = [matmul_residual([mid_p, mid_s], weights["ab_w_out"], j, xs)]
            outs["conv_p"].append(u_p.reshape(bp, tp, d_conv)[:, tp - CONV_BUF:])
            outs["conv_s"].append(conv_s)
            outs["pool_p"].append(jnp.stack([z[(b + 1) * tp - POOL_BUF:(b + 1) * tp, 2 * d_conv:] for b in range(bp)]))
            outs["pool_s"].append(pool_s)
        else:
            z = norm_matmul(xs, norm_mix[i], weights["cd_w_in"], j)
            nsa_bias = rel_bias[:, :KV_GROUPS * GROUP_HEADS]
            band_p, band_s = _band_tiles(rel_bias, grp_p.tq), _band_tiles(rel_bias, grp_s.tq)
            wexp = jnp.tile(jnp.repeat(jnp.transpose(cd_w_cmp[j], (1, 0, 2)).reshape(NSA_BLOCK, 2 * KV_GROUPS),
                                       HEAD_DIM, axis=1), (PAGE // NSA_BLOCK, 1))
            pt = page_table + j * n_pool
            pools = [c.reshape((-1,) + KV_PAGE) for c in (cache_nsa_cmp, cache_nsa_sel, cache_dsa_kv)]
            idx_pool = jnp.swapaxes(cache_dsa_idx, 2, 3).reshape(-1, IDX_DIM, PAGE)
            win_pool = cache_nsa_win.reshape((-1,) + KV_PAGE)
            wpt = win_pages + j * bs * (win_len // PAGE)
            mids = []
            for grp, gc, gw, ptab, wtab, band in ((grp_p, grp_p, grp_p, no_pages, no_pages, band_p),
                                                  (grp_s, grp_sc, grp_sw, pt, wpt, band_s)):
                o_c, msel = nsa_compress(gc, z, ptab, pools[0], wexp, _cmp_bias_table(nsa_bias, gc))
                o_s = sparse_attention("sel", gc, z, ptab, pools[1], band[:KV_GROUPS],
                                       q_col=COL_QN, kv_col=COL_KVS, mask=msel)
                o_w = sparse_attention("win", gw, z, wtab, win_pool, band[:KV_GROUPS],
                                       q_col=COL_QN, kv_col=COL_KVW)
                top = topk_mask(grp, dsa_index_scores(grp, z, ptab, idx_pool), min(DSA_TOPK, grp.n_keys // 4))
                o_d = sparse_attention("mask", gc, z, ptab, pools[2], band[KV_GROUPS:],
                                       q_col=COL_QD, kv_col=COL_KVD, mask=top)
                mids.append(nsa_dsa_combine(o_c, o_s, o_w, o_d, z, grp.row0, tm=min(512, grp.rows)))
            xs = [matmul_residual(mids, weights["cd_w_out"], j, xs)]

            def kv_out(col, width, tail):
                seg = z[:, col:col + width]
                return seg[:mp].reshape((bp, tp) + tail), seg[mp:].reshape((bs, ts) + tail)

            kv_tail = (2, KV_GROUPS, HEAD_DIM)
            for name, col in (("cmp", COL_KVC), ("sel", COL_KVS), ("dsa", COL_KVD)):
                p_new, s_new = kv_out(col, kv_w, kv_tail)
                outs[name + "_p"].append(p_new)
                outs[name + "_s"].append(s_new)
            w_p, w_s = kv_out(COL_KVW, kv_w, kv_tail)
            outs["win_p"].append(w_p[:, tp - NSA_WINDOW:])
            outs["win_s"].append(jnp.concatenate([cache_nsa_win[j], w_s], 1)[:, ts:])
            i_p, i_s = kv_out(COL_MISC + MISC_KI, IDX_DIM, (IDX_DIM,))
            outs["idx_p"].append(i_p)
            outs["idx_s"].append(i_s)
        a = norm_matmul(xs, norm_ffn[i], weights["ffn_w1"], i, relu2=True, out_dtype=BF16)
        if i == depth - 1:
            y_p, y_s = matmul_residual([a], weights["ffn_w2"], i, xs, norm_final, split_out=(mp, ms))
        else:
            xs = [matmul_residual([a], weights["ffn_w2"], i, xs)]

    st = {k: jnp.stack(v) for k, v in outs.items()}
    return (y_p.reshape(bp, tp, d_model), y_s.reshape(bs, ts, d_model),
            st["conv_p"], st["conv_s"], st["pool_p"], st["pool_s"], st["cmp_p"], st["cmp_s"],
            st["sel_p"], st["sel_s"], st["win_p"], st["win_s"], st["dsa_p"], st["dsa_s"],
            st["idx_p"], st["idx_s"])
```

```python
import functools
import math

import numpy as np
import jax
import jax.numpy as jnp
from jax import lax
from jax.experimental import pallas as pl
from jax.experimental.pallas import tpu as pltpu

F32 = jnp.float32
BF16 = jnp.bfloat16

EPS = 1e-6
NEG_INF = -1e30
LOG2E = math.log2(math.e)
HEAD_DIM = 128
LANES = 128
SUBLANES = 8
BF16_SUBLANES = 16
CONV_WIDTH = 31
CONV_BUF = CONV_WIDTH - 1
POOL_WINDOWS = (2, 4, 8, 16)
POOL_BUF = max(POOL_WINDOWS) - 1
HALO = 32
POOL_HALO = 16
VMEM_LIMIT = 56 * 1024 * 1024


def _cparams(*sem):
    return pltpu.CompilerParams(dimension_semantics=sem, vmem_limit_bytes=VMEM_LIMIT)


class _Rows:
    def __init__(self, arrays, tm):
        self.arrays = list(arrays)
        self.tm = tm
        assert all(a.shape[0] % tm == 0 for a in self.arrays)
        self.tiles = [a.shape[0] // tm for a in self.arrays]
        self.n_tiles = sum(self.tiles)
        self.n = len(self.arrays)

    def specs(self, width, col):
        out, t0 = [], 0
        for nt in self.tiles:
            out.append(pl.BlockSpec((self.tm, width), lambda i, j, t0=t0, nt=nt: (jnp.clip(i - t0, 0, nt - 1), col(j))))
            t0 += nt
        return out

    def select(self, i, refs, fn):
        if self.n == 1:
            fn(refs[0])
            return
        t0 = 0
        for nt, ref in zip(self.tiles, refs):
            pl.when((i >= t0) & (i < t0 + nt))(functools.partial(fn, ref))
            t0 += nt


def _rmsnorm_rows(x, g):
    return (x * lax.rsqrt(jnp.mean(x * x, -1, keepdims=True) + EPS)) * g


def _norm_matmul_body(*refs, rows, relu2):
    x_refs = refs[:rows.n]
    g_ref, w_ref, o_ref, h_ref = refs[rows.n:]

    @pl.when(pl.program_id(1) == 0)
    def _():
        def norm(x_ref):
            h_ref[...] = _rmsnorm_rows(x_ref[...], g_ref[...]).astype(BF16)

        rows.select(pl.program_id(0), x_refs, norm)

    y = jnp.dot(h_ref[...], w_ref[...].astype(BF16), preferred_element_type=F32)
    if relu2:
        y = jnp.square(jnp.maximum(y, 0.0))
    o_ref[...] = y.astype(o_ref.dtype)


def norm_matmul(xs, g, w, layer, *, relu2=False, out_dtype=F32, tm=1024, tn=1024):
    rows = _Rows(xs, tm)
    _, d, n = w.shape
    assert n % tn == 0
    return pl.pallas_call(
        functools.partial(_norm_matmul_body, rows=rows, relu2=relu2),
        grid=(rows.n_tiles, n // tn),
        in_specs=rows.specs(d, lambda j: 0) + [pl.BlockSpec((1, d), lambda i, j: (0, 0)),
                                               pl.BlockSpec((None, d, tn), lambda i, j: (layer, 0, j))],
        out_specs=pl.BlockSpec((tm, tn), lambda i, j: (i, j)),
        out_shape=jax.ShapeDtypeStruct((rows.n_tiles * tm, n), out_dtype),
        scratch_shapes=[pltpu.VMEM((tm, d), BF16)],
        compiler_params=_cparams("parallel", "arbitrary"),
        name="norm_matmul",
    )(*rows.arrays, g.reshape(1, d), w)


def _matmul_residual_body(*refs, a_rows, r_rows, o_rows, final_norm, single_k):
    a_refs, refs = refs[:a_rows.n], refs[a_rows.n:]
    w_ref, refs = refs[0], refs[1:]
    r_refs, refs = refs[:r_rows.n], refs[r_rows.n:]
    if final_norm:
        g_ref, refs = refs[0], refs[1:]
    o_refs, acc_ref = refs[:o_rows.n], refs[o_rows.n]
    i, k = pl.program_id(0), pl.program_id(1)

    def product(a_ref):
        return jnp.dot(a_ref[...], w_ref[...], preferred_element_type=F32)

    def assign(a_ref):
        acc_ref[...] = product(a_ref)

    def accumulate(a_ref):
        acc_ref[...] += product(a_ref)

    if single_k:
        a_rows.select(i, a_refs, assign)
    else:
        @pl.when(k == 0)
        def _():
            acc_ref[...] = jnp.zeros_like(acc_ref)

        a_rows.select(i, a_refs, accumulate)

    @pl.when(k == pl.num_programs(1) - 1)
    def _():
        def add_residual(r_ref):
            acc_ref[...] += r_ref[...]

        def write(o_ref):
            o = acc_ref[...]
            o_ref[...] = _rmsnorm_rows(o, g_ref[...]) if final_norm else o

        r_rows.select(i, r_refs, add_residual)
        o_rows.select(i, o_refs, write)


def matmul_residual(a_list, w, layer, r_list, g_final=None, *, split_out=None, tm=512, tk=2048):
    a_rows, r_rows = _Rows(a_list, tm), _Rows(r_list, tm)
    _, kdim, n = w.shape
    m = a_rows.n_tiles * tm
    assert kdim % tk == 0 and r_rows.n_tiles == a_rows.n_tiles
    final_norm = g_final is not None
    o_rows = _Rows([jax.ShapeDtypeStruct((r, n), F32) for r in (split_out or (m,))], tm)
    assert o_rows.n_tiles == a_rows.n_tiles
    in_specs = (a_rows.specs(tk, lambda k: k) + [pl.BlockSpec((None, tk, n), lambda i, k: (layer, k, 0))]
                + r_rows.specs(n, lambda k: 0))
    args = a_rows.arrays + [w] + r_rows.arrays
    if final_norm:
        in_specs.append(pl.BlockSpec((1, n), lambda i, k: (0, 0)))
        args.append(g_final.reshape(1, n))
    out = pl.pallas_call(
        functools.partial(_matmul_residual_body, a_rows=a_rows, r_rows=r_rows, o_rows=o_rows, final_norm=final_norm,
                          single_k=kdim == tk),
        grid=(a_rows.n_tiles, kdim // tk),
        in_specs=in_specs,
        out_specs=o_rows.specs(n, lambda k: 0),
        out_shape=o_rows.arrays,
        scratch_shapes=[pltpu.VMEM((tm, n), F32)],
        compiler_params=_cparams("parallel", "arbitrary"),
        name="matmul_residual",
    )(*args)
    return out if split_out else out[0]


def _layernorm_silu(c, g, b):
    mu = jnp.mean(c, -1, keepdims=True)
    xc = c - mu
    y = xc * lax.rsqrt(jnp.mean(xc * xc, -1, keepdims=True) + EPS)
    y = y * g + b
    return y * jax.nn.sigmoid(y)


def _ab_mid_body(z_ref, zp_ref, cw_ref, cb_ref, lg_ref, lb_ref, pw_ref, ps_ref, y_ref, u_ref,
                 ext_ref, vext_ref, conv_ref, *, tt, d_conv, d_pool):
    ti = pl.program_id(1)
    keep = (ti > 0).astype(F32)
    a_p = zp_ref[:, 0:d_conv]
    g_p = zp_ref[:, d_conv:2 * d_conv]
    ext_ref[0:HALO, :] = a_p * jax.nn.sigmoid(g_p) * keep
    vext_ref[0:HALO, :] = zp_ref[:, 2 * d_conv:] * keep
    u = z_ref[:, 0:d_conv] * jax.nn.sigmoid(z_ref[:, d_conv:2 * d_conv])
    ext_ref[HALO:, :] = u
    u_ref[...] = u
    vext_ref[HALO:, :] = z_ref[:, 2 * d_conv:]

    off = HALO - CONV_BUF
    sub = SUBLANES
    for c in range(d_conv // LANES):
        cs = slice(c * LANES, (c + 1) * LANES)
        acc = jnp.zeros((tt, LANES), F32)
        for s in range(sub):
            n = tt if s == 0 else tt + sub
            part = jnp.zeros((n, LANES), F32)
            for j in range(CONV_WIDTH):
                if (off + j) % sub == s:
                    start = off + j - s
                    part = part + cw_ref[j:j + 1, cs] * ext_ref[start:start + n, cs]
            acc = acc + part[s:s + tt]
        conv_ref[:, cs] = acc + cb_ref[:, cs]
    y_ref[:, 0:d_conv] = _layernorm_silu(conv_ref[...], lg_ref[...], lb_ref[...]).astype(y_ref.dtype)

    pos = ti * tt + lax.broadcasted_iota(jnp.int32, (tt, 1), 0)
    pg = d_pool // len(POOL_WINDOWS)
    for gi, w in enumerate(POOL_WINDOWS):
        gs = slice(gi * pg, (gi + 1) * pg)
        tok = vext_ref[HALO:, gs]
        acc = tok
        for i in range(1, w):
            acc = acc + vext_ref[HALO - i:HALO - i + tt, gs]
        cnt = jnp.minimum(pos + 1, w).astype(F32)
        d = acc / cnt - tok
        yp = jnp.dot(d.astype(BF16), pw_ref[gi], preferred_element_type=F32) * ps_ref[:, gs]
        y_ref[:, d_conv + gi * pg:d_conv + (gi + 1) * pg] = yp.astype(y_ref.dtype)


def ab_mid_prompt(z, n_seq, t_len, conv_w, conv_b, ln_g, ln_b, pool_w, pool_scale, *, tt=256):
    d_conv = conv_w.shape[1]
    d_pool = pool_scale.shape[0]
    nt = t_len // tt
    hb = tt // HALO
    row = lambda b, t: (b * nt + t, 0)
    const = lambda b, t: (0, 0)
    return pl.pallas_call(
        functools.partial(_ab_mid_body, tt=tt, d_conv=d_conv, d_pool=d_pool),
        grid=(n_seq, nt),
        in_specs=[pl.BlockSpec((tt, z.shape[1]), row),
                  pl.BlockSpec((HALO, z.shape[1]), lambda b, t: (jnp.maximum((b * nt + t) * hb - 1, 0), 0)),
                  pl.BlockSpec(conv_w.shape, const),
                  pl.BlockSpec((1, d_conv), const),
                  pl.BlockSpec((1, d_conv), const),
                  pl.BlockSpec((1, d_conv), const),
                  pl.BlockSpec(pool_w.shape, lambda b, t: (0, 0, 0)),
                  pl.BlockSpec((1, d_pool), const)],
        out_specs=[pl.BlockSpec((tt, d_conv + d_pool), row),
                   pl.BlockSpec((tt, d_conv), row)],
        out_shape=[jax.ShapeDtypeStruct((n_seq * t_len, d_conv + d_pool), BF16),
                   jax.ShapeDtypeStruct((n_seq * t_len, d_conv), F32)],
        scratch_shapes=[pltpu.VMEM((HALO + tt, d_conv), F32),
                        pltpu.VMEM((HALO + tt, d_pool), F32),
                        pltpu.VMEM((tt, d_conv), F32)],
        compiler_params=_cparams("parallel", "parallel"),
        name="ab_mid_prompt",
    )(z, z, conv_w, conv_b.reshape(1, -1), ln_g.reshape(1, -1), ln_b.reshape(1, -1),
      pool_w.astype(BF16), pool_scale.reshape(1, -1))


def _ab_mid_step_body(z_ref, sc_ref, sp_ref, cw_ref, cb_ref, lg_ref, lb_ref, pw_ref, ps_ref,
                      y_ref, nc_ref, np_ref, ext_ref, vext_ref, *, nb, t, pos0, d_conv, d_pool):
    e0 = HALO - CONV_BUF
    p0 = POOL_HALO - POOL_BUF
    z = z_ref[...].reshape(nb, t, z_ref.shape[1])
    u = z[:, :, 0:d_conv] * jax.nn.sigmoid(z[:, :, d_conv:2 * d_conv])
    ext_ref[:, e0:HALO, :] = sc_ref[...]
    ext_ref[:, HALO:, :] = u
    vext_ref[:, p0:POOL_HALO, :] = sp_ref[...]
    vext_ref[:, POOL_HALO:, :] = z[:, :, 2 * d_conv:]
    nc_ref[...] = ext_ref[:, HALO + t - CONV_BUF:, :]
    np_ref[...] = vext_ref[:, POOL_HALO + t - POOL_BUF:, :]

    acc = jnp.zeros((nb, t, d_conv), F32)
    for j in range(CONV_WIDTH):
        acc = acc + cw_ref[j:j + 1, :][None] * ext_ref[:, e0 + j:e0 + j + t, :]
    c = acc + cb_ref[...][None]
    yc = _layernorm_silu(c, lg_ref[...][None], lb_ref[...][None])
    y_ref[:, 0:d_conv] = yc.reshape(nb * t, d_conv).astype(y_ref.dtype)

    pg = d_pool // len(POOL_WINDOWS)
    for gi, w in enumerate(POOL_WINDOWS):
        gs = slice(gi * pg, (gi + 1) * pg)
        tok = vext_ref[:, POOL_HALO:, gs]
        acc = tok
        for i in range(1, w):
            acc = acc + vext_ref[:, POOL_HALO - i:POOL_HALO - i + t, gs]
        cnt = jnp.minimum(pos0 + 1 + lax.broadcasted_iota(jnp.int32, (1, t, 1), 1), w).astype(F32)
        d = (acc / cnt - tok).reshape(nb * t, pg)
        yp = jnp.dot(d.astype(BF16), pw_ref[gi], preferred_element_type=F32) * ps_ref[:, gs]
        y_ref[:, d_conv + gi * pg:d_conv + (gi + 1) * pg] = yp.astype(y_ref.dtype)


def ab_mid_step(z, row0, n_seq, t, pos0, state_conv, state_pool, conv_w, conv_b, ln_g, ln_b, pool_w,
                pool_scale, *, nb=16):
    d_conv = conv_w.shape[1]
    d_pool = pool_scale.shape[0]
    rb = nb * t
    assert row0 % rb == 0 and n_seq % nb == 0
    const = lambda i: (0, 0)
    seq3 = lambda i: (i, 0, 0)
    return pl.pallas_call(
        functools.partial(_ab_mid_step_body, nb=nb, t=t, pos0=pos0, d_conv=d_conv, d_pool=d_pool),
        grid=(n_seq // nb,),
        in_specs=[pl.BlockSpec((rb, z.shape[1]), lambda i: (row0 // rb + i, 0)),
                  pl.BlockSpec((nb, CONV_BUF, d_conv), seq3),
                  pl.BlockSpec((nb, POOL_BUF, d_pool), seq3),
                  pl.BlockSpec(conv_w.shape, const),
                  pl.BlockSpec((1, d_conv), const),
                  pl.BlockSpec((1, d_conv), const),
                  pl.BlockSpec((1, d_conv), const),
                  pl.BlockSpec(pool_w.shape, lambda i: (0, 0, 0)),
                  pl.BlockSpec((1, d_pool), const)],
        out_specs=[pl.BlockSpec((rb, d_conv + d_pool), lambda i: (i, 0)),
                   pl.BlockSpec((nb, CONV_BUF, d_conv), seq3),
                   pl.BlockSpec((nb, POOL_BUF, d_pool), seq3)],
        out_shape=[jax.ShapeDtypeStruct((n_seq * t, d_conv + d_pool), BF16),
                   jax.ShapeDtypeStruct((n_seq, CONV_BUF, d_conv), F32),
                   jax.ShapeDtypeStruct((n_seq, POOL_BUF, d_pool), F32)],
        scratch_shapes=[pltpu.VMEM((nb, HALO + t, d_conv), F32),
                        pltpu.VMEM((nb, POOL_HALO + t, d_pool), F32)],
        compiler_params=_cparams("parallel"),
        name="ab_mid_step",
    )(z, state_conv, state_pool, conv_w, conv_b.reshape(1, -1), ln_g.reshape(1, -1),
      ln_b.reshape(1, -1), pool_w.astype(BF16), pool_scale.reshape(1, -1))


N_BUCKETS = 32
MAX_DISTANCE = 128
NSA_BLOCK = 64
NSA_TOPN = 16
NSA_WINDOW = 512
DSA_TOPK = 256
IDX_HEADS = 8
IDX_DIM = 64
KV_GROUPS = 2
GROUP_HEADS = 4
PAGE = 128
BAND = 2 * PAGE
INT_MIN = -2 ** 31
KV_PAGE = (2 * KV_GROUPS * PAGE, HEAD_DIM)
TOPK_ROW_GROUPS = 4
TOPK_KEY_BITS = 32
TOPK_INDEX_BITS = 12
SAMPLE_SEQS_PER_STEP = 4
CAUSAL_WIDTH_STEP = 2 * PAGE
CMP_BIAS_LANE0 = 64

COL_QN, COL_QD, COL_KVC, COL_KVS, COL_KVW, COL_KVD, COL_QI, COL_MISC = 0, 1024, 2048, 2560, 3072, 3584, 4096, 4608
MISC_KI, MISC_GATES, MISC_WI = 0, 64, 88
NZ = 5120


def _bucket_np(n):
    n = np.maximum(np.asarray(n, np.int32), 0)
    exact = N_BUCKETS // 2
    nf = np.maximum(n, 1).astype(np.float32)
    big = exact + (np.log(nf / np.float32(exact)) / np.float32(math.log(MAX_DISTANCE / exact))
                   * np.float32(N_BUCKETS - exact)).astype(np.int32)
    return np.where(n < exact, n, np.minimum(big, N_BUCKETS - 1))


_BUCKETS = _bucket_np(np.arange(BAND))
assert _BUCKETS[PAGE:].min() == N_BUCKETS - 1


def _softmax_rows(s, mask):
    s = jnp.where(mask, s, NEG_INF)
    m = jnp.max(s, -1, keepdims=True)
    p = jnp.where(mask, jnp.exp(s - m), 0.0)
    return p, jnp.sum(p, -1, keepdims=True)


def _dot_nt(a, b):
    return lax.dot_general(a, b, (((1,), (1,)), ((), ())), preferred_element_type=F32)


def _new_chunks(new_ref, t_new):
    chunks = [new_ref[c * PAGE:(c + 1) * PAGE, :] for c in range(t_new // PAGE)]
    rem = t_new % PAGE
    if rem:
        tail = new_ref[(t_new // PAGE) * PAGE:, :]
        chunks.append(jnp.concatenate([tail, jnp.zeros((PAGE - rem, tail.shape[1]), F32)], 0))
    return chunks


def _kv_chunks(page_refs, new_ref, t_new):
    n_parts = 2 * KV_GROUPS
    chunks = [[r[0, pl.ds(part, PAGE, stride=n_parts), :] for part in range(n_parts)] for r in page_refs]
    for x in _new_chunks(new_ref, t_new):
        chunks.append([x[:, part * HEAD_DIM:(part + 1) * HEAD_DIM] for part in range(n_parts)])
    return chunks


def _seq_view(ref, sq, seqs):
    n = ref.shape[0] // seqs
    return ref.at[pl.ds(sq * n, n)]


def _cmp_body(pt_ref, q_ref, kvn_ref, *rest, seqs, n_pages, **statics):
    del pt_ref
    pages, (wexp_ref, bias_ref, o_ref, msel_ref, *scratch) = rest[:seqs * n_pages], rest[seqs * n_pages:]
    for sq in range(seqs):
        _cmp_one(_seq_view(q_ref, sq, seqs), _seq_view(kvn_ref, sq, seqs), pages[sq * n_pages:(sq + 1) * n_pages],
                 wexp_ref, bias_ref, _seq_view(o_ref, sq, seqs), _seq_view(msel_ref, sq, seqs), *scratch,
                 n_pages=n_pages, **statics)


def _cmp_one(q_ref, kvn_ref, page_refs, wexp_ref, bias_ref, o_ref, msel_ref, comp_ref, ck_ref, cv_ref, *,
             n_pages, t_new, tq, pos0):
    qi = pl.program_id(1)
    n_keys = n_pages * PAGE + t_new
    n_cmp = n_keys // NSA_BLOCK
    n_sel = -(-n_keys // NSA_BLOCK)
    per = PAGE // NSA_BLOCK

    @pl.when(qi == 0)
    def _():
        comp_ref[...] = jnp.zeros_like(comp_ref)
        chunks = _kv_chunks(page_refs, kvn_ref, t_new)[:n_cmp // per]
        for part in range(2 * KV_GROUPS):
            cols = slice(part * HEAD_DIM, (part + 1) * HEAD_DIM)
            xw = jnp.concatenate([parts[part] * wexp_ref[:, cols] for parts in chunks], 0)
            comp_ref[0:per * len(chunks), cols] = xw.reshape(per * len(chunks), NSA_BLOCK, HEAD_DIM).sum(1)
        for g in range(KV_GROUPS):
            ck_ref[g] = comp_ref[:, g * HEAD_DIM:(g + 1) * HEAD_DIM].astype(BF16)
            cv_ref[g] = comp_ref[:, (KV_GROUPS + g) * HEAD_DIM:(KV_GROUPS + g + 1) * HEAD_DIM].astype(BF16)

    scale = HEAD_DIM ** -0.5
    rows = GROUP_HEADS * tq
    blk = lax.broadcasted_iota(jnp.int32, (1, LANES), 1)
    q0 = pos0 + qi * tq
    assert tq & (tq - 1) == 0
    qpos_st = q0 + (lax.broadcasted_iota(jnp.int32, (rows, 1), 0) & (tq - 1))
    mask = (qpos_st - ((blk + 1) * NSA_BLOCK - 1) >= 0) & (blk < n_cmp)
    cur = (q0 + lax.broadcasted_iota(jnp.int32, (tq, 1), 0)) // NSA_BLOCK
    scores = []
    for g in range(KV_GROUPS):
        heads = [g * GROUP_HEADS + r for r in range(GROUP_HEADS)]
        bias = bias_ref[g]
        if t_new != tq:
            bias = pltpu.roll(bias, qi * (tq // NSA_BLOCK) + (LANES - CMP_BIAS_LANE0), 1)
        q = jnp.concatenate([q_ref[:, h * HEAD_DIM:(h + 1) * HEAD_DIM] for h in heads], 0).astype(BF16)
        scores.append(_dot_nt(q, ck_ref[g]) * scale + bias)
    probs = []
    for g in range(KV_GROUPS):
        p, l = _softmax_rows(scores[g], mask)
        probs.append(p / jnp.maximum(l, 1e-30))
    for g in range(KV_GROUPS):
        o = jnp.dot(probs[g].astype(BF16), cv_ref[g], preferred_element_type=F32)
        for r in range(GROUP_HEADS):
            h = g * GROUP_HEADS + r
            o_ref[:, h * HEAD_DIM:(h + 1) * HEAD_DIM] = o[r * tq:(r + 1) * tq]
    for g in range(KV_GROUPS):
        imp = probs[g][0:tq]
        for r in range(1, GROUP_HEADS):
            imp = imp + probs[g][r * tq:(r + 1) * tq]
        imp = jnp.where(blk == cur, 2.0, jnp.where(blk > cur, -1.0, imp))
        imp = jnp.where(blk < n_sel, imp, -2.0)
        n_top = min(NSA_TOPN, n_sel)
        cols = slice(g * LANES, (g + 1) * LANES)

        def by_rank(imp=imp, cols=cols):
            rank = jnp.zeros((tq, LANES), F32)
            for i in range(n_sel):
                col = imp[:, i:i + 1]
                ahead = (col > imp) | ((col == imp) & (blk > i))
                rank = rank + jnp.where(ahead, 1.0, 0.0)
            msel_ref[:, cols] = jnp.where((rank < float(n_top)) & (blk < n_sel), 1.0, 0.0)

        def first_blocks(cols=cols):
            msel_ref[:, cols] = jnp.where(blk < n_top, 1.0, 0.0) + jnp.zeros((tq, LANES), F32)

        if t_new == tq:
            if pos0 + tq <= n_top * NSA_BLOCK:
                first_blocks()
            else:
                by_rank()
        else:
            early = pos0 + (qi + 1) * tq <= n_top * NSA_BLOCK
            pl.when(early)(first_blocks)
            pl.when(jnp.logical_not(early))(by_rank)


def _on_causal_width(qi, tq, widths, tile):
    if len(widths) == 1:
        tile(widths[0], True)
        return
    need = (qi * tq + tq - 1) // widths[0]
    for nw, w in enumerate(widths):
        pl.when(need == nw)(functools.partial(tile, w, nw == 0))


def _attn_body(pt_ref, q_ref, kvn_ref, *rest, mode, seqs, n_pages, **statics):
    del pt_ref
    pages, rest = rest[:seqs * n_pages], rest[seqs * n_pages:]
    m_ref = None
    if mode in ("sel", "mask"):
        m_ref, rest = rest[0], rest[1:]
    band_ref, o_ref, *scratch = rest
    for sq in range(seqs):
        _attn_one(_seq_view(q_ref, sq, seqs), _seq_view(kvn_ref, sq, seqs), pages[sq * n_pages:(sq + 1) * n_pages],
                  None if m_ref is None else _seq_view(m_ref, sq, seqs), band_ref, _seq_view(o_ref, sq, seqs),
                  *scratch, mode=mode, n_pages=n_pages, **statics)


def _attn_one(q_ref, kvn_ref, page_refs, m_ref, band_ref, o_ref, kc_ref, vc_ref, s_ref, cap_ref, *maybe_p_ref,
              mode, n_pages, t_new, tq, pos0, widths):
    p_ref = maybe_p_ref[0] if maybe_p_ref else s_ref
    qi = pl.program_id(1)
    single = t_new == tq
    n_keys = n_pages * PAGE + t_new
    kbase = pos0 - n_pages * PAGE
    scale = HEAD_DIM ** -0.5
    q0 = pos0 if single else pos0 + qi * tq

    @pl.when(qi == 0)
    def _():
        for c, parts in enumerate(_kv_chunks(page_refs, kvn_ref, t_new)):
            rows = slice(c * PAGE, (c + 1) * PAGE)
            for g in range(KV_GROUPS):
                kc_ref[g, rows, :] = parts[g].astype(BF16)
                vc_ref[g, rows, :] = parts[KV_GROUPS + g].astype(BF16)

    def tile(c0, w, band_at, maybe_first):
        qpos = q0 + lax.broadcasted_iota(jnp.int32, (tq, 1), 0)
        col = c0 + lax.broadcasted_iota(jnp.int32, (1, w), 1)
        dist = qpos - (kbase + col)
        visible = (dist >= 0) & (col < n_keys)
        if mode == "win":
            visible = visible & (dist < NSA_WINDOW)
        if mode == "mask":
            visible = visible & (m_ref[:, 0:w] > 0.5)
        keys = pl.ds(c0, w)
        groups = range(KV_GROUPS)
        for g in groups:
            mask = visible
            if mode == "sel":
                expand = (lax.broadcasted_iota(jnp.int32, (LANES, w), 1) // NSA_BLOCK
                          == lax.broadcasted_iota(jnp.int32, (LANES, w), 0))
                chosen = jnp.dot(m_ref[:, g * LANES:(g + 1) * LANES].astype(BF16),
                                 jnp.where(expand, 1.0, 0.0).astype(BF16), preferred_element_type=F32)
                mask = visible & (chosen > 0.5)
            cap_ref[g, :, 0:w] = jnp.where(mask, jnp.inf, NEG_INF)
            heads = [g * GROUP_HEADS + r for r in range(GROUP_HEADS)]
            q = jnp.concatenate([q_ref[:, h * HEAD_DIM:(h + 1) * HEAD_DIM] for h in heads], 0).astype(BF16)
            s_ref[g, :, 0:w] = _dot_nt(q, kc_ref[g, keys, :]) * (scale * LOG2E)
        for g in groups:
            if band_at is not None:
                s_ref[g, :, band_at:band_at + BAND] += band_ref[g]
            else:
                if maybe_first:
                    @pl.when(qi == 0)
                    def _():
                        s_ref[g, :, 0:PAGE] += band_ref[g, :, PAGE:]

                @pl.when(qi > 0)
                def _():
                    s_ref[g, :, pl.ds(pl.multiple_of(q0 - PAGE - kbase, PAGE), BAND)] += band_ref[g]
        sums, alive = [], []
        for g in groups:
            for r in range(GROUP_HEADS):
                rows = slice(r * tq, (r + 1) * tq)
                m = jnp.max(jnp.minimum(s_ref[g, rows, 0:w], cap_ref[g, :, 0:w]), -1, keepdims=True)
                p = jnp.exp2(jnp.minimum(s_ref[g, rows, 0:w], cap_ref[g, :, 0:w]) - m)
                p_ref[g, rows, 0:w] = p.astype(p_ref.dtype)
                sums.append(jnp.sum(p, -1, keepdims=True))
                alive.append(m > NEG_INF)
        for g in groups:
            o = jnp.dot(p_ref[g, :, 0:w].astype(BF16), vc_ref[g, keys, :], preferred_element_type=F32)
            for r in range(GROUP_HEADS):
                h = g * GROUP_HEADS + r
                o_h = o[r * tq:(r + 1) * tq] / jnp.maximum(sums[h], 1e-30)
                o_ref[:, h * HEAD_DIM:(h + 1) * HEAD_DIM] = jnp.where(alive[h], o_h, 0.0)

    if single:
        tile(0, widths[0], pos0 - PAGE - kbase, False)
    elif mode == "win":
        wch = NSA_WINDOW // PAGE
        pl.when(qi < wch)(functools.partial(tile, 0, NSA_WINDOW, None, True))
        pl.when(qi >= wch)(lambda: tile(pl.multiple_of((qi - wch) * PAGE, PAGE), NSA_WINDOW + PAGE,
                                        NSA_WINDOW - PAGE, False))
    else:
        _on_causal_width(qi, tq, widths, lambda w, first: tile(0, w, None, first))


def _index_body(pt_ref, qidx_ref, miscq_ref, misck_ref, *rest, seqs, n_pages, **statics):
    del pt_ref
    pages, (o_ref, kidx_ref) = rest[:seqs * n_pages], rest[seqs * n_pages:]
    for sq in range(seqs):
        _index_one(_seq_view(qidx_ref, sq, seqs), _seq_view(miscq_ref, sq, seqs), _seq_view(misck_ref, sq, seqs),
                   pages[sq * n_pages:(sq + 1) * n_pages], _seq_view(o_ref, sq, seqs), kidx_ref,
                   n_pages=n_pages, **statics)


def _index_one(qidx_ref, miscq_ref, misck_ref, ipage_refs, o_ref, kidx_ref, *, n_pages, t_new, tq, pos0, widths):
    qi = pl.program_id(1)
    lk = o_ref.shape[1]
    n_keys = n_pages * PAGE + t_new
    kbase = pos0 - n_pages * PAGE
    q0 = pos0 if t_new == tq else pos0 + qi * tq

    @pl.when(qi == 0)
    def _():
        for c, r in enumerate(ipage_refs):
            kidx_ref[:, c * PAGE:(c + 1) * PAGE] = r[0].astype(BF16)
        for c, x in enumerate(_new_chunks(misck_ref, t_new)):
            cols = slice((n_pages + c) * PAGE, (n_pages + c + 1) * PAGE)
            kidx_ref[:, cols] = x.T[MISC_KI:MISC_KI + IDX_DIM, :].astype(BF16)

    def tile(w, maybe_first):
        del maybe_first
        qpos = q0 + lax.broadcasted_iota(jnp.int32, (tq, 1), 0)
        col = lax.broadcasted_iota(jnp.int32, (1, w), 1)
        visible = (qpos - (kbase + col) >= 0) & (col < n_keys)
        q = jnp.concatenate([qidx_ref[:, hh * IDX_DIM:(hh + 1) * IDX_DIM] for hh in range(IDX_HEADS)], 0)
        sc = jnp.dot(q.astype(BF16), kidx_ref[:, 0:w], preferred_element_type=F32)
        score = jnp.zeros((tq, w), F32)
        for hh in range(IDX_HEADS):
            wi = miscq_ref[:, MISC_WI + hh:MISC_WI + hh + 1] * (IDX_HEADS ** -0.5)
            score = score + jnp.maximum(sc[hh * tq:(hh + 1) * tq] * (IDX_DIM ** -0.5), 0.0) * wi
        o_ref[:, 0:w] = jnp.where(visible, score, NEG_INF)
        if w < lk:
            o_ref[:, w:] = jnp.full((tq, lk - w), NEG_INF, F32)

    _on_causal_width(qi, tq, widths, tile)


def _topk_body(s_ref, m_ref, key_ref, *, k, nq, tr, widths):
    lk = s_ref.shape[1]
    assert lk <= 1 << TOPK_INDEX_BITS
    neg_key = int(np.array(NEG_INF, np.float32).view(np.int32)) ^ 0x7FFFFFFF
    kf = float(k)

    def tile(w, maybe_first):
        del maybe_first
        bits = lax.bitcast_convert_type(s_ref[:, 0:w] + 0.0, jnp.int32)
        key_ref[:, 0:w] = jnp.where(bits >= 0, bits, bits ^ 0x7FFFFFFF)
        col = lax.broadcasted_iota(jnp.int32, (1, w), 1)
        unseen = float(lk - w)

        groups = [slice(a * (tr // TOPK_ROW_GROUPS), (a + 1) * (tr // TOPK_ROW_GROUPS)) for a in range(TOPK_ROW_GROUPS)]
        zeros = tuple(jnp.zeros((tr // TOPK_ROW_GROUPS, 1), jnp.int32) for _ in groups)

        def thr_step(i, tus):
            out = []
            for rows, tu in zip(groups, tus):
                cand = tu | jnp.left_shift(jnp.int32(1), TOPK_KEY_BITS - 1 - i)
                cs = cand ^ INT_MIN
                cnt = jnp.sum(jnp.where(key_ref[rows, 0:w] >= cs, 1.0, 0.0), -1, keepdims=True)
                cnt = cnt + jnp.where(cs <= neg_key, unseen, 0.0)
                out.append(jnp.where(cnt >= kf, cand, tu))
            return tuple(out)

        thr = jnp.concatenate(lax.fori_loop(0, TOPK_KEY_BITS, thr_step, zeros, unroll=8), 0) ^ INT_MIN
        key = key_ref[:, 0:w]
        above = key > thr
        tied = key == thr
        need = kf - jnp.sum(jnp.where(above, 1.0, 0.0), -1, keepdims=True)

        def tie_step(i, j0s):
            out = []
            for rows, j0 in zip(groups, j0s):
                cand = j0 | jnp.left_shift(jnp.int32(1), TOPK_INDEX_BITS - 1 - i)
                hit = (key_ref[rows, 0:w] == thr[rows]) & (col < cand)
                cnt = jnp.sum(jnp.where(hit, 1.0, 0.0), -1, keepdims=True)
                out.append(jnp.where(cnt < need[rows], cand, j0))
            return tuple(out)

        m_ref[:, 0:w] = jnp.where(above | tied, 1.0, 0.0)
        n_tied = jnp.sum(jnp.where(tied, 1.0, 0.0), -1, keepdims=True)
        excess = jnp.max(jnp.where(thr > neg_key, n_tied - need, 0.0))

        @pl.when(excess > 0.0)
        def _():
            j0 = jnp.concatenate(lax.fori_loop(0, TOPK_INDEX_BITS, tie_step, zeros, unroll=4), 0)
            k2 = key_ref[:, 0:w]
            m_ref[:, 0:w] = jnp.where((k2 > thr) | ((k2 == thr) & (col <= j0)), 1.0, 0.0)
        if w < lk:
            m_ref[:, w:] = jnp.zeros((tr, lk - w), F32)

    _on_causal_width(pl.program_id(0) % nq, tr, widths, tile)


class _Group:
    def __init__(self, row0, n_seq, t_new, tq, pos0, n_pages, seqs=1):
        assert t_new % tq == 0 and row0 % (seqs * tq) == 0 and row0 % (seqs * t_new) == 0
        assert n_seq % seqs == 0 and (seqs == 1 or t_new == tq)
        self.seqs = seqs
        assert pos0 == n_pages * PAGE or n_pages * PAGE < pos0
        assert t_new == tq or (tq == PAGE and pos0 == 0)
        assert t_new % PAGE == 0 or t_new % PAGE < NSA_BLOCK
        self.row0, self.n_seq, self.t_new, self.tq, self.pos0, self.n_pages = row0, n_seq, t_new, tq, pos0, n_pages
        self.nq = t_new // tq
        self.rows = n_seq * t_new
        self.lk = (n_pages + -(-t_new // PAGE)) * PAGE
        self.n_keys = n_pages * PAGE + t_new

    def grid(self):
        return (self.n_seq // self.seqs, self.nq)

    def q_spec(self, width, col):
        rows = self.seqs * self.tq
        return pl.BlockSpec((rows, width), lambda b, qi, pt: (self.row0 // rows + b * self.nq + qi, col // width))

    def seq_spec(self, width, col):
        rows = self.seqs * self.t_new
        return pl.BlockSpec((rows, width), lambda b, qi, pt: (self.row0 // rows + b, col // width))

    def page_specs(self, shape):
        return [pl.BlockSpec((1,) + shape, lambda b, qi, pt, sq=sq, p=p: (pt[b * self.seqs + sq, p], 0, 0))
                for sq in range(self.seqs) for p in range(self.n_pages)]

    def page_args(self, pool):
        return [pool] * (self.seqs * self.n_pages)

    def out_spec(self, width):
        return pl.BlockSpec((self.seqs * self.tq, width), lambda b, qi, pt: (b * self.nq + qi, 0))

    def statics(self):
        return dict(seqs=self.seqs, n_pages=self.n_pages, t_new=self.t_new, tq=self.tq, pos0=self.pos0)

    def widths(self):
        if self.nq == 1:
            return (self.lk,)
        step = CAUSAL_WIDTH_STEP
        assert self.lk % step == 0
        return tuple(range(step, self.lk + 1, step))


def _cmp_bias_table(rel_bias, grp):
    lane0 = 0 if grp.nq == 1 else CMP_BIAS_LANE0
    tab = rel_bias[_BUCKETS]
    pieces, n_far = [], 0
    for lane in range(LANES + 1):
        d0 = grp.pos0 - (NSA_BLOCK * (lane - lane0 + 1) - 1)
        plain = lane < LANES and (d0 >= PAGE or d0 + grp.tq - 1 < 0)
        if plain:
            n_far += 1
            continue
        if n_far:
            pieces.append(jnp.broadcast_to(rel_bias[N_BUCKETS - 1], (grp.tq, n_far, rel_bias.shape[1])))
            n_far = 0
        if lane < LANES:
            pieces.append(tab[np.clip(d0 + np.arange(grp.tq), 0, BAND - 1)][:, None, :])
    table = jnp.transpose(jnp.concatenate(pieces, 1), (2, 0, 1))
    return table.reshape(-1, GROUP_HEADS * grp.tq, LANES)


def nsa_compress(grp, z, page_table, pool, wexp, bias):
    kv_w = KV_GROUPS * 2 * HEAD_DIM
    qw = KV_GROUPS * GROUP_HEADS * HEAD_DIM
    const2 = lambda b, qi, pt: (0, 0)
    return pl.pallas_call(
        functools.partial(_cmp_body, **grp.statics()),
        grid_spec=pltpu.PrefetchScalarGridSpec(
            num_scalar_prefetch=1,
            grid=grp.grid(),
            in_specs=[grp.q_spec(qw, COL_QN), grp.seq_spec(kv_w, COL_KVC)] + grp.page_specs(KV_PAGE)
            + [pl.BlockSpec((PAGE, kv_w), const2), pl.BlockSpec(bias.shape, lambda b, qi, pt: (0, 0, 0))],
            out_specs=[grp.out_spec(qw), grp.out_spec(KV_GROUPS * LANES)],
            scratch_shapes=[pltpu.VMEM((LANES, kv_w), F32),
                            pltpu.VMEM((KV_GROUPS, LANES, HEAD_DIM), BF16),
                            pltpu.VMEM((KV_GROUPS, LANES, HEAD_DIM), BF16)]),
        out_shape=[jax.ShapeDtypeStruct((grp.rows, qw), F32),
                   jax.ShapeDtypeStruct((grp.rows, KV_GROUPS * LANES), F32)],
        compiler_params=_cparams("parallel", "arbitrary"),
        name="nsa_compress",
    )(page_table, z, z, *grp.page_args(pool), wexp, bias)


def sparse_attention(mode, grp, z, page_table, pool, band, *, q_col, kv_col, mask=None):
    kv_w = KV_GROUPS * 2 * HEAD_DIM
    qw = KV_GROUPS * GROUP_HEADS * HEAD_DIM
    in_specs = [grp.q_spec(qw, q_col), grp.seq_spec(kv_w, kv_col)] + grp.page_specs(KV_PAGE)
    args = [z, z] + grp.page_args(pool)
    if mode in ("sel", "mask"):
        in_specs.append(grp.out_spec(mask.shape[1]))
        args.append(mask)
    in_specs.append(pl.BlockSpec((KV_GROUPS, GROUP_HEADS * grp.tq, BAND), lambda b, qi, pt: (0, 0, 0)))
    args.append(band)
    widths = grp.widths()
    s_cols = max(widths) if (mode != "win" or grp.nq == 1) else NSA_WINDOW + PAGE
    rows = GROUP_HEADS * grp.tq
    scratch = [pltpu.VMEM((KV_GROUPS, grp.lk, HEAD_DIM), BF16),
               pltpu.VMEM((KV_GROUPS, grp.lk, HEAD_DIM), BF16),
               pltpu.VMEM((KV_GROUPS, rows, s_cols), F32),
               pltpu.VMEM((KV_GROUPS, grp.tq, s_cols), F32)]
    if grp.tq % BF16_SUBLANES == 0:
        scratch.append(pltpu.VMEM((KV_GROUPS, rows, s_cols), BF16))
    return pl.pallas_call(
        functools.partial(_attn_body, mode=mode, widths=widths, **grp.statics()),
        grid_spec=pltpu.PrefetchScalarGridSpec(
            num_scalar_prefetch=1,
            grid=grp.grid(),
            in_specs=in_specs,
            out_specs=grp.out_spec(qw),
            scratch_shapes=scratch),
        out_shape=jax.ShapeDtypeStruct((grp.rows, qw), F32),
        compiler_params=_cparams("parallel", "arbitrary"),
        name="sparse_attention_" + mode,
    )(page_table, *args)


def dsa_index_scores(grp, z, page_table, idx_pool):
    return pl.pallas_call(
        functools.partial(_index_body, widths=grp.widths(), **grp.statics()),
        grid_spec=pltpu.PrefetchScalarGridSpec(
            num_scalar_prefetch=1,
            grid=grp.grid(),
            in_specs=[grp.q_spec(IDX_HEADS * IDX_DIM, COL_QI), grp.q_spec(LANES, COL_MISC),
                      grp.seq_spec(LANES, COL_MISC)] + grp.page_specs((IDX_DIM, PAGE)),
            out_specs=grp.out_spec(grp.lk),
            scratch_shapes=[pltpu.VMEM((IDX_DIM, grp.lk), BF16)]),
        out_shape=jax.ShapeDtypeStruct((grp.rows, grp.lk), F32),
        compiler_params=_cparams("parallel", "arbitrary"),
        name="dsa_index_scores",
    )(page_table, z, z, z, *grp.page_args(idx_pool))


def topk_mask(grp, scores, k):
    rows, lk = scores.shape
    tr = PAGE
    assert rows % tr == 0 and (grp.nq == 1 or grp.tq == tr)
    blk = pl.BlockSpec((tr, lk), lambda i: (i, 0))
    return pl.pallas_call(
        functools.partial(_topk_body, k=k, nq=grp.nq, tr=tr, widths=grp.widths()),
        grid=(rows // tr,),
        in_specs=[blk],
        out_specs=blk,
        out_shape=jax.ShapeDtypeStruct((rows, lk), F32),
        scratch_shapes=[pltpu.VMEM((tr, lk), jnp.int32)],
        compiler_params=_cparams("parallel"),
        name="topk_mask",
    )(scores)


def _combine_body(oc_ref, os_ref, ow_ref, od_ref, misc_ref, y_ref):
    n_heads = KV_GROUPS * GROUP_HEADS
    gates = jax.nn.sigmoid(misc_ref[:, MISC_GATES:MISC_GATES + 3 * n_heads])
    for h in range(n_heads):
        hs = slice(h * HEAD_DIM, (h + 1) * HEAD_DIM)
        o = (gates[:, 3 * h:3 * h + 1] * oc_ref[:, hs] + gates[:, 3 * h + 1:3 * h + 2] * os_ref[:, hs]
             + gates[:, 3 * h + 2:3 * h + 3] * ow_ref[:, hs])
        y_ref[:, hs] = o.astype(y_ref.dtype)
    y_ref[:, n_heads * HEAD_DIM:] = od_ref[...].astype(y_ref.dtype)


def _attn_out_body(*refs, rows):
    n = rows.n
    branch_refs = [refs[k * n:(k + 1) * n] for k in range(4)]
    misc_ref, w_ref, r_ref, o_ref, a_ref = refs[4 * n:]
    i = pl.program_id(0)
    for src in range(n):
        def gate(src=src):
            _combine_body(*[b[src] for b in branch_refs], misc_ref, a_ref)

        if n == 1:
            gate()
        else:
            t0 = sum(rows.tiles[:src])
            pl.when((i >= t0) & (i < t0 + rows.tiles[src]))(gate)
    o_ref[...] = r_ref[...] + jnp.dot(a_ref[...], w_ref[...], preferred_element_type=F32)


def attn_out_residual(branches, z, w, layer, r, *, tm=256):
    groups = list(zip(*branches))
    rows = _Rows(groups[0], tm)
    width = groups[0][0].shape[1]
    _, kdim, n = w.shape
    assert kdim == 2 * width and r.shape[0] == rows.n_tiles * tm
    in_specs = []
    for g in groups:
        in_specs += _Rows(g, tm).specs(width, lambda j: 0)
    in_specs = [pl.BlockSpec(s.block_shape, lambda i, s=s: s.index_map(i, 0)) for s in in_specs]
    in_specs += [pl.BlockSpec((tm, LANES), lambda i: (i, COL_MISC // LANES)),
                 pl.BlockSpec((None, kdim, n), lambda i: (layer, 0, 0)),
                 pl.BlockSpec((tm, n), lambda i: (i, 0))]
    return pl.pallas_call(
        functools.partial(_attn_out_body, rows=rows),
        grid=(rows.n_tiles,),
        in_specs=in_specs,
        out_specs=pl.BlockSpec((tm, n), lambda i: (i, 0)),
        out_shape=jax.ShapeDtypeStruct((rows.n_tiles * tm, n), F32),
        scratch_shapes=[pltpu.VMEM((tm, kdim), BF16)],
        compiler_params=_cparams("parallel"),
        name="attn_out_residual",
    )(*[a for g in groups for a in g], z, w, r)


def _band_tiles(rel_bias, tq):
    delta = (rel_bias[_BUCKETS] - rel_bias[N_BUCKETS - 1]).T
    rev = jnp.concatenate([delta[:, ::-1], jnp.zeros((delta.shape[0], PAGE), delta.dtype)], 1)
    tiles = jnp.stack([rev[:, PAGE - 1 - i:PAGE - 1 - i + BAND] for i in range(tq)], 1)
    return tiles.reshape(-1, GROUP_HEADS * tq, BAND) * LOG2E


def _widen_cd_w_in(w):
    sizes = (1024, 512, 512, 512, 24, 1024, 512, 512, 64, 8)
    q_n, kv_c, kv_s, kv_w, gates, q_d, kv_d, q_i, k_i, w_i = jnp.split(w, np.cumsum(sizes)[:-1].tolist(), axis=-1)
    cols = [q_n, q_d, kv_c, kv_s, kv_w, kv_d, q_i, k_i, gates, w_i]
    used = sum(c.shape[-1] for c in cols)
    return jnp.concatenate(cols + [jnp.zeros(w.shape[:-1] + (NZ - used,), w.dtype)], axis=-1)


def kernel(x_prompt, x_sample, state_conv, state_pool, cache_nsa_cmp, cache_nsa_sel, cache_nsa_win, cache_dsa_kv, cache_dsa_idx, page_table, norm_mix, norm_ffn, norm_final, ab_w_in, ab_conv_w, ab_conv_b, ab_ln_g, ab_ln_b, ab_pool_w, ab_pool_scale, ab_w_out, cd_w_in, cd_w_cmp, cd_w_out, rel_bias, ffn_w1, ffn_w2):
    bp, tp, d_model = x_prompt.shape
    bs, ts, _ = x_sample.shape
    mp, ms = bp * tp, bs * ts
    depth = norm_mix.shape[0]
    n_pages = page_table.shape[1]
    n_pool = cache_nsa_cmp.shape[1]
    past_len = n_pages * PAGE
    assert cache_nsa_cmp.shape[2] == PAGE
    win_len = cache_nsa_win.shape[2]
    assert win_len % PAGE == 0 and win_len == NSA_WINDOW and tp >= NSA_WINDOW
    kv_w = KV_GROUPS * 2 * HEAD_DIM

    xs = [x_prompt.reshape(mp, d_model), x_sample.reshape(ms, d_model)]
    grp_p = _Group(0, bp, tp, PAGE, 0, 0)
    grp_s = _Group(mp, bs, ts, ts, past_len, n_pages)
    grp_sc = _Group(mp, bs, ts, ts, past_len, n_pages, SAMPLE_SEQS_PER_STEP)
    grp_sw = _Group(mp, bs, ts, ts, past_len, win_len // PAGE, SAMPLE_SEQS_PER_STEP)
    no_pages = jnp.zeros((1, 1), jnp.int32)
    win_pages = jnp.arange(bs * (win_len // PAGE), dtype=jnp.int32).reshape(bs, win_len // PAGE)

    outs = {k: [] for k in ("conv_p", "conv_s", "pool_p", "pool_s", "cmp_p", "cmp_s", "sel_p", "sel_s",
                            "win_p", "win_s", "dsa_p", "dsa_s", "idx_p", "idx_s")}
    weights = {"ab_w_in": ab_w_in.astype(BF16), "ab_w_out": ab_w_out.astype(BF16),
               "cd_w_in": _widen_cd_w_in(cd_w_in).astype(BF16), "cd_w_out": cd_w_out.astype(BF16),
               "ffn_w1": ffn_w1, "ffn_w2": ffn_w2.astype(BF16)}
    y_p = y_s = None
    for i in range(depth):
        j = i // 2
        if i % 2 == 0:
            d_conv = ab_conv_w.shape[2]
            z = norm_matmul(xs, norm_mix[i], weights["ab_w_in"], j)
            mid_p, u_p = ab_mid_prompt(z, bp, tp, ab_conv_w[j], ab_conv_b[j], ab_ln_g[j], ab_ln_b[j],
                                       ab_pool_w[j], ab_pool_scale[j])
            mid_s, conv_s, pool_s = ab_mid_step(z, mp, bs, ts, past_len, state_conv[j], state_pool[j], ab_conv_w[j],
                                                ab_conv_b[j], ab_ln_g[j], ab_ln_b[j], ab_pool_w[j], ab_pool_scale[j])
            xs = [matmul_residual([mid_p, mid_s], weights["ab_w_out"], j, xs)]
            outs["conv_p"].append(u_p.reshape(bp, tp, d_conv)[:, tp - CONV_BUF:])
            outs["conv_s"].append(conv_s)
            outs["pool_p"].append(jnp.stack([z[(b + 1) * tp - POOL_BUF:(b + 1) * tp, 2 * d_conv:] for b in range(bp)]))
            outs["pool_s"].append(pool_s)
        else:
            z = norm_matmul(xs, norm_mix[i], weights["cd_w_in"], j)
            nsa_bias = rel_bias[:, :KV_GROUPS * GROUP_HEADS]
            band_p, band_s = _band_tiles(rel_bias, grp_p.tq), _band_tiles(rel_bias, grp_s.tq)
            wexp = jnp.tile(jnp.repeat(jnp.transpose(cd_w_cmp[j], (1, 0, 2)).reshape(NSA_BLOCK, 2 * KV_GROUPS),
                                       HEAD_DIM, axis=1), (PAGE // NSA_BLOCK, 1))
            pt = page_table + j * n_pool
            pools = [c.reshape((-1,) + KV_PAGE) for c in (cache_nsa_cmp, cache_nsa_sel, cache_dsa_kv)]
            idx_pool = jnp.swapaxes(cache_dsa_idx, 2, 3).reshape(-1, IDX_DIM, PAGE)
            win_pool = cache_nsa_win.reshape((-1,) + KV_PAGE)
            wpt = win_pages + j * bs * (win_len // PAGE)
            mids = []
            for grp, gc, gw, ptab, wtab, band in ((grp_p, grp_p, grp_p, no_pages, no_pages, band_p),
                                                  (grp_s, grp_sc, grp_sw, pt, wpt, band_s)):
                o_c, msel = nsa_compress(gc, z, ptab, pools[0], wexp, _cmp_bias_table(nsa_bias, gc))
                o_s = sparse_attention("sel", gc, z, ptab, pools[1], band[:KV_GROUPS],
                                       q_col=COL_QN, kv_col=COL_KVS, mask=msel)
                o_w = sparse_attention("win", gw, z, wtab, win_pool, band[:KV_GROUPS],
                                       q_col=COL_QN, kv_col=COL_KVW)
                top = topk_mask(grp, dsa_index_scores(grp, z, ptab, idx_pool), min(DSA_TOPK, grp.n_keys // 4))
                o_d = sparse_attention("mask", gc, z, ptab, pools[2], band[KV_GROUPS:],
                                       q_col=COL_QD, kv_col=COL_KVD, mask=top)
                mids.append((o_c, o_s, o_w, o_d))
            xs = [attn_out_residual(mids, z, weights["cd_w_out"], j, xs[0])]

            def kv_out(col, width, tail):
                seg = z[:, col:col + width]
                return seg[:mp].reshape((bp, tp) + tail), seg[mp:].reshape((bs, ts) + tail)

            kv_tail = (2, KV_GROUPS, HEAD_DIM)
            for name, col in (("cmp", COL_KVC), ("sel", COL_KVS), ("dsa", COL_KVD)):
                p_new, s_new = kv_out(col, kv_w, kv_tail)
                outs[name + "_p"].append(p_new)
                outs[name + "_s"].append(s_new)
            w_p, w_s = kv_out(COL_KVW, kv_w, kv_tail)
            outs["win_p"].append(w_p[:, tp - NSA_WINDOW:])
            outs["win_s"].append(jnp.concatenate([cache_nsa_win[j], w_s], 1)[:, ts:])
            i_p, i_s = kv_out(COL_MISC + MISC_KI, IDX_DIM, (IDX_DIM,))
            outs["idx_p"].append(i_p)
            outs["idx_s"].append(i_s)
        a = norm_matmul(xs, norm_ffn[i], weights["ffn_w1"], i, relu2=True, out_dtype=BF16)
        if i == depth - 1:
            y_p, y_s = matmul_residual([a], weights["ffn_w2"], i, xs, norm_final, split_out=(mp, ms))
        else:
            xs = [matmul_residual([a], weights["ffn_w2"], i, xs)]

    st = {k: jnp.stack(v) for k, v in outs.items()}
    return (y_p.reshape(bp, tp, d_model), y_s.reshape(bs, ts, d_model),
            st["conv_p"], st["conv_s"], st["pool_p"], st["pool_s"], st["cmp_p"], st["cmp_s"],
            st["sel_p"], st["sel_s"], st["win_p"], st["win_s"], st["dsa_p"], st["dsa_s"],
            st["idx_p"], st["idx_s"])
```

```python
import functools
import math

import numpy as np
import jax
import jax.numpy as jnp
from jax import lax
from jax.experimental import pallas as pl
from jax.experimental.pallas import tpu as pltpu

F32 = jnp.float32
BF16 = jnp.bfloat16

EPS = 1e-6
NEG_INF = -1e30
LOG2E = math.log2(math.e)
HEAD_DIM = 128
LANES = 128
SUBLANES = 8
BF16_SUBLANES = 16
CONV_WIDTH = 31
CONV_BUF = CONV_WIDTH - 1
POOL_WINDOWS = (2, 4, 8, 16)
POOL_BUF = max(POOL_WINDOWS) - 1
HALO = 32
POOL_HALO = 16
VMEM_LIMIT = 56 * 1024 * 1024


def _cparams(*sem):
    return pltpu.CompilerParams(dimension_semantics=sem, vmem_limit_bytes=VMEM_LIMIT)


class _Rows:
    def __init__(self, arrays, tm):
        self.arrays = list(arrays)
        self.tm = tm
        assert all(a.shape[0] % tm == 0 for a in self.arrays)
        self.tiles = [a.shape[0] // tm for a in self.arrays]
        self.n_tiles = sum(self.tiles)
        self.n = len(self.arrays)

    def specs(self, width, col):
        out, t0 = [], 0
        for nt in self.tiles:
            out.append(pl.BlockSpec((self.tm, width), lambda i, j, t0=t0, nt=nt: (jnp.clip(i - t0, 0, nt - 1), col(j))))
            t0 += nt
        return out

    def select(self, i, refs, fn):
        if self.n == 1:
            fn(refs[0])
            return
        t0 = 0
        for nt, ref in zip(self.tiles, refs):
            pl.when((i >= t0) & (i < t0 + nt))(functools.partial(fn, ref))
            t0 += nt


def _rmsnorm_rows(x, g):
    return (x * lax.rsqrt(jnp.mean(x * x, -1, keepdims=True) + EPS)) * g


def _norm_matmul_body(*refs, rows, relu2):
    x_refs = refs[:rows.n]
    g_ref, w_ref, o_ref, h_ref = refs[rows.n:]

    @pl.when(pl.program_id(1) == 0)
    def _():
        def norm(x_ref):
            h_ref[...] = _rmsnorm_rows(x_ref[...], g_ref[...]).astype(BF16)

        rows.select(pl.program_id(0), x_refs, norm)

    y = jnp.dot(h_ref[...], w_ref[...].astype(BF16), preferred_element_type=F32)
    if relu2:
        y = jnp.square(jnp.maximum(y, 0.0))
    o_ref[...] = y.astype(o_ref.dtype)


def norm_matmul(xs, g, w, layer, *, relu2=False, out_dtype=F32, tm=1024, tn=1024):
    rows = _Rows(xs, tm)
    _, d, n = w.shape
    assert n % tn == 0
    return pl.pallas_call(
        functools.partial(_norm_matmul_body, rows=rows, relu2=relu2),
        grid=(rows.n_tiles, n // tn),
        in_specs=rows.specs(d, lambda j: 0) + [pl.BlockSpec((1, d), lambda i, j: (0, 0)),
                                               pl.BlockSpec((None, d, tn), lambda i, j: (layer, 0, j))],
        out_specs=pl.BlockSpec((tm, tn), lambda i, j: (i, j)),
        out_shape=jax.ShapeDtypeStruct((rows.n_tiles * tm, n), out_dtype),
        scratch_shapes=[pltpu.VMEM((tm, d), BF16)],
        compiler_params=_cparams("parallel", "arbitrary"),
        name="norm_matmul",
    )(*rows.arrays, g.reshape(1, d), w)


def _matmul_residual_body(*refs, a_rows, r_rows, o_rows, final_norm, single_k):
    a_refs, refs = refs[:a_rows.n], refs[a_rows.n:]
    w_ref, refs = refs[0], refs[1:]
    r_refs, refs = refs[:r_rows.n], refs[r_rows.n:]
    if final_norm:
        g_ref, refs = refs[0], refs[1:]
    o_refs, acc_ref = refs[:o_rows.n], refs[o_rows.n]
    i, k = pl.program_id(0), pl.program_id(1)

    def product(a_ref):
        return jnp.dot(a_ref[...], w_ref[...], preferred_element_type=F32)

    def assign(a_ref):
        acc_ref[...] = product(a_ref)

    def accumulate(a_ref):
        acc_ref[...] += product(a_ref)

    if single_k:
        a_rows.select(i, a_refs, assign)
    else:
        @pl.when(k == 0)
        def _():
            acc_ref[...] = jnp.zeros_like(acc_ref)

        a_rows.select(i, a_refs, accumulate)

    @pl.when(k == pl.num_programs(1) - 1)
    def _():
        def add_residual(r_ref):
            acc_ref[...] += r_ref[...]

        def write(o_ref):
            o = acc_ref[...]
            o_ref[...] = _rmsnorm_rows(o, g_ref[...]) if final_norm else o

        r_rows.select(i, r_refs, add_residual)
        o_rows.select(i, o_refs, write)


def matmul_residual(a_list, w, layer, r_list, g_final=None, *, split_out=None, tm=512, tk=2048):
    a_rows, r_rows = _Rows(a_list, tm), _Rows(r_list, tm)
    _, kdim, n = w.shape
    m = a_rows.n_tiles * tm
    assert kdim % tk == 0 and r_rows.n_tiles == a_rows.n_tiles
    final_norm = g_final is not None
    o_rows = _Rows([jax.ShapeDtypeStruct((r, n), F32) for r in (split_out or (m,))], tm)
    assert o_rows.n_tiles == a_rows.n_tiles
    in_specs = (a_rows.specs(tk, lambda k: k) + [pl.BlockSpec((None, tk, n), lambda i, k: (layer, k, 0))]
                + r_rows.specs(n, lambda k: 0))
    args = a_rows.arrays + [w] + r_rows.arrays
    if final_norm:
        in_specs.append(pl.BlockSpec((1, n), lambda i, k: (0, 0)))
        args.append(g_final.reshape(1, n))
    out = pl.pallas_call(
        functools.partial(_matmul_residual_body, a_rows=a_rows, r_rows=r_rows, o_rows=o_rows, final_norm=final_norm,
                          single_k=kdim == tk),
        grid=(a_rows.n_tiles, kdim // tk),
        in_specs=in_specs,
        out_specs=o_rows.specs(n, lambda k: 0),
        out_shape=o_rows.arrays,
        scratch_shapes=[pltpu.VMEM((tm, n), F32)],
        compiler_params=_cparams("parallel", "arbitrary"),
        name="matmul_residual",
    )(*args)
    return out if split_out else out[0]


def _layernorm_silu(c, g, b):
    mu = jnp.mean(c, -1, keepdims=True)
    xc = c - mu
    y = xc * lax.rsqrt(jnp.mean(xc * xc, -1, keepdims=True) + EPS)
    y = y * g + b
    return y * jax.nn.sigmoid(y)


def _ab_mid_body(z_ref, zp_ref, cw_ref, cb_ref, lg_ref, lb_ref, pw_ref, ps_ref, y_ref, u_ref,
                 ext_ref, vext_ref, conv_ref, *, tt, d_conv, d_pool):
    ti = pl.program_id(1)
    keep = (ti > 0).astype(F32)
    a_p = zp_ref[:, 0:d_conv]
    g_p = zp_ref[:, d_conv:2 * d_conv]
    ext_ref[0:HALO, :] = a_p * jax.nn.sigmoid(g_p) * keep
    vext_ref[0:HALO, :] = zp_ref[:, 2 * d_conv:] * keep
    u = z_ref[:, 0:d_conv] * jax.nn.sigmoid(z_ref[:, d_conv:2 * d_conv])
    ext_ref[HALO:, :] = u
    u_ref[...] = u
    vext_ref[HALO:, :] = z_ref[:, 2 * d_conv:]

    off = HALO - CONV_BUF
    sub = SUBLANES
    for c in range(d_conv // LANES):
        cs = slice(c * LANES, (c + 1) * LANES)
        acc = jnp.zeros((tt, LANES), F32)
        for s in range(sub):
            n = tt if s == 0 else tt + sub
            part = jnp.zeros((n, LANES), F32)
            for j in range(CONV_WIDTH):
                if (off + j) % sub == s:
                    start = off + j - s
                    part = part + cw_ref[j:j + 1, cs] * ext_ref[start:start + n, cs]
            acc = acc + part[s:s + tt]
        conv_ref[:, cs] = acc + cb_ref[:, cs]
    y_ref[:, 0:d_conv] = _layernorm_silu(conv_ref[...], lg_ref[...], lb_ref[...]).astype(y_ref.dtype)

    pos = ti * tt + lax.broadcasted_iota(jnp.int32, (tt, 1), 0)
    pg = d_pool // len(POOL_WINDOWS)
    for gi, w in enumerate(POOL_WINDOWS):
        gs = slice(gi * pg, (gi + 1) * pg)
        tok = vext_ref[HALO:, gs]
        acc = tok
        for i in range(1, w):
            acc = acc + vext_ref[HALO - i:HALO - i + tt, gs]
        cnt = jnp.minimum(pos + 1, w).astype(F32)
        d = acc / cnt - tok
        yp = jnp.dot(d.astype(BF16), pw_ref[gi], preferred_element_type=F32) * ps_ref[:, gs]
        y_ref[:, d_conv + gi * pg:d_conv + (gi + 1) * pg] = yp.astype(y_ref.dtype)


def ab_mid_prompt(z, n_seq, t_len, conv_w, conv_b, ln_g, ln_b, pool_w, pool_scale, *, tt=256):
    d_conv = conv_w.shape[1]
    d_pool = pool_scale.shape[0]
    nt = t_len // tt
    hb = tt // HALO
    row = lambda b, t: (b * nt + t, 0)
    const = lambda b, t: (0, 0)
    return pl.pallas_call(
        functools.partial(_ab_mid_body, tt=tt, d_conv=d_conv, d_pool=d_pool),
        grid=(n_seq, nt),
        in_specs=[pl.BlockSpec((tt, z.shape[1]), row),
                  pl.BlockSpec((HALO, z.shape[1]), lambda b, t: (jnp.maximum((b * nt + t) * hb - 1, 0), 0)),
                  pl.BlockSpec(conv_w.shape, const),
                  pl.BlockSpec((1, d_conv), const),
                  pl.BlockSpec((1, d_conv), const),
                  pl.BlockSpec((1, d_conv), const),
                  pl.BlockSpec(pool_w.shape, lambda b, t: (0, 0, 0)),
                  pl.BlockSpec((1, d_pool), const)],
        out_specs=[pl.BlockSpec((tt, d_conv + d_pool), row),
                   pl.BlockSpec((tt, d_conv), row)],
        out_shape=[jax.ShapeDtypeStruct((n_seq * t_len, d_conv + d_pool), BF16),
                   jax.ShapeDtypeStruct((n_seq * t_len, d_conv), F32)],
        scratch_shapes=[pltpu.VMEM((HALO + tt, d_conv), F32),
                        pltpu.VMEM((HALO + tt, d_pool), F32),
                        pltpu.VMEM((tt, d_conv), F32)],
        compiler_params=_cparams("parallel", "parallel"),
        name="ab_mid_prompt",
    )(z, z, conv_w, conv_b.reshape(1, -1), ln_g.reshape(1, -1), ln_b.reshape(1, -1),
      pool_w.astype(BF16), pool_scale.reshape(1, -1))


def _ab_mid_step_body(z_ref, sc_ref, sp_ref, cw_ref, cb_ref, lg_ref, lb_ref, pw_ref, ps_ref,
                      y_ref, nc_ref, np_ref, ext_ref, vext_ref, *, nb, t, pos0, d_conv, d_pool):
    e0 = HALO - CONV_BUF
    p0 = POOL_HALO - POOL_BUF
    z = z_ref[...].reshape(nb, t, z_ref.shape[1])
    u = z[:, :, 0:d_conv] * jax.nn.sigmoid(z[:, :, d_conv:2 * d_conv])
    ext_ref[:, e0:HALO, :] = sc_ref[...]
    ext_ref[:, HALO:, :] = u
    vext_ref[:, p0:POOL_HALO, :] = sp_ref[...]
    vext_ref[:, POOL_HALO:, :] = z[:, :, 2 * d_conv:]
    nc_ref[...] = ext_ref[:, HALO + t - CONV_BUF:, :]
    np_ref[...] = vext_ref[:, POOL_HALO + t - POOL_BUF:, :]

    acc = jnp.zeros((nb, t, d_conv), F32)
    for j in range(CONV_WIDTH):
        acc = acc + cw_ref[j:j + 1, :][None] * ext_ref[:, e0 + j:e0 + j + t, :]
    c = acc + cb_ref[...][None]
    yc = _layernorm_silu(c, lg_ref[...][None], lb_ref[...][None])
    y_ref[:, 0:d_conv] = yc.reshape(nb * t, d_conv).astype(y_ref.dtype)

    pg = d_pool // len(POOL_WINDOWS)
    for gi, w in enumerate(POOL_WINDOWS):
        gs = slice(gi * pg, (gi + 1) * pg)
        tok = vext_ref[:, POOL_HALO:, gs]
        acc = tok
        for i in range(1, w):
            acc = acc + vext_ref[:, POOL_HALO - i:POOL_HALO - i + t, gs]
        cnt = jnp.minimum(pos0 + 1 + lax.broadcasted_iota(jnp.int32, (1, t, 1), 1), w).astype(F32)
        d = (acc / cnt - tok).reshape(nb * t, pg)
        yp = jnp.dot(d.astype(BF16), pw_ref[gi], preferred_element_type=F32) * ps_ref[:, gs]
        y_ref[:, d_conv + gi * pg:d_conv + (gi + 1) * pg] = yp.astype(y_ref.dtype)


def ab_mid_step(z, row0, n_seq, t, pos0, state_conv, state_pool, conv_w, conv_b, ln_g, ln_b, pool_w,
                pool_scale, *, nb=16):
    d_conv = conv_w.shape[1]
    d_pool = pool_scale.shape[0]
    rb = nb * t
    assert row0 % rb == 0 and n_seq % nb == 0
    const = lambda i: (0, 0)
    seq3 = lambda i: (i, 0, 0)
    return pl.pallas_call(
        functools.partial(_ab_mid_step_body, nb=nb, t=t, pos0=pos0, d_conv=d_conv, d_pool=d_pool),
        grid=(n_seq // nb,),
        in_specs=[pl.BlockSpec((rb, z.shape[1]), lambda i: (row0 // rb + i, 0)),
                  pl.BlockSpec((nb, CONV_BUF, d_conv), seq3),
                  pl.BlockSpec((nb, POOL_BUF, d_pool), seq3),
                  pl.BlockSpec(conv_w.shape, const),
                  pl.BlockSpec((1, d_conv), const),
                  pl.BlockSpec((1, d_conv), const),
                  pl.BlockSpec((1, d_conv), const),
                  pl.BlockSpec(pool_w.shape, lambda i: (0, 0, 0)),
                  pl.BlockSpec((1, d_pool), const)],
        out_specs=[pl.BlockSpec((rb, d_conv + d_pool), lambda i: (i, 0)),
                   pl.BlockSpec((nb, CONV_BUF, d_conv), seq3),
                   pl.BlockSpec((nb, POOL_BUF, d_pool), seq3)],
        out_shape=[jax.ShapeDtypeStruct((n_seq * t, d_conv + d_pool), BF16),
                   jax.ShapeDtypeStruct((n_seq, CONV_BUF, d_conv), F32),
                   jax.ShapeDtypeStruct((n_seq, POOL_BUF, d_pool), F32)],
        scratch_shapes=[pltpu.VMEM((nb, HALO + t, d_conv), F32),
                        pltpu.VMEM((nb, POOL_HALO + t, d_pool), F32)],
        compiler_params=_cparams("parallel"),
        name="ab_mid_step",
    )(z, state_conv, state_pool, conv_w, conv_b.reshape(1, -1), ln_g.reshape(1, -1),
      ln_b.reshape(1, -1), pool_w.astype(BF16), pool_scale.reshape(1, -1))


N_BUCKETS = 32
MAX_DISTANCE = 128
NSA_BLOCK = 64
NSA_TOPN = 16
NSA_WINDOW = 512
DSA_TOPK = 256
IDX_HEADS = 8
IDX_DIM = 64
KV_GROUPS = 2
GROUP_HEADS = 4
PAGE = 128
BAND = 2 * PAGE
INT_MIN = -2 ** 31
KV_PAGE = (2 * KV_GROUPS * PAGE, HEAD_DIM)
TOPK_TILE_ROWS = 256
TOPK_ROW_GROUPS = 8
TOPK_KEY_BITS = 32
TOPK_INDEX_BITS = 12
SAMPLE_SEQS_PER_STEP = 4
CAUSAL_WIDTH_STEP = 2 * PAGE
CMP_BIAS_LANE0 = 64

COL_QN, COL_QD, COL_KVC, COL_KVS, COL_KVW, COL_KVD, COL_QI, COL_MISC = 0, 1024, 2048, 2560, 3072, 3584, 4096, 4608
MISC_KI, MISC_GATES, MISC_WI = 0, 64, 88
NZ = 5120


def _bucket_np(n):
    n = np.maximum(np.asarray(n, np.int32), 0)
    exact = N_BUCKETS // 2
    nf = np.maximum(n, 1).astype(np.float32)
    big = exact + (np.log(nf / np.float32(exact)) / np.float32(math.log(MAX_DISTANCE / exact))
                   * np.float32(N_BUCKETS - exact)).astype(np.int32)
    return np.where(n < exact, n, np.minimum(big, N_BUCKETS - 1))


_BUCKETS = _bucket_np(np.arange(BAND))
assert _BUCKETS[PAGE:].min() == N_BUCKETS - 1


def _softmax_rows(s, mask):
    s = jnp.where(mask, s, NEG_INF)
    m = jnp.max(s, -1, keepdims=True)
    p = jnp.where(mask, jnp.exp(s - m), 0.0)
    return p, jnp.sum(p, -1, keepdims=True)


def _dot_nt(a, b):
    return lax.dot_general(a, b, (((1,), (1,)), ((), ())), preferred_element_type=F32)


def _new_chunks(new_ref, t_new):
    chunks = [new_ref[c * PAGE:(c + 1) * PAGE, :] for c in range(t_new // PAGE)]
    rem = t_new % PAGE
    if rem:
        tail = new_ref[(t_new // PAGE) * PAGE:, :]
        chunks.append(jnp.concatenate([tail, jnp.zeros((PAGE - rem, tail.shape[1]), F32)], 0))
    return chunks


def _kv_chunks(page_refs, new_ref, t_new):
    n_parts = 2 * KV_GROUPS
    chunks = [[r[0, pl.ds(part, PAGE, stride=n_parts), :] for part in range(n_parts)] for r in page_refs]
    for x in _new_chunks(new_ref, t_new):
        chunks.append([x[:, part * HEAD_DIM:(part + 1) * HEAD_DIM] for part in range(n_parts)])
    return chunks


def _seq_view(ref, sq, seqs):
    n = ref.shape[0] // seqs
    return ref.at[pl.ds(sq * n, n)]


def _cmp_body(pt_ref, q_ref, kvn_ref, *rest, seqs, n_pages, **statics):
    del pt_ref
    pages, (wexp_ref, bias_ref, o_ref, msel_ref, *scratch) = rest[:seqs * n_pages], rest[seqs * n_pages:]
    for sq in range(seqs):
        _cmp_one(_seq_view(q_ref, sq, seqs), _seq_view(kvn_ref, sq, seqs), pages[sq * n_pages:(sq + 1) * n_pages],
                 wexp_ref, bias_ref, _seq_view(o_ref, sq, seqs), _seq_view(msel_ref, sq, seqs), *scratch,
                 n_pages=n_pages, **statics)


def _cmp_one(q_ref, kvn_ref, page_refs, wexp_ref, bias_ref, o_ref, msel_ref, comp_ref, ck_ref, cv_ref, *,
             n_pages, t_new, tq, pos0):
    qi = pl.program_id(1)
    n_keys = n_pages * PAGE + t_new
    n_cmp = n_keys // NSA_BLOCK
    n_sel = -(-n_keys // NSA_BLOCK)
    per = PAGE // NSA_BLOCK

    @pl.when(qi == 0)
    def _():
        comp_ref[...] = jnp.zeros_like(comp_ref)
        chunks = _kv_chunks(page_refs, kvn_ref, t_new)[:n_cmp // per]
        for part in range(2 * KV_GROUPS):
            cols = slice(part * HEAD_DIM, (part + 1) * HEAD_DIM)
            xw = jnp.concatenate([parts[part] * wexp_ref[:, cols] for parts in chunks], 0)
            comp_ref[0:per * len(chunks), cols] = xw.reshape(per * len(chunks), NSA_BLOCK, HEAD_DIM).sum(1)
        for g in range(KV_GROUPS):
            ck_ref[g] = comp_ref[:, g * HEAD_DIM:(g + 1) * HEAD_DIM].astype(BF16)
            cv_ref[g] = comp_ref[:, (KV_GROUPS + g) * HEAD_DIM:(KV_GROUPS + g + 1) * HEAD_DIM].astype(BF16)

    scale = HEAD_DIM ** -0.5
    rows = GROUP_HEADS * tq
    blk = lax.broadcasted_iota(jnp.int32, (1, LANES), 1)
    q0 = pos0 + qi * tq
    assert tq & (tq - 1) == 0
    qpos_st = q0 + (lax.broadcasted_iota(jnp.int32, (rows, 1), 0) & (tq - 1))
    mask = (qpos_st - ((blk + 1) * NSA_BLOCK - 1) >= 0) & (blk < n_cmp)
    cur = (q0 + lax.broadcasted_iota(jnp.int32, (tq, 1), 0)) // NSA_BLOCK
    scores = []
    for g in range(KV_GROUPS):
        heads = [g * GROUP_HEADS + r for r in range(GROUP_HEADS)]
        bias = bias_ref[g]
        if t_new != tq:
            bias = pltpu.roll(bias, qi * (tq // NSA_BLOCK) + (LANES - CMP_BIAS_LANE0), 1)
        q = jnp.concatenate([q_ref[:, h * HEAD_DIM:(h + 1) * HEAD_DIM] for h in heads], 0).astype(BF16)
        scores.append(_dot_nt(q, ck_ref[g]) * scale + bias)
    probs = []
    for g in range(KV_GROUPS):
        p, l = _softmax_rows(scores[g], mask)
        probs.append(p / jnp.maximum(l, 1e-30))
    for g in range(KV_GROUPS):
        o = jnp.dot(probs[g].astype(BF16), cv_ref[g], preferred_element_type=F32)
        for r in range(GROUP_HEADS):
            h = g * GROUP_HEADS + r
            o_ref[:, h * HEAD_DIM:(h + 1) * HEAD_DIM] = o[r * tq:(r + 1) * tq]
    for g in range(KV_GROUPS):
        imp = probs[g][0:tq]
        for r in range(1, GROUP_HEADS):
            imp = imp + probs[g][r * tq:(r + 1) * tq]
        imp = jnp.where(blk == cur, 2.0, jnp.where(blk > cur, -1.0, imp))
        imp = jnp.where(blk < n_sel, imp, -2.0)
        n_top = min(NSA_TOPN, n_sel)
        cols = slice(g * LANES, (g + 1) * LANES)

        def by_rank(imp=imp, cols=cols):
            rank = jnp.zeros((tq, LANES), F32)
            for i in range(n_sel):
                col = imp[:, i:i + 1]
                ahead = (col > imp) | ((col == imp) & (blk > i))
                rank = rank + jnp.where(ahead, 1.0, 0.0)
            msel_ref[:, cols] = jnp.where((rank < float(n_top)) & (blk < n_sel), 1.0, 0.0)

        def first_blocks(cols=cols):
            msel_ref[:, cols] = jnp.where(blk < n_top, 1.0, 0.0) + jnp.zeros((tq, LANES), F32)

        if t_new == tq:
            if pos0 + tq <= n_top * NSA_BLOCK:
                first_blocks()
            else:
                by_rank()
        else:
            early = pos0 + (qi + 1) * tq <= n_top * NSA_BLOCK
            pl.when(early)(first_blocks)
            pl.when(jnp.logical_not(early))(by_rank)


def _on_causal_width(qi, tq, widths, tile):
    if len(widths) == 1:
        tile(widths[0], True)
        return
    need = (qi * tq + tq - 1) // widths[0]
    for nw, w in enumerate(widths):
        pl.when(need == nw)(functools.partial(tile, w, nw == 0))


def _attn_body(pt_ref, q_ref, kvn_ref, *rest, mode, seqs, n_pages, **statics):
    del pt_ref
    pages, rest = rest[:seqs * n_pages], rest[seqs * n_pages:]
    m_ref = None
    if mode in ("sel", "mask"):
        m_ref, rest = rest[0], rest[1:]
    band_ref, o_ref, *scratch = rest
    for sq in range(seqs):
        _attn_one(_seq_view(q_ref, sq, seqs), _seq_view(kvn_ref, sq, seqs), pages[sq * n_pages:(sq + 1) * n_pages],
                  None if m_ref is None else _seq_view(m_ref, sq, seqs), band_ref, _seq_view(o_ref, sq, seqs),
                  *scratch, mode=mode, n_pages=n_pages, **statics)


def _attn_one(q_ref, kvn_ref, page_refs, m_ref, band_ref, o_ref, kc_ref, vc_ref, s_ref, cap_ref, *maybe_p_ref,
              mode, n_pages, t_new, tq, pos0, widths):
    p_ref = maybe_p_ref[0] if maybe_p_ref else s_ref
    qi = pl.program_id(1)
    single = t_new == tq
    n_keys = n_pages * PAGE + t_new
    kbase = pos0 - n_pages * PAGE
    scale = HEAD_DIM ** -0.5
    q0 = pos0 if single else pos0 + qi * tq

    @pl.when(qi == 0)
    def _():
        for c, parts in enumerate(_kv_chunks(page_refs, kvn_ref, t_new)):
            rows = slice(c * PAGE, (c + 1) * PAGE)
            for g in range(KV_GROUPS):
                kc_ref[g, rows, :] = parts[g].astype(BF16)
                vc_ref[g, rows, :] = parts[KV_GROUPS + g].astype(BF16)

    def tile(c0, w, band_at, maybe_first):
        qpos = q0 + lax.broadcasted_iota(jnp.int32, (tq, 1), 0)
        col = c0 + lax.broadcasted_iota(jnp.int32, (1, w), 1)
        dist = qpos - (kbase + col)
        visible = (dist >= 0) & (col < n_keys)
        if mode == "win":
            visible = visible & (dist < NSA_WINDOW)
        if mode == "mask":
            visible = visible & (m_ref[:, 0:w] > 0.5)
        keys = pl.ds(c0, w)
        groups = range(KV_GROUPS)
        for g in groups:
            mask = visible
            if mode == "sel":
                expand = (lax.broadcasted_iota(jnp.int32, (LANES, w), 1) // NSA_BLOCK
                          == lax.broadcasted_iota(jnp.int32, (LANES, w), 0))
                chosen = jnp.dot(m_ref[:, g * LANES:(g + 1) * LANES].astype(BF16),
                                 jnp.where(expand, 1.0, 0.0).astype(BF16), preferred_element_type=F32)
                mask = visible & (chosen > 0.5)
            cap_ref[g, :, 0:w] = jnp.where(mask, jnp.inf, NEG_INF)
            heads = [g * GROUP_HEADS + r for r in range(GROUP_HEADS)]
            q = jnp.concatenate([q_ref[:, h * HEAD_DIM:(h + 1) * HEAD_DIM] for h in heads], 0).astype(BF16)
            s_ref[g, :, 0:w] = _dot_nt(q, kc_ref[g, keys, :]) * (scale * LOG2E)
        for g in groups:
            if band_at is not None:
                s_ref[g, :, band_at:band_at + BAND] += band_ref[g]
            else:
                if maybe_first:
                    @pl.when(qi == 0)
                    def _():
                        s_ref[g, :, 0:PAGE] += band_ref[g, :, PAGE:]

                @pl.when(qi > 0)
                def _():
                    s_ref[g, :, pl.ds(pl.multiple_of(q0 - PAGE - kbase, PAGE), BAND)] += band_ref[g]
        sums, alive = [], []
        for g in groups:
            for r in range(GROUP_HEADS):
                rows = slice(r * tq, (r + 1) * tq)
                m = jnp.max(jnp.minimum(s_ref[g, rows, 0:w], cap_ref[g, :, 0:w]), -1, keepdims=True)
                p = jnp.exp2(jnp.minimum(s_ref[g, rows, 0:w], cap_ref[g, :, 0:w]) - m)
                p_ref[g, rows, 0:w] = p.astype(p_ref.dtype)
                sums.append(jnp.sum(p, -1, keepdims=True))
                alive.append(m > NEG_INF)
        for g in groups:
            o = jnp.dot(p_ref[g, :, 0:w].astype(BF16), vc_ref[g, keys, :], preferred_element_type=F32)
            for r in range(GROUP_HEADS):
                h = g * GROUP_HEADS + r
                o_h = o[r * tq:(r + 1) * tq] / jnp.maximum(sums[h], 1e-30)
                o_ref[:, h * HEAD_DIM:(h + 1) * HEAD_DIM] = jnp.where(alive[h], o_h, 0.0)

    if single:
        tile(0, widths[0], pos0 - PAGE - kbase, False)
    elif mode == "win":
        wch = NSA_WINDOW // PAGE
        pl.when(qi < wch)(functools.partial(tile, 0, NSA_WINDOW, None, True))
        pl.when(qi >= wch)(lambda: tile(pl.multiple_of((qi - wch) * PAGE, PAGE), NSA_WINDOW + PAGE,
                                        NSA_WINDOW - PAGE, False))
    else:
        _on_causal_width(qi, tq, widths, lambda w, first: tile(0, w, None, first))


def _index_body(pt_ref, qidx_ref, miscq_ref, misck_ref, *rest, seqs, n_pages, **statics):
    del pt_ref
    pages, (o_ref, kidx_ref) = rest[:seqs * n_pages], rest[seqs * n_pages:]
    for sq in range(seqs):
        _index_one(_seq_view(qidx_ref, sq, seqs), _seq_view(miscq_ref, sq, seqs), _seq_view(misck_ref, sq, seqs),
                   pages[sq * n_pages:(sq + 1) * n_pages], _seq_view(o_ref, sq, seqs), kidx_ref,
                   n_pages=n_pages, **statics)


def _index_one(qidx_ref, miscq_ref, misck_ref, ipage_refs, o_ref, kidx_ref, *, n_pages, t_new, tq, pos0, widths):
    qi = pl.program_id(1)
    lk = o_ref.shape[1]
    n_keys = n_pages * PAGE + t_new
    kbase = pos0 - n_pages * PAGE
    q0 = pos0 if t_new == tq else pos0 + qi * tq

    @pl.when(qi == 0)
    def _():
        for c, r in enumerate(ipage_refs):
            kidx_ref[:, c * PAGE:(c + 1) * PAGE] = r[0].astype(BF16)
        for c, x in enumerate(_new_chunks(misck_ref, t_new)):
            cols = slice((n_pages + c) * PAGE, (n_pages + c + 1) * PAGE)
            kidx_ref[:, cols] = x.T[MISC_KI:MISC_KI + IDX_DIM, :].astype(BF16)

    def tile(w, maybe_first):
        del maybe_first
        qpos = q0 + lax.broadcasted_iota(jnp.int32, (tq, 1), 0)
        col = lax.broadcasted_iota(jnp.int32, (1, w), 1)
        visible = (qpos - (kbase + col) >= 0) & (col < n_keys)
        q = jnp.concatenate([qidx_ref[:, hh * IDX_DIM:(hh + 1) * IDX_DIM] for hh in range(IDX_HEADS)], 0)
        sc = jnp.dot(q.astype(BF16), kidx_ref[:, 0:w], preferred_element_type=F32)
        score = jnp.zeros((tq, w), F32)
        for hh in range(IDX_HEADS):
            wi = miscq_ref[:, MISC_WI + hh:MISC_WI + hh + 1] * (IDX_HEADS ** -0.5)
            score = score + jnp.maximum(sc[hh * tq:(hh + 1) * tq] * (IDX_DIM ** -0.5), 0.0) * wi
        o_ref[:, 0:w] = jnp.where(visible, score, NEG_INF)
        if w < lk:
            o_ref[:, w:] = jnp.full((tq, lk - w), NEG_INF, F32)

    _on_causal_width(qi, tq, widths, tile)


def _topk_body(s_ref, m_ref, key_ref, *, k, nq, tr, widths):
    lk = s_ref.shape[1]
    assert lk <= 1 << TOPK_INDEX_BITS
    neg_key = int(np.array(NEG_INF, np.float32).view(np.int32)) ^ 0x7FFFFFFF
    kf = float(k)

    def tile(w, maybe_first):
        del maybe_first
        bits = lax.bitcast_convert_type(s_ref[:, 0:w] + 0.0, jnp.int32)
        key_ref[:, 0:w] = jnp.where(bits >= 0, bits, bits ^ 0x7FFFFFFF)
        col = lax.broadcasted_iota(jnp.int32, (1, w), 1)
        unseen = float(lk - w)

        groups = [slice(a * (tr // TOPK_ROW_GROUPS), (a + 1) * (tr // TOPK_ROW_GROUPS)) for a in range(TOPK_ROW_GROUPS)]
        zeros = tuple(jnp.zeros((tr // TOPK_ROW_GROUPS, 1), jnp.int32) for _ in groups)

        def thr_step(i, tus):
            out = []
            for rows, tu in zip(groups, tus):
                cand = tu | jnp.left_shift(jnp.int32(1), TOPK_KEY_BITS - 1 - i)
                cs = cand ^ INT_MIN
                cnt = jnp.sum(jnp.where(key_ref[rows, 0:w] >= cs, 1.0, 0.0), -1, keepdims=True)
                cnt = cnt + jnp.where(cs <= neg_key, unseen, 0.0)
                out.append(jnp.where(cnt >= kf, cand, tu))
            return tuple(out)

        thr = jnp.concatenate(lax.fori_loop(0, TOPK_KEY_BITS, thr_step, zeros, unroll=4), 0) ^ INT_MIN
        key = key_ref[:, 0:w]
        above = key > thr
        tied = key == thr
        need = kf - jnp.sum(jnp.where(above, 1.0, 0.0), -1, keepdims=True)

        def tie_step(i, j0s):
            out = []
            for rows, j0 in zip(groups, j0s):
                cand = j0 | jnp.left_shift(jnp.int32(1), TOPK_INDEX_BITS - 1 - i)
                hit = (key_ref[rows, 0:w] == thr[rows]) & (col < cand)
                cnt = jnp.sum(jnp.where(hit, 1.0, 0.0), -1, keepdims=True)
                out.append(jnp.where(cnt < need[rows], cand, j0))
            return tuple(out)

        m_ref[:, 0:w] = jnp.where(above | tied, 1.0, 0.0)
        n_tied = jnp.sum(jnp.where(tied, 1.0, 0.0), -1, keepdims=True)
        excess = jnp.max(jnp.where(thr > neg_key, n_tied - need, 0.0))

        @pl.when(excess > 0.0)
        def _():
            j0 = jnp.concatenate(lax.fori_loop(0, TOPK_INDEX_BITS, tie_step, zeros, unroll=4), 0)
            k2 = key_ref[:, 0:w]
            m_ref[:, 0:w] = jnp.where((k2 > thr) | ((k2 == thr) & (col <= j0)), 1.0, 0.0)
        if w < lk:
            m_ref[:, w:] = jnp.zeros((tr, lk - w), F32)

    _on_causal_width(pl.program_id(0) % nq, tr, widths, tile)


class _Group:
    def __init__(self, row0, n_seq, t_new, tq, pos0, n_pages, seqs=1):
        assert t_new % tq == 0 and row0 % (seqs * tq) == 0 and row0 % (seqs * t_new) == 0
        assert n_seq % seqs == 0 and (seqs == 1 or t_new == tq)
        self.seqs = seqs
        assert pos0 == n_pages * PAGE or n_pages * PAGE < pos0
        assert t_new == tq or (tq == PAGE and pos0 == 0)
        assert t_new % PAGE == 0 or t_new % PAGE < NSA_BLOCK
        self.row0, self.n_seq, self.t_new, self.tq, self.pos0, self.n_pages = row0, n_seq, t_new, tq, pos0, n_pages
        self.nq = t_new // tq
        self.rows = n_seq * t_new
        self.lk = (n_pages + -(-t_new // PAGE)) * PAGE
        self.n_keys = n_pages * PAGE + t_new

    def grid(self):
        return (self.n_seq // self.seqs, self.nq)

    def q_spec(self, width, col):
        rows = self.seqs * self.tq
        return pl.BlockSpec((rows, width), lambda b, qi, pt: (self.row0 // rows + b * self.nq + qi, col // width))

    def seq_spec(self, width, col):
        rows = self.seqs * self.t_new
        return pl.BlockSpec((rows, width), lambda b, qi, pt: (self.row0 // rows + b, col // width))

    def page_specs(self, shape):
        return [pl.BlockSpec((1,) + shape, lambda b, qi, pt, sq=sq, p=p: (pt[b * self.seqs + sq, p], 0, 0))
                for sq in range(self.seqs) for p in range(self.n_pages)]

    def page_args(self, pool):
        return [pool] * (self.seqs * self.n_pages)

    def out_spec(self, width):
        return pl.BlockSpec((self.seqs * self.tq, width), lambda b, qi, pt: (b * self.nq + qi, 0))

    def statics(self):
        return dict(seqs=self.seqs, n_pages=self.n_pages, t_new=self.t_new, tq=self.tq, pos0=self.pos0)

    def widths(self):
        if self.nq == 1:
            return (self.lk,)
        step = CAUSAL_WIDTH_STEP
        assert self.lk % step == 0
        return tuple(range(step, self.lk + 1, step))


def _cmp_bias_table(rel_bias, grp):
    lane0 = 0 if grp.nq == 1 else CMP_BIAS_LANE0
    tab = rel_bias[_BUCKETS]
    pieces, n_far = [], 0
    for lane in range(LANES + 1):
        d0 = grp.pos0 - (NSA_BLOCK * (lane - lane0 + 1) - 1)
        plain = lane < LANES and (d0 >= PAGE or d0 + grp.tq - 1 < 0)
        if plain:
            n_far += 1
            continue
        if n_far:
            pieces.append(jnp.broadcast_to(rel_bias[N_BUCKETS - 1], (grp.tq, n_far, rel_bias.shape[1])))
            n_far = 0
        if lane < LANES:
            pieces.append(tab[np.clip(d0 + np.arange(grp.tq), 0, BAND - 1)][:, None, :])
    table = jnp.transpose(jnp.concatenate(pieces, 1), (2, 0, 1))
    return table.reshape(-1, GROUP_HEADS * grp.tq, LANES)


def nsa_compress(grp, z, page_table, pool, wexp, bias):
    kv_w = KV_GROUPS * 2 * HEAD_DIM
    qw = KV_GROUPS * GROUP_HEADS * HEAD_DIM
    const2 = lambda b, qi, pt: (0, 0)
    return pl.pallas_call(
        functools.partial(_cmp_body, **grp.statics()),
        grid_spec=pltpu.PrefetchScalarGridSpec(
            num_scalar_prefetch=1,
            grid=grp.grid(),
            in_specs=[grp.q_spec(qw, COL_QN), grp.seq_spec(kv_w, COL_KVC)] + grp.page_specs(KV_PAGE)
            + [pl.BlockSpec((PAGE, kv_w), const2), pl.BlockSpec(bias.shape, lambda b, qi, pt: (0, 0, 0))],
            out_specs=[grp.out_spec(qw), grp.out_spec(KV_GROUPS * LANES)],
            scratch_shapes=[pltpu.VMEM((LANES, kv_w), F32),
                            pltpu.VMEM((KV_GROUPS, LANES, HEAD_DIM), BF16),
                            pltpu.VMEM((KV_GROUPS, LANES, HEAD_DIM), BF16)]),
        out_shape=[jax.ShapeDtypeStruct((grp.rows, qw), F32),
                   jax.ShapeDtypeStruct((grp.rows, KV_GROUPS * LANES), F32)],
        compiler_params=_cparams("parallel", "arbitrary"),
        name="nsa_compress",
    )(page_table, z, z, *grp.page_args(pool), wexp, bias)


def sparse_attention(mode, grp, z, page_table, pool, band, *, q_col, kv_col, mask=None):
    kv_w = KV_GROUPS * 2 * HEAD_DIM
    qw = KV_GROUPS * GROUP_HEADS * HEAD_DIM
    in_specs = [grp.q_spec(qw, q_col), grp.seq_spec(kv_w, kv_col)] + grp.page_specs(KV_PAGE)
    args = [z, z] + grp.page_args(pool)
    if mode in ("sel", "mask"):
        in_specs.append(grp.out_spec(mask.shape[1]))
        args.append(mask)
    in_specs.append(pl.BlockSpec((KV_GROUPS, GROUP_HEADS * grp.tq, BAND), lambda b, qi, pt: (0, 0, 0)))
    args.append(band)
    widths = grp.widths()
    s_cols = max(widths) if (mode != "win" or grp.nq == 1) else NSA_WINDOW + PAGE
    rows = GROUP_HEADS * grp.tq
    scratch = [pltpu.VMEM((KV_GROUPS, grp.lk, HEAD_DIM), BF16),
               pltpu.VMEM((KV_GROUPS, grp.lk, HEAD_DIM), BF16),
               pltpu.VMEM((KV_GROUPS, rows, s_cols), F32),
               pltpu.VMEM((KV_GROUPS, grp.tq, s_cols), F32)]
    if grp.tq % BF16_SUBLANES == 0:
        scratch.append(pltpu.VMEM((KV_GROUPS, rows, s_cols), BF16))
    return pl.pallas_call(
        functools.partial(_attn_body, mode=mode, widths=widths, **grp.statics()),
        grid_spec=pltpu.PrefetchScalarGridSpec(
            num_scalar_prefetch=1,
            grid=grp.grid(),
            in_specs=in_specs,
            out_specs=grp.out_spec(qw),
            scratch_shapes=scratch),
        out_shape=jax.ShapeDtypeStruct((grp.rows, qw), F32),
        compiler_params=_cparams("parallel", "arbitrary"),
        name="sparse_attention_" + mode,
    )(page_table, *args)


def dsa_index_scores(grp, z, page_table, idx_pool):
    return pl.pallas_call(
        functools.partial(_index_body, widths=grp.widths(), **grp.statics()),
        grid_spec=pltpu.PrefetchScalarGridSpec(
            num_scalar_prefetch=1,
            grid=grp.grid(),
            in_specs=[grp.q_spec(IDX_HEADS * IDX_DIM, COL_QI), grp.q_spec(LANES, COL_MISC),
                      grp.seq_spec(LANES, COL_MISC)] + grp.page_specs((IDX_DIM, PAGE)),
            out_specs=grp.out_spec(grp.lk),
            scratch_shapes=[pltpu.VMEM((IDX_DIM, grp.lk), BF16)]),
        out_shape=jax.ShapeDtypeStruct((grp.rows, grp.lk), F32),
        compiler_params=_cparams("parallel", "arbitrary"),
        name="dsa_index_scores",
    )(page_table, z, z, z, *grp.page_args(idx_pool))


def topk_mask(grp, scores, k):
    rows, lk = scores.shape
    tr = TOPK_TILE_ROWS
    assert rows % tr == 0 and (grp.nq == 1 or (grp.t_new % tr == 0 and tr == CAUSAL_WIDTH_STEP))
    blk = pl.BlockSpec((tr, lk), lambda i: (i, 0))
    return pl.pallas_call(
        functools.partial(_topk_body, k=k, nq=max(grp.t_new // tr, 1), tr=tr, widths=grp.widths()),
        grid=(rows // tr,),
        in_specs=[blk],
        out_specs=blk,
        out_shape=jax.ShapeDtypeStruct((rows, lk), F32),
        scratch_shapes=[pltpu.VMEM((tr, lk), jnp.int32)],
        compiler_params=_cparams("parallel"),
        name="topk_mask",
    )(scores)


def _combine_body(oc_ref, os_ref, ow_ref, od_ref, misc_ref, y_ref):
    n_heads = KV_GROUPS * GROUP_HEADS
    gates = jax.nn.sigmoid(misc_ref[:, MISC_GATES:MISC_GATES + 3 * n_heads])
    for h in range(n_heads):
        hs = slice(h * HEAD_DIM, (h + 1) * HEAD_DIM)
        o = (gates[:, 3 * h:3 * h + 1] * oc_ref[:, hs] + gates[:, 3 * h + 1:3 * h + 2] * os_ref[:, hs]
             + gates[:, 3 * h + 2:3 * h + 3] * ow_ref[:, hs])
        y_ref[:, hs] = o.astype(y_ref.dtype)
    y_ref[:, n_heads * HEAD_DIM:] = od_ref[...].astype(y_ref.dtype)


def _attn_out_body(*refs, rows):
    n = rows.n
    branch_refs = [refs[k * n:(k + 1) * n] for k in range(4)]
    misc_ref, w_ref, r_ref, o_ref, a_ref = refs[4 * n:]
    i = pl.program_id(0)
    for src in range(n):
        def gate(src=src):
            _combine_body(*[b[src] for b in branch_refs], misc_ref, a_ref)

        if n == 1:
            gate()
        else:
            t0 = sum(rows.tiles[:src])
            pl.when((i >= t0) & (i < t0 + rows.tiles[src]))(gate)
    o_ref[...] = r_ref[...] + jnp.dot(a_ref[...], w_ref[...], preferred_element_type=F32)


def attn_out_residual(branches, z, w, layer, r, *, tm=256):
    groups = list(zip(*branches))
    rows = _Rows(groups[0], tm)
    width = groups[0][0].shape[1]
    _, kdim, n = w.shape
    assert kdim == 2 * width and r.shape[0] == rows.n_tiles * tm
    in_specs = []
    for g in groups:
        in_specs += _Rows(g, tm).specs(width, lambda j: 0)
    in_specs = [pl.BlockSpec(s.block_shape, lambda i, s=s: s.index_map(i, 0)) for s in in_specs]
    in_specs += [pl.BlockSpec((tm, LANES), lambda i: (i, COL_MISC // LANES)),
                 pl.BlockSpec((None, kdim, n), lambda i: (layer, 0, 0)),
                 pl.BlockSpec((tm, n), lambda i: (i, 0))]
    return pl.pallas_call(
        functools.partial(_attn_out_body, rows=rows),
        grid=(rows.n_tiles,),
        in_specs=in_specs,
        out_specs=pl.BlockSpec((tm, n), lambda i: (i, 0)),
        out_shape=jax.ShapeDtypeStruct((rows.n_tiles * tm, n), F32),
        scratch_shapes=[pltpu.VMEM((tm, kdim), BF16)],
        compiler_params=_cparams("parallel"),
        name="attn_out_residual",
    )(*[a for g in groups for a in g], z, w, r)


def _band_tiles(rel_bias, tq):
    delta = (rel_bias[_BUCKETS] - rel_bias[N_BUCKETS - 1]).T
    rev = jnp.concatenate([delta[:, ::-1], jnp.zeros((delta.shape[0], PAGE), delta.dtype)], 1)
    tiles = jnp.stack([rev[:, PAGE - 1 - i:PAGE - 1 - i + BAND] for i in range(tq)], 1)
    return tiles.reshape(-1, GROUP_HEADS * tq, BAND) * LOG2E


def _widen_cd_w_in(w):
    sizes = (1024, 512, 512, 512, 24, 1024, 512, 512, 64, 8)
    q_n, kv_c, kv_s, kv_w, gates, q_d, kv_d, q_i, k_i, w_i = jnp.split(w, np.cumsum(sizes)[:-1].tolist(), axis=-1)
    cols = [q_n, q_d, kv_c, kv_s, kv_w, kv_d, q_i, k_i, gates, w_i]
    used = sum(c.shape[-1] for c in cols)
    return jnp.concatenate(cols + [jnp.zeros(w.shape[:-1] + (NZ - used,), w.dtype)], axis=-1)


def kernel(x_prompt, x_sample, state_conv, state_pool, cache_nsa_cmp, cache_nsa_sel, cache_nsa_win, cache_dsa_kv, cache_dsa_idx, page_table, norm_mix, norm_ffn, norm_final, ab_w_in, ab_conv_w, ab_conv_b, ab_ln_g, ab_ln_b, ab_pool_w, ab_pool_scale, ab_w_out, cd_w_in, cd_w_cmp, cd_w_out, rel_bias, ffn_w1, ffn_w2):
    bp, tp, d_model = x_prompt.shape
    bs, ts, _ = x_sample.shape
    mp, ms = bp * tp, bs * ts
    depth = norm_mix.shape[0]
    n_pages = page_table.shape[1]
    n_pool = cache_nsa_cmp.shape[1]
    past_len = n_pages * PAGE
    assert cache_nsa_cmp.shape[2] == PAGE
    win_len = cache_nsa_win.shape[2]
    assert win_len % PAGE == 0 and win_len == NSA_WINDOW and tp >= NSA_WINDOW
    kv_w = KV_GROUPS * 2 * HEAD_DIM

    xs = [x_prompt.reshape(mp, d_model), x_sample.reshape(ms, d_model)]
    grp_p = _Group(0, bp, tp, PAGE, 0, 0)
    grp_s = _Group(mp, bs, ts, ts, past_len, n_pages)
    grp_sc = _Group(mp, bs, ts, ts, past_len, n_pages, SAMPLE_SEQS_PER_STEP)
    grp_sw = _Group(mp, bs, ts, ts, past_len, win_len // PAGE, SAMPLE_SEQS_PER_STEP)
    no_pages = jnp.zeros((1, 1), jnp.int32)
    win_pages = jnp.arange(bs * (win_len // PAGE), dtype=jnp.int32).reshape(bs, win_len // PAGE)

    outs = {k: [] for k in ("conv_p", "conv_s", "pool_p", "pool_s", "cmp_p", "cmp_s", "sel_p", "sel_s",
                            "win_p", "win_s", "dsa_p", "dsa_s", "idx_p", "idx_s")}
    weights = {"ab_w_in": ab_w_in.astype(BF16), "ab_w_out": ab_w_out.astype(BF16),
               "cd_w_in": _widen_cd_w_in(cd_w_in).astype(BF16), "cd_w_out": cd_w_out.astype(BF16),
               "ffn_w1": ffn_w1, "ffn_w2": ffn_w2.astype(BF16)}
    y_p = y_s = None
    for i in range(depth):
        j = i // 2
        if i % 2 == 0:
            d_conv = ab_conv_w.shape[2]
            z = norm_matmul(xs, norm_mix[i], weights["ab_w_in"], j)
            mid_p, u_p = ab_mid_prompt(z, bp, tp, ab_conv_w[j], ab_conv_b[j], ab_ln_g[j], ab_ln_b[j],
                                       ab_pool_w[j], ab_pool_scale[j])
            mid_s, conv_s, pool_s = ab_mid_step(z, mp, bs, ts, past_len, state_conv[j], state_pool[j], ab_conv_w[j],
                                                ab_conv_b[j], ab_ln_g[j], ab_ln_b[j], ab_pool_w[j], ab_pool_scale[j])
            xs = [matmul_residual([mid_p, mid_s], weights["ab_w_out"], j, xs)]
            outs["conv_p"].append(u_p.reshape(bp, tp, d_conv)[:, tp - CONV_BUF:])
            outs["conv_s"].append(conv_s)
            outs["pool_p"].append(jnp.stack([z[(b + 1) * tp - POOL_BUF:(b + 1) * tp, 2 * d_conv:] for b in range(bp)]))
            outs["pool_s"].append(pool_s)
        else:
            z = norm_matmul(xs, norm_mix[i], weights["cd_w_in"], j)
            nsa_bias = rel_bias[:, :KV_GROUPS * GROUP_HEADS]
            band_p, band_s = _band_tiles(rel_bias, grp_p.tq), _band_tiles(rel_bias, grp_s.tq)
            wexp = jnp.tile(jnp.repeat(jnp.transpose(cd_w_cmp[j], (1, 0, 2)).reshape(NSA_BLOCK, 2 * KV_GROUPS),
                                       HEAD_DIM, axis=1), (PAGE // NSA_BLOCK, 1))
            pt = page_table + j * n_pool
            pools = [c.reshape((-1,) + KV_PAGE) for c in (cache_nsa_cmp, cache_nsa_sel, cache_dsa_kv)]
            idx_pool = jnp.swapaxes(cache_dsa_idx, 2, 3).reshape(-1, IDX_DIM, PAGE)
            win_pool = cache_nsa_win.reshape((-1,) + KV_PAGE)
            wpt = win_pages + j * bs * (win_len // PAGE)
            mids = []
            for grp, gc, gw, ptab, wtab, band in ((grp_p, grp_p, grp_p, no_pages, no_pages, band_p),
                                                  (grp_s, grp_sc, grp_sw, pt, wpt, band_s)):
                o_c, msel = nsa_compress(gc, z, ptab, pools[0], wexp, _cmp_bias_table(nsa_bias, gc))
                o_s = sparse_attention("sel", gc, z, ptab, pools[1], band[:KV_GROUPS],
                                       q_col=COL_QN, kv_col=COL_KVS, mask=msel)
                o_w = sparse_attention("win", gw, z, wtab, win_pool, band[:KV_GROUPS],
                                       q_col=COL_QN, kv_col=COL_KVW)
                top = topk_mask(grp, dsa_index_scores(grp, z, ptab, idx_pool), min(DSA_TOPK, grp.n_keys // 4))
                o_d = sparse_attention("mask", gc, z, ptab, pools[2], band[KV_GROUPS:],
                                       q_col=COL_QD, kv_col=COL_KVD, mask=top)
                mids.append((o_c, o_s, o_w, o_d))
            xs = [attn_out_residual(mids, z, weights["cd_w_out"], j, xs[0])]

            def kv_out(col, width, tail):
                seg = z[:, col:col + width]
                return seg[:mp].reshape((bp, tp) + tail), seg[mp:].reshape((bs, ts) + tail)

            kv_tail = (2, KV_GROUPS, HEAD_DIM)
            for name, col in (("cmp", COL_KVC), ("sel", COL_KVS), ("dsa", COL_KVD)):
                p_new, s_new = kv_out(col, kv_w, kv_tail)
                outs[name + "_p"].append(p_new)
                outs[name + "_s"].append(s_new)
            w_p, w_s = kv_out(COL_KVW, kv_w, kv_tail)
            outs["win_p"].append(w_p[:, tp - NSA_WINDOW:])
            outs["win_s"].append(jnp.concatenate([cache_nsa_win[j], w_s], 1)[:, ts:])
            i_p, i_s = kv_out(COL_MISC + MISC_KI, IDX_DIM, (IDX_DIM,))
            outs["idx_p"].append(i_p)
            outs["idx_s"].append(i_s)
        a = norm_matmul(xs, norm_ffn[i], weights["ffn_w1"], i, relu2=True, out_dtype=BF16)
        if i == depth - 1:
            y_p, y_s = matmul_residual([a], weights["ffn_w2"], i, xs, norm_final, split_out=(mp, ms))
        else:
            xs = [matmul_residual([a], weights["ffn_w2"], i, xs)]

    st = {k: jnp.stack(v) for k, v in outs.items()}
    return (y_p.reshape(bp, tp, d_model), y_s.reshape(bs, ts, d_model),
            st["conv_p"], st["conv_s"], st["pool_p"], st["pool_s"], st["cmp_p"], st["cmp_s"],
            st["sel_p"], st["sel_s"], st["win_p"], st["win_s"], st["dsa_p"], st["dsa_s"],
            st["idx_p"], st["idx_s"])
```
